```python
import math
import jax, jax.numpy as jnp
from jax import lax
import numpy as np

D_MODEL = 1024
BATCH = 8
SEQ = 4096
DEPTH = 2
DEC_BATCH = 2
DEC_SEQ = 8192
PAST_LEN = 128

HEAD_DIM = 64
N_RET_HEADS = 8
N_ATT_HEADS = 8
N_KV_HEADS = 2
RET_WIDTH = N_RET_HEADS * HEAD_DIM
ATT_WIDTH = N_ATT_HEADS * HEAD_DIM
KV_WIDTH = N_KV_HEADS * HEAD_DIM
MIX_WIDTH = RET_WIDTH + ATT_WIDTH
IN_WIDTH = 4 * RET_WIDTH + ATT_WIDTH + 2 * KV_WIDTH
SPLITS = (RET_WIDTH, 2 * RET_WIDTH, 3 * RET_WIDTH, 4 * RET_WIDTH,
          4 * RET_WIDTH + ATT_WIDTH, 4 * RET_WIDTH + ATT_WIDTH + KV_WIDTH)
WINDOW = 128
BLOCK = 128
N_META = 16
PAD_FRONT = BLOCK - N_META
ROPE_THETA = 500000.0
ROPE_DIMS = HEAD_DIM // 4
RET_ROPE_THETA = 10000.0
D_FF = 2816
N_EXPERTS = 8
TOP_K = 2
D_FF_EXPERT = 3584
N_DENSE = (DEPTH + 1) // 2
N_MOE = DEPTH // 2
EPS = 1e-6
NEG = -1e30

kernel_name = "hymba_retention_swa_moe_encoder"


def rms_norm(x, gain):
    xf = x.astype(jnp.float32)
    y = xf * lax.rsqrt(jnp.mean(xf * xf, axis=-1, keepdims=True) + EPS)
    return (y * gain.astype(jnp.float32)).astype(x.dtype)


def rotary(x, pos, n_rot, theta):
    half = n_rot // 2
    inv = theta ** (-jnp.arange(half, dtype=jnp.float32) * 2.0 / n_rot)
    ang = pos.astype(jnp.float32)[:, None] * inv[None, :]
    cos = jnp.cos(ang)[None, :, None, :]
    sin = jnp.sin(ang)[None, :, None, :]
    xf = x.astype(jnp.float32)
    x1 = xf[..., :half]
    x2 = xf[..., half:n_rot]
    out = jnp.concatenate([x1 * cos - x2 * sin, x1 * sin + x2 * cos, xf[..., n_rot:]], axis=-1)
    return out.astype(x.dtype)


def bidirectional_retention(q, k, v, log_gf, log_gb):
    B, Lp, H, D = q.shape
    C = BLOCK
    nc = Lp // C
    f32 = jnp.float32
    qc = q.astype(f32).reshape(B, nc, C, H, D)
    kc = (k.astype(f32) * D ** -0.5).reshape(B, nc, C, H, D)
    vc = v.astype(f32).reshape(B, nc, C, H, D)
    lgf = log_gf.astype(f32)
    lgb = log_gb.astype(f32)
    idx = jnp.arange(C, dtype=f32)
    diff = idx[:, None] - idx[None, :]
    dmat = jnp.where(diff[None] >= 0,
                     jnp.exp(jnp.maximum(diff, 0.0)[None] * lgf[:, None, None]),
                     jnp.exp(jnp.maximum(-diff, 0.0)[None] * lgb[:, None, None]))
    s = jnp.einsum('bnihd,bnjhd->bnhij', qc, kc) * dmat
    intra = jnp.einsum('bnhij,bnjhe->bnihe', s, vc)
    wf = jnp.exp((C - 1.0 - idx)[:, None] * lgf[None, :])
    wb = jnp.exp(idx[:, None] * lgb[None, :])
    uf = jnp.einsum('bnjhd,bnjhe->nbhde', kc * wf[None, None, :, :, None], vc)
    ub = jnp.einsum('bnjhd,bnjhe->nbhde', kc * wb[None, None, :, :, None], vc)
    cf = jnp.exp(C * lgf)[None, :, None, None]
    cb = jnp.exp(C * lgb)[None, :, None, None]
    s0 = jnp.zeros((B, H, D, D), f32)
    _, sf = lax.scan(lambda S, u: (cf * S + u, S), s0, uf)
    _, sb = lax.scan(lambda S, u: (cb * S + u, S), s0, ub, reverse=True)
    qf = jnp.exp((idx + 1.0)[:, None] * lgf[None, :])[:, :, None]
    qb = jnp.exp((C - idx)[:, None] * lgb[None, :])[:, :, None]
    inter = (jnp.einsum('bnihd,nbhde->bnihe', qc, sf) * qf
             + jnp.einsum('bnihd,nbhde->bnihe', qc, sb) * qb)
    return (intra + inter).reshape(B, Lp, H, D).astype(q.dtype)


def banded_sink_attention(q, k, v, sink, L):
    B, Lp, _, D = q.shape
    C = BLOCK
    nb = Lp // C
    G = N_ATT_HEADS // N_KV_HEADS
    qb = q.reshape(B, nb, C, N_KV_HEADS, G, D)

    def bands(t):
        tp = jnp.pad(t, ((0, 0), (C, C), (0, 0), (0, 0))).reshape(B, nb + 2, C, N_KV_HEADS, D)
        return jnp.concatenate([tp[:, :-2], tp[:, 1:-1], tp[:, 2:]], axis=2)

    kb = bands(k)
    vb = bands(v)
    km = k[:, PAD_FRONT:BLOCK]
    vm = v[:, PAD_FRONT:BLOCK]
    qpos = (jnp.arange(Lp) - PAD_FRONT).reshape(nb, C)
    kp = (jnp.arange(-C, Lp + C) - PAD_FRONT).reshape(nb + 2, C)
    kpos = jnp.concatenate([kp[:-2], kp[1:-1], kp[2:]], axis=1)
    valid = ((jnp.abs(qpos[:, :, None] - kpos[:, None, :]) <= WINDOW)
             & (kpos[:, None, :] >= N_META) & (kpos[:, None, :] < L))
    scale = D ** -0.5
    s_band = jnp.einsum('bnqhgd,bnkhd->bnhgqk', qb, kb).astype(jnp.float32) * scale
    s_band = jnp.where(valid[None, :, None, None], s_band, NEG)
    s_meta = jnp.einsum('bnqhgd,bmhd->bnhgqm', qb, km).astype(jnp.float32) * scale
    s_sink = jnp.broadcast_to(sink.astype(jnp.float32).reshape(N_KV_HEADS, G)[None, None, :, :, None, None],
                              s_meta.shape[:-1] + (1,))
    p = jax.nn.softmax(jnp.concatenate([s_sink, s_band, s_meta], axis=-1), axis=-1)
    p_band = p[..., 1:1 + 3 * C].astype(v.dtype)
    p_meta = p[..., 1 + 3 * C:].astype(v.dtype)
    out = (jnp.einsum('bnhgqk,bnkhd->bnqhgd', p_band, vb)
           + jnp.einsum('bnhgqm,bmhd->bnqhgd', p_meta, vm))
    return out.reshape(B, Lp, N_ATT_HEADS, D)


def token_mixer(a, pos, w_in, log_gf, log_gb, ret_gain, q_gain, k_gain, sink, att_gain, w_out):
    B, L, _ = a.shape
    proj = a @ w_in
    rq, rk, rv, rg, aq, ak, av = jnp.split(proj, SPLITS, axis=-1)
    heads = lambda t, n: t.reshape(B, L, n, HEAD_DIM)
    rq = rotary(heads(rq, N_RET_HEADS), pos, HEAD_DIM, RET_ROPE_THETA)
    rk = rotary(heads(rk, N_RET_HEADS), pos, HEAD_DIM, RET_ROPE_THETA)
    rv = heads(rv, N_RET_HEADS)
    aq = rotary(rms_norm(heads(aq, N_ATT_HEADS), q_gain), pos, ROPE_DIMS, ROPE_THETA)
    ak = rotary(rms_norm(heads(ak, N_KV_HEADS), k_gain), pos, ROPE_DIMS, ROPE_THETA)
    av = heads(av, N_KV_HEADS)
    padf = lambda t: jnp.pad(t, ((0, 0), (PAD_FRONT, 0), (0, 0), (0, 0)))
    ret = bidirectional_retention(padf(rq), padf(rk), padf(rv), log_gf, log_gb)[:, PAD_FRONT:]
    ret = rms_norm(ret, ret_gain.reshape(N_RET_HEADS, HEAD_DIM)).reshape(B, L, RET_WIDTH) * jax.nn.silu(rg)
    att = banded_sink_attention(padf(aq), padf(ak), padf(av), sink, L)[:, PAD_FRONT:]
    att = rms_norm(att.reshape(B, L, ATT_WIDTH), att_gain)
    merged = jnp.concatenate([ret.astype(a.dtype), att.astype(a.dtype)], axis=-1)
    return merged @ w_out


def swiglu(x, w_gate, w_up, w_down):
    return (jax.nn.silu(x @ w_gate) * (x @ w_up)) @ w_down


def moe_swiglu(x, w_router, w_gate, w_up, w_down):
    B, L, Dm = x.shape
    t = x.reshape(B * L, Dm)
    logits = (t @ w_router).astype(jnp.float32)
    vals, idx = lax.top_k(logits, TOP_K)
    gates = jax.nn.softmax(vals, axis=-1)
    comb = jnp.sum(jax.nn.one_hot(idx, N_EXPERTS, dtype=jnp.float32) * gates[..., None], axis=1)
    y = jnp.zeros_like(t)
    for e in range(N_EXPERTS):
        y = y + comb[:, e:e + 1].astype(t.dtype) * swiglu(t, w_gate[e], w_up[e], w_down[e])
    return y.reshape(B, L, Dm)


def encoder_trunk(x, meta_tokens, norm_mix, w_in, ret_log_decay_fwd, ret_log_decay_bwd, ret_out_gain,
                  q_norm_gain, k_norm_gain, attn_sink, attn_out_gain, w_out, norm_ffn,
                  ffn_w_gate, ffn_w_up, ffn_w_down, moe_router, moe_w_gate, moe_w_up, moe_w_down):
    B, S, Dm = x.shape
    L = N_META + S
    h = jnp.concatenate([jnp.broadcast_to(meta_tokens.astype(x.dtype)[None], (B, N_META, Dm)), x], axis=1)
    pos = jnp.arange(L)
    for layer in range(DEPTH):
        a = rms_norm(h, norm_mix[layer])
        h = h + token_mixer(a, pos, w_in[layer], ret_log_decay_fwd[layer], ret_log_decay_bwd[layer],
                            ret_out_gain[layer], q_norm_gain[layer], k_norm_gain[layer],
                            attn_sink[layer], attn_out_gain[layer], w_out[layer])
        f = rms_norm(h, norm_ffn[layer])
        i = layer // 2
        if layer % 2 == 0:
            h = h + swiglu(f, ffn_w_gate[i], ffn_w_up[i], ffn_w_down[i])
        else:
            h = h + moe_swiglu(f, moe_router[i], moe_w_gate[i], moe_w_up[i], moe_w_down[i])
    return h[:, N_META:]


def setup_inputs(seed: int = 0) -> dict:
    key = jax.random.key(seed)
    ks = jax.random.split(key, 24)
    f32 = jnp.float32
    nrm = lambda k, shape, scale: jax.random.normal(k, shape, f32) * scale
    base_decay = jnp.log(1.0 - 2.0 ** (-5.0 - jnp.arange(N_RET_HEADS, dtype=f32)))
    return {
        "x_prompt": nrm(ks[0], (BATCH, SEQ, D_MODEL), 1.0),
        "x_sample": nrm(ks[1], (DEC_BATCH, DEC_SEQ, D_MODEL), 1.0),
        "meta_tokens": nrm(ks[2], (N_META, D_MODEL), 1.0),
        "norm_mix": 1.0 + nrm(ks[3], (DEPTH, D_MODEL), 0.02),
        "w_in": nrm(ks[4], (DEPTH, D_MODEL, IN_WIDTH), D_MODEL ** -0.5),
        "ret_log_decay_fwd": base_decay[None] * jnp.exp(nrm(ks[5], (DEPTH, N_RET_HEADS), 0.1)),
        "ret_log_decay_bwd": base_decay[None] * jnp.exp(nrm(ks[6], (DEPTH, N_RET_HEADS), 0.1)),
        "ret_out_gain": 1.0 + nrm(ks[7], (DEPTH, RET_WIDTH), 0.02),
        "q_norm_gain": 1.0 + nrm(ks[8], (DEPTH, HEAD_DIM), 0.02),
        "k_norm_gain": 1.0 + nrm(ks[9], (DEPTH, HEAD_DIM), 0.02),
        "attn_sink": nrm(ks[10], (DEPTH, N_ATT_HEADS), 0.5),
        "attn_out_gain": 1.0 + nrm(ks[11], (DEPTH, ATT_WIDTH), 0.02),
        "w_out": nrm(ks[12], (DEPTH, MIX_WIDTH, D_MODEL), 0.5 * MIX_WIDTH ** -0.5),
        "norm_ffn": 1.0 + nrm(ks[13], (DEPTH, D_MODEL), 0.02),
        "ffn_w_gate": nrm(ks[14], (N_DENSE, D_MODEL, D_FF), D_MODEL ** -0.5),
        "ffn_w_up": nrm(ks[15], (N_DENSE, D_MODEL, D_FF), D_MODEL ** -0.5),
        "ffn_w_down": nrm(ks[16], (N_DENSE, D_FF, D_MODEL), 0.5 * D_FF ** -0.5),
        "moe_router": nrm(ks[17], (N_MOE, D_MODEL, N_EXPERTS), D_MODEL ** -0.5),
        "moe_w_gate": nrm(ks[18], (N_MOE, N_EXPERTS, D_MODEL, D_FF_EXPERT), D_MODEL ** -0.5),
        "moe_w_up": nrm(ks[19], (N_MOE, N_EXPERTS, D_MODEL, D_FF_EXPERT), D_MODEL ** -0.5),
        "moe_w_down": nrm(ks[20], (N_MOE, N_EXPERTS, D_FF_EXPERT, D_MODEL), 0.5 * D_FF_EXPERT ** -0.5),
    }


def reference(x_prompt, x_sample, meta_tokens, norm_mix, w_in, ret_log_decay_fwd, ret_log_decay_bwd,
              ret_out_gain, q_norm_gain, k_norm_gain, attn_sink, attn_out_gain, w_out, norm_ffn,
              ffn_w_gate, ffn_w_up, ffn_w_down, moe_router, moe_w_gate, moe_w_up, moe_w_down):
    y_prompt = encoder_trunk(x_prompt, meta_tokens, norm_mix, w_in, ret_log_decay_fwd, ret_log_decay_bwd,
                             ret_out_gain, q_norm_gain, k_norm_gain, attn_sink, attn_out_gain, w_out,
                             norm_ffn, ffn_w_gate, ffn_w_up, ffn_w_down, moe_router, moe_w_gate,
                             moe_w_up, moe_w_down)
    y_sample = encoder_trunk(x_sample, meta_tokens, norm_mix, w_in, ret_log_decay_fwd, ret_log_decay_bwd,
                             ret_out_gain, q_norm_gain, k_norm_gain, attn_sink, attn_out_gain, w_out,
                             norm_ffn, ffn_w_gate, ffn_w_up, ffn_w_down, moe_router, moe_w_gate,
                             moe_w_up, moe_w_down)
    return (y_prompt, y_sample)
```

```python
import functools

import jax
import jax.numpy as jnp
from jax import lax
from jax.experimental import pallas as pl
from jax.experimental.pallas import tpu as pltpu

F32 = jnp.float32
BF16 = jnp.bfloat16

D_MODEL = 1024
HEAD_DIM = 64
N_RET_HEADS = 8
N_ATT_HEADS = 8
N_KV_HEADS = 2
RET_WIDTH = N_RET_HEADS * HEAD_DIM
ATT_WIDTH = N_ATT_HEADS * HEAD_DIM
KV_WIDTH = N_KV_HEADS * HEAD_DIM
IN_WIDTH = 4 * RET_WIDTH + ATT_WIDTH + 2 * KV_WIDTH
CHUNK = 128
LANES = 128
N_META = 16
PAD_FRONT = CHUNK - N_META
ROPE_THETA = 500000.0
ROPE_DIMS = HEAD_DIM // 4
RET_ROPE_THETA = 10000.0
N_EXPERTS = 8
EPS = 1e-6
NEG = -1e30
QK_SCALE = HEAD_DIM ** -0.5
MIB = 1024 * 1024


def _row_tile(rows, prefs):
    for t in prefs:
        if rows % t == 0:
            return t
    raise ValueError(f"no row tile for {rows} rows among {prefs}")


def _resident(shape):
    return pl.BlockSpec(shape, lambda *_: (0,) * len(shape), pipeline_mode=pl.Buffered(1))


def _rms(x, gain):
    ms = jnp.mean(x * x, axis=-1, keepdims=True)
    return x * lax.rsqrt(ms + EPS) * gain


def _silu(x):
    return x * jax.nn.sigmoid(x)


def _inproj_kernel(h_ref, gain_ref, w_ref, cr_ref, sr_ref, ca_ref, sa_ref, qg_ref, kg_ref, ones_ref,
                   rq_ref, rk_ref, rv_ref, rg_ref, aq_ref, ak_ref, av_ref):
    a = _rms(h_ref[...], gain_ref[...]).astype(BF16)
    tm = a.shape[0]
    lane = lax.broadcasted_iota(jnp.int32, (tm, LANES), 1)
    in_head = lane & (HEAD_DIM - 1)
    low_head = lane < HEAD_DIM

    def proj(c0, c1):
        return jnp.dot(a, w_ref[:, c0:c1], preferred_element_type=F32)

    def rotate(x, cos, sin_signed, half):
        partner = jnp.where(in_head < half, pltpu.roll(x, LANES - half, 1), pltpu.roll(x, half, 1))
        return x * cos + partner * sin_signed

    def head_norm(x, g):
        ss = jnp.dot((x * x).astype(BF16), ones_ref[...], preferred_element_type=F32)
        return x * lax.rsqrt(ss * (1.0 / HEAD_DIM) + EPS) * g

    cr, sr, ca, sa = cr_ref[...], sr_ref[...], ca_ref[...], sa_ref[...]
    cols = lambda j: slice(j * LANES, (j + 1) * LANES)
    n_lane_tiles = RET_WIDTH // LANES

    p = proj(0, RET_WIDTH)
    for j in range(n_lane_tiles):
        rq_ref[:, cols(j)] = rotate(p[:, cols(j)], cr, sr, HEAD_DIM // 2).astype(BF16)
    p = proj(RET_WIDTH, 2 * RET_WIDTH)
    for j in range(n_lane_tiles):
        rk_ref[:, cols(j)] = (rotate(p[:, cols(j)], cr, sr, HEAD_DIM // 2) * QK_SCALE).astype(BF16)
    rv_ref[...] = proj(2 * RET_WIDTH, 3 * RET_WIDTH).astype(BF16)
    rg_ref[...] = proj(3 * RET_WIDTH, 4 * RET_WIDTH).astype(BF16)

    base = 4 * RET_WIDTH
    p = proj(base, base + ATT_WIDTH)
    for j in range(ATT_WIDTH // LANES):
        n = head_norm(p[:, cols(j)], qg_ref[...])
        aq_ref[:, cols(j)] = (rotate(n, ca, sa, ROPE_DIMS // 2) * QK_SCALE).astype(BF16)

    p = proj(base + ATT_WIDTH, IN_WIDTH)
    k = rotate(head_norm(p[:, :KV_WIDTH], kg_ref[...]), ca, sa, ROPE_DIMS // 2)
    v = p[:, KV_WIDTH:]
    k_sw = pltpu.roll(k, HEAD_DIM, 1)
    v_sw = pltpu.roll(v, HEAD_DIM, 1)
    ak_ref[:, 0:LANES] = jnp.where(low_head, k, k_sw).astype(BF16)
    ak_ref[:, LANES:2 * LANES] = jnp.where(low_head, k_sw, k).astype(BF16)
    av_ref[:, 0:LANES] = jnp.where(low_head, v, v_sw).astype(BF16)
    av_ref[:, LANES:2 * LANES] = jnp.where(low_head, v_sw, v).astype(BF16)


def _inproj(h, gain, w, tabs, qg, kg, ones_bd):
    rows = h.shape[0]
    tm = _row_tile(rows, (1024, 640, 512, 384, 256, 128))
    row_spec = lambda width: pl.BlockSpec((tm, width), lambda i: (i, 0))
    out_widths = (RET_WIDTH,) * 4 + (ATT_WIDTH, 2 * KV_WIDTH, 2 * KV_WIDTH)
    return pl.pallas_call(
        _inproj_kernel,
        grid=(rows // tm,),
        in_specs=[row_spec(D_MODEL), _resident((1, D_MODEL)), _resident((D_MODEL, IN_WIDTH)),
                  row_spec(LANES), row_spec(LANES), row_spec(LANES), row_spec(LANES),
                  _resident((1, LANES)), _resident((1, LANES)), _resident((LANES, LANES))],
        out_specs=[row_spec(wd) for wd in out_widths],
        out_shape=[jax.ShapeDtypeStruct((rows, wd), BF16) for wd in out_widths],
        compiler_params=pltpu.CompilerParams(dimension_semantics=("parallel",), vmem_limit_bytes=56 * MIB),
        name="inproj",
    )(h, gain, w, *tabs, qg, kg, ones_bd)


def _retention_kernel(q_ref, k_ref, v_ref, g_ref, dec_ref, o_ref, sb_ref, *, n_chunks):
    lgf, lgb = dec_ref[0:1, :], dec_ref[1:2, :]
    gain = dec_ref[6:7, :]
    ri = lax.broadcasted_iota(jnp.int32, (CHUNK, LANES), 0)
    ci = lax.broadcasted_iota(jnp.int32, (CHUNK, LANES), 1)
    r = ri.astype(F32)
    diff = (ri - ci).astype(F32)

    def decay_mask(lf, lb):
        return jnp.where(diff >= 0, jnp.exp(jnp.maximum(diff, 0.0) * lf), jnp.exp(jnp.maximum(-diff, 0.0) * lb))

    dm = jnp.concatenate([decay_mask(dec_ref[2:3, :], dec_ref[4:5, :]),
                          decay_mask(dec_ref[3:4, :], dec_ref[5:6, :])], axis=1)
    wf = jnp.exp((CHUNK - 1.0 - r) * lgf)
    wb = jnp.exp(r * lgb)
    qf = jnp.exp((r + 1.0) * lgf)
    qb = jnp.exp((CHUNK - r) * lgb)
    cf = jnp.exp(float(CHUNK) * lgf)
    cb = jnp.exp(float(CHUNK) * lgb)
    low_c = ci < HEAD_DIM
    same_head = (ri < HEAD_DIM) == low_c
    ones_bd = jnp.where(same_head, 1.0, 0.0).astype(BF16)
    m0 = jnp.where(low_c, 1.0, 0.0).astype(BF16)
    m1 = jnp.where(low_c, 0.0, 1.0).astype(BF16)

    def chunk(ref, n):
        return ref[pl.ds(pl.multiple_of(n * CHUNK, CHUNK), CHUNK), :]

    def pair_rows(x):
        return jnp.concatenate([x * m0, x * m1], axis=0)

    def state_update(state, k_weighted, v, carry_decay):
        u = jnp.dot(k_weighted.T.astype(BF16), v, preferred_element_type=F32)
        return carry_decay * state + jnp.where(same_head, u, 0.0)

    def backward(i, state):
        n = n_chunks - 1 - i
        sb_ref[n] = state
        return state_update(state, chunk(k_ref, n).astype(F32) * wb, chunk(v_ref, n), cb)

    zero_state = jnp.zeros((LANES, LANES), F32)
    lax.fori_loop(0, n_chunks, backward, zero_state)

    def forward(n, state):
        q, k, v = chunk(q_ref, n), chunk(k_ref, n), chunk(v_ref, n)
        s = lax.dot_general(q, pair_rows(k), (((1,), (1,)), ((), ())), preferred_element_type=F32) * dm
        o = jnp.dot(s.astype(BF16), pair_rows(v), preferred_element_type=F32)
        o = o + jnp.dot(q, state.astype(BF16), preferred_element_type=F32) * qf
        o = o + jnp.dot(q, sb_ref[n].astype(BF16), preferred_element_type=F32) * qb
        ss = jnp.dot((o * o).astype(BF16), ones_bd, preferred_element_type=F32)
        o = o * lax.rsqrt(ss * (1.0 / HEAD_DIM) + EPS) * gain * _silu(chunk(g_ref, n).astype(F32))
        o_ref[pl.ds(pl.multiple_of(n * CHUNK, CHUNK), CHUNK), :] = o.astype(BF16)
        return state_update(state, k.astype(F32) * wf, v, cf)

    lax.fori_loop(0, n_chunks, forward, zero_state)


def _retention(rq, rk, rv, rg, dec, batch, seq_rows):
    n_chunks = seq_rows // CHUNK
    n_pairs = RET_WIDTH // LANES
    view = lambda t: t.reshape(batch, seq_rows, RET_WIDTH)
    seq_spec = pl.BlockSpec((None, seq_rows, LANES), lambda b, p: (b, 0, p))
    out = pl.pallas_call(
        functools.partial(_retention_kernel, n_chunks=n_chunks),
        grid=(batch, n_pairs),
        in_specs=[seq_spec, seq_spec, seq_spec, seq_spec,
                  pl.BlockSpec((None, 8, LANES), lambda b, p: (p, 0, 0))],
        out_specs=seq_spec,
        out_shape=jax.ShapeDtypeStruct((batch, seq_rows, RET_WIDTH), BF16),
        scratch_shapes=[pltpu.VMEM((n_chunks, LANES, LANES), F32)],
        compiler_params=pltpu.CompilerParams(dimension_semantics=("parallel", "parallel"),
                                             vmem_limit_bytes=48 * MIB),
        name="retention",
    )(view(rq), view(rk), view(rv), view(rg), dec)
    return out.reshape(batch * seq_rows, RET_WIDTH)


def _attention_kernel(q_ref, kp_ref, kc_ref, kn_ref, km_ref, vp_ref, vc_ref, vn_ref, vm_ref,
                      sink_ref, gain_ref, o_ref, *, n_chunks):
    c = pl.program_id(1)
    ri = lax.broadcasted_iota(jnp.int32, (CHUNK, LANES), 0)
    ci = lax.broadcasted_iota(jnp.int32, (CHUNK, LANES), 1)
    never = 2 * CHUNK
    prev_off = jnp.where(c >= 2, 0, never)
    cur_off = jnp.where(c >= 1, 0, never)
    next_off = jnp.where(c + 1 <= n_chunks - 1, 0, never)
    to_bias = lambda ok: jnp.where(ok, 0.0, NEG)
    bias = jnp.concatenate([to_bias(ci >= ri + prev_off), to_bias(ci >= cur_off),
                            to_bias(ci + next_off <= ri), to_bias(ci >= PAD_FRONT)], axis=1)
    n_keys = 4 * CHUNK
    low_c = ci < HEAD_DIM
    m0 = jnp.where(low_c, 1.0, 0.0).astype(BF16)
    m1 = jnp.where(low_c, 0.0, 1.0).astype(BF16)
    m0k = jnp.concatenate([m0] * 4, axis=0)
    m1k = jnp.concatenate([m1] * 4, axis=0)

    outs = []
    for g in range(N_KV_HEADS):
        ksl = slice(g * LANES, (g + 1) * LANES)
        kd = jnp.concatenate([kp_ref[:, ksl], kc_ref[:, ksl], kn_ref[:, ksl], km_ref[:, ksl]], axis=0)
        vd = jnp.concatenate([vp_ref[:, ksl], vc_ref[:, ksl], vn_ref[:, ksl], vm_ref[:, ksl]], axis=0)
        k_pair = jnp.concatenate([kd * m0k, kd * m1k], axis=0)
        v_pair = jnp.concatenate([vd * m0k, vd * m1k], axis=0)
        pairs_per_kv = N_ATT_HEADS // N_KV_HEADS // 2
        for pp in range(pairs_per_kv):
            p = g * pairs_per_kv + pp
            q2 = q_ref[:, p * LANES:(p + 1) * LANES]
            s = lax.dot_general(q2, k_pair, (((1,), (1,)), ((), ())), preferred_element_type=F32)
            probs, inv = [], []
            for hh in range(2):
                sh = s[:, hh * n_keys:(hh + 1) * n_keys] + bias
                sk = sink_ref[2 * p + hh:2 * p + hh + 1, 0:1]
                m = jnp.maximum(jnp.max(sh, axis=-1, keepdims=True), sk)
                e = jnp.exp(sh - m)
                inv.append(1.0 / (jnp.sum(e, axis=-1, keepdims=True) + jnp.exp(sk - m)))
                probs.append(e.astype(BF16))
            o = jnp.dot(jnp.concatenate(probs, axis=1), v_pair, preferred_element_type=F32)
            outs.append(o * jnp.where(low_c, inv[0], inv[1]))
    att = jnp.concatenate(outs, axis=1)
    row = lax.broadcasted_iota(jnp.int32, att.shape, 0)
    att = jnp.where(row >= jnp.where(c == 0, PAD_FRONT, 0), att, 0.0)
    o_ref[...] = _rms(att, gain_ref[...]).astype(BF16)


def _attention(aq, ak, av, sink, gain, batch, seq_rows):
    n_chunks = seq_rows // CHUNK
    q3 = aq.reshape(batch, seq_rows, ATT_WIDTH)
    k3 = ak.reshape(batch, seq_rows, 2 * KV_WIDTH)
    v3 = av.reshape(batch, seq_rows, 2 * KV_WIDTH)
    kv = lambda fn: pl.BlockSpec((None, CHUNK, 2 * KV_WIDTH), lambda b, c: (b, fn(c), 0))
    kv_specs = [kv(lambda c: jnp.maximum(c - 1, 0)), kv(lambda c: c),
                kv(lambda c: jnp.minimum(c + 1, n_chunks - 1)), kv(lambda c: 0)]
    out = pl.pallas_call(
        functools.partial(_attention_kernel, n_chunks=n_chunks),
        grid=(batch, n_chunks),
        in_specs=[pl.BlockSpec((None, CHUNK, ATT_WIDTH), lambda b, c: (b, c, 0))] + kv_specs + kv_specs
                 + [_resident((N_ATT_HEADS, LANES)), _resident((1, ATT_WIDTH))],
        out_specs=pl.BlockSpec((None, CHUNK, ATT_WIDTH), lambda b, c: (b, c, 0)),
        out_shape=jax.ShapeDtypeStruct((batch, seq_rows, ATT_WIDTH), BF16),
        compiler_params=pltpu.CompilerParams(dimension_semantics=("parallel", "parallel"),
                                             vmem_limit_bytes=32 * MIB),
        name="attention",
    )(q3, k3, k3, k3, k3, v3, v3, v3, v3, sink, gain)
    return out.reshape(batch * seq_rows, ATT_WIDTH)


def _outproj_kernel(ret_ref, att_ref, w_ref, h_ref, o_ref):
    acc = jnp.dot(ret_ref[...], w_ref[0:RET_WIDTH, :], preferred_element_type=F32)
    acc = acc + jnp.dot(att_ref[...], w_ref[RET_WIDTH:, :], preferred_element_type=F32)
    o_ref[...] = h_ref[...] + acc


def _outproj(ret, att, w, h):
    rows = h.shape[0]
    tm = _row_tile(rows, (1024, 640, 512, 384, 256, 128))
    row_spec = lambda width: pl.BlockSpec((tm, width), lambda i: (i, 0))
    return pl.pallas_call(
        _outproj_kernel,
        grid=(rows // tm,),
        in_specs=[row_spec(RET_WIDTH), row_spec(ATT_WIDTH), _resident((RET_WIDTH + ATT_WIDTH, D_MODEL)),
                  row_spec(D_MODEL)],
        out_specs=row_spec(D_MODEL),
        out_shape=jax.ShapeDtypeStruct((rows, D_MODEL), F32),
        input_output_aliases={3: 0},
        compiler_params=pltpu.CompilerParams(dimension_semantics=("parallel",), vmem_limit_bytes=48 * MIB),
        name="outproj",
    )(ret, att, w, h)


def _ffn_kernel(h_ref, gain_ref, wg_ref, wu_ref, wd_ref, o_ref, *, ff_chunk):
    x = h_ref[...]
    f = _rms(x, gain_ref[...]).astype(BF16)
    acc = x
    for c0 in range(0, wg_ref.shape[1], ff_chunk):
        gate = jnp.dot(f, wg_ref[:, c0:c0 + ff_chunk], preferred_element_type=F32)
        up = jnp.dot(f, wu_ref[:, c0:c0 + ff_chunk], preferred_element_type=F32)
        act = (_silu(gate) * up).astype(BF16)
        acc = acc + jnp.dot(act, wd_ref[c0:c0 + ff_chunk, :], preferred_element_type=F32)
    o_ref[...] = acc


def _ffn(h, gain, wg, wu, wd):
    rows = h.shape[0]
    d_ff = wg.shape[1]
    tm = _row_tile(rows, (640, 512, 384, 256, 128))
    row_spec = pl.BlockSpec((tm, D_MODEL), lambda i: (i, 0))
    return pl.pallas_call(
        functools.partial(_ffn_kernel, ff_chunk=d_ff // 2),
        grid=(rows // tm,),
        in_specs=[row_spec, _resident((1, D_MODEL)), _resident((D_MODEL, d_ff)), _resident((D_MODEL, d_ff)),
                  _resident((d_ff, D_MODEL))],
        out_specs=row_spec,
        out_shape=jax.ShapeDtypeStruct((rows, D_MODEL), F32),
        input_output_aliases={0: 0},
        compiler_params=pltpu.CompilerParams(dimension_semantics=("parallel",), vmem_limit_bytes=56 * MIB),
        name="ffn",
    )(h, gain, wg, wu, wd)


def _router_kernel(h_ref, gain_ref, whi_ref, wlo_ref, f_ref, comb_ref):
    f = _rms(h_ref[...], gain_ref[...])
    f_hi = f.astype(BF16)
    f_lo = (f - f_hi.astype(F32)).astype(BF16)
    logits = (jnp.dot(f_hi, whi_ref[...], preferred_element_type=F32)
              + jnp.dot(f_lo, whi_ref[...], preferred_element_type=F32)
              + jnp.dot(f_hi, wlo_ref[...], preferred_element_type=F32))
    lane = lax.broadcasted_iota(jnp.int32, logits.shape, 1)
    lg = jnp.where(lane < N_EXPERTS, logits, -jnp.inf)
    m1 = jnp.max(lg, axis=-1, keepdims=True)
    i1 = jnp.min(jnp.where(lg == m1, lane, LANES), axis=-1, keepdims=True)
    lg2 = jnp.where(lane == i1, -jnp.inf, lg)
    m2 = jnp.max(lg2, axis=-1, keepdims=True)
    i2 = jnp.min(jnp.where(lg2 == m2, lane, LANES), axis=-1, keepdims=True)
    e2 = jnp.exp(m2 - m1)
    g1 = 1.0 / (1.0 + e2)
    comb_ref[...] = jnp.where(lane == i1, g1, 0.0) + jnp.where(lane == i2, e2 * g1, 0.0)
    f_ref[...] = f_hi


def _router(h, gain, w_hi, w_lo):
    rows = h.shape[0]
    tm = _row_tile(rows, (1024, 640, 512, 384, 256, 128))
    row_spec = lambda width: pl.BlockSpec((tm, width), lambda i: (i, 0))
    return pl.pallas_call(
        _router_kernel,
        grid=(rows // tm,),
        in_specs=[row_spec(D_MODEL), _resident((1, D_MODEL)), _resident((D_MODEL, LANES)),
                  _resident((D_MODEL, LANES))],
        out_specs=[row_spec(D_MODEL), row_spec(LANES)],
        out_shape=[jax.ShapeDtypeStruct((rows, D_MODEL), BF16), jax.ShapeDtypeStruct((rows, LANES), F32)],
        compiler_params=pltpu.CompilerParams(dimension_semantics=("parallel",), vmem_limit_bytes=48 * MIB),
        name="router",
    )(h, gain, w_hi, w_lo)


def _moe_kernel(f_ref, comb_ref, h_ref, wg_ref, wu_ref, wd_ref, o_ref):
    e = pl.program_id(1)

    @pl.when(jnp.logical_and(e == 0, pl.program_id(2) == 0))
    def _():
        o_ref[...] = h_ref[...]

    f = f_ref[...]
    lane = lax.broadcasted_iota(jnp.int32, comb_ref.shape, 1)
    weight = jnp.sum(jnp.where(lane == e, comb_ref[...], 0.0), axis=-1, keepdims=True)
    gate = jnp.dot(f, wg_ref[...], preferred_element_type=F32)
    up = jnp.dot(f, wu_ref[...], preferred_element_type=F32)
    act = (_silu(gate) * up * weight).astype(BF16)
    o_ref[...] += jnp.dot(act, wd_ref[...], preferred_element_type=F32)


def _moe(f, comb, h, wg, wu, wd):
    rows = h.shape[0]
    d_ff = wg.shape[2]
    tm = _row_tile(rows, (1024, 640, 512, 384, 256, 128))
    ff_chunk = d_ff // 4
    row_spec = lambda width: pl.BlockSpec((tm, width), lambda i, e, c: (i, 0))
    return pl.pallas_call(
        _moe_kernel,
        grid=(rows // tm, N_EXPERTS, d_ff // ff_chunk),
        in_specs=[row_spec(D_MODEL), row_spec(LANES), row_spec(D_MODEL),
                  pl.BlockSpec((None, D_MODEL, ff_chunk), lambda i, e, c: (e, 0, c)),
                  pl.BlockSpec((None, D_MODEL, ff_chunk), lambda i, e, c: (e, 0, c)),
                  pl.BlockSpec((None, ff_chunk, D_MODEL), lambda i, e, c: (e, c, 0))],
        out_specs=row_spec(D_MODEL),
        out_shape=jax.ShapeDtypeStruct((rows, D_MODEL), F32),
        input_output_aliases={2: 0},
        compiler_params=pltpu.CompilerParams(dimension_semantics=("parallel", "arbitrary", "arbitrary"),
                                             vmem_limit_bytes=56 * MIB),
        name="moe",
    )(f, comb, h, wg, wu, wd)


def _rotary_tables(batch, seq_rows):
    pos = (jnp.arange(seq_rows, dtype=jnp.int32) - PAD_FRONT).astype(F32)

    def cos_sin(n_rot, theta):
        half = n_rot // 2
        inv = theta ** (-jnp.arange(half, dtype=F32) * 2.0 / n_rot)
        ang = pos[:, None] * inv[None, :]
        return jnp.cos(ang), jnp.sin(ang)

    cos, sin = cos_sin(HEAD_DIM, RET_ROPE_THETA)
    cr = jnp.tile(cos, (1, 2 * LANES // HEAD_DIM))
    sr = jnp.tile(jnp.concatenate([-sin, sin], axis=1), (1, LANES // HEAD_DIM))
    cos, sin = cos_sin(ROPE_DIMS, ROPE_THETA)
    rest = HEAD_DIM - ROPE_DIMS
    ca = jnp.tile(jnp.concatenate([cos, cos, jnp.ones((seq_rows, rest), F32)], axis=1), (1, LANES // HEAD_DIM))
    sa = jnp.tile(jnp.concatenate([-sin, sin, jnp.zeros((seq_rows, rest), F32)], axis=1), (1, LANES // HEAD_DIM))
    return tuple(jnp.tile(t, (batch, 1)) for t in (cr, sr, ca, sa))


def _retention_params(log_gf, log_gb, gain):
    n_pairs = RET_WIDTH // LANES
    per_lane = lambda t: jnp.repeat(t.astype(F32).reshape(n_pairs, 2), HEAD_DIM, axis=1)
    per_head = lambda t, j: jnp.broadcast_to(t.astype(F32).reshape(n_pairs, 2)[:, j:j + 1], (n_pairs, LANES))
    rows = [per_lane(log_gf), per_lane(log_gb), per_head(log_gf, 0), per_head(log_gf, 1),
            per_head(log_gb, 0), per_head(log_gb, 1), gain.astype(F32).reshape(n_pairs, LANES),
            jnp.zeros((n_pairs, LANES), F32)]
    return jnp.stack(rows, axis=1)


def _trunk(x, meta_tokens, layers):
    batch, seq, _ = x.shape
    seq_rows = seq + CHUNK
    h = jnp.concatenate([jnp.zeros((batch, PAD_FRONT, D_MODEL), F32),
                         jnp.broadcast_to(meta_tokens.astype(F32)[None], (batch, N_META, D_MODEL)),
                         x.astype(F32)], axis=1).reshape(batch * seq_rows, D_MODEL)
    tabs = _rotary_tables(batch, seq_rows)
    for lp in layers:
        rq, rk, rv, rg, aq, ak, av = _inproj(h, lp["norm_mix"], lp["w_in"], tabs, lp["q_gain"], lp["k_gain"],
                                             lp["ones_bd"])
        ret = _retention(rq, rk, rv, rg, lp["ret_dec"], batch, seq_rows)
        att = _attention(aq, ak, av, lp["sink"], lp["att_gain"], batch, seq_rows)
        h = _outproj(ret, att, lp["w_out"], h)
        if "ffn" in lp:
            h = _ffn(h, lp["norm_ffn"], *lp["ffn"])
        else:
            f, comb = _router(h, lp["norm_ffn"], *lp["router"])
            h = _moe(f, comb, h, *lp["moe"])
    return h.reshape(batch, seq_rows, D_MODEL)[:, CHUNK:].astype(x.dtype)


def kernel(x_prompt, x_sample, meta_tokens, norm_mix, w_in, ret_log_decay_fwd, ret_log_decay_bwd, ret_out_gain,
           q_norm_gain, k_norm_gain, attn_sink, attn_out_gain, w_out, norm_ffn, ffn_w_gate, ffn_w_up, ffn_w_down,
           moe_router, moe_w_gate, moe_w_up, moe_w_down):
    depth = w_in.shape[0]
    ri = lax.broadcasted_iota(jnp.int32, (LANES, LANES), 0)
    ci = lax.broadcasted_iota(jnp.int32, (LANES, LANES), 1)
    ones_bd = ((ri < HEAD_DIM) == (ci < HEAD_DIM)).astype(BF16)
    row = lambda t: t.astype(F32).reshape(1, -1)
    layers = []
    for l in range(depth):
        lp = {
            "norm_mix": row(norm_mix[l]),
            "w_in": w_in[l].astype(BF16),
            "q_gain": jnp.tile(row(q_norm_gain[l]), (1, LANES // HEAD_DIM)),
            "k_gain": jnp.tile(row(k_norm_gain[l]), (1, LANES // HEAD_DIM)),
            "ones_bd": ones_bd,
            "ret_dec": _retention_params(ret_log_decay_fwd[l], ret_log_decay_bwd[l], ret_out_gain[l]),
            "sink": jnp.broadcast_to(attn_sink[l].astype(F32)[:, None], (N_ATT_HEADS, LANES)),
            "att_gain": row(attn_out_gain[l]),
            "w_out": w_out[l].astype(BF16),
            "norm_ffn": row(norm_ffn[l]),
        }
        i = l // 2
        if l % 2 == 0:
            lp["ffn"] = (ffn_w_gate[i].astype(BF16), ffn_w_up[i].astype(BF16), ffn_w_down[i].astype(BF16))
        else:
            wr = jnp.pad(moe_router[i].astype(F32), ((0, 0), (0, LANES - N_EXPERTS)))
            wr_hi = wr.astype(BF16)
            lp["router"] = (wr_hi, (wr - wr_hi.astype(F32)).astype(BF16))
            lp["moe"] = (moe_w_gate[i].astype(BF16), moe_w_up[i].astype(BF16), moe_w_down[i].astype(BF16))
        layers.append(lp)
    return (_trunk(x_prompt, meta_tokens, layers), _trunk(x_sample, meta_tokens, layers))
```

```python
import functools

import jax
import jax.numpy as jnp
from jax import lax
from jax.experimental import pallas as pl
from jax.experimental.pallas import tpu as pltpu

F32 = jnp.float32
BF16 = jnp.bfloat16
I32 = jnp.int32

D_MODEL = 1024
HEAD_DIM = 64
N_RET_HEADS = 8
N_ATT_HEADS = 8
N_KV_HEADS = 2
RET_WIDTH = N_RET_HEADS * HEAD_DIM
ATT_WIDTH = N_ATT_HEADS * HEAD_DIM
KV_WIDTH = N_KV_HEADS * HEAD_DIM
IN_WIDTH = 4 * RET_WIDTH + ATT_WIDTH + 2 * KV_WIDTH
CHUNK = 128
LANES = 128
SUBLANES = 8
N_META = 16
PAD_FRONT = CHUNK - N_META
ROPE_THETA = 500000.0
ROPE_DIMS = HEAD_DIM // 4
RET_ROPE_THETA = 10000.0
N_EXPERTS = 8
TOP_K = 2
EXPERT_ROW_TILE = 512
EPS = 1e-6
NEG = -1e30
QK_SCALE = HEAD_DIM ** -0.5
MIB = 1024 * 1024


def _row_tile(rows, prefs):
    for t in prefs:
        if rows % t == 0:
            return t
    raise ValueError(f"no row tile for {rows} rows among {prefs}")


def _resident(shape):
    return pl.BlockSpec(shape, lambda *_: (0,) * len(shape), pipeline_mode=pl.Buffered(1))


def _rms(x, gain):
    ms = jnp.mean(x * x, axis=-1, keepdims=True)
    return x * lax.rsqrt(ms + EPS) * gain


def _silu(x):
    return x * jax.nn.sigmoid(x)


def _inproj_kernel(h_ref, gain_ref, w_ref, cr_ref, sr_ref, ca_ref, sa_ref, qg_ref, kg_ref, ones_ref,
                   rq_ref, rk_ref, rv_ref, rg_ref, aq_ref, ak_ref, av_ref):
    a = _rms(h_ref[...], gain_ref[...]).astype(BF16)
    tm = a.shape[0]
    lane = lax.broadcasted_iota(I32, (tm, LANES), 1)
    in_head = lane & (HEAD_DIM - 1)
    low_head = lane < HEAD_DIM

    def proj(c0, c1):
        return jnp.dot(a, w_ref[:, c0:c1], preferred_element_type=F32)

    def rotate(x, cos, sin_signed, half):
        partner = jnp.where(in_head < half, pltpu.roll(x, LANES - half, 1), pltpu.roll(x, half, 1))
        return x * cos + partner * sin_signed

    def head_norm(x, g):
        ss = jnp.dot((x * x).astype(BF16), ones_ref[...], preferred_element_type=F32)
        return x * lax.rsqrt(ss * (1.0 / HEAD_DIM) + EPS) * g

    cr, sr, ca, sa = cr_ref[...], sr_ref[...], ca_ref[...], sa_ref[...]
    cols = lambda j: slice(j * LANES, (j + 1) * LANES)
    n_lane_tiles = RET_WIDTH // LANES

    p = proj(0, RET_WIDTH)
    for j in range(n_lane_tiles):
        rq_ref[:, cols(j)] = rotate(p[:, cols(j)], cr, sr, HEAD_DIM // 2).astype(BF16)
    p = proj(RET_WIDTH, 2 * RET_WIDTH)
    for j in range(n_lane_tiles):
        rk_ref[:, cols(j)] = (rotate(p[:, cols(j)], cr, sr, HEAD_DIM // 2) * QK_SCALE).astype(BF16)
    rv_ref[...] = proj(2 * RET_WIDTH, 3 * RET_WIDTH).astype(BF16)
    rg_ref[...] = proj(3 * RET_WIDTH, 4 * RET_WIDTH).astype(BF16)

    base = 4 * RET_WIDTH
    p = proj(base, base + ATT_WIDTH)
    for j in range(ATT_WIDTH // LANES):
        n = head_norm(p[:, cols(j)], qg_ref[...])
        aq_ref[:, cols(j)] = (rotate(n, ca, sa, ROPE_DIMS // 2) * QK_SCALE).astype(BF16)

    p = proj(base + ATT_WIDTH, IN_WIDTH)
    k = rotate(head_norm(p[:, :KV_WIDTH], kg_ref[...]), ca, sa, ROPE_DIMS // 2)
    v = p[:, KV_WIDTH:]
    k_sw = pltpu.roll(k, HEAD_DIM, 1)
    v_sw = pltpu.roll(v, HEAD_DIM, 1)
    ak_ref[:, 0:LANES] = jnp.where(low_head, k, k_sw).astype(BF16)
    ak_ref[:, LANES:2 * LANES] = jnp.where(low_head, k_sw, k).astype(BF16)
    av_ref[:, 0:LANES] = jnp.where(low_head, v, v_sw).astype(BF16)
    av_ref[:, LANES:2 * LANES] = jnp.where(low_head, v_sw, v).astype(BF16)


def _inproj(h, gain, w, tabs, qg, kg, ones_bd):
    rows = h.shape[0]
    tm = _row_tile(rows, (1024, 640, 512, 384, 256, 128))
    row_spec = lambda width: pl.BlockSpec((tm, width), lambda i: (i, 0))
    out_widths = (RET_WIDTH,) * 4 + (ATT_WIDTH, 2 * KV_WIDTH, 2 * KV_WIDTH)
    return pl.pallas_call(
        _inproj_kernel,
        grid=(rows // tm,),
        in_specs=[row_spec(D_MODEL), _resident((1, D_MODEL)), _resident((D_MODEL, IN_WIDTH)),
                  row_spec(LANES), row_spec(LANES), row_spec(LANES), row_spec(LANES),
                  _resident((1, LANES)), _resident((1, LANES)), _resident((LANES, LANES))],
        out_specs=[row_spec(wd) for wd in out_widths],
        out_shape=[jax.ShapeDtypeStruct((rows, wd), BF16) for wd in out_widths],
        compiler_params=pltpu.CompilerParams(dimension_semantics=("parallel",), vmem_limit_bytes=56 * MIB),
        name="inproj",
    )(h, gain, w, *tabs, qg, kg, ones_bd)


def _retention_kernel(q_ref, k_ref, v_ref, g_ref, dec_ref, o_ref, sb_ref, *, n_chunks):
    lgf, lgb = dec_ref[0:1, :], dec_ref[1:2, :]
    gain = dec_ref[6:7, :]
    ri = lax.broadcasted_iota(I32, (CHUNK, LANES), 0)
    ci = lax.broadcasted_iota(I32, (CHUNK, LANES), 1)
    r = ri.astype(F32)
    diff = (ri - ci).astype(F32)

    def decay_mask(lf, lb):
        return jnp.where(diff >= 0, jnp.exp(jnp.maximum(diff, 0.0) * lf), jnp.exp(jnp.maximum(-diff, 0.0) * lb))

    dm = jnp.concatenate([decay_mask(dec_ref[2:3, :], dec_ref[4:5, :]),
                          decay_mask(dec_ref[3:4, :], dec_ref[5:6, :])], axis=1)
    wf = jnp.exp((CHUNK - 1.0 - r) * lgf)
    wb = jnp.exp(r * lgb)
    qf = jnp.exp((r + 1.0) * lgf)
    qb = jnp.exp((CHUNK - r) * lgb)
    cf = jnp.exp(float(CHUNK) * lgf)
    cb = jnp.exp(float(CHUNK) * lgb)
    low_c = ci < HEAD_DIM
    same_head = (ri < HEAD_DIM) == low_c
    ones_bd = jnp.where(same_head, 1.0, 0.0).astype(BF16)
    m0 = jnp.where(low_c, 1.0, 0.0).astype(BF16)
    m1 = jnp.where(low_c, 0.0, 1.0).astype(BF16)

    def chunk(ref, n):
        return ref[pl.ds(pl.multiple_of(n * CHUNK, CHUNK), CHUNK), :]

    def pair_rows(x):
        return jnp.concatenate([x * m0, x * m1], axis=0)

    def state_update(state, k_weighted, v, carry_decay):
        u = jnp.dot(k_weighted.T.astype(BF16), v, preferred_element_type=F32)
        return carry_decay * state + jnp.where(same_head, u, 0.0)

    def backward(i, state):
        n = n_chunks - 1 - i
        sb_ref[n] = state
        return state_update(state, chunk(k_ref, n).astype(F32) * wb, chunk(v_ref, n), cb)

    zero_state = jnp.zeros((LANES, LANES), F32)
    lax.fori_loop(0, n_chunks, backward, zero_state)

    def forward(n, state):
        q, k, v = chunk(q_ref, n), chunk(k_ref, n), chunk(v_ref, n)
        s = lax.dot_general(q, pair_rows(k), (((1,), (1,)), ((), ())), preferred_element_type=F32) * dm
        o = jnp.dot(s.astype(BF16), pair_rows(v), preferred_element_type=F32)
        o = o + jnp.dot(q, state.astype(BF16), preferred_element_type=F32) * qf
        o = o + jnp.dot(q, sb_ref[n].astype(BF16), preferred_element_type=F32) * qb
        ss = jnp.dot((o * o).astype(BF16), ones_bd, preferred_element_type=F32)
        o = o * lax.rsqrt(ss * (1.0 / HEAD_DIM) + EPS) * gain * _silu(chunk(g_ref, n).astype(F32))
        o_ref[pl.ds(pl.multiple_of(n * CHUNK, CHUNK), CHUNK), :] = o.astype(BF16)
        return state_update(state, k.astype(F32) * wf, v, cf)

    lax.fori_loop(0, n_chunks, forward, zero_state)


def _retention(rq, rk, rv, rg, dec, batch, seq_rows):
    n_chunks = seq_rows // CHUNK
    n_pairs = RET_WIDTH // LANES
    view = lambda t: t.reshape(batch, seq_rows, RET_WIDTH)
    seq_spec = pl.BlockSpec((None, seq_rows, LANES), lambda b, p: (b, 0, p))
    out = pl.pallas_call(
        functools.partial(_retention_kernel, n_chunks=n_chunks),
        grid=(batch, n_pairs),
        in_specs=[seq_spec, seq_spec, seq_spec, seq_spec,
                  pl.BlockSpec((None, 8, LANES), lambda b, p: (p, 0, 0))],
        out_specs=seq_spec,
        out_shape=jax.ShapeDtypeStruct((batch, seq_rows, RET_WIDTH), BF16),
        scratch_shapes=[pltpu.VMEM((n_chunks, LANES, LANES), F32)],
        compiler_params=pltpu.CompilerParams(dimension_semantics=("parallel", "parallel"),
                                             vmem_limit_bytes=48 * MIB),
        name="retention",
    )(view(rq), view(rk), view(rv), view(rg), dec)
    return out.reshape(batch * seq_rows, RET_WIDTH)


def _attention_kernel(q_ref, kp_ref, kc_ref, kn_ref, km_ref, vp_ref, vc_ref, vn_ref, vm_ref,
                      sink_ref, gain_ref, o_ref, *, n_chunks):
    c = pl.program_id(1)
    ri = lax.broadcasted_iota(I32, (CHUNK, LANES), 0)
    ci = lax.broadcasted_iota(I32, (CHUNK, LANES), 1)
    never = 2 * CHUNK
    prev_off = jnp.where(c >= 2, 0, never)
    cur_off = jnp.where(c >= 1, 0, never)
    next_off = jnp.where(c + 1 <= n_chunks - 1, 0, never)
    to_bias = lambda ok: jnp.where(ok, 0.0, NEG)
    bias = jnp.concatenate([to_bias(ci >= ri + prev_off), to_bias(ci >= cur_off),
                            to_bias(ci + next_off <= ri), to_bias(ci >= PAD_FRONT)], axis=1)
    n_keys = 4 * CHUNK
    low_c = ci < HEAD_DIM
    m0 = jnp.where(low_c, 1.0, 0.0).astype(BF16)
    m1 = jnp.where(low_c, 0.0, 1.0).astype(BF16)
    m0k = jnp.concatenate([m0] * 4, axis=0)
    m1k = jnp.concatenate([m1] * 4, axis=0)

    outs = []
    for g in range(N_KV_HEADS):
        ksl = slice(g * LANES, (g + 1) * LANES)
        kd = jnp.concatenate([kp_ref[:, ksl], kc_ref[:, ksl], kn_ref[:, ksl], km_ref[:, ksl]], axis=0)
        vd = jnp.concatenate([vp_ref[:, ksl], vc_ref[:, ksl], vn_ref[:, ksl], vm_ref[:, ksl]], axis=0)
        k_pair = jnp.concatenate([kd * m0k, kd * m1k], axis=0)
        v_pair = jnp.concatenate([vd * m0k, vd * m1k], axis=0)
        pairs_per_kv = N_ATT_HEADS // N_KV_HEADS // 2
        for pp in range(pairs_per_kv):
            p = g * pairs_per_kv + pp
            q2 = q_ref[:, p * LANES:(p + 1) * LANES]
            s = lax.dot_general(q2, k_pair, (((1,), (1,)), ((), ())), preferred_element_type=F32)
            probs, inv = [], []
            for hh in range(2):
                sh = s[:, hh * n_keys:(hh + 1) * n_keys] + bias
                sk = sink_ref[2 * p + hh:2 * p + hh + 1, 0:1]
                m = jnp.maximum(jnp.max(sh, axis=-1, keepdims=True), sk)
                e = jnp.exp(sh - m)
                inv.append(1.0 / (jnp.sum(e, axis=-1, keepdims=True) + jnp.exp(sk - m)))
                probs.append(e.astype(BF16))
            o = jnp.dot(jnp.concatenate(probs, axis=1), v_pair, preferred_element_type=F32)
            outs.append(o * jnp.where(low_c, inv[0], inv[1]))
    att = jnp.concatenate(outs, axis=1)
    row = lax.broadcasted_iota(I32, att.shape, 0)
    att = jnp.where(row >= jnp.where(c == 0, PAD_FRONT, 0), att, 0.0)
    o_ref[...] = _rms(att, gain_ref[...]).astype(BF16)


def _attention(aq, ak, av, sink, gain, batch, seq_rows):
    n_chunks = seq_rows // CHUNK
    q3 = aq.reshape(batch, seq_rows, ATT_WIDTH)
    k3 = ak.reshape(batch, seq_rows, 2 * KV_WIDTH)
    v3 = av.reshape(batch, seq_rows, 2 * KV_WIDTH)
    kv = lambda fn: pl.BlockSpec((None, CHUNK, 2 * KV_WIDTH), lambda b, c: (b, fn(c), 0))
    kv_specs = [kv(lambda c: jnp.maximum(c - 1, 0)), kv(lambda c: c),
                kv(lambda c: jnp.minimum(c + 1, n_chunks - 1)), kv(lambda c: 0)]
    out = pl.pallas_call(
        functools.partial(_attention_kernel, n_chunks=n_chunks),
        grid=(batch, n_chunks),
        in_specs=[pl.BlockSpec((None, CHUNK, ATT_WIDTH), lambda b, c: (b, c, 0))] + kv_specs + kv_specs
                 + [_resident((N_ATT_HEADS, LANES)), _resident((1, ATT_WIDTH))],
        out_specs=pl.BlockSpec((None, CHUNK, ATT_WIDTH), lambda b, c: (b, c, 0)),
        out_shape=jax.ShapeDtypeStruct((batch, seq_rows, ATT_WIDTH), BF16),
        compiler_params=pltpu.CompilerParams(dimension_semantics=("parallel", "parallel"),
                                             vmem_limit_bytes=32 * MIB),
        name="attention",
    )(q3, k3, k3, k3, k3, v3, v3, v3, v3, sink, gain)
    return out.reshape(batch * seq_rows, ATT_WIDTH)


def _outproj_kernel(ret_ref, att_ref, w_ref, h_ref, o_ref):
    acc = jnp.dot(ret_ref[...], w_ref[0:RET_WIDTH, :], preferred_element_type=F32)
    acc = acc + jnp.dot(att_ref[...], w_ref[RET_WIDTH:, :], preferred_element_type=F32)
    o_ref[...] = h_ref[...] + acc


def _outproj(ret, att, w, h):
    rows = h.shape[0]
    tm = _row_tile(rows, (1024, 640, 512, 384, 256, 128))
    row_spec = lambda width: pl.BlockSpec((tm, width), lambda i: (i, 0))
    return pl.pallas_call(
        _outproj_kernel,
        grid=(rows // tm,),
        in_specs=[row_spec(RET_WIDTH), row_spec(ATT_WIDTH), _resident((RET_WIDTH + ATT_WIDTH, D_MODEL)),
                  row_spec(D_MODEL)],
        out_specs=row_spec(D_MODEL),
        out_shape=jax.ShapeDtypeStruct((rows, D_MODEL), F32),
        input_output_aliases={3: 0},
        compiler_params=pltpu.CompilerParams(dimension_semantics=("parallel",), vmem_limit_bytes=48 * MIB),
        name="outproj",
    )(ret, att, w, h)


def _ffn_kernel(h_ref, gain_ref, wg_ref, wu_ref, wd_ref, o_ref, *, ff_chunk):
    x = h_ref[...]
    f = _rms(x, gain_ref[...]).astype(BF16)
    acc = x
    for c0 in range(0, wg_ref.shape[1], ff_chunk):
        gate = jnp.dot(f, wg_ref[:, c0:c0 + ff_chunk], preferred_element_type=F32)
        up = jnp.dot(f, wu_ref[:, c0:c0 + ff_chunk], preferred_element_type=F32)
        act = (_silu(gate) * up).astype(BF16)
        acc = acc + jnp.dot(act, wd_ref[c0:c0 + ff_chunk, :], preferred_element_type=F32)
    o_ref[...] = acc


def _ffn(h, gain, wg, wu, wd):
    rows = h.shape[0]
    d_ff = wg.shape[1]
    tm = _row_tile(rows, (640, 512, 384, 256, 128))
    row_spec = pl.BlockSpec((tm, D_MODEL), lambda i: (i, 0))
    return pl.pallas_call(
        functools.partial(_ffn_kernel, ff_chunk=d_ff // 2),
        grid=(rows // tm,),
        in_specs=[row_spec, _resident((1, D_MODEL)), _resident((D_MODEL, d_ff)), _resident((D_MODEL, d_ff)),
                  _resident((d_ff, D_MODEL))],
        out_specs=row_spec,
        out_shape=jax.ShapeDtypeStruct((rows, D_MODEL), F32),
        input_output_aliases={0: 0},
        compiler_params=pltpu.CompilerParams(dimension_semantics=("parallel",), vmem_limit_bytes=56 * MIB),
        name="ffn",
    )(h, gain, wg, wu, wd)


ROUTE_G1, ROUTE_G2, ROUTE_D1, ROUTE_D2 = range(4)
TILE_BASE, TILE_COUNT, TILE_START = range(3)


def _lane_pick(x, lane, j):
    return jnp.sum(jnp.where(lane == j, x, 0.0), axis=-1, keepdims=True)


def _lane_scalar(vec, lane_row, e):
    return jnp.sum(jnp.where(lane_row == e, vec, 0.0)).astype(I32)


RUN_ALIGN_BITS = 3
RUN_ALIGN = 1 << RUN_ALIGN_BITS


def _run_piece_count(tm):
    return tm.bit_length() - RUN_ALIGN_BITS


def _run_copies(count, src0, dst0, src_ref, dst_ref, sems):
    copies = []
    for j in range(sems.shape[0]):
        k = j + RUN_ALIGN_BITS
        size = 1 << k
        start = (count >> (k + 1)) << (k + 1)
        src = src_ref.at[pl.ds(pl.multiple_of(src0 + start, RUN_ALIGN), size)]
        dst = dst_ref.at[pl.ds(pl.multiple_of(dst0 + start, RUN_ALIGN), size)]
        copies.append((((count >> k) & 1) == 1, pltpu.make_async_copy(src, dst, sems.at[j])))
    return copies


def _start_then_wait(copies):
    for cond, cp in copies:
        pl.when(cond)(cp.start)
    for cond, cp in copies:
        pl.when(cond)(cp.wait)


def _one_hot_rows(dest, width):
    col = lax.broadcasted_iota(I32, (dest.shape[0], width), 1)
    return jnp.where(col == dest.astype(I32), 1.0, 0.0)


def _router_kernel(*refs, region_rows, finalize, aliased):
    h_ref, gain_ref, whi_ref, wlo_ref, base_ref = refs[:5]
    refs = refs[5 + (1 if aliased else 0):]
    xs_ref, route_ref, tile_ref, cnt_ref, sorted_ref, zero_ref, sems, zsem = refs
    i = pl.program_id(0)

    @pl.when(i == 0)
    def _():
        cnt_ref[...] = base_ref[...]

    f = _rms(h_ref[...], gain_ref[...])
    f_hi = f.astype(BF16)
    f_lo = (f - f_hi.astype(F32)).astype(BF16)
    logits = (jnp.dot(f_hi, whi_ref[...], preferred_element_type=F32)
              + jnp.dot(f_lo, whi_ref[...], preferred_element_type=F32)
              + jnp.dot(f_hi, wlo_ref[...], preferred_element_type=F32))
    tm = logits.shape[0]
    lane = lax.broadcasted_iota(I32, logits.shape, 1)
    lg = jnp.where(lane < N_EXPERTS, logits, -jnp.inf)
    m1 = jnp.max(lg, axis=-1, keepdims=True)
    i1 = jnp.min(jnp.where(lg == m1, lane, LANES), axis=-1, keepdims=True)
    lg2 = jnp.where(lane == i1, -jnp.inf, lg)
    m2 = jnp.max(lg2, axis=-1, keepdims=True)
    i2 = jnp.min(jnp.where(lg2 == m2, lane, LANES), axis=-1, keepdims=True)
    e2 = jnp.exp(m2 - m1)
    g1 = 1.0 / (1.0 + e2)
    g2 = e2 * g1

    sel = jnp.where(lane == i1, 1.0, 0.0) + jnp.where(lane == i2, 1.0, 0.0)
    sel_b = sel.astype(BF16)
    ri = lax.broadcasted_iota(I32, (tm, tm), 0)
    ci = lax.broadcasted_iota(I32, (tm, tm), 1)
    earlier_tokens = jnp.where(ri > ci, 1.0, 0.0).astype(BF16)
    rank = jnp.dot(earlier_tokens, sel_b, preferred_element_type=F32)
    li = lax.broadcasted_iota(I32, (LANES, LANES), 0)
    lj = lax.broadcasted_iota(I32, (LANES, LANES), 1)
    earlier_experts = jnp.where(li < lj, 1.0, 0.0).astype(BF16)
    tiles_per_run = jnp.floor((jnp.sum(sel, axis=0, keepdims=True) + (RUN_ALIGN - 1.0)) * (1.0 / RUN_ALIGN))
    count = tiles_per_run * RUN_ALIGN
    start = RUN_ALIGN * jnp.dot(jnp.broadcast_to(tiles_per_run, (SUBLANES, LANES)).astype(BF16), earlier_experts,
                                preferred_element_type=F32)[0:1, :]
    dest = start + rank
    d1 = _lane_pick(dest, lane, i1)
    d2 = _lane_pick(dest, lane, i2)
    route = jnp.zeros(logits.shape, F32)
    for j, val in ((ROUTE_G1, g1), (ROUTE_G2, g2), (ROUTE_D1, d1), (ROUTE_D2, d2)):
        route = jnp.where(lane == j, val, route)
    route_ref[...] = route

    n_sorted = sorted_ref.shape[0]
    onehot = _one_hot_rows(d1, n_sorted) + _one_hot_rows(d2, n_sorted)
    sorted_ref[...] = jnp.dot(onehot.T.astype(BF16), f_hi, preferred_element_type=F32)

    base = cnt_ref[0:1, :]
    sub = lax.broadcasted_iota(I32, (SUBLANES, LANES), 0)
    tile_ref[...] = jnp.where(sub == TILE_BASE, base, jnp.where(sub == TILE_COUNT, count,
                                                                jnp.where(sub == TILE_START, start, 0.0)))
    lane_row = lane[0:1, :]
    copies = []
    for e in range(N_EXPERTS):
        n_e = _lane_scalar(count, lane_row, e)
        src0 = _lane_scalar(start, lane_row, e)
        dst0 = e * region_rows + _lane_scalar(base, lane_row, e)
        copies += _run_copies(n_e, src0, dst0, sorted_ref, xs_ref, sems.at[e])
    _start_then_wait(copies)
    cnt_ref[...] = cnt_ref[...] + count

    if finalize:
        @pl.when(i == pl.num_programs(0) - 1)
        def _():
            zero_ref[...] = jnp.zeros(zero_ref.shape, F32)
            total = cnt_ref[0:1, :]
            for e in range(N_EXPERTS):
                end = pl.multiple_of(e * region_rows + _lane_scalar(total, lane_row, e), RUN_ALIGN)
                cp = pltpu.make_async_copy(zero_ref, xs_ref.at[pl.ds(end, EXPERT_ROW_TILE)], zsem)
                cp.start()
                cp.wait()


def _router_tile(rows):
    return _row_tile(rows, (512, 640, 384, 256, 128))


def _sorted_rows(tm):
    return TOP_K * tm + pl.cdiv(N_EXPERTS * (RUN_ALIGN - 1), LANES) * LANES


def _router(h, gain, w_hi, w_lo, base_counts, xs, region_rows, finalize):
    rows = h.shape[0]
    tm = _router_tile(rows)
    n_tiles = rows // tm
    aliased = xs is not None
    row_spec = lambda width: pl.BlockSpec((tm, width), lambda i: (i, 0))
    in_specs = [row_spec(D_MODEL), _resident((1, D_MODEL)), _resident((D_MODEL, LANES)),
                _resident((D_MODEL, LANES)), _resident((SUBLANES, LANES))]
    args = [h, gain, w_hi, w_lo, base_counts]
    if aliased:
        in_specs.append(pl.BlockSpec(memory_space=pl.ANY))
        args.append(xs)
    n_pieces = _run_piece_count(tm)
    return pl.pallas_call(
        functools.partial(_router_kernel, region_rows=region_rows, finalize=finalize, aliased=aliased),
        grid=(n_tiles,),
        in_specs=in_specs,
        out_specs=[pl.BlockSpec(memory_space=pl.ANY), row_spec(LANES),
                   pl.BlockSpec((None, SUBLANES, LANES), lambda i: (i, 0, 0)),
                   pl.BlockSpec((SUBLANES, LANES), lambda i: (0, 0))],
        out_shape=[jax.ShapeDtypeStruct((N_EXPERTS * region_rows, D_MODEL), F32),
                   jax.ShapeDtypeStruct((rows, LANES), F32),
                   jax.ShapeDtypeStruct((n_tiles, SUBLANES, LANES), F32),
                   jax.ShapeDtypeStruct((SUBLANES, LANES), F32)],
        scratch_shapes=[pltpu.VMEM((_sorted_rows(tm), D_MODEL), F32), pltpu.VMEM((EXPERT_ROW_TILE, D_MODEL), F32),
                        pltpu.SemaphoreType.DMA((N_EXPERTS, n_pieces)), pltpu.SemaphoreType.DMA(())],
        input_output_aliases={5: 0} if aliased else {},
        compiler_params=pltpu.CompilerParams(dimension_semantics=("arbitrary",), vmem_limit_bytes=48 * MIB),
        name="router",
    )(*args)


def _experts_kernel(blk_ref, exp_ref, nvalid_ref, x_ref, wg_ref, wu_ref, wd_ref, y_ref):
    i, c = pl.program_id(0), pl.program_id(1)
    valid = i < nvalid_ref[0]

    @pl.when(valid)
    def _():
        x = x_ref[...].astype(BF16)
        gate = jnp.dot(x, wg_ref[...], preferred_element_type=F32)
        up = jnp.dot(x, wu_ref[...], preferred_element_type=F32)
        part = jnp.dot((_silu(gate) * up).astype(BF16), wd_ref[...], preferred_element_type=F32)

        @pl.when(c == 0)
        def _():
            y_ref[...] = part

        @pl.when(c > 0)
        def _():
            y_ref[...] += part

    @pl.when(jnp.logical_and(jnp.logical_not(valid), c == 0))
    def _():
        y_ref[...] = jnp.zeros(y_ref.shape, F32)


def _experts(xs, tile_block, tile_expert, n_valid, wg, wu, wd):
    n_tiles = tile_block.shape[0]
    d_ff = wg.shape[2]
    ff_chunk = d_ff // 4
    n_ff = d_ff // ff_chunk
    tg = EXPERT_ROW_TILE
    ff_idx = lambda i, c, nv: jnp.where(i < nv[0], c, n_ff - 1)
    grid_spec = pltpu.PrefetchScalarGridSpec(
        num_scalar_prefetch=3,
        grid=(n_tiles, n_ff),
        in_specs=[pl.BlockSpec((tg, D_MODEL), lambda i, c, blk, ex, nv: (blk[i], 0)),
                  pl.BlockSpec((None, D_MODEL, ff_chunk), lambda i, c, blk, ex, nv: (ex[i], 0, ff_idx(i, c, nv))),
                  pl.BlockSpec((None, D_MODEL, ff_chunk), lambda i, c, blk, ex, nv: (ex[i], 0, ff_idx(i, c, nv))),
                  pl.BlockSpec((None, ff_chunk, D_MODEL), lambda i, c, blk, ex, nv: (ex[i], ff_idx(i, c, nv), 0))],
        out_specs=pl.BlockSpec((tg, D_MODEL), lambda i, c, blk, ex, nv: (i, 0)),
    )
    return pl.pallas_call(
        _experts_kernel,
        grid_spec=grid_spec,
        out_shape=jax.ShapeDtypeStruct((n_tiles * tg, D_MODEL), F32),
        compiler_params=pltpu.CompilerParams(dimension_semantics=("arbitrary", "arbitrary"),
                                             vmem_limit_bytes=48 * MIB),
        name="experts",
    )(tile_block, tile_expert, n_valid, xs, wg, wu, wd)


def _combine_kernel(tinfo_ref, yoff_ref, h_ref, route_ref, ys_ref, o_ref, sorted_ref, sems):
    i = pl.program_id(0)
    n_sorted = sorted_ref.shape[0]

    @pl.when(i == 0)
    def _():
        sorted_ref[...] = jnp.zeros(sorted_ref.shape, F32)

    copies = []
    for e in range(N_EXPERTS):
        rec = (i * 3) * N_EXPERTS + e
        base = tinfo_ref[rec + TILE_BASE * N_EXPERTS]
        n_e = tinfo_ref[rec + TILE_COUNT * N_EXPERTS]
        dst0 = tinfo_ref[rec + TILE_START * N_EXPERTS]
        copies += _run_copies(n_e, yoff_ref[e] + base, dst0, ys_ref, sorted_ref, sems.at[e])
    _start_then_wait(copies)
    y = sorted_ref[...].astype(BF16)
    route = route_ref[...]
    lane = lax.broadcasted_iota(I32, route.shape, 1)
    pick = lambda j: _lane_pick(route, lane, j)
    y1 = jnp.dot(_one_hot_rows(pick(ROUTE_D1), n_sorted).astype(BF16), y, preferred_element_type=F32)
    y2 = jnp.dot(_one_hot_rows(pick(ROUTE_D2), n_sorted).astype(BF16), y, preferred_element_type=F32)
    o_ref[...] = h_ref[...] + pick(ROUTE_G1) * y1 + pick(ROUTE_G2) * y2


def _combine(h, route, tile_info, y_offsets, ys):
    rows = h.shape[0]
    tm = rows // tile_info.shape[0]
    n_pieces = _run_piece_count(tm)
    tinfo = tile_info[:, :3, :N_EXPERTS].astype(I32).reshape(-1)
    grid_spec = pltpu.PrefetchScalarGridSpec(
        num_scalar_prefetch=2,
        grid=(rows // tm,),
        in_specs=[pl.BlockSpec((tm, D_MODEL), lambda i, *_: (i, 0)),
                  pl.BlockSpec((tm, LANES), lambda i, *_: (i, 0)),
                  pl.BlockSpec(memory_space=pl.ANY)],
        out_specs=pl.BlockSpec((tm, D_MODEL), lambda i, *_: (i, 0)),
        scratch_shapes=[pltpu.VMEM((_sorted_rows(tm), D_MODEL), F32),
                        pltpu.SemaphoreType.DMA((N_EXPERTS, n_pieces))],
    )
    return pl.pallas_call(
        _combine_kernel,
        grid_spec=grid_spec,
        out_shape=jax.ShapeDtypeStruct((rows, D_MODEL), F32),
        input_output_aliases={2: 0},
        compiler_params=pltpu.CompilerParams(dimension_semantics=("arbitrary",), vmem_limit_bytes=48 * MIB),
        name="combine",
    )(tinfo, y_offsets, h, route, ys)


def _routed_experts(hs, gain, router_w, expert_w):
    tg = EXPERT_ROW_TILE
    total_rows = sum(h.shape[0] for h in hs)
    run_pad = (RUN_ALIGN - 1) * sum(h.shape[0] // _router_tile(h.shape[0]) for h in hs)
    region_rows = (pl.cdiv(total_rows + run_pad, tg) + 1) * tg
    counts = jnp.zeros((SUBLANES, LANES), F32)
    xs, routes, tiles = None, [], []
    for g, h in enumerate(hs):
        xs, route, tile_info, counts = _router(h, gain, *router_w, counts, xs, region_rows,
                                               finalize=(g == len(hs) - 1))
        routes.append(route)
        tiles.append(tile_info)
    n_e = counts[0, :N_EXPERTS].astype(I32)
    tiles_e = (n_e + tg - 1) // tg
    first_tile = jnp.cumsum(tiles_e) - tiles_e
    n_valid = jnp.sum(tiles_e)
    n_tiles = (TOP_K * total_rows + N_EXPERTS * (run_pad + tg - 1)) // tg + 1
    t = jnp.minimum(jnp.arange(n_tiles, dtype=I32), n_valid - 1)
    tile_expert = jnp.sum((t[:, None] >= (first_tile + tiles_e)[None, :]).astype(I32), axis=1)
    tile_block = tile_expert * (region_rows // tg) + (t - first_tile[tile_expert])
    ys = _experts(xs, tile_block.astype(I32), tile_expert.astype(I32), n_valid.reshape(1).astype(I32), *expert_w)
    y_offsets = (first_tile * tg).astype(I32)
    return [_combine(h, route, tile_info, y_offsets, ys) for h, route, tile_info in zip(hs, routes, tiles)]


def _rotary_tables(batch, seq_rows):
    pos = (jnp.arange(seq_rows, dtype=I32) - PAD_FRONT).astype(F32)

    def cos_sin(n_rot, theta):
        half = n_rot // 2
        inv = theta ** (-jnp.arange(half, dtype=F32) * 2.0 / n_rot)
        ang = pos[:, None] * inv[None, :]
        return jnp.cos(ang), jnp.sin(ang)

    cos, sin = cos_sin(HEAD_DIM, RET_ROPE_THETA)
    cr = jnp.tile(cos, (1, 2 * LANES // HEAD_DIM))
    sr = jnp.tile(jnp.concatenate([-sin, sin], axis=1), (1, LANES // HEAD_DIM))
    cos, sin = cos_sin(ROPE_DIMS, ROPE_THETA)
    rest = HEAD_DIM - ROPE_DIMS
    ca = jnp.tile(jnp.concatenate([cos, cos, jnp.ones((seq_rows, rest), F32)], axis=1), (1, LANES // HEAD_DIM))
    sa = jnp.tile(jnp.concatenate([-sin, sin, jnp.zeros((seq_rows, rest), F32)], axis=1), (1, LANES // HEAD_DIM))
    return tuple(jnp.tile(t, (batch, 1)) for t in (cr, sr, ca, sa))


def _retention_params(log_gf, log_gb, gain):
    n_pairs = RET_WIDTH // LANES
    per_lane = lambda t: jnp.repeat(t.astype(F32).reshape(n_pairs, 2), HEAD_DIM, axis=1)
    per_head = lambda t, j: jnp.broadcast_to(t.astype(F32).reshape(n_pairs, 2)[:, j:j + 1], (n_pairs, LANES))
    rows = [per_lane(log_gf), per_lane(log_gb), per_head(log_gf, 0), per_head(log_gf, 1),
            per_head(log_gb, 0), per_head(log_gb, 1), gain.astype(F32).reshape(n_pairs, LANES),
            jnp.zeros((n_pairs, LANES), F32)]
    return jnp.stack(rows, axis=1)


def _to_padded_rows(x, meta_tokens):
    batch, seq, _ = x.shape
    h = jnp.concatenate([jnp.zeros((batch, PAD_FRONT, D_MODEL), F32),
                         jnp.broadcast_to(meta_tokens.astype(F32)[None], (batch, N_META, D_MODEL)),
                         x.astype(F32)], axis=1)
    return h.reshape(batch * (seq + CHUNK), D_MODEL)


def _token_mixer(h, lp, tabs, batch, seq_rows):
    rq, rk, rv, rg, aq, ak, av = _inproj(h, lp["norm_mix"], lp["w_in"], tabs, lp["q_gain"], lp["k_gain"],
                                         lp["ones_bd"])
    ret = _retention(rq, rk, rv, rg, lp["ret_dec"], batch, seq_rows)
    att = _attention(aq, ak, av, lp["sink"], lp["att_gain"], batch, seq_rows)
    return _outproj(ret, att, lp["w_out"], h)


def kernel(x_prompt, x_sample, meta_tokens, norm_mix, w_in, ret_log_decay_fwd, ret_log_decay_bwd, ret_out_gain,
           q_norm_gain, k_norm_gain, attn_sink, attn_out_gain, w_out, norm_ffn, ffn_w_gate, ffn_w_up, ffn_w_down,
           moe_router, moe_w_gate, moe_w_up, moe_w_down):
    depth = w_in.shape[0]
    ri = lax.broadcasted_iota(I32, (LANES, LANES), 0)
    ci = lax.broadcasted_iota(I32, (LANES, LANES), 1)
    ones_bd = ((ri < HEAD_DIM) == (ci < HEAD_DIM)).astype(BF16)
    row = lambda t: t.astype(F32).reshape(1, -1)

    xs = (x_prompt, x_sample)
    shapes = [(x.shape[0], x.shape[1] + CHUNK) for x in xs]
    hs = [_to_padded_rows(x, meta_tokens) for x in xs]
    tabs = [_rotary_tables(b, r) for b, r in shapes]

    for l in range(depth):
        lp = {
            "norm_mix": row(norm_mix[l]),
            "w_in": w_in[l].astype(BF16),
            "q_gain": jnp.tile(row(q_norm_gain[l]), (1, LANES // HEAD_DIM)),
            "k_gain": jnp.tile(row(k_norm_gain[l]), (1, LANES // HEAD_DIM)),
            "ones_bd": ones_bd,
            "ret_dec": _retention_params(ret_log_decay_fwd[l], ret_log_decay_bwd[l], ret_out_gain[l]),
            "sink": jnp.broadcast_to(attn_sink[l].astype(F32)[:, None], (N_ATT_HEADS, LANES)),
            "att_gain": row(attn_out_gain[l]),
            "w_out": w_out[l].astype(BF16),
        }
        hs = [_token_mixer(h, lp, tab, b, r) for h, tab, (b, r) in zip(hs, tabs, shapes)]
        i = l // 2
        if l % 2 == 0:
            w = (ffn_w_gate[i].astype(BF16), ffn_w_up[i].astype(BF16), ffn_w_down[i].astype(BF16))
            hs = [_ffn(h, row(norm_ffn[l]), *w) for h in hs]
        else:
            wr = jnp.pad(moe_router[i].astype(F32), ((0, 0), (0, LANES - N_EXPERTS)))
            wr_hi = wr.astype(BF16)
            router_w = (wr_hi, (wr - wr_hi.astype(F32)).astype(BF16))
            expert_w = (moe_w_gate[i].astype(BF16), moe_w_up[i].astype(BF16), moe_w_down[i].astype(BF16))
            hs = _routed_experts(hs, row(norm_ffn[l]), router_w, expert_w)
    return tuple(h.reshape(b, r, D_MODEL)[:, CHUNK:].astype(x.dtype) for h, x, (b, r) in zip(hs, xs, shapes))
```

```python
import functools

import jax
import jax.numpy as jnp
from jax import lax
from jax.experimental import pallas as pl
from jax.experimental.pallas import tpu as pltpu

F32 = jnp.float32
BF16 = jnp.bfloat16
I32 = jnp.int32

D_MODEL = 1024
HEAD_DIM = 64
N_RET_HEADS = 8
N_ATT_HEADS = 8
N_KV_HEADS = 2
RET_WIDTH = N_RET_HEADS * HEAD_DIM
ATT_WIDTH = N_ATT_HEADS * HEAD_DIM
KV_WIDTH = N_KV_HEADS * HEAD_DIM
IN_WIDTH = 4 * RET_WIDTH + ATT_WIDTH + 2 * KV_WIDTH
CHUNK = 128
LANES = 128
SUBLANES = 8
N_META = 16
PAD_FRONT = CHUNK - N_META
ROPE_THETA = 500000.0
ROPE_DIMS = HEAD_DIM // 4
RET_ROPE_THETA = 10000.0
N_EXPERTS = 8
TOP_K = 2
EXPERT_ROW_TILE = 1024
EPS = 1e-6
NEG = -1e30
QK_SCALE = HEAD_DIM ** -0.5
LOG2_E = 1.4426950408889634
MIB = 1024 * 1024


def _row_tile(rows, prefs):
    for t in prefs:
        if rows % t == 0:
            return t
    raise ValueError(f"no row tile for {rows} rows among {prefs}")


def _resident(shape):
    return pl.BlockSpec(shape, lambda *_: (0,) * len(shape), pipeline_mode=pl.Buffered(1))


def _rms(x, gain):
    ms = jnp.mean(x * x, axis=-1, keepdims=True)
    return x * lax.rsqrt(ms + EPS) * gain


def _silu(x):
    return x * jax.nn.sigmoid(x)


def _inproj_kernel(h_ref, gain_ref, w_ref, cr_ref, sr_ref, ca_ref, sa_ref, qg_ref, kg_ref, ones_ref,
                   rq_ref, rk_ref, rv_ref, rg_ref, aq_ref, ak_ref, av_ref):
    a = _rms(h_ref[...], gain_ref[...]).astype(BF16)
    tm = a.shape[0]
    lane = lax.broadcasted_iota(I32, (tm, LANES), 1)
    in_head = lane & (HEAD_DIM - 1)
    low_head = lane < HEAD_DIM

    def proj(c0, c1):
        return jnp.dot(a, w_ref[:, c0:c1], preferred_element_type=F32)

    def rotate(x, cos, sin_signed, half):
        partner = jnp.where(in_head < half, pltpu.roll(x, LANES - half, 1), pltpu.roll(x, half, 1))
        return x * cos + partner * sin_signed

    def head_norm(x, g):
        ss = jnp.dot((x * x).astype(BF16), ones_ref[...], preferred_element_type=F32)
        return x * lax.rsqrt(ss * (1.0 / HEAD_DIM) + EPS) * g

    cr, sr, ca, sa = cr_ref[...], sr_ref[...], ca_ref[...], sa_ref[...]
    cols = lambda j: slice(j * LANES, (j + 1) * LANES)
    n_lane_tiles = RET_WIDTH // LANES

    p = proj(0, RET_WIDTH)
    for j in range(n_lane_tiles):
        rq_ref[:, cols(j)] = rotate(p[:, cols(j)], cr, sr, HEAD_DIM // 2).astype(BF16)
    p = proj(RET_WIDTH, 2 * RET_WIDTH)
    for j in range(n_lane_tiles):
        rk_ref[:, cols(j)] = (rotate(p[:, cols(j)], cr, sr, HEAD_DIM // 2) * QK_SCALE).astype(BF16)
    rv_ref[...] = proj(2 * RET_WIDTH, 3 * RET_WIDTH).astype(BF16)
    rg_ref[...] = proj(3 * RET_WIDTH, 4 * RET_WIDTH).astype(BF16)

    base = 4 * RET_WIDTH
    p = proj(base, base + ATT_WIDTH)
    for j in range(ATT_WIDTH // LANES):
        n = head_norm(p[:, cols(j)], qg_ref[...])
        aq_ref[:, cols(j)] = (rotate(n, ca, sa, ROPE_DIMS // 2) * (QK_SCALE * LOG2_E)).astype(BF16)

    p = proj(base + ATT_WIDTH, IN_WIDTH)
    k = rotate(head_norm(p[:, :KV_WIDTH], kg_ref[...]), ca, sa, ROPE_DIMS // 2)
    v = p[:, KV_WIDTH:]
    k_sw = pltpu.roll(k, HEAD_DIM, 1)
    v_sw = pltpu.roll(v, HEAD_DIM, 1)
    ak_ref[:, 0:LANES] = jnp.where(low_head, k, k_sw).astype(BF16)
    ak_ref[:, LANES:2 * LANES] = jnp.where(low_head, k_sw, k).astype(BF16)
    av_ref[:, 0:LANES] = jnp.where(low_head, v, v_sw).astype(BF16)
    av_ref[:, LANES:2 * LANES] = jnp.where(low_head, v_sw, v).astype(BF16)


def _inproj(h, gain, w, tabs, qg, kg, ones_bd):
    rows = h.shape[0]
    tm = _row_tile(rows, (1024, 640, 512, 384, 256, 128))
    row_spec = lambda width: pl.BlockSpec((tm, width), lambda i: (i, 0))
    out_widths = (RET_WIDTH,) * 4 + (ATT_WIDTH, 2 * KV_WIDTH, 2 * KV_WIDTH)
    return pl.pallas_call(
        _inproj_kernel,
        grid=(rows // tm,),
        in_specs=[row_spec(D_MODEL), _resident((1, D_MODEL)), _resident((D_MODEL, IN_WIDTH)),
                  row_spec(LANES), row_spec(LANES), row_spec(LANES), row_spec(LANES),
                  _resident((1, LANES)), _resident((1, LANES)), _resident((LANES, LANES))],
        out_specs=[row_spec(wd) for wd in out_widths],
        out_shape=[jax.ShapeDtypeStruct((rows, wd), BF16) for wd in out_widths],
        compiler_params=pltpu.CompilerParams(dimension_semantics=("parallel",), vmem_limit_bytes=56 * MIB),
        name="inproj",
    )(h, gain, w, *tabs, qg, kg, ones_bd)


def _retention_kernel(q_ref, k_ref, v_ref, g_ref, dec_ref, o_ref, sb_ref, *, n_chunks):
    lgf, lgb = dec_ref[0:1, :], dec_ref[1:2, :]
    gain = dec_ref[6:7, :]
    ri = lax.broadcasted_iota(I32, (CHUNK, LANES), 0)
    ci = lax.broadcasted_iota(I32, (CHUNK, LANES), 1)
    r = ri.astype(F32)
    diff = (ri - ci).astype(F32)

    def decay_mask(lf, lb):
        return jnp.where(diff >= 0, jnp.exp(jnp.maximum(diff, 0.0) * lf), jnp.exp(jnp.maximum(-diff, 0.0) * lb))

    dm = jnp.concatenate([decay_mask(dec_ref[2:3, :], dec_ref[4:5, :]),
                          decay_mask(dec_ref[3:4, :], dec_ref[5:6, :])], axis=1)
    wf = jnp.exp((CHUNK - 1.0 - r) * lgf)
    wb = jnp.exp(r * lgb)
    qf = jnp.exp((r + 1.0) * lgf)
    qb = jnp.exp((CHUNK - r) * lgb)
    cf = jnp.exp(float(CHUNK) * lgf)
    cb = jnp.exp(float(CHUNK) * lgb)
    low_c = ci < HEAD_DIM
    same_head = (ri < HEAD_DIM) == low_c
    ones_bd = jnp.where(same_head, 1.0, 0.0).astype(BF16)
    m0 = jnp.where(low_c, 1.0, 0.0).astype(BF16)
    m1 = jnp.where(low_c, 0.0, 1.0).astype(BF16)

    def chunk_rows(n):
        return pl.ds(pl.multiple_of(n * CHUNK, CHUNK), CHUNK)

    def pair_rows(x):
        return jnp.concatenate([x * m0, x * m1], axis=0)

    def state_delta(k_weighted, v):
        u = jnp.dot(k_weighted.T.astype(BF16), v, preferred_element_type=F32)
        return jnp.where(same_head, u, 0.0)

    group = next(u for u in (4, 3, 5, 2, 1) if n_chunks % u == 0)
    n_groups = n_chunks // group
    members = range(group)
    zero_state = jnp.zeros((LANES, LANES), F32)

    def backward(i, state):
        ns = [n_chunks - 1 - i * group - j for j in members]
        deltas = [state_delta(k_ref[chunk_rows(n), :].astype(F32) * wb, v_ref[chunk_rows(n), :]) for n in ns]
        for n, delta in zip(ns, deltas):
            sb_ref[n] = state.astype(BF16)
            state = cb * state + delta
        return state

    lax.fori_loop(0, n_groups, backward, zero_state)

    def forward(i, state):
        ns = [i * group + j for j in members]
        q = [q_ref[chunk_rows(n), :] for n in ns]
        k = [k_ref[chunk_rows(n), :] for n in ns]
        v = [v_ref[chunk_rows(n), :] for n in ns]
        deltas = [state_delta(k[j].astype(F32) * wf, v[j]) for j in members]
        s = [lax.dot_general(q[j], pair_rows(k[j]), (((1,), (1,)), ((), ())), preferred_element_type=F32) * dm
             for j in members]
        both = []
        for j in members:
            both.append(jnp.concatenate([state.astype(BF16), sb_ref[ns[j]]], axis=1))
            state = cf * state + deltas[j]
        intra = [jnp.dot(s[j].astype(BF16), pair_rows(v[j]), preferred_element_type=F32) for j in members]
        inter = [jnp.dot(q[j], both[j], preferred_element_type=F32) for j in members]
        o = [intra[j] + inter[j][:, :LANES] * qf + inter[j][:, LANES:] * qb for j in members]
        ss = [jnp.dot((o[j] * o[j]).astype(BF16), ones_bd, preferred_element_type=F32) for j in members]
        for j in members:
            gate = _silu(g_ref[chunk_rows(ns[j]), :].astype(F32))
            o_ref[chunk_rows(ns[j]), :] = (o[j] * lax.rsqrt(ss[j] * (1.0 / HEAD_DIM) + EPS) * gain * gate).astype(BF16)
        return state

    lax.fori_loop(0, n_groups, forward, zero_state)


def _retention(rq, rk, rv, rg, dec, batch, seq_rows):
    n_chunks = seq_rows // CHUNK
    n_pairs = RET_WIDTH // LANES
    view = lambda t: t.reshape(batch, seq_rows, RET_WIDTH)
    seq_spec = pl.BlockSpec((None, seq_rows, LANES), lambda b, p: (b, 0, p))
    out = pl.pallas_call(
        functools.partial(_retention_kernel, n_chunks=n_chunks),
        grid=(batch, n_pairs),
        in_specs=[seq_spec, seq_spec, seq_spec, seq_spec,
                  pl.BlockSpec((None, 8, LANES), lambda b, p: (p, 0, 0))],
        out_specs=seq_spec,
        out_shape=jax.ShapeDtypeStruct((batch, seq_rows, RET_WIDTH), BF16),
        scratch_shapes=[pltpu.VMEM((n_chunks, LANES, LANES), BF16)],
        compiler_params=pltpu.CompilerParams(dimension_semantics=("parallel", "parallel"),
                                             vmem_limit_bytes=48 * MIB),
        name="retention",
    )(view(rq), view(rk), view(rv), view(rg), dec)
    return out.reshape(batch * seq_rows, RET_WIDTH)


def _attention_kernel(q_ref, kp_ref, kc_ref, kn_ref, km_ref, vp_ref, vc_ref, vn_ref, vm_ref,
                      sink_ref, gain_ref, o_ref, *, n_chunks, chunks_per_step):
    first_chunk = pl.program_id(1) * chunks_per_step
    ri = lax.broadcasted_iota(I32, (CHUNK, LANES), 0)
    ci = lax.broadcasted_iota(I32, (CHUNK, LANES), 1)
    never = 2 * CHUNK
    to_bias = lambda ok: jnp.where(ok, 0.0, NEG).astype(BF16)
    meta_bias = to_bias(ri >= PAD_FRONT)
    n_keys = 4 * CHUNK
    low_c = ci < HEAD_DIM
    m0 = jnp.where(low_c, 1.0, 0.0).astype(BF16)
    m1 = jnp.where(low_c, 0.0, 1.0).astype(BF16)
    eye = jnp.where(ri == ci, 1.0, 0.0).astype(BF16)
    row_sums = jnp.concatenate([m0] * 4 + [m1] * 4, axis=0)
    pairs_per_kv = N_ATT_HEADS // N_KV_HEADS // 2
    rows = lambda j: slice(j * CHUNK, (j + 1) * CHUNK)

    def split_heads(ref, blk, g):
        x = ref[blk, g * LANES:(g + 1) * LANES]
        return x * m0, x * m1

    def blocks(prev_ref, main_ref, next_ref, meta_ref, g):
        full = slice(0, CHUNK)
        out = [split_heads(prev_ref, full, g)]
        out += [split_heads(main_ref, rows(j), g) for j in range(chunks_per_step)]
        out += [split_heads(next_ref, full, g), split_heads(meta_ref, full, g)]
        return out

    k_blocks = [blocks(kp_ref, kc_ref, kn_ref, km_ref, g) for g in range(N_KV_HEADS)]
    v_blocks = [blocks(vp_ref, vc_ref, vn_ref, vm_ref, g) for g in range(N_KV_HEADS)]

    def pair_rows(blks, j):
        use = [blks[j], blks[j + 1], blks[j + 2], blks[-1]]
        return jnp.concatenate([b[0] for b in use] + [b[1] for b in use], axis=0)

    for j in range(chunks_per_step):
        c = first_chunk + j
        prev_off = jnp.where(c >= 2, 0, never)
        cur_off = jnp.where(c >= 1, 0, never)
        next_off = jnp.where(c + 1 <= n_chunks - 1, 0, never)
        bias_t = jnp.concatenate([to_bias(ri >= ci + prev_off), to_bias(ri >= cur_off),
                                  to_bias(ri + next_off <= ci), meta_bias] * 2, axis=0)
        outs = []
        for g in range(N_KV_HEADS):
            k_ext = jnp.concatenate([pair_rows(k_blocks[g], j), bias_t], axis=1)
            v_ext = jnp.concatenate([pair_rows(v_blocks[g], j), row_sums], axis=1)
            for pp in range(pairs_per_kv):
                p = g * pairs_per_kv + pp
                q_ext = jnp.concatenate([q_ref[rows(j), p * LANES:(p + 1) * LANES], eye], axis=1)
                s = lax.dot_general(q_ext, k_ext, (((1,), (1,)), ((), ())), preferred_element_type=F32)
                probs, sink_terms = [], []
                for hh in range(2):
                    sh = s[:, hh * n_keys:(hh + 1) * n_keys]
                    sk = sink_ref[2 * p + hh:2 * p + hh + 1, 0:1]
                    m = jnp.maximum(jnp.max(sh, axis=-1, keepdims=True), sk)
                    probs.append(jnp.exp2(sh - m).astype(BF16))
                    sink_terms.append(jnp.exp2(sk - m))
                ol = jnp.dot(jnp.concatenate(probs, axis=1), v_ext, preferred_element_type=F32)
                denom = ol[:, LANES:] + jnp.where(low_c, sink_terms[0], sink_terms[1])
                outs.append(ol[:, :LANES] / denom)
        att = jnp.concatenate(outs, axis=1)
        row = lax.broadcasted_iota(I32, att.shape, 0)
        att = jnp.where(row >= jnp.where(c == 0, PAD_FRONT, 0), att, 0.0)
        o_ref[rows(j), :] = _rms(att, gain_ref[...]).astype(BF16)


def _attention(aq, ak, av, sink, gain, batch, seq_rows):
    n_chunks = seq_rows // CHUNK
    per_step = next(r for r in (3, 5, 4, 2, 1) if n_chunks % r == 0)
    n_steps = n_chunks // per_step
    q3 = aq.reshape(batch, seq_rows, ATT_WIDTH)
    k3 = ak.reshape(batch, seq_rows, 2 * KV_WIDTH)
    v3 = av.reshape(batch, seq_rows, 2 * KV_WIDTH)
    kv = lambda fn: pl.BlockSpec((None, CHUNK, 2 * KV_WIDTH), lambda b, s: (b, fn(s), 0))
    kv_specs = [kv(lambda s: jnp.maximum(s * per_step - 1, 0)),
                pl.BlockSpec((None, per_step * CHUNK, 2 * KV_WIDTH), lambda b, s: (b, s, 0)),
                kv(lambda s: jnp.minimum((s + 1) * per_step, n_chunks - 1)), kv(lambda s: 0)]
    q_spec = pl.BlockSpec((None, per_step * CHUNK, ATT_WIDTH), lambda b, s: (b, s, 0))
    out = pl.pallas_call(
        functools.partial(_attention_kernel, n_chunks=n_chunks, chunks_per_step=per_step),
        grid=(batch, n_steps),
        in_specs=[q_spec] + kv_specs + kv_specs + [_resident((N_ATT_HEADS, LANES)), _resident((1, ATT_WIDTH))],
        out_specs=q_spec,
        out_shape=jax.ShapeDtypeStruct((batch, seq_rows, ATT_WIDTH), BF16),
        compiler_params=pltpu.CompilerParams(dimension_semantics=("parallel", "parallel"),
                                             vmem_limit_bytes=32 * MIB),
        name="attention",
    )(q3, k3, k3, k3, k3, v3, v3, v3, v3, sink, gain)
    return out.reshape(batch * seq_rows, ATT_WIDTH)


def _outproj_kernel(ret_ref, att_ref, w_ref, h_ref, o_ref):
    acc = jnp.dot(ret_ref[...], w_ref[0:RET_WIDTH, :], preferred_element_type=F32)
    acc = acc + jnp.dot(att_ref[...], w_ref[RET_WIDTH:, :], preferred_element_type=F32)
    o_ref[...] = h_ref[...] + acc


def _outproj(ret, att, w, h):
    rows = h.shape[0]
    tm = _row_tile(rows, (1024, 640, 512, 384, 256, 128))
    row_spec = lambda width: pl.BlockSpec((tm, width), lambda i: (i, 0))
    return pl.pallas_call(
        _outproj_kernel,
        grid=(rows // tm,),
        in_specs=[row_spec(RET_WIDTH), row_spec(ATT_WIDTH), _resident((RET_WIDTH + ATT_WIDTH, D_MODEL)),
                  row_spec(D_MODEL)],
        out_specs=row_spec(D_MODEL),
        out_shape=jax.ShapeDtypeStruct((rows, D_MODEL), F32),
        input_output_aliases={3: 0},
        compiler_params=pltpu.CompilerParams(dimension_semantics=("parallel",), vmem_limit_bytes=48 * MIB),
        name="outproj",
    )(ret, att, w, h)


def _ffn_kernel(h_ref, gain_ref, wg_ref, wu_ref, wd_ref, o_ref, *, ff_chunk):
    x = h_ref[...]
    f = _rms(x, gain_ref[...]).astype(BF16)
    acc = x
    for c0 in range(0, wg_ref.shape[1], ff_chunk):
        gate = jnp.dot(f, wg_ref[:, c0:c0 + ff_chunk], preferred_element_type=F32)
        up = jnp.dot(f, wu_ref[:, c0:c0 + ff_chunk], preferred_element_type=F32)
        act = (_silu(gate) * up).astype(BF16)
        acc = acc + jnp.dot(act, wd_ref[c0:c0 + ff_chunk, :], preferred_element_type=F32)
    o_ref[...] = acc


def _ffn(h, gain, wg, wu, wd):
    rows = h.shape[0]
    d_ff = wg.shape[1]
    tm = _row_tile(rows, (640, 512, 384, 256, 128))
    row_spec = pl.BlockSpec((tm, D_MODEL), lambda i: (i, 0))
    return pl.pallas_call(
        functools.partial(_ffn_kernel, ff_chunk=d_ff // 2),
        grid=(rows // tm,),
        in_specs=[row_spec, _resident((1, D_MODEL)), _resident((D_MODEL, d_ff)), _resident((D_MODEL, d_ff)),
                  _resident((d_ff, D_MODEL))],
        out_specs=row_spec,
        out_shape=jax.ShapeDtypeStruct((rows, D_MODEL), F32),
        input_output_aliases={0: 0},
        compiler_params=pltpu.CompilerParams(dimension_semantics=("parallel",), vmem_limit_bytes=56 * MIB),
        name="ffn",
    )(h, gain, wg, wu, wd)


ROUTE_G1, ROUTE_G2, ROUTE_D1, ROUTE_D2 = range(4)
TILE_BASE, TILE_COUNT, TILE_START = range(3)


def _lane_pick(x, lane, j):
    return jnp.sum(jnp.where(lane == j, x, 0.0), axis=-1, keepdims=True)


def _lane_scalar(vec, lane_row, e):
    return jnp.sum(jnp.where(lane_row == e, vec, 0.0)).astype(I32)


RUN_ALIGN_BITS = 3
RUN_ALIGN = 1 << RUN_ALIGN_BITS


def _run_piece_count(tm):
    return tm.bit_length() - RUN_ALIGN_BITS


def _run_copies(count, src0, dst0, src_ref, dst_ref, sems):
    copies = []
    for j in range(sems.shape[0]):
        k = j + RUN_ALIGN_BITS
        size = 1 << k
        start = (count >> (k + 1)) << (k + 1)
        src = src_ref.at[pl.ds(pl.multiple_of(src0 + start, RUN_ALIGN), size)]
        dst = dst_ref.at[pl.ds(pl.multiple_of(dst0 + start, RUN_ALIGN), size)]
        copies.append((((count >> k) & 1) == 1, pltpu.make_async_copy(src, dst, sems.at[j])))
    return copies


def _start_then_wait(copies):
    for cond, cp in copies:
        pl.when(cond)(cp.start)
    for cond, cp in copies:
        pl.when(cond)(cp.wait)


def _one_hot_rows(dest, width):
    col = lax.broadcasted_iota(I32, (dest.shape[0], width), 1)
    return jnp.where(col == dest.astype(I32), 1.0, 0.0)


def _router_kernel(*refs, region_rows, finalize, aliased):
    h_ref, gain_ref, whi_ref, wlo_ref, base_ref = refs[:5]
    refs = refs[5 + (1 if aliased else 0):]
    xs_ref, route_ref, tile_ref, cnt_ref, sorted_ref, zero_ref, sems, zsem = refs
    i = pl.program_id(0)

    @pl.when(i == 0)
    def _():
        cnt_ref[...] = base_ref[...]

    f = _rms(h_ref[...], gain_ref[...])
    f_hi = f.astype(BF16)
    f_lo = (f - f_hi.astype(F32)).astype(BF16)
    logits = (jnp.dot(f_hi, whi_ref[...], preferred_element_type=F32)
              + jnp.dot(f_lo, whi_ref[...], preferred_element_type=F32)
              + jnp.dot(f_hi, wlo_ref[...], preferred_element_type=F32))
    tm = logits.shape[0]
    lane = lax.broadcasted_iota(I32, logits.shape, 1)
    lg = jnp.where(lane < N_EXPERTS, logits, -jnp.inf)
    m1 = jnp.max(lg, axis=-1, keepdims=True)
    i1 = jnp.min(jnp.where(lg == m1, lane, LANES), axis=-1, keepdims=True)
    lg2 = jnp.where(lane == i1, -jnp.inf, lg)
    m2 = jnp.max(lg2, axis=-1, keepdims=True)
    i2 = jnp.min(jnp.where(lg2 == m2, lane, LANES), axis=-1, keepdims=True)
    e2 = jnp.exp(m2 - m1)
    g1 = 1.0 / (1.0 + e2)
    g2 = e2 * g1

    sel = jnp.where(lane == i1, 1.0, 0.0) + jnp.where(lane == i2, 1.0, 0.0)
    sel_b = sel.astype(BF16)
    ri = lax.broadcasted_iota(I32, (tm, tm), 0)
    ci = lax.broadcasted_iota(I32, (tm, tm), 1)
    earlier_tokens = jnp.where(ri > ci, 1.0, 0.0).astype(BF16)
    rank = jnp.dot(earlier_tokens, sel_b, preferred_element_type=F32)
    li = lax.broadcasted_iota(I32, (LANES, LANES), 0)
    lj = lax.broadcasted_iota(I32, (LANES, LANES), 1)
    earlier_experts = jnp.where(li < lj, 1.0, 0.0).astype(BF16)
    tiles_per_run = jnp.floor((jnp.sum(sel, axis=0, keepdims=True) + (RUN_ALIGN - 1.0)) * (1.0 / RUN_ALIGN))
    count = tiles_per_run * RUN_ALIGN
    start = RUN_ALIGN * jnp.dot(jnp.broadcast_to(tiles_per_run, (SUBLANES, LANES)).astype(BF16), earlier_experts,
                                preferred_element_type=F32)[0:1, :]
    dest = start + rank
    d1 = _lane_pick(dest, lane, i1)
    d2 = _lane_pick(dest, lane, i2)
    route = jnp.zeros(logits.shape, F32)
    for j, val in ((ROUTE_G1, g1), (ROUTE_G2, g2), (ROUTE_D1, d1), (ROUTE_D2, d2)):
        route = jnp.where(lane == j, val, route)
    route_ref[...] = route

    n_sorted = sorted_ref.shape[0]
    onehot = _one_hot_rows(d1, n_sorted) + _one_hot_rows(d2, n_sorted)
    sorted_ref[...] = jnp.dot(onehot.T.astype(BF16), f_hi, preferred_element_type=F32)

    base = cnt_ref[0:1, :]
    sub = lax.broadcasted_iota(I32, (SUBLANES, LANES), 0)
    tile_ref[...] = jnp.where(sub == TILE_BASE, base, jnp.where(sub == TILE_COUNT, count,
                                                                jnp.where(sub == TILE_START, start, 0.0)))
    lane_row = lane[0:1, :]
    copies = []
    for e in range(N_EXPERTS):
        n_e = _lane_scalar(count, lane_row, e)
        src0 = _lane_scalar(start, lane_row, e)
        dst0 = e * region_rows + _lane_scalar(base, lane_row, e)
        copies += _run_copies(n_e, src0, dst0, sorted_ref, xs_ref, sems.at[e])
    _start_then_wait(copies)
    cnt_ref[...] = cnt_ref[...] + count

    if finalize:
        @pl.when(i == pl.num_programs(0) - 1)
        def _():
            zero_ref[...] = jnp.zeros(zero_ref.shape, F32)
            total = cnt_ref[0:1, :]
            for e in range(N_EXPERTS):
                end = pl.multiple_of(e * region_rows + _lane_scalar(total, lane_row, e), RUN_ALIGN)
                cp = pltpu.make_async_copy(zero_ref, xs_ref.at[pl.ds(end, EXPERT_ROW_TILE)], zsem)
                cp.start()
                cp.wait()


def _router_tile(rows):
    return _row_tile(rows, (512, 640, 384, 256, 128))


def _sorted_rows(tm):
    return TOP_K * tm + pl.cdiv(N_EXPERTS * (RUN_ALIGN - 1), LANES) * LANES


def _router(h, gain, w_hi, w_lo, base_counts, xs, region_rows, finalize):
    rows = h.shape[0]
    tm = _router_tile(rows)
    n_tiles = rows // tm
    aliased = xs is not None
    row_spec = lambda width: pl.BlockSpec((tm, width), lambda i: (i, 0))
    in_specs = [row_spec(D_MODEL), _resident((1, D_MODEL)), _resident((D_MODEL, LANES)),
                _resident((D_MODEL, LANES)), _resident((SUBLANES, LANES))]
    args = [h, gain, w_hi, w_lo, base_counts]
    if aliased:
        in_specs.append(pl.BlockSpec(memory_space=pl.ANY))
        args.append(xs)
    n_pieces = _run_piece_count(tm)
    return pl.pallas_call(
        functools.partial(_router_kernel, region_rows=region_rows, finalize=finalize, aliased=aliased),
        grid=(n_tiles,),
        in_specs=in_specs,
        out_specs=[pl.BlockSpec(memory_space=pl.ANY), row_spec(LANES),
                   pl.BlockSpec((None, SUBLANES, LANES), lambda i: (i, 0, 0)),
                   pl.BlockSpec((SUBLANES, LANES), lambda i: (0, 0))],
        out_shape=[jax.ShapeDtypeStruct((N_EXPERTS * region_rows, D_MODEL), F32),
                   jax.ShapeDtypeStruct((rows, LANES), F32),
                   jax.ShapeDtypeStruct((n_tiles, SUBLANES, LANES), F32),
                   jax.ShapeDtypeStruct((SUBLANES, LANES), F32)],
        scratch_shapes=[pltpu.VMEM((_sorted_rows(tm), D_MODEL), F32), pltpu.VMEM((EXPERT_ROW_TILE, D_MODEL), F32),
                        pltpu.SemaphoreType.DMA((N_EXPERTS, n_pieces)), pltpu.SemaphoreType.DMA(())],
        input_output_aliases={5: 0} if aliased else {},
        compiler_params=pltpu.CompilerParams(dimension_semantics=("arbitrary",), vmem_limit_bytes=48 * MIB),
        name="router",
    )(*args)


def _experts_kernel(blk_ref, exp_ref, nvalid_ref, x_ref, wg_ref, wu_ref, wd_ref, y_ref):
    i, c = pl.program_id(0), pl.program_id(1)
    valid = i < nvalid_ref[0]

    @pl.when(valid)
    def _():
        x = x_ref[...].astype(BF16)
        gate = jnp.dot(x, wg_ref[...], preferred_element_type=F32)
        up = jnp.dot(x, wu_ref[...], preferred_element_type=F32)
        part = jnp.dot((_silu(gate) * up).astype(BF16), wd_ref[...], preferred_element_type=F32)

        @pl.when(c == 0)
        def _():
            y_ref[...] = part

        @pl.when(c > 0)
        def _():
            y_ref[...] += part

    @pl.when(jnp.logical_and(jnp.logical_not(valid), c == 0))
    def _():
        y_ref[...] = jnp.zeros(y_ref.shape, F32)


def _experts(xs, tile_block, tile_expert, n_valid, wg, wu, wd):
    n_tiles = tile_block.shape[0]
    d_ff = wg.shape[2]
    ff_chunk = d_ff // 4
    n_ff = d_ff // ff_chunk
    tg = EXPERT_ROW_TILE
    ff_idx = lambda i, c, nv: jnp.where(i < nv[0], c, n_ff - 1)
    grid_spec = pltpu.PrefetchScalarGridSpec(
        num_scalar_prefetch=3,
        grid=(n_tiles, n_ff),
        in_specs=[pl.BlockSpec((tg, D_MODEL), lambda i, c, blk, ex, nv: (blk[i], 0)),
                  pl.BlockSpec((None, D_MODEL, ff_chunk), lambda i, c, blk, ex, nv: (ex[i], 0, ff_idx(i, c, nv))),
                  pl.BlockSpec((None, D_MODEL, ff_chunk), lambda i, c, blk, ex, nv: (ex[i], 0, ff_idx(i, c, nv))),
                  pl.BlockSpec((None, ff_chunk, D_MODEL), lambda i, c, blk, ex, nv: (ex[i], ff_idx(i, c, nv), 0))],
        out_specs=pl.BlockSpec((tg, D_MODEL), lambda i, c, blk, ex, nv: (i, 0)),
    )
    return pl.pallas_call(
        _experts_kernel,
        grid_spec=grid_spec,
        out_shape=jax.ShapeDtypeStruct((n_tiles * tg, D_MODEL), F32),
        compiler_params=pltpu.CompilerParams(dimension_semantics=("arbitrary", "arbitrary"),
                                             vmem_limit_bytes=48 * MIB),
        name="experts",
    )(tile_block, tile_expert, n_valid, xs, wg, wu, wd)


def _combine_kernel(tinfo_ref, yoff_ref, h_ref, route_ref, ys_ref, o_ref, sorted_ref, sems):
    i = pl.program_id(0)
    n_sorted = sorted_ref.shape[0]

    @pl.when(i == 0)
    def _():
        sorted_ref[...] = jnp.zeros(sorted_ref.shape, F32)

    copies = []
    for e in range(N_EXPERTS):
        rec = (i * 3) * N_EXPERTS + e
        base = tinfo_ref[rec + TILE_BASE * N_EXPERTS]
        n_e = tinfo_ref[rec + TILE_COUNT * N_EXPERTS]
        dst0 = tinfo_ref[rec + TILE_START * N_EXPERTS]
        copies += _run_copies(n_e, yoff_ref[e] + base, dst0, ys_ref, sorted_ref, sems.at[e])
    _start_then_wait(copies)
    y = sorted_ref[...].astype(BF16)
    route = route_ref[...]
    lane = lax.broadcasted_iota(I32, route.shape, 1)
    pick = lambda j: _lane_pick(route, lane, j)
    y1 = jnp.dot(_one_hot_rows(pick(ROUTE_D1), n_sorted).astype(BF16), y, preferred_element_type=F32)
    y2 = jnp.dot(_one_hot_rows(pick(ROUTE_D2), n_sorted).astype(BF16), y, preferred_element_type=F32)
    o_ref[...] = h_ref[...] + pick(ROUTE_G1) * y1 + pick(ROUTE_G2) * y2


def _combine(h, route, tile_info, y_offsets, ys):
    rows = h.shape[0]
    tm = rows // tile_info.shape[0]
    n_pieces = _run_piece_count(tm)
    tinfo = tile_info[:, :3, :N_EXPERTS].astype(I32).reshape(-1)
    grid_spec = pltpu.PrefetchScalarGridSpec(
        num_scalar_prefetch=2,
        grid=(rows // tm,),
        in_specs=[pl.BlockSpec((tm, D_MODEL), lambda i, *_: (i, 0)),
                  pl.BlockSpec((tm, LANES), lambda i, *_: (i, 0)),
                  pl.BlockSpec(memory_space=pl.ANY)],
        out_specs=pl.BlockSpec((tm, D_MODEL), lambda i, *_: (i, 0)),
        scratch_shapes=[pltpu.VMEM((_sorted_rows(tm), D_MODEL), F32),
                        pltpu.SemaphoreType.DMA((N_EXPERTS, n_pieces))],
    )
    return pl.pallas_call(
        _combine_kernel,
        grid_spec=grid_spec,
        out_shape=jax.ShapeDtypeStruct((rows, D_MODEL), F32),
        input_output_aliases={2: 0},
        compiler_params=pltpu.CompilerParams(dimension_semantics=("arbitrary",), vmem_limit_bytes=48 * MIB),
        name="combine",
    )(tinfo, y_offsets, h, route, ys)


def _routed_experts(hs, gain, router_w, expert_w):
    tg = EXPERT_ROW_TILE
    total_rows = sum(h.shape[0] for h in hs)
    run_pad = (RUN_ALIGN - 1) * sum(h.shape[0] // _router_tile(h.shape[0]) for h in hs)
    region_rows = (pl.cdiv(total_rows + run_pad, tg) + 1) * tg
    counts = jnp.zeros((SUBLANES, LANES), F32)
    xs, routes, tiles = None, [], []
    for g, h in enumerate(hs):
        xs, route, tile_info, counts = _router(h, gain, *router_w, counts, xs, region_rows,
                                               finalize=(g == len(hs) - 1))
        routes.append(route)
        tiles.append(tile_info)
    n_e = counts[0, :N_EXPERTS].astype(I32)
    tiles_e = (n_e + tg - 1) // tg
    first_tile = jnp.cumsum(tiles_e) - tiles_e
    n_valid = jnp.sum(tiles_e)
    n_tiles = (TOP_K * total_rows + N_EXPERTS * (run_pad + tg - 1)) // tg + 1
    t = jnp.minimum(jnp.arange(n_tiles, dtype=I32), n_valid - 1)
    tile_expert = jnp.sum((t[:, None] >= (first_tile + tiles_e)[None, :]).astype(I32), axis=1)
    tile_block = tile_expert * (region_rows // tg) + (t - first_tile[tile_expert])
    ys = _experts(xs, tile_block.astype(I32), tile_expert.astype(I32), n_valid.reshape(1).astype(I32), *expert_w)
    y_offsets = (first_tile * tg).astype(I32)
    return [_combine(h, route, tile_info, y_offsets, ys) for h, route, tile_info in zip(hs, routes, tiles)]


def _rotary_tables(batch, seq_rows):
    pos = (jnp.arange(seq_rows, dtype=I32) - PAD_FRONT).astype(F32)

    def cos_sin(n_rot, theta):
        half = n_rot // 2
        inv = theta ** (-jnp.arange(half, dtype=F32) * 2.0 / n_rot)
        ang = pos[:, None] * inv[None, :]
        return jnp.cos(ang), jnp.sin(ang)

    cos, sin = cos_sin(HEAD_DIM, RET_ROPE_THETA)
    cr = jnp.tile(cos, (1, 2 * LANES // HEAD_DIM))
    sr = jnp.tile(jnp.concatenate([-sin, sin], axis=1), (1, LANES // HEAD_DIM))
    cos, sin = cos_sin(ROPE_DIMS, ROPE_THETA)
    rest = HEAD_DIM - ROPE_DIMS
    ca = jnp.tile(jnp.concatenate([cos, cos, jnp.ones((seq_rows, rest), F32)], axis=1), (1, LANES // HEAD_DIM))
    sa = jnp.tile(jnp.concatenate([-sin, sin, jnp.zeros((seq_rows, rest), F32)], axis=1), (1, LANES // HEAD_DIM))
    return tuple(jnp.tile(t, (batch, 1)) for t in (cr, sr, ca, sa))


def _retention_params(log_gf, log_gb, gain):
    n_pairs = RET_WIDTH // LANES
    per_lane = lambda t: jnp.repeat(t.astype(F32).reshape(n_pairs, 2), HEAD_DIM, axis=1)
    per_head = lambda t, j: jnp.broadcast_to(t.astype(F32).reshape(n_pairs, 2)[:, j:j + 1], (n_pairs, LANES))
    rows = [per_lane(log_gf), per_lane(log_gb), per_head(log_gf, 0), per_head(log_gf, 1),
            per_head(log_gb, 0), per_head(log_gb, 1), gain.astype(F32).reshape(n_pairs, LANES),
            jnp.zeros((n_pairs, LANES), F32)]
    return jnp.stack(rows, axis=1)


def _to_padded_rows(x, meta_tokens):
    batch, seq, _ = x.shape
    h = jnp.concatenate([jnp.zeros((batch, PAD_FRONT, D_MODEL), F32),
                         jnp.broadcast_to(meta_tokens.astype(F32)[None], (batch, N_META, D_MODEL)),
                         x.astype(F32)], axis=1)
    return h.reshape(batch * (seq + CHUNK), D_MODEL)


def _token_mixer(h, lp, tabs, batch, seq_rows):
    rq, rk, rv, rg, aq, ak, av = _inproj(h, lp["norm_mix"], lp["w_in"], tabs, lp["q_gain"], lp["k_gain"],
                                         lp["ones_bd"])
    ret = _retention(rq, rk, rv, rg, lp["ret_dec"], batch, seq_rows)
    att = _attention(aq, ak, av, lp["sink"], lp["att_gain"], batch, seq_rows)
    return _outproj(ret, att, lp["w_out"], h)


def kernel(x_prompt, x_sample, meta_tokens, norm_mix, w_in, ret_log_decay_fwd, ret_log_decay_bwd, ret_out_gain,
           q_norm_gain, k_norm_gain, attn_sink, attn_out_gain, w_out, norm_ffn, ffn_w_gate, ffn_w_up, ffn_w_down,
           moe_router, moe_w_gate, moe_w_up, moe_w_down):
    depth = w_in.shape[0]
    ri = lax.broadcasted_iota(I32, (LANES, LANES), 0)
    ci = lax.broadcasted_iota(I32, (LANES, LANES), 1)
    ones_bd = ((ri < HEAD_DIM) == (ci < HEAD_DIM)).astype(BF16)
    row = lambda t: t.astype(F32).reshape(1, -1)

    xs = (x_prompt, x_sample)
    shapes = [(x.shape[0], x.shape[1] + CHUNK) for x in xs]
    hs = [_to_padded_rows(x, meta_tokens) for x in xs]
    tabs = [_rotary_tables(b, r) for b, r in shapes]

    for l in range(depth):
        lp = {
            "norm_mix": row(norm_mix[l]),
            "w_in": w_in[l].astype(BF16),
            "q_gain": jnp.tile(row(q_norm_gain[l]), (1, LANES // HEAD_DIM)),
            "k_gain": jnp.tile(row(k_norm_gain[l]), (1, LANES // HEAD_DIM)),
            "ones_bd": ones_bd,
            "ret_dec": _retention_params(ret_log_decay_fwd[l], ret_log_decay_bwd[l], ret_out_gain[l]),
            "sink": jnp.broadcast_to(attn_sink[l].astype(F32)[:, None] * LOG2_E, (N_ATT_HEADS, LANES)),
            "att_gain": row(attn_out_gain[l]),
            "w_out": w_out[l].astype(BF16),
        }
        hs = [_token_mixer(h, lp, tab, b, r) for h, tab, (b, r) in zip(hs, tabs, shapes)]
        i = l // 2
        if l % 2 == 0:
            w = (ffn_w_gate[i].astype(BF16), ffn_w_up[i].astype(BF16), ffn_w_down[i].astype(BF16))
            hs = [_ffn(h, row(norm_ffn[l]), *w) for h in hs]
        else:
            wr = jnp.pad(moe_router[i].astype(F32), ((0, 0), (0, LANES - N_EXPERTS)))
            wr_hi = wr.astype(BF16)
            router_w = (wr_hi, (wr - wr_hi.astype(F32)).astype(BF16))
            expert_w = (moe_w_gate[i].astype(BF16), moe_w_up[i].astype(BF16), moe_w_down[i].astype(BF16))
            hs = _routed_experts(hs, row(norm_ffn[l]), router_w, expert_w)
    return tuple(h.reshape(b, r, D_MODEL)[:, CHUNK:].astype(x.dtype) for h, x, (b, r) in zip(hs, xs, shapes))
```

```python
import functools

import jax
import jax.numpy as jnp
from jax import lax
from jax.experimental import pallas as pl
from jax.experimental.pallas import tpu as pltpu

F32 = jnp.float32
BF16 = jnp.bfloat16
I32 = jnp.int32

D_MODEL = 1024
HEAD_DIM = 64
N_RET_HEADS = 8
N_ATT_HEADS = 8
N_KV_HEADS = 2
RET_WIDTH = N_RET_HEADS * HEAD_DIM
ATT_WIDTH = N_ATT_HEADS * HEAD_DIM
KV_WIDTH = N_KV_HEADS * HEAD_DIM
IN_WIDTH = 4 * RET_WIDTH + ATT_WIDTH + 2 * KV_WIDTH
CHUNK = 128
LANES = 128
SUBLANES = 8
N_META = 16
PAD_FRONT = CHUNK - N_META
ROPE_THETA = 500000.0
ROPE_DIMS = HEAD_DIM // 4
RET_ROPE_THETA = 10000.0
N_EXPERTS = 8
TOP_K = 2
EXPERT_ROW_TILE = 512
MXU_V7X_COLUMNS = 256
EPS = 1e-6
NEG = -1e30
QK_SCALE = HEAD_DIM ** -0.5
LOG2_E = 1.4426950408889634
MIB = 1024 * 1024


def _row_tile(rows, prefs):
    for t in prefs:
        if rows % t == 0:
            return t
    raise ValueError(f"no row tile for {rows} rows among {prefs}")


def _resident(shape):
    return pl.BlockSpec(shape, lambda *_: (0,) * len(shape), pipeline_mode=pl.Buffered(1))


def _rms(x, gain):
    ms = jnp.mean(x * x, axis=-1, keepdims=True)
    return x * lax.rsqrt(ms + EPS) * gain


def _silu(x):
    return x * jax.nn.sigmoid(x)


def _inproj_kernel(h_ref, gain_ref, w_ref, cr_ref, sr_ref, ca_ref, sa_ref, qg_ref, kg_ref, ones_ref,
                   rq_ref, rk_ref, rv_ref, rg_ref, aq_ref, ak_ref, av_ref):
    a = _rms(h_ref[...], gain_ref[...]).astype(BF16)
    tm = a.shape[0]
    lane = lax.broadcasted_iota(I32, (tm, LANES), 1)
    in_head = lane & (HEAD_DIM - 1)
    low_head = lane < HEAD_DIM

    def proj(c0, c1):
        return jnp.dot(a, w_ref[:, c0:c1], preferred_element_type=F32)

    def rotate(x, cos, sin_signed, half):
        partner = jnp.where(in_head < half, pltpu.roll(x, LANES - half, 1), pltpu.roll(x, half, 1))
        return x * cos + partner * sin_signed

    def head_norm(x, g):
        ss = jnp.dot((x * x).astype(BF16), ones_ref[...], preferred_element_type=F32)
        return x * lax.rsqrt(ss * (1.0 / HEAD_DIM) + EPS) * g

    cr, sr, ca, sa = cr_ref[...], sr_ref[...], ca_ref[...], sa_ref[...]
    cols = lambda j: slice(j * LANES, (j + 1) * LANES)
    n_lane_tiles = RET_WIDTH // LANES

    def store_keys(kt_ref, j, x):
        for c in range(tm // CHUNK):
            kt_ref[c, cols(j), :] = x[c * CHUNK:(c + 1) * CHUNK, :].T.astype(BF16)

    p = proj(0, RET_WIDTH)
    for j in range(n_lane_tiles):
        rq_ref[:, cols(j)] = rotate(p[:, cols(j)], cr, sr, HEAD_DIM // 2).astype(BF16)
    p = proj(RET_WIDTH, 2 * RET_WIDTH)
    for j in range(n_lane_tiles):
        store_keys(rk_ref, j, rotate(p[:, cols(j)], cr, sr, HEAD_DIM // 2) * QK_SCALE)
    rv_ref[...] = proj(2 * RET_WIDTH, 3 * RET_WIDTH).astype(BF16)
    rg_ref[...] = proj(3 * RET_WIDTH, 4 * RET_WIDTH).astype(BF16)

    base = 4 * RET_WIDTH
    p = proj(base, base + ATT_WIDTH)
    for j in range(ATT_WIDTH // LANES):
        n = head_norm(p[:, cols(j)], qg_ref[...])
        aq_ref[:, cols(j)] = (rotate(n, ca, sa, ROPE_DIMS // 2) * (QK_SCALE * LOG2_E)).astype(BF16)

    p = proj(base + ATT_WIDTH, IN_WIDTH)
    k = rotate(head_norm(p[:, :KV_WIDTH], kg_ref[...]), ca, sa, ROPE_DIMS // 2)
    v = p[:, KV_WIDTH:]
    k_sw = pltpu.roll(k, HEAD_DIM, 1)
    v_sw = pltpu.roll(v, HEAD_DIM, 1)
    store_keys(ak_ref, 0, jnp.where(low_head, k, k_sw))
    store_keys(ak_ref, 1, jnp.where(low_head, k_sw, k))
    av_ref[:, 0:LANES] = jnp.where(low_head, v, v_sw).astype(BF16)
    av_ref[:, LANES:2 * LANES] = jnp.where(low_head, v_sw, v).astype(BF16)


def _inproj(h, gain, w, tabs, qg, kg, ones_bd):
    rows = h.shape[0]
    tm = _row_tile(rows, (1024, 640, 512, 384, 256, 128))
    row_spec = lambda width: pl.BlockSpec((tm, width), lambda i: (i, 0))
    outs = ((RET_WIDTH, False), (RET_WIDTH, True), (RET_WIDTH, False), (RET_WIDTH, False),
            (ATT_WIDTH, False), (2 * KV_WIDTH, True), (2 * KV_WIDTH, False))
    chunk_spec = lambda width: pl.BlockSpec((tm // CHUNK, width, CHUNK), lambda i: (i, 0, 0))
    return pl.pallas_call(
        _inproj_kernel,
        grid=(rows // tm,),
        in_specs=[row_spec(D_MODEL), _resident((1, D_MODEL)), _resident((D_MODEL, IN_WIDTH)),
                  row_spec(LANES), row_spec(LANES), row_spec(LANES), row_spec(LANES),
                  _resident((1, LANES)), _resident((1, LANES)), _resident((LANES, LANES))],
        out_specs=[chunk_spec(wd) if t else row_spec(wd) for wd, t in outs],
        out_shape=[jax.ShapeDtypeStruct((rows // CHUNK, wd, CHUNK) if t else (rows, wd), BF16) for wd, t in outs],
        compiler_params=pltpu.CompilerParams(dimension_semantics=("parallel",), vmem_limit_bytes=56 * MIB),
        name="inproj",
    )(h, gain, w, *tabs, qg, kg, ones_bd)


def _retention_kernel(q_ref, k_ref, v_ref, g_ref, dec_ref, o_ref, sb_ref, *, n_chunks):
    lgf, lgb = dec_ref[0:1, :], dec_ref[1:2, :]
    gain = dec_ref[6:7, :]
    ri = lax.broadcasted_iota(I32, (CHUNK, LANES), 0)
    ci = lax.broadcasted_iota(I32, (CHUNK, LANES), 1)
    r = ri.astype(F32)
    diff = (ri - ci).astype(F32)

    def decay_mask(lf, lb):
        return jnp.where(diff >= 0, jnp.exp(jnp.maximum(diff, 0.0) * lf), jnp.exp(jnp.maximum(-diff, 0.0) * lb))

    dm = jnp.concatenate([decay_mask(dec_ref[2:3, :], dec_ref[4:5, :]),
                          decay_mask(dec_ref[3:4, :], dec_ref[5:6, :])], axis=1)
    tok = ci.astype(F32)
    wf = jnp.exp((CHUNK - 1.0 - tok) * jnp.broadcast_to(lgf, (LANES, LANES)).T)
    wb = jnp.exp(tok * jnp.broadcast_to(lgb, (LANES, LANES)).T)
    qf = jnp.exp((r + 1.0) * lgf)
    qb = jnp.exp((CHUNK - r) * lgb)
    cf = jnp.exp(float(CHUNK) * lgf)
    cb = jnp.exp(float(CHUNK) * lgb)
    low_c = ci < HEAD_DIM
    same_head = (ri < HEAD_DIM) == low_c
    ones_bd = jnp.where(same_head, 1.0, 0.0).astype(BF16)
    m0 = jnp.where(low_c, 1.0, 0.0).astype(BF16)
    m1 = jnp.where(low_c, 0.0, 1.0).astype(BF16)
    low_r = ri < HEAD_DIM
    rm0 = jnp.where(low_r, 1.0, 0.0).astype(BF16)
    rm1 = jnp.where(low_r, 0.0, 1.0).astype(BF16)

    def chunk_rows(n):
        return pl.ds(pl.multiple_of(n * CHUNK, CHUNK), CHUNK)

    def pair_rows(x):
        return jnp.concatenate([x * m0, x * m1], axis=0)

    def pair_cols(xt):
        return jnp.concatenate([xt * rm0, xt * rm1], axis=1)

    def state_delta(kt, weights, v):
        u = jnp.dot((kt.astype(F32) * weights).astype(BF16), v, preferred_element_type=F32)
        return jnp.where(same_head, u, 0.0)

    group = next(u for u in (4, 3, 5, 2, 1) if n_chunks % u == 0)
    n_groups = n_chunks // group
    members = range(group)
    zero_state = jnp.zeros((LANES, LANES), F32)

    def backward(i, state):
        ns = [n_chunks - 1 - i * group - j for j in members]
        deltas = [state_delta(k_ref[n], wb, v_ref[chunk_rows(n), :]) for n in ns]
        for n, delta in zip(ns, deltas):
            sb_ref[n] = state.astype(BF16)
            state = cb * state + delta
        return state

    lax.fori_loop(0, n_groups, backward, zero_state)

    def forward(i, state):
        ns = [i * group + j for j in members]
        q = [q_ref[chunk_rows(n), :] for n in ns]
        kt = [k_ref[n] for n in ns]
        v = [v_ref[chunk_rows(n), :] for n in ns]
        deltas = [state_delta(kt[j], wf, v[j]) for j in members]
        s = [jnp.dot(q[j], pair_cols(kt[j]), preferred_element_type=F32) * dm for j in members]
        both = []
        for j in members:
            both.append(jnp.concatenate([state.astype(BF16), sb_ref[ns[j]]], axis=1))
            state = cf * state + deltas[j]
        intra = [jnp.dot(s[j].astype(BF16), pair_rows(v[j]), preferred_element_type=F32) for j in members]
        inter = [jnp.dot(q[j], both[j], preferred_element_type=F32) for j in members]
        o = [intra[j] + inter[j][:, :LANES] * qf + inter[j][:, LANES:] * qb for j in members]
        ss = [jnp.dot((o[j] * o[j]).astype(BF16), ones_bd, preferred_element_type=F32) for j in members]
        for j in members:
            gate = _silu(g_ref[chunk_rows(ns[j]), :].astype(F32))
            o_ref[chunk_rows(ns[j]), :] = (o[j] * lax.rsqrt(ss[j] * (1.0 / HEAD_DIM) + EPS) * gain * gate).astype(BF16)
        return state

    lax.fori_loop(0, n_groups, forward, zero_state)


def _retention(rq, rk, rv, rg, dec, batch, seq_rows):
    n_chunks = seq_rows // CHUNK
    n_pairs = RET_WIDTH // LANES
    view = lambda t: t.reshape(batch, seq_rows, RET_WIDTH)
    seq_spec = pl.BlockSpec((None, seq_rows, LANES), lambda b, p: (b, 0, p))
    key_spec = pl.BlockSpec((n_chunks, LANES, CHUNK), lambda b, p: (b, p, 0))
    out = pl.pallas_call(
        functools.partial(_retention_kernel, n_chunks=n_chunks),
        grid=(batch, n_pairs),
        in_specs=[seq_spec, key_spec, seq_spec, seq_spec,
                  pl.BlockSpec((None, 8, LANES), lambda b, p: (p, 0, 0))],
        out_specs=seq_spec,
        out_shape=jax.ShapeDtypeStruct((batch, seq_rows, RET_WIDTH), BF16),
        scratch_shapes=[pltpu.VMEM((n_chunks, LANES, LANES), BF16)],
        compiler_params=pltpu.CompilerParams(dimension_semantics=("parallel", "parallel"),
                                             vmem_limit_bytes=48 * MIB),
        name="retention",
    )(view(rq), rk, view(rv), view(rg), dec)
    return out.reshape(batch * seq_rows, RET_WIDTH)


def _attention_kernel(q_ref, kp_ref, kc_ref, kn_ref, km_ref, vp_ref, vc_ref, vn_ref, vm_ref,
                      sink_ref, gain_ref, o_ref, *, n_chunks, chunks_per_step):
    first_chunk = pl.program_id(1) * chunks_per_step
    ri = lax.broadcasted_iota(I32, (CHUNK, LANES), 0)
    ci = lax.broadcasted_iota(I32, (CHUNK, LANES), 1)
    never = 2 * CHUNK
    to_bias = lambda ok: jnp.where(ok, 0.0, NEG).astype(BF16)
    meta_bias = to_bias(ci >= PAD_FRONT)
    n_keys = 4 * CHUNK
    low_c = ci < HEAD_DIM
    m0 = jnp.where(low_c, 1.0, 0.0).astype(BF16)
    m1 = jnp.where(low_c, 0.0, 1.0).astype(BF16)
    low_r = ri < HEAD_DIM
    rm0 = jnp.where(low_r, 1.0, 0.0).astype(BF16)
    rm1 = jnp.where(low_r, 0.0, 1.0).astype(BF16)
    eye = jnp.where(ri == ci, 1.0, 0.0).astype(BF16)
    row_sums = jnp.concatenate([m0] * 4 + [m1] * 4, axis=0)
    pairs_per_kv = N_ATT_HEADS // N_KV_HEADS // 2
    rows = lambda j: slice(j * CHUNK, (j + 1) * CHUNK)

    def split_values(ref, blk, g):
        x = ref[blk, g * LANES:(g + 1) * LANES]
        return x * m0, x * m1

    def split_keys(ref, idx, g):
        xt = ref[idx, g * LANES:(g + 1) * LANES, :]
        return xt * rm0, xt * rm1

    def blocks(split, at, prev_ref, main_ref, next_ref, meta_ref, g):
        out = [split(prev_ref, at(0), g)]
        out += [split(main_ref, at(j), g) for j in range(chunks_per_step)]
        out += [split(next_ref, at(0), g), split(meta_ref, at(0), g)]
        return out

    k_blocks = [blocks(split_keys, lambda j: j, kp_ref, kc_ref, kn_ref, km_ref, g) for g in range(N_KV_HEADS)]
    v_blocks = [blocks(split_values, rows, vp_ref, vc_ref, vn_ref, vm_ref, g) for g in range(N_KV_HEADS)]

    def pair_blocks(blks, j, axis):
        use = [blks[j], blks[j + 1], blks[j + 2], blks[-1]]
        return jnp.concatenate([b[0] for b in use] + [b[1] for b in use], axis=axis)

    for j in range(chunks_per_step):
        c = first_chunk + j
        prev_off = jnp.where(c >= 2, 0, never)
        cur_off = jnp.where(c >= 1, 0, never)
        next_off = jnp.where(c + 1 <= n_chunks - 1, 0, never)
        bias = jnp.concatenate([to_bias(ci >= ri + prev_off), to_bias(ci >= cur_off),
                                to_bias(ci + next_off <= ri), meta_bias] * 2, axis=1)
        outs = []
        for g in range(N_KV_HEADS):
            k_ext = jnp.concatenate([pair_blocks(k_blocks[g], j, 1), bias], axis=0)
            v_ext = jnp.concatenate([pair_blocks(v_blocks[g], j, 0), row_sums], axis=1)
            pair_ids = [g * pairs_per_kv + pp for pp in range(pairs_per_kv)]
            q_ext = jnp.concatenate(
                [jnp.concatenate([q_ref[rows(j), p * LANES:(p + 1) * LANES], eye], axis=1) for p in pair_ids], axis=0)
            s = jnp.dot(q_ext, k_ext, preferred_element_type=F32)
            probs, sink_rows = [], []
            for pp, p in enumerate(pair_ids):
                halves, sink_terms = [], []
                for hh in range(2):
                    sh = s[rows(pp), hh * n_keys:(hh + 1) * n_keys]
                    sk = sink_ref[2 * p + hh:2 * p + hh + 1, 0:1]
                    m = jnp.maximum(jnp.max(sh, axis=-1, keepdims=True), sk)
                    halves.append(jnp.exp2(sh - m).astype(BF16))
                    sink_terms.append(jnp.exp2(sk - m))
                probs.append(jnp.concatenate(halves, axis=1))
                sink_rows.append(jnp.where(low_c, sink_terms[0], sink_terms[1]))
            ol = jnp.dot(jnp.concatenate(probs, axis=0), v_ext, preferred_element_type=F32)
            for pp in range(pairs_per_kv):
                outs.append(ol[rows(pp), :LANES] / (ol[rows(pp), LANES:] + sink_rows[pp]))
        att = jnp.concatenate(outs, axis=1)
        row = lax.broadcasted_iota(I32, att.shape, 0)
        att = jnp.where(row >= jnp.where(c == 0, PAD_FRONT, 0), att, 0.0)
        o_ref[rows(j), :] = _rms(att, gain_ref[...]).astype(BF16)


def _attention(aq, ak, av, sink, gain, batch, seq_rows):
    n_chunks = seq_rows // CHUNK
    per_step = next(r for r in (3, 5, 4, 2, 1) if n_chunks % r == 0)
    n_steps = n_chunks // per_step
    q3 = aq.reshape(batch, seq_rows, ATT_WIDTH)
    v3 = av.reshape(batch, seq_rows, 2 * KV_WIDTH)
    prev_chunk = lambda s: jnp.maximum(s * per_step - 1, 0)
    next_chunk = lambda s: jnp.minimum((s + 1) * per_step, n_chunks - 1)
    v_one = lambda fn: pl.BlockSpec((None, CHUNK, 2 * KV_WIDTH), lambda b, s: (b, fn(s), 0))
    v_specs = [v_one(prev_chunk), pl.BlockSpec((None, per_step * CHUNK, 2 * KV_WIDTH), lambda b, s: (b, s, 0)),
               v_one(next_chunk), v_one(lambda s: 0)]
    k_one = lambda fn: pl.BlockSpec((1, 2 * KV_WIDTH, CHUNK), lambda b, s: (b * n_chunks + fn(s), 0, 0))
    k_specs = [k_one(prev_chunk),
               pl.BlockSpec((per_step, 2 * KV_WIDTH, CHUNK), lambda b, s: (b * n_steps + s, 0, 0)),
               k_one(next_chunk), k_one(lambda s: 0)]
    q_spec = pl.BlockSpec((None, per_step * CHUNK, ATT_WIDTH), lambda b, s: (b, s, 0))
    out = pl.pallas_call(
        functools.partial(_attention_kernel, n_chunks=n_chunks, chunks_per_step=per_step),
        grid=(batch, n_steps),
        in_specs=[q_spec] + k_specs + v_specs + [_resident((N_ATT_HEADS, LANES)), _resident((1, ATT_WIDTH))],
        out_specs=q_spec,
        out_shape=jax.ShapeDtypeStruct((batch, seq_rows, ATT_WIDTH), BF16),
        compiler_params=pltpu.CompilerParams(dimension_semantics=("parallel", "parallel"),
                                             vmem_limit_bytes=32 * MIB),
        name="attention",
    )(q3, ak, ak, ak, ak, v3, v3, v3, v3, sink, gain)
    return out.reshape(batch * seq_rows, ATT_WIDTH)


def _outproj_kernel(ret_ref, att_ref, w_ref, h_ref, o_ref):
    acc = jnp.dot(ret_ref[...], w_ref[0:RET_WIDTH, :], preferred_element_type=F32)
    acc = acc + jnp.dot(att_ref[...], w_ref[RET_WIDTH:, :], preferred_element_type=F32)
    o_ref[...] = h_ref[...] + acc


def _outproj(ret, att, w, h):
    rows = h.shape[0]
    tm = _row_tile(rows, (1024, 640, 512, 384, 256, 128))
    row_spec = lambda width: pl.BlockSpec((tm, width), lambda i: (i, 0))
    return pl.pallas_call(
        _outproj_kernel,
        grid=(rows // tm,),
        in_specs=[row_spec(RET_WIDTH), row_spec(ATT_WIDTH), _resident((RET_WIDTH + ATT_WIDTH, D_MODEL)),
                  row_spec(D_MODEL)],
        out_specs=row_spec(D_MODEL),
        out_shape=jax.ShapeDtypeStruct((rows, D_MODEL), F32),
        input_output_aliases={3: 0},
        compiler_params=pltpu.CompilerParams(dimension_semantics=("parallel",), vmem_limit_bytes=48 * MIB),
        name="outproj",
    )(ret, att, w, h)


def _ff_halves(d_ff):
    n_mxu_tiles = pl.cdiv(d_ff, MXU_V7X_COLUMNS)
    first = min(d_ff, pl.cdiv(n_mxu_tiles, 2) * MXU_V7X_COLUMNS)
    return tuple((c0, c1) for c0, c1 in ((0, first), (first, d_ff)) if c1 > c0)


def _ffn_kernel(h_ref, gain_ref, wg_ref, wu_ref, wd_ref, o_ref):
    x = h_ref[...]
    f = _rms(x, gain_ref[...]).astype(BF16)
    acc = x
    for c0, c1 in _ff_halves(wg_ref.shape[1]):
        gate = jnp.dot(f, wg_ref[:, c0:c1], preferred_element_type=F32)
        up = jnp.dot(f, wu_ref[:, c0:c1], preferred_element_type=F32)
        act = (_silu(gate) * up).astype(BF16)
        acc = acc + jnp.dot(act, wd_ref[c0:c1, :], preferred_element_type=F32)
    o_ref[...] = acc


def _ffn(h, gain, wg, wu, wd):
    rows = h.shape[0]
    d_ff = wg.shape[1]
    tm = _row_tile(rows, (640, 512, 384, 256, 128))
    row_spec = pl.BlockSpec((tm, D_MODEL), lambda i: (i, 0))
    return pl.pallas_call(
        _ffn_kernel,
        grid=(rows // tm,),
        in_specs=[row_spec, _resident((1, D_MODEL)), _resident((D_MODEL, d_ff)), _resident((D_MODEL, d_ff)),
                  _resident((d_ff, D_MODEL))],
        out_specs=row_spec,
        out_shape=jax.ShapeDtypeStruct((rows, D_MODEL), F32),
        input_output_aliases={0: 0},
        compiler_params=pltpu.CompilerParams(dimension_semantics=("parallel",), vmem_limit_bytes=56 * MIB),
        name="ffn",
    )(h, gain, wg, wu, wd)


ROUTE_G1, ROUTE_G2, ROUTE_D1, ROUTE_D2 = range(4)
TILE_BASE, TILE_COUNT, TILE_START = range(3)


def _lane_pick(x, lane, j):
    return jnp.sum(jnp.where(lane == j, x, 0.0), axis=-1, keepdims=True)


def _lane_scalar(vec, lane_row, e):
    return jnp.sum(jnp.where(lane_row == e, vec, 0.0)).astype(I32)


RUN_ALIGN_BITS = 3
RUN_ALIGN = 1 << RUN_ALIGN_BITS


def _run_piece_count(tm):
    return tm.bit_length() - RUN_ALIGN_BITS


def _run_copies(count, src0, dst0, src_ref, dst_ref, sems):
    copies = []
    for j in range(sems.shape[0]):
        k = j + RUN_ALIGN_BITS
        size = 1 << k
        start = (count >> (k + 1)) << (k + 1)
        src = src_ref.at[pl.ds(pl.multiple_of(src0 + start, RUN_ALIGN), size)]
        dst = dst_ref.at[pl.ds(pl.multiple_of(dst0 + start, RUN_ALIGN), size)]
        copies.append((((count >> k) & 1) == 1, pltpu.make_async_copy(src, dst, sems.at[j])))
    return copies


def _start_then_wait(copies):
    for cond, cp in copies:
        pl.when(cond)(cp.start)
    for cond, cp in copies:
        pl.when(cond)(cp.wait)


def _one_hot_rows(dest, width):
    col = lax.broadcasted_iota(I32, (dest.shape[0], width), 1)
    return jnp.where(col == dest.astype(I32), 1.0, 0.0)


def _router_kernel(*refs, region_rows, finalize, aliased):
    h_ref, gain_ref, whi_ref, wlo_ref, base_ref = refs[:5]
    refs = refs[5 + (1 if aliased else 0):]
    xs_ref, route_ref, tile_ref, cnt_ref, sorted_ref, zero_ref, sems, zsem = refs
    i = pl.program_id(0)

    @pl.when(i == 0)
    def _():
        cnt_ref[...] = base_ref[...]

    f = _rms(h_ref[...], gain_ref[...])
    f_hi = f.astype(BF16)
    f_lo = (f - f_hi.astype(F32)).astype(BF16)
    logits = (jnp.dot(f_hi, whi_ref[...], preferred_element_type=F32)
              + jnp.dot(f_lo, whi_ref[...], preferred_element_type=F32)
              + jnp.dot(f_hi, wlo_ref[...], preferred_element_type=F32))
    tm = logits.shape[0]
    lane = lax.broadcasted_iota(I32, logits.shape, 1)
    lg = jnp.where(lane < N_EXPERTS, logits, -jnp.inf)
    m1 = jnp.max(lg, axis=-1, keepdims=True)
    i1 = jnp.min(jnp.where(lg == m1, lane, LANES), axis=-1, keepdims=True)
    lg2 = jnp.where(lane == i1, -jnp.inf, lg)
    m2 = jnp.max(lg2, axis=-1, keepdims=True)
    i2 = jnp.min(jnp.where(lg2 == m2, lane, LANES), axis=-1, keepdims=True)
    e2 = jnp.exp(m2 - m1)
    g1 = 1.0 / (1.0 + e2)
    g2 = e2 * g1

    sel = jnp.where(lane == i1, 1.0, 0.0) + jnp.where(lane == i2, 1.0, 0.0)
    sel_b = sel.astype(BF16)
    ri = lax.broadcasted_iota(I32, (tm, tm), 0)
    ci = lax.broadcasted_iota(I32, (tm, tm), 1)
    earlier_tokens = jnp.where(ri > ci, 1.0, 0.0).astype(BF16)
    rank = jnp.dot(earlier_tokens, sel_b, preferred_element_type=F32)
    li = lax.broadcasted_iota(I32, (LANES, LANES), 0)
    lj = lax.broadcasted_iota(I32, (LANES, LANES), 1)
    earlier_experts = jnp.where(li < lj, 1.0, 0.0).astype(BF16)
    tiles_per_run = jnp.floor((jnp.sum(sel, axis=0, keepdims=True) + (RUN_ALIGN - 1.0)) * (1.0 / RUN_ALIGN))
    count = tiles_per_run * RUN_ALIGN
    start = RUN_ALIGN * jnp.dot(jnp.broadcast_to(tiles_per_run, (SUBLANES, LANES)).astype(BF16), earlier_experts,
                                preferred_element_type=F32)[0:1, :]
    dest = start + rank
    d1 = _lane_pick(dest, lane, i1)
    d2 = _lane_pick(dest, lane, i2)
    route = jnp.zeros(logits.shape, F32)
    for j, val in ((ROUTE_G1, g1), (ROUTE_G2, g2), (ROUTE_D1, d1), (ROUTE_D2, d2)):
        route = jnp.where(lane == j, val, route)
    route_ref[...] = route

    n_sorted = sorted_ref.shape[0]
    onehot = _one_hot_rows(d1, n_sorted) + _one_hot_rows(d2, n_sorted)
    sorted_ref[...] = jnp.dot(onehot.T.astype(BF16), f_hi, preferred_element_type=F32)

    base = cnt_ref[0:1, :]
    sub = lax.broadcasted_iota(I32, (SUBLANES, LANES), 0)
    tile_ref[...] = jnp.where(sub == TILE_BASE, base, jnp.where(sub == TILE_COUNT, count,
                                                                jnp.where(sub == TILE_START, start, 0.0)))
    lane_row = lane[0:1, :]
    copies = []
    for e in range(N_EXPERTS):
        n_e = _lane_scalar(count, lane_row, e)
        src0 = _lane_scalar(start, lane_row, e)
        dst0 = e * region_rows + _lane_scalar(base, lane_row, e)
        copies += _run_copies(n_e, src0, dst0, sorted_ref, xs_ref, sems.at[e])
    _start_then_wait(copies)
    cnt_ref[...] = cnt_ref[...] + count

    if finalize:
        @pl.when(i == pl.num_programs(0) - 1)
        def _():
            zero_ref[...] = jnp.zeros(zero_ref.shape, F32)
            total = cnt_ref[0:1, :]
            for e in range(N_EXPERTS):
                end = pl.multiple_of(e * region_rows + _lane_scalar(total, lane_row, e), RUN_ALIGN)
                cp = pltpu.make_async_copy(zero_ref, xs_ref.at[pl.ds(end, EXPERT_ROW_TILE)], zsem)
                cp.start()
                cp.wait()


def _router_tile(rows):
    return _row_tile(rows, (512, 640, 384, 256, 128))


def _sorted_rows(tm):
    return TOP_K * tm + pl.cdiv(N_EXPERTS * (RUN_ALIGN - 1), LANES) * LANES


def _router(h, gain, w_hi, w_lo, base_counts, xs, region_rows, finalize):
    rows = h.shape[0]
    tm = _router_tile(rows)
    n_tiles = rows // tm
    aliased = xs is not None
    row_spec = lambda width: pl.BlockSpec((tm, width), lambda i: (i, 0))
    in_specs = [row_spec(D_MODEL), _resident((1, D_MODEL)), _resident((D_MODEL, LANES)),
                _resident((D_MODEL, LANES)), _resident((SUBLANES, LANES))]
    args = [h, gain, w_hi, w_lo, base_counts]
    if aliased:
        in_specs.append(pl.BlockSpec(memory_space=pl.ANY))
        args.append(xs)
    n_pieces = _run_piece_count(tm)
    return pl.pallas_call(
        functools.partial(_router_kernel, region_rows=region_rows, finalize=finalize, aliased=aliased),
        grid=(n_tiles,),
        in_specs=in_specs,
        out_specs=[pl.BlockSpec(memory_space=pl.ANY), row_spec(LANES),
                   pl.BlockSpec((None, SUBLANES, LANES), lambda i: (i, 0, 0)),
                   pl.BlockSpec((SUBLANES, LANES), lambda i: (0, 0))],
        out_shape=[jax.ShapeDtypeStruct((N_EXPERTS * region_rows, D_MODEL), F32),
                   jax.ShapeDtypeStruct((rows, LANES), F32),
                   jax.ShapeDtypeStruct((n_tiles, SUBLANES, LANES), F32),
                   jax.ShapeDtypeStruct((SUBLANES, LANES), F32)],
        scratch_shapes=[pltpu.VMEM((_sorted_rows(tm), D_MODEL), F32), pltpu.VMEM((EXPERT_ROW_TILE, D_MODEL), F32),
                        pltpu.SemaphoreType.DMA((N_EXPERTS, n_pieces)), pltpu.SemaphoreType.DMA(())],
        input_output_aliases={5: 0} if aliased else {},
        compiler_params=pltpu.CompilerParams(dimension_semantics=("arbitrary",), vmem_limit_bytes=48 * MIB),
        name="router",
    )(*args)


def _experts_kernel(blk_ref, exp_ref, nvalid_ref, x_ref, wg_ref, wu_ref, wd_ref, y_ref):
    i, c = pl.program_id(0), pl.program_id(1)
    valid = i < nvalid_ref[0]

    @pl.when(valid)
    def _():
        x = x_ref[...].astype(BF16)
        gate = jnp.dot(x, wg_ref[...], preferred_element_type=F32)
        up = jnp.dot(x, wu_ref[...], preferred_element_type=F32)
        part = jnp.dot((_silu(gate) * up).astype(BF16), wd_ref[...], preferred_element_type=F32)

        @pl.when(c == 0)
        def _():
            y_ref[...] = part

        @pl.when(c > 0)
        def _():
            y_ref[...] += part

    @pl.when(jnp.logical_and(jnp.logical_not(valid), c == 0))
    def _():
        y_ref[...] = jnp.zeros(y_ref.shape, F32)


def _experts(xs, tile_block, tile_expert, n_valid, wg, wu, wd):
    n_tiles = tile_block.shape[0]
    d_ff = wg.shape[2]
    ff_chunk = d_ff // 2
    assert ff_chunk % MXU_V7X_COLUMNS == 0
    n_ff = d_ff // ff_chunk
    tg = EXPERT_ROW_TILE
    ff_idx = lambda i, c, nv: jnp.where(i < nv[0], c, n_ff - 1)
    grid_spec = pltpu.PrefetchScalarGridSpec(
        num_scalar_prefetch=3,
        grid=(n_tiles, n_ff),
        in_specs=[pl.BlockSpec((tg, D_MODEL), lambda i, c, blk, ex, nv: (blk[i], 0)),
                  pl.BlockSpec((None, D_MODEL, ff_chunk), lambda i, c, blk, ex, nv: (ex[i], 0, ff_idx(i, c, nv))),
                  pl.BlockSpec((None, D_MODEL, ff_chunk), lambda i, c, blk, ex, nv: (ex[i], 0, ff_idx(i, c, nv))),
                  pl.BlockSpec((None, ff_chunk, D_MODEL), lambda i, c, blk, ex, nv: (ex[i], ff_idx(i, c, nv), 0))],
        out_specs=pl.BlockSpec((tg, D_MODEL), lambda i, c, blk, ex, nv: (i, 0)),
    )
    return pl.pallas_call(
        _experts_kernel,
        grid_spec=grid_spec,
        out_shape=jax.ShapeDtypeStruct((n_tiles * tg, D_MODEL), F32),
        compiler_params=pltpu.CompilerParams(dimension_semantics=("arbitrary", "arbitrary"),
                                             vmem_limit_bytes=56 * MIB),
        name="experts",
    )(tile_block, tile_expert, n_valid, xs, wg, wu, wd)


def _combine_kernel(tinfo_ref, yoff_ref, h_ref, route_ref, ys_ref, o_ref, sorted_ref, sems):
    i = pl.program_id(0)
    n_sorted = sorted_ref.shape[0]

    @pl.when(i == 0)
    def _():
        sorted_ref[...] = jnp.zeros(sorted_ref.shape, F32)

    copies = []
    for e in range(N_EXPERTS):
        rec = (i * 3) * N_EXPERTS + e
        base = tinfo_ref[rec + TILE_BASE * N_EXPERTS]
        n_e = tinfo_ref[rec + TILE_COUNT * N_EXPERTS]
        dst0 = tinfo_ref[rec + TILE_START * N_EXPERTS]
        copies += _run_copies(n_e, yoff_ref[e] + base, dst0, ys_ref, sorted_ref, sems.at[e])
    _start_then_wait(copies)
    y = sorted_ref[...].astype(BF16)
    route = route_ref[...]
    lane = lax.broadcasted_iota(I32, route.shape, 1)
    pick = lambda j: _lane_pick(route, lane, j)
    y1 = jnp.dot(_one_hot_rows(pick(ROUTE_D1), n_sorted).astype(BF16), y, preferred_element_type=F32)
    y2 = jnp.dot(_one_hot_rows(pick(ROUTE_D2), n_sorted).astype(BF16), y, preferred_element_type=F32)
    o_ref[...] = h_ref[...] + pick(ROUTE_G1) * y1 + pick(ROUTE_G2) * y2


def _combine(h, route, tile_info, y_offsets, ys):
    rows = h.shape[0]
    tm = rows // tile_info.shape[0]
    n_pieces = _run_piece_count(tm)
    tinfo = tile_info[:, :3, :N_EXPERTS].astype(I32).reshape(-1)
    grid_spec = pltpu.PrefetchScalarGridSpec(
        num_scalar_prefetch=2,
        grid=(rows // tm,),
        in_specs=[pl.BlockSpec((tm, D_MODEL), lambda i, *_: (i, 0)),
                  pl.BlockSpec((tm, LANES), lambda i, *_: (i, 0)),
                  pl.BlockSpec(memory_space=pl.ANY)],
        out_specs=pl.BlockSpec((tm, D_MODEL), lambda i, *_: (i, 0)),
        scratch_shapes=[pltpu.VMEM((_sorted_rows(tm), D_MODEL), F32),
                        pltpu.SemaphoreType.DMA((N_EXPERTS, n_pieces))],
    )
    return pl.pallas_call(
        _combine_kernel,
        grid_spec=grid_spec,
        out_shape=jax.ShapeDtypeStruct((rows, D_MODEL), F32),
        input_output_aliases={2: 0},
        compiler_params=pltpu.CompilerParams(dimension_semantics=("arbitrary",), vmem_limit_bytes=48 * MIB),
        name="combine",
    )(tinfo, y_offsets, h, route, ys)


def _routed_experts(hs, gain, router_w, expert_w):
    tg = EXPERT_ROW_TILE
    total_rows = sum(h.shape[0] for h in hs)
    run_pad = (RUN_ALIGN - 1) * sum(h.shape[0] // _router_tile(h.shape[0]) for h in hs)
    region_rows = (pl.cdiv(total_rows + run_pad, tg) + 1) * tg
    counts = jnp.zeros((SUBLANES, LANES), F32)
    xs, routes, tiles = None, [], []
    for g, h in enumerate(hs):
        xs, route, tile_info, counts = _router(h, gain, *router_w, counts, xs, region_rows,
                                               finalize=(g == len(hs) - 1))
        routes.append(route)
        tiles.append(tile_info)
    n_e = counts[0, :N_EXPERTS].astype(I32)
    tiles_e = (n_e + tg - 1) // tg
    first_tile = jnp.cumsum(tiles_e) - tiles_e
    n_valid = jnp.sum(tiles_e)
    n_tiles = (TOP_K * total_rows + N_EXPERTS * (run_pad + tg - 1)) // tg + 1
    t = jnp.minimum(jnp.arange(n_tiles, dtype=I32), n_valid - 1)
    tile_expert = jnp.sum((t[:, None] >= (first_tile + tiles_e)[None, :]).astype(I32), axis=1)
    tile_block = tile_expert * (region_rows // tg) + (t - first_tile[tile_expert])
    ys = _experts(xs, tile_block.astype(I32), tile_expert.astype(I32), n_valid.reshape(1).astype(I32), *expert_w)
    y_offsets = (first_tile * tg).astype(I32)
    return [_combine(h, route, tile_info, y_offsets, ys) for h, route, tile_info in zip(hs, routes, tiles)]


def _rotary_tables(batch, seq_rows):
    pos = (jnp.arange(seq_rows, dtype=I32) - PAD_FRONT).astype(F32)

    def cos_sin(n_rot, theta):
        half = n_rot // 2
        inv = theta ** (-jnp.arange(half, dtype=F32) * 2.0 / n_rot)
        ang = pos[:, None] * inv[None, :]
        return jnp.cos(ang), jnp.sin(ang)

    cos, sin = cos_sin(HEAD_DIM, RET_ROPE_THETA)
    cr = jnp.tile(cos, (1, 2 * LANES // HEAD_DIM))
    sr = jnp.tile(jnp.concatenate([-sin, sin], axis=1), (1, LANES // HEAD_DIM))
    cos, sin = cos_sin(ROPE_DIMS, ROPE_THETA)
    rest = HEAD_DIM - ROPE_DIMS
    ca = jnp.tile(jnp.concatenate([cos, cos, jnp.ones((seq_rows, rest), F32)], axis=1), (1, LANES // HEAD_DIM))
    sa = jnp.tile(jnp.concatenate([-sin, sin, jnp.zeros((seq_rows, rest), F32)], axis=1), (1, LANES // HEAD_DIM))
    return tuple(jnp.tile(t, (batch, 1)) for t in (cr, sr, ca, sa))


def _retention_params(log_gf, log_gb, gain):
    n_pairs = RET_WIDTH // LANES
    per_lane = lambda t: jnp.repeat(t.astype(F32).reshape(n_pairs, 2), HEAD_DIM, axis=1)
    per_head = lambda t, j: jnp.broadcast_to(t.astype(F32).reshape(n_pairs, 2)[:, j:j + 1], (n_pairs, LANES))
    rows = [per_lane(log_gf), per_lane(log_gb), per_head(log_gf, 0), per_head(log_gf, 1),
            per_head(log_gb, 0), per_head(log_gb, 1), gain.astype(F32).reshape(n_pairs, LANES),
            jnp.zeros((n_pairs, LANES), F32)]
    return jnp.stack(rows, axis=1)


def _to_padded_rows(x, meta_tokens):
    batch, seq, _ = x.shape
    h = jnp.concatenate([jnp.zeros((batch, PAD_FRONT, D_MODEL), F32),
                         jnp.broadcast_to(meta_tokens.astype(F32)[None], (batch, N_META, D_MODEL)),
                         x.astype(F32)], axis=1)
    return h.reshape(batch * (seq + CHUNK), D_MODEL)


def _token_mixer(h, lp, tabs, batch, seq_rows):
    rq, rk, rv, rg, aq, ak, av = _inproj(h, lp["norm_mix"], lp["w_in"], tabs, lp["q_gain"], lp["k_gain"],
                                         lp["ones_bd"])
    ret = _retention(rq, rk, rv, rg, lp["ret_dec"], batch, seq_rows)
    att = _attention(aq, ak, av, lp["sink"], lp["att_gain"], batch, seq_rows)
    return _outproj(ret, att, lp["w_out"], h)


def kernel(x_prompt, x_sample, meta_tokens, norm_mix, w_in, ret_log_decay_fwd, ret_log_decay_bwd, ret_out_gain,
           q_norm_gain, k_norm_gain, attn_sink, attn_out_gain, w_out, norm_ffn, ffn_w_gate, ffn_w_up, ffn_w_down,
           moe_router, moe_w_gate, moe_w_up, moe_w_down):
    depth = w_in.shape[0]
    ri = lax.broadcasted_iota(I32, (LANES, LANES), 0)
    ci = lax.broadcasted_iota(I32, (LANES, LANES), 1)
    ones_bd = ((ri < HEAD_DIM) == (ci < HEAD_DIM)).astype(BF16)
    row = lambda t: t.astype(F32).reshape(1, -1)

    xs = (x_prompt, x_sample)
    shapes = [(x.shape[0], x.shape[1] + CHUNK) for x in xs]
    hs = [_to_padded_rows(x, meta_tokens) for x in xs]
    tabs = [_rotary_tables(b, r) for b, r in shapes]

    for l in range(depth):
        lp = {
            "norm_mix": row(norm_mix[l]),
            "w_in": w_in[l].astype(BF16),
            "q_gain": jnp.tile(row(q_norm_gain[l]), (1, LANES // HEAD_DIM)),
            "k_gain": jnp.tile(row(k_norm_gain[l]), (1, LANES // HEAD_DIM)),
            "ones_bd": ones_bd,
            "ret_dec": _retention_params(ret_log_decay_fwd[l], ret_log_decay_bwd[l], ret_out_gain[l]),
            "sink": jnp.broadcast_to(attn_sink[l].astype(F32)[:, None] * LOG2_E, (N_ATT_HEADS, LANES)),
            "att_gain": row(attn_out_gain[l]),
            "w_out": w_out[l].astype(BF16),
        }
        hs = [_token_mixer(h, lp, tab, b, r) for h, tab, (b, r) in zip(hs, tabs, shapes)]
        i = l // 2
        if l % 2 == 0:
            w = (ffn_w_gate[i].astype(BF16), ffn_w_up[i].astype(BF16), ffn_w_down[i].astype(BF16))
            hs = [_ffn(h, row(norm_ffn[l]), *w) for h in hs]
        else:
            wr = jnp.pad(moe_router[i].astype(F32), ((0, 0), (0, LANES - N_EXPERTS)))
            wr_hi = wr.astype(BF16)
            router_w = (wr_hi, (wr - wr_hi.astype(F32)).astype(BF16))
            expert_w = (moe_w_gate[i].astype(BF16), moe_w_up[i].astype(BF16), moe_w_down[i].astype(BF16))
            hs = _routed_experts(hs, row(norm_ffn[l]), router_w, expert_w)
    return tuple(h.reshape(b, r, D_MODEL)[:, CHUNK:].astype(x.dtype) for h, x, (b, r) in zip(hs, xs, shapes))
```

```python
import functools

import jax
import jax.numpy as jnp
from jax import lax
from jax.experimental import pallas as pl
from jax.experimental.pallas import tpu as pltpu

F32 = jnp.float32
BF16 = jnp.bfloat16
I32 = jnp.int32

D_MODEL = 1024
HEAD_DIM = 64
N_RET_HEADS = 8
N_ATT_HEADS = 8
N_KV_HEADS = 2
RET_WIDTH = N_RET_HEADS * HEAD_DIM
ATT_WIDTH = N_ATT_HEADS * HEAD_DIM
KV_WIDTH = N_KV_HEADS * HEAD_DIM
IN_WIDTH = 4 * RET_WIDTH + ATT_WIDTH + 2 * KV_WIDTH
CHUNK = 128
LANES = 128
SUBLANES = 8
N_META = 16
PAD_FRONT = CHUNK - N_META
ROPE_THETA = 500000.0
ROPE_DIMS = HEAD_DIM // 4
RET_ROPE_THETA = 10000.0
N_EXPERTS = 8
TOP_K = 2
EXPERT_ROW_TILE = 512
MXU_V7X_COLUMNS = 256
EPS = 1e-6
NEG = -1e30
QK_SCALE = HEAD_DIM ** -0.5
LOG2_E = 1.4426950408889634
MIB = 1024 * 1024


def _row_tile(rows, prefs):
    for t in prefs:
        if rows % t == 0:
            return t
    raise ValueError(f"no row tile for {rows} rows among {prefs}")


def _resident(shape):
    return pl.BlockSpec(shape, lambda *_: (0,) * len(shape), pipeline_mode=pl.Buffered(1))


def _rms(x, gain):
    ms = jnp.mean(x * x, axis=-1, keepdims=True)
    return x * lax.rsqrt(ms + EPS) * gain


def _silu(x):
    return x * jax.nn.sigmoid(x)


def _inproj_kernel(h_ref, gain_ref, w_ref, cr_ref, sr_ref, ca_ref, sa_ref, qg_ref, kg_ref, ones_ref,
                   rq_ref, rk_ref, rv_ref, rg_ref, aq_ref, ak_ref, av_ref):
    a = _rms(h_ref[...], gain_ref[...]).astype(BF16)
    tm = a.shape[0]
    lane = lax.broadcasted_iota(I32, (tm, LANES), 1)
    in_head = lane & (HEAD_DIM - 1)
    low_head = lane < HEAD_DIM

    def proj(c0, c1):
        return jnp.dot(a, w_ref[:, c0:c1], preferred_element_type=F32)

    def rotate(x, cos, sin_signed, half):
        partner = jnp.where(in_head < half, pltpu.roll(x, LANES - half, 1), pltpu.roll(x, half, 1))
        return x * cos + partner * sin_signed

    def head_norm(x, g):
        ss = jnp.dot((x * x).astype(BF16), ones_ref[...], preferred_element_type=F32)
        return x * lax.rsqrt(ss * (1.0 / HEAD_DIM) + EPS) * g

    cr, sr, ca, sa = cr_ref[...], sr_ref[...], ca_ref[...], sa_ref[...]
    cols = lambda j: slice(j * LANES, (j + 1) * LANES)
    n_lane_tiles = RET_WIDTH // LANES

    def store_keys(kt_ref, j, x):
        for c in range(tm // CHUNK):
            kt_ref[c, cols(j), :] = x[c * CHUNK:(c + 1) * CHUNK, :].T.astype(BF16)

    p = proj(0, RET_WIDTH)
    for j in range(n_lane_tiles):
        rq_ref[:, cols(j)] = rotate(p[:, cols(j)], cr, sr, HEAD_DIM // 2).astype(BF16)
    p = proj(RET_WIDTH, 2 * RET_WIDTH)
    for j in range(n_lane_tiles):
        store_keys(rk_ref, j, rotate(p[:, cols(j)], cr, sr, HEAD_DIM // 2) * QK_SCALE)
    rv_ref[...] = proj(2 * RET_WIDTH, 3 * RET_WIDTH).astype(BF16)
    rg_ref[...] = proj(3 * RET_WIDTH, 4 * RET_WIDTH).astype(BF16)

    base = 4 * RET_WIDTH
    p = proj(base, base + ATT_WIDTH)
    for j in range(ATT_WIDTH // LANES):
        n = head_norm(p[:, cols(j)], qg_ref[...])
        aq_ref[:, cols(j)] = (rotate(n, ca, sa, ROPE_DIMS // 2) * (QK_SCALE * LOG2_E)).astype(BF16)

    p = proj(base + ATT_WIDTH, IN_WIDTH)
    k = rotate(head_norm(p[:, :KV_WIDTH], kg_ref[...]), ca, sa, ROPE_DIMS // 2)
    v = p[:, KV_WIDTH:]
    k_sw = pltpu.roll(k, HEAD_DIM, 1)
    v_sw = pltpu.roll(v, HEAD_DIM, 1)
    store_keys(ak_ref, 0, jnp.where(low_head, k, k_sw))
    store_keys(ak_ref, 1, jnp.where(low_head, k_sw, k))
    av_ref[:, 0:LANES] = jnp.where(low_head, v, v_sw).astype(BF16)
    av_ref[:, LANES:2 * LANES] = jnp.where(low_head, v_sw, v).astype(BF16)


def _inproj(h, gain, w, tabs, qg, kg, ones_bd):
    rows = h.shape[0]
    tm = _row_tile(rows, (1024, 640, 512, 384, 256, 128))
    row_spec = lambda width: pl.BlockSpec((tm, width), lambda i: (i, 0))
    outs = ((RET_WIDTH, False), (RET_WIDTH, True), (RET_WIDTH, False), (RET_WIDTH, False),
            (ATT_WIDTH, False), (2 * KV_WIDTH, True), (2 * KV_WIDTH, False))
    chunk_spec = lambda width: pl.BlockSpec((tm // CHUNK, width, CHUNK), lambda i: (i, 0, 0))
    return pl.pallas_call(
        _inproj_kernel,
        grid=(rows // tm,),
        in_specs=[row_spec(D_MODEL), _resident((1, D_MODEL)), _resident((D_MODEL, IN_WIDTH)),
                  row_spec(LANES), row_spec(LANES), row_spec(LANES), row_spec(LANES),
                  _resident((1, LANES)), _resident((1, LANES)), _resident((LANES, LANES))],
        out_specs=[chunk_spec(wd) if t else row_spec(wd) for wd, t in outs],
        out_shape=[jax.ShapeDtypeStruct((rows // CHUNK, wd, CHUNK) if t else (rows, wd), BF16) for wd, t in outs],
        compiler_params=pltpu.CompilerParams(dimension_semantics=("parallel",), vmem_limit_bytes=56 * MIB),
        name="inproj",
    )(h, gain, w, *tabs, qg, kg, ones_bd)


def _retention_kernel(q_ref, k_ref, v_ref, g_ref, dec_ref, o_ref, sb_ref, *, n_chunks):
    lgf, lgb = dec_ref[0:1, :], dec_ref[1:2, :]
    gain = dec_ref[6:7, :]
    ri = lax.broadcasted_iota(I32, (CHUNK, LANES), 0)
    ci = lax.broadcasted_iota(I32, (CHUNK, LANES), 1)
    r = ri.astype(F32)
    diff = (ri - ci).astype(F32)

    def decay_mask(lf, lb):
        return jnp.where(diff >= 0, jnp.exp(jnp.maximum(diff, 0.0) * lf), jnp.exp(jnp.maximum(-diff, 0.0) * lb))

    dm = jnp.concatenate([decay_mask(dec_ref[2:3, :], dec_ref[4:5, :]),
                          decay_mask(dec_ref[3:4, :], dec_ref[5:6, :])], axis=1)
    tok = ci.astype(F32)
    wf = jnp.exp((CHUNK - 1.0 - tok) * jnp.broadcast_to(lgf, (LANES, LANES)).T)
    wb = jnp.exp(tok * jnp.broadcast_to(lgb, (LANES, LANES)).T)
    qf = jnp.exp((r + 1.0) * lgf)
    qb = jnp.exp((CHUNK - r) * lgb)
    cf = jnp.exp(float(CHUNK) * lgf)
    cb = jnp.exp(float(CHUNK) * lgb)
    low_c = ci < HEAD_DIM
    same_head = (ri < HEAD_DIM) == low_c
    ones_bd = jnp.where(same_head, 1.0, 0.0).astype(BF16)
    m0 = jnp.where(low_c, 1.0, 0.0).astype(BF16)
    m1 = jnp.where(low_c, 0.0, 1.0).astype(BF16)
    low_r = ri < HEAD_DIM
    rm0 = jnp.where(low_r, 1.0, 0.0).astype(BF16)
    rm1 = jnp.where(low_r, 0.0, 1.0).astype(BF16)

    def chunk_rows(n):
        return pl.ds(pl.multiple_of(n * CHUNK, CHUNK), CHUNK)

    def pair_rows(x):
        return jnp.concatenate([x * m0, x * m1], axis=0)

    def pair_cols(xt):
        return jnp.concatenate([xt * rm0, xt * rm1], axis=1)

    def state_delta(kt, weights, v):
        u = jnp.dot((kt.astype(F32) * weights).astype(BF16), v, preferred_element_type=F32)
        return jnp.where(same_head, u, 0.0)

    group = next(u for u in (4, 3, 5, 2, 1) if n_chunks % u == 0)
    n_groups = n_chunks // group
    members = range(group)
    zero_state = jnp.zeros((LANES, LANES), F32)

    def backward(i, state):
        ns = [n_chunks - 1 - i * group - j for j in members]
        deltas = [state_delta(k_ref[n], wb, v_ref[chunk_rows(n), :]) for n in ns]
        for n, delta in zip(ns, deltas):
            sb_ref[n] = state.astype(BF16)
            state = cb * state + delta
        return state

    lax.fori_loop(0, n_groups, backward, zero_state)

    def forward(i, state):
        ns = [i * group + j for j in members]
        q = [q_ref[chunk_rows(n), :] for n in ns]
        kt = [k_ref[n] for n in ns]
        v = [v_ref[chunk_rows(n), :] for n in ns]
        deltas = [state_delta(kt[j], wf, v[j]) for j in members]
        s = [jnp.dot(q[j], pair_cols(kt[j]), preferred_element_type=F32) * dm for j in members]
        both = []
        for j in members:
            both.append(jnp.concatenate([state.astype(BF16), sb_ref[ns[j]]], axis=1))
            state = cf * state + deltas[j]
        intra = [jnp.dot(s[j].astype(BF16), pair_rows(v[j]), preferred_element_type=F32) for j in members]
        inter = [jnp.dot(q[j], both[j], preferred_element_type=F32) for j in members]
        o = [intra[j] + inter[j][:, :LANES] * qf + inter[j][:, LANES:] * qb for j in members]
        ss = [jnp.dot((o[j] * o[j]).astype(BF16), ones_bd, preferred_element_type=F32) for j in members]
        for j in members:
            gate = _silu(g_ref[chunk_rows(ns[j]), :].astype(F32))
            o_ref[chunk_rows(ns[j]), :] = (o[j] * lax.rsqrt(ss[j] * (1.0 / HEAD_DIM) + EPS) * gain * gate).astype(BF16)
        return state

    lax.fori_loop(0, n_groups, forward, zero_state)


def _retention(rq, rk, rv, rg, dec, batch, seq_rows):
    n_chunks = seq_rows // CHUNK
    n_pairs = RET_WIDTH // LANES
    view = lambda t: t.reshape(batch, seq_rows, RET_WIDTH)
    seq_spec = pl.BlockSpec((None, seq_rows, LANES), lambda b, p: (b, 0, p))
    key_spec = pl.BlockSpec((n_chunks, LANES, CHUNK), lambda b, p: (b, p, 0))
    out = pl.pallas_call(
        functools.partial(_retention_kernel, n_chunks=n_chunks),
        grid=(batch, n_pairs),
        in_specs=[seq_spec, key_spec, seq_spec, seq_spec,
                  pl.BlockSpec((None, 8, LANES), lambda b, p: (p, 0, 0))],
        out_specs=seq_spec,
        out_shape=jax.ShapeDtypeStruct((batch, seq_rows, RET_WIDTH), BF16),
        scratch_shapes=[pltpu.VMEM((n_chunks, LANES, LANES), BF16)],
        compiler_params=pltpu.CompilerParams(dimension_semantics=("parallel", "parallel"),
                                             vmem_limit_bytes=48 * MIB),
        name="retention",
    )(view(rq), rk, view(rv), view(rg), dec)
    return out.reshape(batch * seq_rows, RET_WIDTH)


def _attention_kernel(q_ref, kp_ref, kc_ref, kn_ref, km_ref, vp_ref, vc_ref, vn_ref, vm_ref,
                      sink_ref, gain_ref, o_ref, *, n_chunks, chunks_per_step):
    first_chunk = pl.program_id(1) * chunks_per_step
    ri = lax.broadcasted_iota(I32, (CHUNK, LANES), 0)
    ci = lax.broadcasted_iota(I32, (CHUNK, LANES), 1)
    never = 2 * CHUNK
    to_bias = lambda ok: jnp.where(ok, 0.0, NEG).astype(BF16)
    meta_bias = to_bias(ci >= PAD_FRONT)
    n_keys = 4 * CHUNK
    low_c = ci < HEAD_DIM
    m0 = jnp.where(low_c, 1.0, 0.0).astype(BF16)
    m1 = jnp.where(low_c, 0.0, 1.0).astype(BF16)
    low_r = ri < HEAD_DIM
    rm0 = jnp.where(low_r, 1.0, 0.0).astype(BF16)
    rm1 = jnp.where(low_r, 0.0, 1.0).astype(BF16)
    eye = jnp.where(ri == ci, 1.0, 0.0).astype(BF16)
    row_sums = jnp.concatenate([m0] * 4 + [m1] * 4, axis=0)
    pairs_per_kv = N_ATT_HEADS // N_KV_HEADS // 2
    rows = lambda j: slice(j * CHUNK, (j + 1) * CHUNK)

    def split_values(ref, blk, g):
        x = ref[blk, g * LANES:(g + 1) * LANES]
        return x * m0, x * m1

    def split_keys(ref, idx, g):
        xt = ref[idx, g * LANES:(g + 1) * LANES, :]
        return xt * rm0, xt * rm1

    def blocks(split, at, prev_ref, main_ref, next_ref, meta_ref, g):
        out = [split(prev_ref, at(0), g)]
        out += [split(main_ref, at(j), g) for j in range(chunks_per_step)]
        out += [split(next_ref, at(0), g), split(meta_ref, at(0), g)]
        return out

    k_blocks = [blocks(split_keys, lambda j: j, kp_ref, kc_ref, kn_ref, km_ref, g) for g in range(N_KV_HEADS)]
    v_blocks = [blocks(split_values, rows, vp_ref, vc_ref, vn_ref, vm_ref, g) for g in range(N_KV_HEADS)]

    def pair_blocks(blks, j, axis):
        use = [blks[j], blks[j + 1], blks[j + 2], blks[-1]]
        return jnp.concatenate([b[0] for b in use] + [b[1] for b in use], axis=axis)

    for j in range(chunks_per_step):
        c = first_chunk + j
        prev_off = jnp.where(c >= 2, 0, never)
        cur_off = jnp.where(c >= 1, 0, never)
        next_off = jnp.where(c + 1 <= n_chunks - 1, 0, never)
        bias = jnp.concatenate([to_bias(ci >= ri + prev_off), to_bias(ci >= cur_off),
                                to_bias(ci + next_off <= ri), meta_bias] * 2, axis=1)
        outs = []
        for g in range(N_KV_HEADS):
            k_ext = jnp.concatenate([pair_blocks(k_blocks[g], j, 1), bias], axis=0)
            v_ext = jnp.concatenate([pair_blocks(v_blocks[g], j, 0), row_sums], axis=1)
            pair_ids = [g * pairs_per_kv + pp for pp in range(pairs_per_kv)]
            q_ext = jnp.concatenate(
                [jnp.concatenate([q_ref[rows(j), p * LANES:(p + 1) * LANES], eye], axis=1) for p in pair_ids], axis=0)
            s = jnp.dot(q_ext, k_ext, preferred_element_type=F32)
            probs, sink_rows = [], []
            for pp, p in enumerate(pair_ids):
                halves, sink_terms = [], []
                for hh in range(2):
                    sh = s[rows(pp), hh * n_keys:(hh + 1) * n_keys]
                    sk = sink_ref[2 * p + hh:2 * p + hh + 1, 0:1]
                    m = jnp.maximum(jnp.max(sh, axis=-1, keepdims=True), sk)
                    halves.append(jnp.exp2(sh - m).astype(BF16))
                    sink_terms.append(jnp.exp2(sk - m))
                probs.append(jnp.concatenate(halves, axis=1))
                sink_rows.append(jnp.where(low_c, sink_terms[0], sink_terms[1]))
            ol = jnp.dot(jnp.concatenate(probs, axis=0), v_ext, preferred_element_type=F32)
            for pp in range(pairs_per_kv):
                outs.append(ol[rows(pp), :LANES] / (ol[rows(pp), LANES:] + sink_rows[pp]))
        att = jnp.concatenate(outs, axis=1)
        row = lax.broadcasted_iota(I32, att.shape, 0)
        att = jnp.where(row >= jnp.where(c == 0, PAD_FRONT, 0), att, 0.0)
        o_ref[rows(j), :] = _rms(att, gain_ref[...]).astype(BF16)


def _attention(aq, ak, av, sink, gain, batch, seq_rows):
    n_chunks = seq_rows // CHUNK
    per_step = next(r for r in (3, 5, 4, 2, 1) if n_chunks % r == 0)
    n_steps = n_chunks // per_step
    q3 = aq.reshape(batch, seq_rows, ATT_WIDTH)
    v3 = av.reshape(batch, seq_rows, 2 * KV_WIDTH)
    prev_chunk = lambda s: jnp.maximum(s * per_step - 1, 0)
    next_chunk = lambda s: jnp.minimum((s + 1) * per_step, n_chunks - 1)
    v_one = lambda fn: pl.BlockSpec((None, CHUNK, 2 * KV_WIDTH), lambda b, s: (b, fn(s), 0))
    v_specs = [v_one(prev_chunk), pl.BlockSpec((None, per_step * CHUNK, 2 * KV_WIDTH), lambda b, s: (b, s, 0)),
               v_one(next_chunk), v_one(lambda s: 0)]
    k_one = lambda fn: pl.BlockSpec((1, 2 * KV_WIDTH, CHUNK), lambda b, s: (b * n_chunks + fn(s), 0, 0))
    k_specs = [k_one(prev_chunk),
               pl.BlockSpec((per_step, 2 * KV_WIDTH, CHUNK), lambda b, s: (b * n_steps + s, 0, 0)),
               k_one(next_chunk), k_one(lambda s: 0)]
    q_spec = pl.BlockSpec((None, per_step * CHUNK, ATT_WIDTH), lambda b, s: (b, s, 0))
    out = pl.pallas_call(
        functools.partial(_attention_kernel, n_chunks=n_chunks, chunks_per_step=per_step),
        grid=(batch, n_steps),
        in_specs=[q_spec] + k_specs + v_specs + [_resident((N_ATT_HEADS, LANES)), _resident((1, ATT_WIDTH))],
        out_specs=q_spec,
        out_shape=jax.ShapeDtypeStruct((batch, seq_rows, ATT_WIDTH), BF16),
        compiler_params=pltpu.CompilerParams(dimension_semantics=("parallel", "parallel"),
                                             vmem_limit_bytes=32 * MIB),
        name="attention",
    )(q3, ak, ak, ak, ak, v3, v3, v3, v3, sink, gain)
    return out.reshape(batch * seq_rows, ATT_WIDTH)


def _mixed_residual(ret_ref, att_ref, w_ref, h_ref):
    acc = jnp.dot(ret_ref[...], w_ref[0:RET_WIDTH, :], preferred_element_type=F32)
    acc = acc + jnp.dot(att_ref[...], w_ref[RET_WIDTH:, :], preferred_element_type=F32)
    return h_ref[...] + acc


def _mix_specs(tm):
    row_spec = lambda width: pl.BlockSpec((tm, width), lambda i, *_: (i, 0))
    return [row_spec(RET_WIDTH), row_spec(ATT_WIDTH), _resident((RET_WIDTH + ATT_WIDTH, D_MODEL)),
            row_spec(D_MODEL)]


def _ff_halves(d_ff):
    n_mxu_tiles = pl.cdiv(d_ff, MXU_V7X_COLUMNS)
    first = min(d_ff, pl.cdiv(n_mxu_tiles, 2) * MXU_V7X_COLUMNS)
    return tuple((c0, c1) for c0, c1 in ((0, first), (first, d_ff)) if c1 > c0)


def _ffn_kernel(ret_ref, att_ref, wo_ref, h_ref, gain_ref, wg_ref, wu_ref, wd_ref, o_ref):
    x = _mixed_residual(ret_ref, att_ref, wo_ref, h_ref)
    f = _rms(x, gain_ref[...]).astype(BF16)
    acc = x
    for c0, c1 in _ff_halves(wg_ref.shape[1]):
        gate = jnp.dot(f, wg_ref[:, c0:c1], preferred_element_type=F32)
        up = jnp.dot(f, wu_ref[:, c0:c1], preferred_element_type=F32)
        act = (_silu(gate) * up).astype(BF16)
        acc = acc + jnp.dot(act, wd_ref[c0:c1, :], preferred_element_type=F32)
    o_ref[...] = acc


def _ffn(ret, att, w_out, h, gain, wg, wu, wd):
    rows = h.shape[0]
    d_ff = wg.shape[1]
    tm = _row_tile(rows, (640, 512, 384, 256, 128))
    return pl.pallas_call(
        _ffn_kernel,
        grid=(rows // tm,),
        in_specs=_mix_specs(tm) + [_resident((1, D_MODEL)), _resident((D_MODEL, d_ff)), _resident((D_MODEL, d_ff)),
                                   _resident((d_ff, D_MODEL))],
        out_specs=pl.BlockSpec((tm, D_MODEL), lambda i: (i, 0)),
        out_shape=jax.ShapeDtypeStruct((rows, D_MODEL), F32),
        input_output_aliases={3: 0},
        compiler_params=pltpu.CompilerParams(dimension_semantics=("parallel",), vmem_limit_bytes=56 * MIB),
        name="ffn",
    )(ret, att, w_out, h, gain, wg, wu, wd)


ROUTE_G1, ROUTE_G2, ROUTE_D1, ROUTE_D2 = range(4)
TILE_BASE, TILE_COUNT, TILE_START = range(3)


def _lane_pick(x, lane, j):
    return jnp.sum(jnp.where(lane == j, x, 0.0), axis=-1, keepdims=True)


def _lane_scalar(vec, lane_row, e):
    return jnp.sum(jnp.where(lane_row == e, vec, 0.0)).astype(I32)


RUN_ALIGN_BITS = 3
RUN_ALIGN = 1 << RUN_ALIGN_BITS


def _run_piece_count(tm):
    return tm.bit_length() - RUN_ALIGN_BITS


def _run_copies(count, src0, dst0, src_ref, dst_ref, sems):
    copies = []
    for j in range(sems.shape[0]):
        k = j + RUN_ALIGN_BITS
        size = 1 << k
        start = (count >> (k + 1)) << (k + 1)
        src = src_ref.at[pl.ds(pl.multiple_of(src0 + start, RUN_ALIGN), size)]
        dst = dst_ref.at[pl.ds(pl.multiple_of(dst0 + start, RUN_ALIGN), size)]
        copies.append((((count >> k) & 1) == 1, pltpu.make_async_copy(src, dst, sems.at[j])))
    return copies


def _start_then_wait(copies):
    for cond, cp in copies:
        pl.when(cond)(cp.start)
    for cond, cp in copies:
        pl.when(cond)(cp.wait)


def _one_hot_rows(dest, width):
    col = lax.broadcasted_iota(I32, (dest.shape[0], width), 1)
    return jnp.where(col == dest.astype(I32), 1.0, 0.0)


def _router_kernel(*refs, region_rows, finalize, aliased):
    ret_ref, att_ref, wo_ref, h_ref, gain_ref, wr_ref, base_ref = refs[:7]
    refs = refs[7 + (1 if aliased else 0):]
    hout_ref, xs_ref, route_ref, tile_ref, cnt_ref, sorted_ref, zero_ref, sems, zsem = refs
    i = pl.program_id(0)

    @pl.when(i == 0)
    def _():
        cnt_ref[...] = base_ref[...]

    h_mixed = _mixed_residual(ret_ref, att_ref, wo_ref, h_ref)
    hout_ref[...] = h_mixed
    f = _rms(h_mixed, gain_ref[...])
    f_hi = f.astype(BF16)
    f_lo = (f - f_hi.astype(F32)).astype(BF16)
    both = (jnp.dot(f_hi, wr_ref[...], preferred_element_type=F32)
            + jnp.dot(f_lo, wr_ref[...], preferred_element_type=F32))
    logits = both[:, :LANES] + both[:, LANES:]
    tm = logits.shape[0]
    lane = lax.broadcasted_iota(I32, logits.shape, 1)
    lg = jnp.where(lane < N_EXPERTS, logits, -jnp.inf)
    m1 = jnp.max(lg, axis=-1, keepdims=True)
    i1 = jnp.min(jnp.where(lg == m1, lane, LANES), axis=-1, keepdims=True)
    lg2 = jnp.where(lane == i1, -jnp.inf, lg)
    m2 = jnp.max(lg2, axis=-1, keepdims=True)
    i2 = jnp.min(jnp.where(lg2 == m2, lane, LANES), axis=-1, keepdims=True)
    e2 = jnp.exp(m2 - m1)
    g1 = 1.0 / (1.0 + e2)
    g2 = e2 * g1

    sel = jnp.where(lane == i1, 1.0, 0.0) + jnp.where(lane == i2, 1.0, 0.0)
    sel_b = sel.astype(BF16)
    ri = lax.broadcasted_iota(I32, (tm, tm), 0)
    ci = lax.broadcasted_iota(I32, (tm, tm), 1)
    earlier_tokens = jnp.where(ri > ci, 1.0, 0.0).astype(BF16)
    rank = jnp.dot(earlier_tokens, sel_b, preferred_element_type=F32)
    li = lax.broadcasted_iota(I32, (LANES, LANES), 0)
    lj = lax.broadcasted_iota(I32, (LANES, LANES), 1)
    earlier_experts = jnp.where(li < lj, 1.0, 0.0).astype(BF16)
    tiles_per_run = jnp.floor((jnp.sum(sel, axis=0, keepdims=True) + (RUN_ALIGN - 1.0)) * (1.0 / RUN_ALIGN))
    count = tiles_per_run * RUN_ALIGN
    start = RUN_ALIGN * jnp.dot(jnp.broadcast_to(tiles_per_run, (SUBLANES, LANES)).astype(BF16), earlier_experts,
                                preferred_element_type=F32)[0:1, :]
    dest = start + rank
    d1 = _lane_pick(dest, lane, i1)
    d2 = _lane_pick(dest, lane, i2)
    route = jnp.zeros(logits.shape, F32)
    for j, val in ((ROUTE_G1, g1), (ROUTE_G2, g2), (ROUTE_D1, d1), (ROUTE_D2, d2)):
        route = jnp.where(lane == j, val, route)
    route_ref[...] = route

    n_sorted = sorted_ref.shape[0]

    def as_row(col):
        wide = jnp.broadcast_to(col, (tm, LANES))
        return jnp.concatenate([wide[c * LANES:(c + 1) * LANES, :].T for c in range(tm // LANES)], axis=1)[0:1, :]

    slot = lax.broadcasted_iota(I32, (n_sorted, tm), 0)
    onehot_t = (jnp.where(slot == as_row(d1).astype(I32), 1.0, 0.0)
                + jnp.where(slot == as_row(d2).astype(I32), 1.0, 0.0))
    sorted_ref[...] = jnp.dot(onehot_t.astype(BF16), f_hi, preferred_element_type=F32)

    base = cnt_ref[0:1, :]
    sub = lax.broadcasted_iota(I32, (SUBLANES, LANES), 0)
    tile_ref[...] = jnp.where(sub == TILE_BASE, base, jnp.where(sub == TILE_COUNT, count,
                                                                jnp.where(sub == TILE_START, start, 0.0)))
    lane_row = lane[0:1, :]
    copies = []
    for e in range(N_EXPERTS):
        n_e = _lane_scalar(count, lane_row, e)
        src0 = _lane_scalar(start, lane_row, e)
        dst0 = e * region_rows + _lane_scalar(base, lane_row, e)
        copies += _run_copies(n_e, src0, dst0, sorted_ref, xs_ref, sems.at[e])
    _start_then_wait(copies)
    cnt_ref[...] = cnt_ref[...] + count

    if finalize:
        @pl.when(i == pl.num_programs(0) - 1)
        def _():
            zero_ref[...] = jnp.zeros(zero_ref.shape, F32)
            total = cnt_ref[0:1, :]
            for e in range(N_EXPERTS):
                end = pl.multiple_of(e * region_rows + _lane_scalar(total, lane_row, e), RUN_ALIGN)
                cp = pltpu.make_async_copy(zero_ref, xs_ref.at[pl.ds(end, EXPERT_ROW_TILE)], zsem)
                cp.start()
                cp.wait()


def _router_tile(rows):
    return _row_tile(rows, (512, 640, 384, 256, 128))


def _sorted_rows(tm):
    return TOP_K * tm + pl.cdiv(N_EXPERTS * (RUN_ALIGN - 1), LANES) * LANES


def _router(mix, gain, w_router, base_counts, xs, region_rows, finalize):
    rows = mix[3].shape[0]
    tm = _router_tile(rows)
    n_tiles = rows // tm
    aliased = xs is not None
    row_spec = lambda width: pl.BlockSpec((tm, width), lambda i: (i, 0))
    in_specs = _mix_specs(tm) + [_resident((1, D_MODEL)), _resident((D_MODEL, 2 * LANES)),
                                 _resident((SUBLANES, LANES))]
    args = [*mix, gain, w_router, base_counts]
    aliases = {3: 0}
    if aliased:
        aliases[len(args)] = 1
        in_specs.append(pl.BlockSpec(memory_space=pl.ANY))
        args.append(xs)
    n_pieces = _run_piece_count(tm)
    return pl.pallas_call(
        functools.partial(_router_kernel, region_rows=region_rows, finalize=finalize, aliased=aliased),
        grid=(n_tiles,),
        in_specs=in_specs,
        out_specs=[row_spec(D_MODEL), pl.BlockSpec(memory_space=pl.ANY), row_spec(LANES),
                   pl.BlockSpec((None, SUBLANES, LANES), lambda i: (i, 0, 0)),
                   pl.BlockSpec((SUBLANES, LANES), lambda i: (0, 0))],
        out_shape=[jax.ShapeDtypeStruct((rows, D_MODEL), F32),
                   jax.ShapeDtypeStruct((N_EXPERTS * region_rows, D_MODEL), F32),
                   jax.ShapeDtypeStruct((rows, LANES), F32),
                   jax.ShapeDtypeStruct((n_tiles, SUBLANES, LANES), F32),
                   jax.ShapeDtypeStruct((SUBLANES, LANES), F32)],
        scratch_shapes=[pltpu.VMEM((_sorted_rows(tm), D_MODEL), F32), pltpu.VMEM((EXPERT_ROW_TILE, D_MODEL), F32),
                        pltpu.SemaphoreType.DMA((N_EXPERTS, n_pieces)), pltpu.SemaphoreType.DMA(())],
        input_output_aliases=aliases,
        compiler_params=pltpu.CompilerParams(dimension_semantics=("arbitrary",), vmem_limit_bytes=48 * MIB),
        name="router",
    )(*args)


def _experts_kernel(blk_ref, exp_ref, nvalid_ref, x_ref, wg_ref, wu_ref, wd_ref, *y_refs):
    y_ref = y_refs[-1]
    partial_ref = y_refs[0] if len(y_refs) == 2 else None
    valid = pl.program_id(0) < nvalid_ref[0]

    @pl.when(valid)
    def _():
        x = x_ref[...].astype(BF16)
        gate = jnp.dot(x, wg_ref[...], preferred_element_type=F32)
        up = jnp.dot(x, wu_ref[...], preferred_element_type=F32)
        part = jnp.dot((_silu(gate) * up).astype(BF16), wd_ref[...], preferred_element_type=F32)
        y_ref[...] = part if partial_ref is None else partial_ref[...] + part

    @pl.when(jnp.logical_not(valid))
    def _():
        y_ref[...] = jnp.zeros(y_ref.shape, F32) if partial_ref is None else partial_ref[...]


def _experts(xs, tile_block, tile_expert, n_valid, wg, wu, wd):
    n_tiles = tile_block.shape[0]
    d_ff = wg.shape[2]
    ff_chunk = d_ff // 2
    assert ff_chunk % MXU_V7X_COLUMNS == 0
    tg = EXPERT_ROW_TILE
    y = None
    for c in range(d_ff // ff_chunk):
        in_specs = [pl.BlockSpec((tg, D_MODEL), lambda i, blk, ex, nv: (blk[i], 0)),
                    pl.BlockSpec((None, D_MODEL, ff_chunk), lambda i, blk, ex, nv, c=c: (ex[i], 0, c)),
                    pl.BlockSpec((None, D_MODEL, ff_chunk), lambda i, blk, ex, nv, c=c: (ex[i], 0, c)),
                    pl.BlockSpec((None, ff_chunk, D_MODEL), lambda i, blk, ex, nv, c=c: (ex[i], c, 0))]
        args = [tile_block, tile_expert, n_valid, xs, wg, wu, wd]
        y_spec = pl.BlockSpec((tg, D_MODEL), lambda i, blk, ex, nv: (i, 0))
        if y is not None:
            in_specs.append(y_spec)
            args.append(y)
        y = pl.pallas_call(
            _experts_kernel,
            grid_spec=pltpu.PrefetchScalarGridSpec(num_scalar_prefetch=3, grid=(n_tiles,), in_specs=in_specs,
                                                   out_specs=y_spec),
            out_shape=jax.ShapeDtypeStruct((n_tiles * tg, D_MODEL), F32),
            input_output_aliases={len(args) - 1: 0} if len(args) == 8 else {},
            compiler_params=pltpu.CompilerParams(dimension_semantics=("arbitrary",), vmem_limit_bytes=56 * MIB),
            name="experts",
        )(*args)
    return y


def _combine_kernel(tinfo_ref, yoff_ref, h_ref, route_ref, ys_ref, o_ref, sorted_ref, sems):
    i = pl.program_id(0)
    n_sorted = sorted_ref.shape[0]

    @pl.when(i == 0)
    def _():
        sorted_ref[...] = jnp.zeros(sorted_ref.shape, F32)

    copies = []
    for e in range(N_EXPERTS):
        rec = (i * 3) * N_EXPERTS + e
        base = tinfo_ref[rec + TILE_BASE * N_EXPERTS]
        n_e = tinfo_ref[rec + TILE_COUNT * N_EXPERTS]
        dst0 = tinfo_ref[rec + TILE_START * N_EXPERTS]
        copies += _run_copies(n_e, yoff_ref[e] + base, dst0, ys_ref, sorted_ref, sems.at[e])
    _start_then_wait(copies)
    y = sorted_ref[...].astype(BF16)
    route = route_ref[...]
    lane = lax.broadcasted_iota(I32, route.shape, 1)
    pick = lambda j: _lane_pick(route, lane, j)
    y1 = jnp.dot(_one_hot_rows(pick(ROUTE_D1), n_sorted).astype(BF16), y, preferred_element_type=F32)
    y2 = jnp.dot(_one_hot_rows(pick(ROUTE_D2), n_sorted).astype(BF16), y, preferred_element_type=F32)
    o_ref[...] = h_ref[...] + pick(ROUTE_G1) * y1 + pick(ROUTE_G2) * y2


def _combine(h, route, tile_info, y_offsets, ys):
    rows = h.shape[0]
    tm = rows // tile_info.shape[0]
    n_pieces = _run_piece_count(tm)
    tinfo = tile_info[:, :3, :N_EXPERTS].astype(I32).reshape(-1)
    grid_spec = pltpu.PrefetchScalarGridSpec(
        num_scalar_prefetch=2,
        grid=(rows // tm,),
        in_specs=[pl.BlockSpec((tm, D_MODEL), lambda i, *_: (i, 0)),
                  pl.BlockSpec((tm, LANES), lambda i, *_: (i, 0)),
                  pl.BlockSpec(memory_space=pl.ANY)],
        out_specs=pl.BlockSpec((tm, D_MODEL), lambda i, *_: (i, 0)),
        scratch_shapes=[pltpu.VMEM((_sorted_rows(tm), D_MODEL), F32),
                        pltpu.SemaphoreType.DMA((N_EXPERTS, n_pieces))],
    )
    return pl.pallas_call(
        _combine_kernel,
        grid_spec=grid_spec,
        out_shape=jax.ShapeDtypeStruct((rows, D_MODEL), F32),
        input_output_aliases={2: 0},
        compiler_params=pltpu.CompilerParams(dimension_semantics=("arbitrary",), vmem_limit_bytes=48 * MIB),
        name="combine",
    )(tinfo, y_offsets, h, route, ys)


def _routed_experts(mixes, gain, router_w, expert_w):
    tg = EXPERT_ROW_TILE
    group_rows = [mix[3].shape[0] for mix in mixes]
    total_rows = sum(group_rows)
    run_pad = (RUN_ALIGN - 1) * sum(rows // _router_tile(rows) for rows in group_rows)
    region_rows = (pl.cdiv(total_rows + run_pad, tg) + 1) * tg
    counts = jnp.zeros((SUBLANES, LANES), F32)
    xs, hs, routes, tiles = None, [], [], []
    for g, mix in enumerate(mixes):
        h, xs, route, tile_info, counts = _router(mix, gain, router_w, counts, xs, region_rows,
                                                  finalize=(g == len(mixes) - 1))
        hs.append(h)
        routes.append(route)
        tiles.append(tile_info)
    n_e = counts[0, :N_EXPERTS].astype(I32)
    tiles_e = (n_e + tg - 1) // tg
    first_tile = jnp.cumsum(tiles_e) - tiles_e
    n_valid = jnp.sum(tiles_e)
    n_tiles = (TOP_K * total_rows + N_EXPERTS * (run_pad + tg - 1)) // tg + 1
    t = jnp.minimum(jnp.arange(n_tiles, dtype=I32), n_valid - 1)
    tile_expert = jnp.sum((t[:, None] >= (first_tile + tiles_e)[None, :]).astype(I32), axis=1)
    tile_block = tile_expert * (region_rows // tg) + (t - first_tile[tile_expert])
    ys = _experts(xs, tile_block.astype(I32), tile_expert.astype(I32), n_valid.reshape(1).astype(I32), *expert_w)
    y_offsets = (first_tile * tg).astype(I32)
    return [_combine(h, route, tile_info, y_offsets, ys) for h, route, tile_info in zip(hs, routes, tiles)]


def _rotary_tables(batch, seq_rows):
    pos = (jnp.arange(seq_rows, dtype=I32) - PAD_FRONT).astype(F32)

    def cos_sin(n_rot, theta):
        half = n_rot // 2
        inv = theta ** (-jnp.arange(half, dtype=F32) * 2.0 / n_rot)
        ang = pos[:, None] * inv[None, :]
        return jnp.cos(ang), jnp.sin(ang)

    cos, sin = cos_sin(HEAD_DIM, RET_ROPE_THETA)
    cr = jnp.tile(cos, (1, 2 * LANES // HEAD_DIM))
    sr = jnp.tile(jnp.concatenate([-sin, sin], axis=1), (1, LANES // HEAD_DIM))
    cos, sin = cos_sin(ROPE_DIMS, ROPE_THETA)
    rest = HEAD_DIM - ROPE_DIMS
    ca = jnp.tile(jnp.concatenate([cos, cos, jnp.ones((seq_rows, rest), F32)], axis=1), (1, LANES // HEAD_DIM))
    sa = jnp.tile(jnp.concatenate([-sin, sin, jnp.zeros((seq_rows, rest), F32)], axis=1), (1, LANES // HEAD_DIM))
    return tuple(jnp.tile(t, (batch, 1)) for t in (cr, sr, ca, sa))


def _retention_params(log_gf, log_gb, gain):
    n_pairs = RET_WIDTH // LANES
    per_lane = lambda t: jnp.repeat(t.astype(F32).reshape(n_pairs, 2), HEAD_DIM, axis=1)
    per_head = lambda t, j: jnp.broadcast_to(t.astype(F32).reshape(n_pairs, 2)[:, j:j + 1], (n_pairs, LANES))
    rows = [per_lane(log_gf), per_lane(log_gb), per_head(log_gf, 0), per_head(log_gf, 1),
            per_head(log_gb, 0), per_head(log_gb, 1), gain.astype(F32).reshape(n_pairs, LANES),
            jnp.zeros((n_pairs, LANES), F32)]
    return jnp.stack(rows, axis=1)


def _to_padded_rows(x, meta_tokens):
    batch, seq, _ = x.shape
    h = jnp.concatenate([jnp.zeros((batch, PAD_FRONT, D_MODEL), F32),
                         jnp.broadcast_to(meta_tokens.astype(F32)[None], (batch, N_META, D_MODEL)),
                         x.astype(F32)], axis=1)
    return h.reshape(batch * (seq + CHUNK), D_MODEL)


def _token_mixer(h, lp, tabs, batch, seq_rows):
    rq, rk, rv, rg, aq, ak, av = _inproj(h, lp["norm_mix"], lp["w_in"], tabs, lp["q_gain"], lp["k_gain"],
                                         lp["ones_bd"])
    ret = _retention(rq, rk, rv, rg, lp["ret_dec"], batch, seq_rows)
    att = _attention(aq, ak, av, lp["sink"], lp["att_gain"], batch, seq_rows)
    return ret, att, lp["w_out"], h


def kernel(x_prompt, x_sample, meta_tokens, norm_mix, w_in, ret_log_decay_fwd, ret_log_decay_bwd, ret_out_gain,
           q_norm_gain, k_norm_gain, attn_sink, attn_out_gain, w_out, norm_ffn, ffn_w_gate, ffn_w_up, ffn_w_down,
           moe_router, moe_w_gate, moe_w_up, moe_w_down):
    depth = w_in.shape[0]
    ri = lax.broadcasted_iota(I32, (LANES, LANES), 0)
    ci = lax.broadcasted_iota(I32, (LANES, LANES), 1)
    ones_bd = ((ri < HEAD_DIM) == (ci < HEAD_DIM)).astype(BF16)
    row = lambda t: t.astype(F32).reshape(1, -1)

    xs = (x_prompt, x_sample)
    shapes = [(x.shape[0], x.shape[1] + CHUNK) for x in xs]
    hs = [_to_padded_rows(x, meta_tokens) for x in xs]
    tabs = [_rotary_tables(b, r) for b, r in shapes]

    for l in range(depth):
        lp = {
            "norm_mix": row(norm_mix[l]),
            "w_in": w_in[l].astype(BF16),
            "q_gain": jnp.tile(row(q_norm_gain[l]), (1, LANES // HEAD_DIM)),
            "k_gain": jnp.tile(row(k_norm_gain[l]), (1, LANES // HEAD_DIM)),
            "ones_bd": ones_bd,
            "ret_dec": _retention_params(ret_log_decay_fwd[l], ret_log_decay_bwd[l], ret_out_gain[l]),
            "sink": jnp.broadcast_to(attn_sink[l].astype(F32)[:, None] * LOG2_E, (N_ATT_HEADS, LANES)),
            "att_gain": row(attn_out_gain[l]),
            "w_out": w_out[l].astype(BF16),
        }
        mixes = [_token_mixer(h, lp, tab, b, r) for h, tab, (b, r) in zip(hs, tabs, shapes)]
        i = l // 2
        if l % 2 == 0:
            w = (ffn_w_gate[i].astype(BF16), ffn_w_up[i].astype(BF16), ffn_w_down[i].astype(BF16))
            hs = [_ffn(*mix, row(norm_ffn[l]), *w) for mix in mixes]
        else:
            wr = jnp.pad(moe_router[i].astype(F32), ((0, 0), (0, LANES - N_EXPERTS)))
            wr_hi = wr.astype(BF16)
            router_w = jnp.concatenate([wr_hi, (wr - wr_hi.astype(F32)).astype(BF16)], axis=1)
            expert_w = (moe_w_gate[i].astype(BF16), moe_w_up[i].astype(BF16), moe_w_down[i].astype(BF16))
            hs = _routed_experts(mixes, row(norm_ffn[l]), router_w, expert_w)
    return tuple(h.reshape(b, r, D_MODEL)[:, CHUNK:].astype(x.dtype) for h, x, (b, r) in zip(hs, xs, shapes))
```

```python
import functools

import jax
import jax.numpy as jnp
from jax import lax
from jax.experimental import pallas as pl
from jax.experimental.pallas import tpu as pltpu

F32 = jnp.float32
BF16 = jnp.bfloat16
I32 = jnp.int32

D_MODEL = 1024
HEAD_DIM = 64
N_RET_HEADS = 8
N_ATT_HEADS = 8
N_KV_HEADS = 2
RET_WIDTH = N_RET_HEADS * HEAD_DIM
ATT_WIDTH = N_ATT_HEADS * HEAD_DIM
KV_WIDTH = N_KV_HEADS * HEAD_DIM
IN_WIDTH = 4 * RET_WIDTH + ATT_WIDTH + 2 * KV_WIDTH
CHUNK = 128
LANES = 128
SUBLANES = 8
N_META = 16
PAD_FRONT = CHUNK - N_META
ROPE_THETA = 500000.0
ROPE_DIMS = HEAD_DIM // 4
RET_ROPE_THETA = 10000.0
N_EXPERTS = 8
TOP_K = 2
EXPERT_ROW_TILE = 512
MXU_V7X_COLUMNS = 256
EPS = 1e-6
NEG = -1e30
QK_SCALE = HEAD_DIM ** -0.5
LOG2_E = 1.4426950408889634
MIB = 1024 * 1024


def _row_tile(rows, prefs):
    for t in prefs:
        if rows % t == 0:
            return t
    raise ValueError(f"no row tile for {rows} rows among {prefs}")


def _resident(shape):
    return pl.BlockSpec(shape, lambda *_: (0,) * len(shape), pipeline_mode=pl.Buffered(1))


def _rms(x, gain):
    ms = jnp.mean(x * x, axis=-1, keepdims=True)
    return x * lax.rsqrt(ms + EPS) * gain


def _silu(x):
    return x * jax.nn.sigmoid(x)


def _inproj_kernel(h_ref, gain_ref, w_ref, cr_ref, sr_ref, ca_ref, sa_ref, qg_ref, kg_ref, ones_ref,
                   rq_ref, rk_ref, rv_ref, rg_ref, aq_ref, ak_ref, av_ref):
    a = _rms(h_ref[...], gain_ref[...]).astype(BF16)
    tm = a.shape[0]
    lane = lax.broadcasted_iota(I32, (tm, LANES), 1)
    in_head = lane & (HEAD_DIM - 1)
    low_head = lane < HEAD_DIM

    def proj(c0, c1):
        return jnp.dot(a, w_ref[:, c0:c1], preferred_element_type=F32)

    def rotate(x, cos, sin_signed, half):
        partner = jnp.where(in_head < half, pltpu.roll(x, LANES - half, 1), pltpu.roll(x, half, 1))
        return x * cos + partner * sin_signed

    def head_norm(x, g):
        ss = jnp.dot((x * x).astype(BF16), ones_ref[...], preferred_element_type=F32)
        return x * lax.rsqrt(ss * (1.0 / HEAD_DIM) + EPS) * g

    cr, sr, ca, sa = cr_ref[...], sr_ref[...], ca_ref[...], sa_ref[...]
    cols = lambda j: slice(j * LANES, (j + 1) * LANES)
    n_lane_tiles = RET_WIDTH // LANES

    def store_keys(kt_ref, j, x):
        for c in range(tm // CHUNK):
            kt_ref[c, cols(j), :] = x[c * CHUNK:(c + 1) * CHUNK, :].T.astype(BF16)

    p = proj(0, RET_WIDTH)
    for j in range(n_lane_tiles):
        rq_ref[:, cols(j)] = rotate(p[:, cols(j)], cr, sr, HEAD_DIM // 2).astype(BF16)
    p = proj(RET_WIDTH, 2 * RET_WIDTH)
    for j in range(n_lane_tiles):
        store_keys(rk_ref, j, rotate(p[:, cols(j)], cr, sr, HEAD_DIM // 2) * QK_SCALE)
    rv_ref[...] = proj(2 * RET_WIDTH, 3 * RET_WIDTH).astype(BF16)
    rg_ref[...] = proj(3 * RET_WIDTH, 4 * RET_WIDTH).astype(BF16)

    base = 4 * RET_WIDTH
    p = proj(base, base + ATT_WIDTH)
    for j in range(ATT_WIDTH // LANES):
        n = head_norm(p[:, cols(j)], qg_ref[...])
        aq_ref[:, cols(j)] = (rotate(n, ca, sa, ROPE_DIMS // 2) * (QK_SCALE * LOG2_E)).astype(BF16)

    p = proj(base + ATT_WIDTH, IN_WIDTH)
    k = rotate(head_norm(p[:, :KV_WIDTH], kg_ref[...]), ca, sa, ROPE_DIMS // 2)
    v = p[:, KV_WIDTH:]
    k_sw = pltpu.roll(k, HEAD_DIM, 1)
    v_sw = pltpu.roll(v, HEAD_DIM, 1)
    store_keys(ak_ref, 0, jnp.where(low_head, k, k_sw))
    store_keys(ak_ref, 1, jnp.where(low_head, k_sw, k))
    av_ref[:, 0:LANES] = jnp.where(low_head, v, v_sw).astype(BF16)
    av_ref[:, LANES:2 * LANES] = jnp.where(low_head, v_sw, v).astype(BF16)


def _inproj(h, gain, w, tabs, qg, kg, ones_bd):
    rows = h.shape[0]
    tm = _row_tile(rows, (1024, 640, 512, 384, 256, 128))
    row_spec = lambda width: pl.BlockSpec((tm, width), lambda i: (i, 0))
    outs = ((RET_WIDTH, False), (RET_WIDTH, True), (RET_WIDTH, False), (RET_WIDTH, False),
            (ATT_WIDTH, False), (2 * KV_WIDTH, True), (2 * KV_WIDTH, False))
    chunk_spec = lambda width: pl.BlockSpec((tm // CHUNK, width, CHUNK), lambda i: (i, 0, 0))
    return pl.pallas_call(
        _inproj_kernel,
        grid=(rows // tm,),
        in_specs=[row_spec(D_MODEL), _resident((1, D_MODEL)), _resident((D_MODEL, IN_WIDTH)),
                  row_spec(LANES), row_spec(LANES), row_spec(LANES), row_spec(LANES),
                  _resident((1, LANES)), _resident((1, LANES)), _resident((LANES, LANES))],
        out_specs=[chunk_spec(wd) if t else row_spec(wd) for wd, t in outs],
        out_shape=[jax.ShapeDtypeStruct((rows // CHUNK, wd, CHUNK) if t else (rows, wd), BF16) for wd, t in outs],
        compiler_params=pltpu.CompilerParams(dimension_semantics=("parallel",), vmem_limit_bytes=56 * MIB),
        name="inproj",
    )(h, gain, w, *tabs, qg, kg, ones_bd)


def _retention_kernel(q_ref, k_ref, v_ref, g_ref, dec_ref, o_ref, sb_ref, *, n_chunks):
    lgf, lgb = dec_ref[0:1, :], dec_ref[1:2, :]
    gain = dec_ref[6:7, :]
    ri = lax.broadcasted_iota(I32, (CHUNK, LANES), 0)
    ci = lax.broadcasted_iota(I32, (CHUNK, LANES), 1)
    r = ri.astype(F32)
    diff = (ri - ci).astype(F32)

    def decay_mask(lf, lb):
        return jnp.where(diff >= 0, jnp.exp(jnp.maximum(diff, 0.0) * lf), jnp.exp(jnp.maximum(-diff, 0.0) * lb))

    dm = jnp.concatenate([decay_mask(dec_ref[2:3, :], dec_ref[4:5, :]),
                          decay_mask(dec_ref[3:4, :], dec_ref[5:6, :])], axis=1)
    tok = ci.astype(F32)
    wf = jnp.exp((CHUNK - 1.0 - tok) * jnp.broadcast_to(lgf, (LANES, LANES)).T)
    wb = jnp.exp(tok * jnp.broadcast_to(lgb, (LANES, LANES)).T)
    qf = jnp.exp((r + 1.0) * lgf)
    qb = jnp.exp((CHUNK - r) * lgb)
    cf = jnp.exp(float(CHUNK) * lgf)
    cb = jnp.exp(float(CHUNK) * lgb)
    low_c = ci < HEAD_DIM
    same_head = (ri < HEAD_DIM) == low_c
    ones_bd = jnp.where(same_head, 1.0, 0.0).astype(BF16)
    m0 = jnp.where(low_c, 1.0, 0.0).astype(BF16)
    m1 = jnp.where(low_c, 0.0, 1.0).astype(BF16)
    low_r = ri < HEAD_DIM
    rm0 = jnp.where(low_r, 1.0, 0.0).astype(BF16)
    rm1 = jnp.where(low_r, 0.0, 1.0).astype(BF16)

    def chunk_rows(n):
        return pl.ds(pl.multiple_of(n * CHUNK, CHUNK), CHUNK)

    def pair_rows(x):
        return jnp.concatenate([x * m0, x * m1], axis=0)

    def pair_cols(xt):
        return jnp.concatenate([xt * rm0, xt * rm1], axis=1)

    def state_delta(kt, weights, v):
        u = jnp.dot((kt.astype(F32) * weights).astype(BF16), v, preferred_element_type=F32)
        return jnp.where(same_head, u, 0.0)

    group = next(u for u in (11, 13, 4, 3, 5, 2, 1) if n_chunks % u == 0)
    n_groups = n_chunks // group
    members = range(group)
    zero_state = jnp.zeros((LANES, LANES), F32)

    def backward(i, state):
        ns = [n_chunks - 1 - i * group - j for j in members]
        deltas = [state_delta(k_ref[n], wb, v_ref[chunk_rows(n), :]) for n in ns]
        for n, delta in zip(ns, deltas):
            sb_ref[n] = state.astype(BF16)
            state = cb * state + delta
        return state

    lax.fori_loop(0, n_groups, backward, zero_state)

    def forward(i, state):
        ns = [i * group + j for j in members]
        q = [q_ref[chunk_rows(n), :] for n in ns]
        kt = [k_ref[n] for n in ns]
        v = [v_ref[chunk_rows(n), :] for n in ns]
        deltas = [state_delta(kt[j], wf, v[j]) for j in members]
        s = [jnp.dot(q[j], pair_cols(kt[j]), preferred_element_type=F32) * dm for j in members]
        both = []
        for j in members:
            both.append(jnp.concatenate([state.astype(BF16), sb_ref[ns[j]]], axis=1))
            state = cf * state + deltas[j]
        intra = [jnp.dot(s[j].astype(BF16), pair_rows(v[j]), preferred_element_type=F32) for j in members]
        inter = [jnp.dot(q[j], both[j], preferred_element_type=F32) for j in members]
        o = [intra[j] + inter[j][:, :LANES] * qf + inter[j][:, LANES:] * qb for j in members]
        ss = [jnp.dot((o[j] * o[j]).astype(BF16), ones_bd, preferred_element_type=F32) for j in members]
        for j in members:
            gate = _silu(g_ref[chunk_rows(ns[j]), :].astype(F32))
            o_ref[chunk_rows(ns[j]), :] = (o[j] * lax.rsqrt(ss[j] * (1.0 / HEAD_DIM) + EPS) * gain * gate).astype(BF16)
        return state

    lax.fori_loop(0, n_groups, forward, zero_state)


def _retention(rq, rk, rv, rg, dec, batch, seq_rows):
    n_chunks = seq_rows // CHUNK
    n_pairs = RET_WIDTH // LANES
    view = lambda t: t.reshape(batch, seq_rows, RET_WIDTH)
    seq_spec = pl.BlockSpec((None, seq_rows, LANES), lambda b, p: (b, 0, p))
    key_spec = pl.BlockSpec((n_chunks, LANES, CHUNK), lambda b, p: (b, p, 0))
    out = pl.pallas_call(
        functools.partial(_retention_kernel, n_chunks=n_chunks),
        grid=(batch, n_pairs),
        in_specs=[seq_spec, key_spec, seq_spec, seq_spec,
                  pl.BlockSpec((None, 8, LANES), lambda b, p: (p, 0, 0))],
        out_specs=seq_spec,
        out_shape=jax.ShapeDtypeStruct((batch, seq_rows, RET_WIDTH), BF16),
        scratch_shapes=[pltpu.VMEM((n_chunks, LANES, LANES), BF16)],
        compiler_params=pltpu.CompilerParams(dimension_semantics=("parallel", "parallel"),
                                             vmem_limit_bytes=48 * MIB),
        name="retention",
    )(view(rq), rk, view(rv), view(rg), dec)
    return out.reshape(batch * seq_rows, RET_WIDTH)


def _attention_kernel(q_ref, kp_ref, kc_ref, kn_ref, km_ref, vp_ref, vc_ref, vn_ref, vm_ref,
                      sink_ref, gain_ref, o_ref, *, n_chunks, chunks_per_step):
    first_chunk = pl.program_id(1) * chunks_per_step
    ri = lax.broadcasted_iota(I32, (CHUNK, LANES), 0)
    ci = lax.broadcasted_iota(I32, (CHUNK, LANES), 1)
    never = 2 * CHUNK
    to_bias = lambda ok: jnp.where(ok, 0.0, NEG).astype(BF16)
    meta_bias = to_bias(ci >= PAD_FRONT)
    n_keys = 4 * CHUNK
    low_c = ci < HEAD_DIM
    m0 = jnp.where(low_c, 1.0, 0.0).astype(BF16)
    m1 = jnp.where(low_c, 0.0, 1.0).astype(BF16)
    low_r = ri < HEAD_DIM
    rm0 = jnp.where(low_r, 1.0, 0.0).astype(BF16)
    rm1 = jnp.where(low_r, 0.0, 1.0).astype(BF16)
    eye = jnp.where(ri == ci, 1.0, 0.0).astype(BF16)
    row_sums = jnp.concatenate([m0] * 4 + [m1] * 4, axis=0)
    pairs_per_kv = N_ATT_HEADS // N_KV_HEADS // 2
    rows = lambda j: slice(j * CHUNK, (j + 1) * CHUNK)

    def split_values(ref, blk, g):
        x = ref[blk, g * LANES:(g + 1) * LANES]
        return x * m0, x * m1

    def split_keys(ref, idx, g):
        xt = ref[idx, g * LANES:(g + 1) * LANES, :]
        return xt * rm0, xt * rm1

    def blocks(split, at, prev_ref, main_ref, next_ref, meta_ref, g):
        out = [split(prev_ref, at(0), g)]
        out += [split(main_ref, at(j), g) for j in range(chunks_per_step)]
        out += [split(next_ref, at(0), g), split(meta_ref, at(0), g)]
        return out

    k_blocks = [blocks(split_keys, lambda j: j, kp_ref, kc_ref, kn_ref, km_ref, g) for g in range(N_KV_HEADS)]
    v_blocks = [blocks(split_values, rows, vp_ref, vc_ref, vn_ref, vm_ref, g) for g in range(N_KV_HEADS)]

    def pair_blocks(blks, j, axis):
        use = [blks[j], blks[j + 1], blks[j + 2], blks[-1]]
        return jnp.concatenate([b[0] for b in use] + [b[1] for b in use], axis=axis)

    for j in range(chunks_per_step):
        c = first_chunk + j
        prev_off = jnp.where(c >= 2, 0, never)
        cur_off = jnp.where(c >= 1, 0, never)
        next_off = jnp.where(c + 1 <= n_chunks - 1, 0, never)
        bias = jnp.concatenate([to_bias(ci >= ri + prev_off), to_bias(ci >= cur_off),
                                to_bias(ci + next_off <= ri), meta_bias] * 2, axis=1)
        outs = []
        for g in range(N_KV_HEADS):
            k_ext = jnp.concatenate([pair_blocks(k_blocks[g], j, 1), bias], axis=0)
            v_ext = jnp.concatenate([pair_blocks(v_blocks[g], j, 0), row_sums], axis=1)
            pair_ids = [g * pairs_per_kv + pp for pp in range(pairs_per_kv)]
            q_ext = jnp.concatenate(
                [jnp.concatenate([q_ref[rows(j), p * LANES:(p + 1) * LANES], eye], axis=1) for p in pair_ids], axis=0)
            s = jnp.dot(q_ext, k_ext, preferred_element_type=F32)
            probs, sink_rows = [], []
            for pp, p in enumerate(pair_ids):
                halves, sink_terms = [], []
                for hh in range(2):
                    sh = s[rows(pp), hh * n_keys:(hh + 1) * n_keys]
                    sk = sink_ref[2 * p + hh:2 * p + hh + 1, 0:1]
                    m = jnp.maximum(jnp.max(sh, axis=-1, keepdims=True), sk)
                    halves.append(jnp.exp2(sh - m).astype(BF16))
                    sink_terms.append(jnp.exp2(sk - m))
                probs.append(jnp.concatenate(halves, axis=1))
                sink_rows.append(jnp.where(low_c, sink_terms[0], sink_terms[1]))
            ol = jnp.dot(jnp.concatenate(probs, axis=0), v_ext, preferred_element_type=F32)
            for pp in range(pairs_per_kv):
                outs.append(ol[rows(pp), :LANES] / (ol[rows(pp), LANES:] + sink_rows[pp]))
        att = jnp.concatenate(outs, axis=1)
        row = lax.broadcasted_iota(I32, att.shape, 0)
        att = jnp.where(row >= jnp.where(c == 0, PAD_FRONT, 0), att, 0.0)
        o_ref[rows(j), :] = _rms(att, gain_ref[...]).astype(BF16)


def _attention(aq, ak, av, sink, gain, batch, seq_rows):
    n_chunks = seq_rows // CHUNK
    per_step = next(r for r in (11, 13, 3, 5, 4, 2, 1) if n_chunks % r == 0)
    n_steps = n_chunks // per_step
    q3 = aq.reshape(batch, seq_rows, ATT_WIDTH)
    v3 = av.reshape(batch, seq_rows, 2 * KV_WIDTH)
    prev_chunk = lambda s: jnp.maximum(s * per_step - 1, 0)
    next_chunk = lambda s: jnp.minimum((s + 1) * per_step, n_chunks - 1)
    v_one = lambda fn: pl.BlockSpec((None, CHUNK, 2 * KV_WIDTH), lambda b, s: (b, fn(s), 0))
    v_specs = [v_one(prev_chunk), pl.BlockSpec((None, per_step * CHUNK, 2 * KV_WIDTH), lambda b, s: (b, s, 0)),
               v_one(next_chunk), v_one(lambda s: 0)]
    k_one = lambda fn: pl.BlockSpec((1, 2 * KV_WIDTH, CHUNK), lambda b, s: (b * n_chunks + fn(s), 0, 0))
    k_specs = [k_one(prev_chunk),
               pl.BlockSpec((per_step, 2 * KV_WIDTH, CHUNK), lambda b, s: (b * n_steps + s, 0, 0)),
               k_one(next_chunk), k_one(lambda s: 0)]
    q_spec = pl.BlockSpec((None, per_step * CHUNK, ATT_WIDTH), lambda b, s: (b, s, 0))
    out = pl.pallas_call(
        functools.partial(_attention_kernel, n_chunks=n_chunks, chunks_per_step=per_step),
        grid=(batch, n_steps),
        in_specs=[q_spec] + k_specs + v_specs + [_resident((N_ATT_HEADS, LANES)), _resident((1, ATT_WIDTH))],
        out_specs=q_spec,
        out_shape=jax.ShapeDtypeStruct((batch, seq_rows, ATT_WIDTH), BF16),
        compiler_params=pltpu.CompilerParams(dimension_semantics=("parallel", "parallel"),
                                             vmem_limit_bytes=32 * MIB),
        name="attention",
    )(q3, ak, ak, ak, ak, v3, v3, v3, v3, sink, gain)
    return out.reshape(batch * seq_rows, ATT_WIDTH)


def _mixed_residual(ret_ref, att_ref, w_ref, h_ref):
    acc = jnp.dot(ret_ref[...], w_ref[0:RET_WIDTH, :], preferred_element_type=F32)
    acc = acc + jnp.dot(att_ref[...], w_ref[RET_WIDTH:, :], preferred_element_type=F32)
    return h_ref[...] + acc


def _mix_specs(tm):
    row_spec = lambda width: pl.BlockSpec((tm, width), lambda i, *_: (i, 0))
    return [row_spec(RET_WIDTH), row_spec(ATT_WIDTH), _resident((RET_WIDTH + ATT_WIDTH, D_MODEL)),
            row_spec(D_MODEL)]


def _ff_halves(d_ff):
    n_mxu_tiles = pl.cdiv(d_ff, MXU_V7X_COLUMNS)
    first = min(d_ff, pl.cdiv(n_mxu_tiles, 2) * MXU_V7X_COLUMNS)
    return tuple((c0, c1) for c0, c1 in ((0, first), (first, d_ff)) if c1 > c0)


def _ffn_kernel(ret_ref, att_ref, wo_ref, h_ref, gain_ref, wg_ref, wu_ref, wd_ref, o_ref):
    x = _mixed_residual(ret_ref, att_ref, wo_ref, h_ref)
    f = _rms(x, gain_ref[...]).astype(BF16)
    acc = x
    for c0, c1 in _ff_halves(wg_ref.shape[1]):
        gate = jnp.dot(f, wg_ref[:, c0:c1], preferred_element_type=F32)
        up = jnp.dot(f, wu_ref[:, c0:c1], preferred_element_type=F32)
        act = (_silu(gate) * up).astype(BF16)
        acc = acc + jnp.dot(act, wd_ref[c0:c1, :], preferred_element_type=F32)
    o_ref[...] = acc


def _ffn(ret, att, w_out, h, gain, wg, wu, wd):
    rows = h.shape[0]
    d_ff = wg.shape[1]
    tm = _row_tile(rows, (640, 512, 384, 256, 128))
    return pl.pallas_call(
        _ffn_kernel,
        grid=(rows // tm,),
        in_specs=_mix_specs(tm) + [_resident((1, D_MODEL)), _resident((D_MODEL, d_ff)), _resident((D_MODEL, d_ff)),
                                   _resident((d_ff, D_MODEL))],
        out_specs=pl.BlockSpec((tm, D_MODEL), lambda i: (i, 0)),
        out_shape=jax.ShapeDtypeStruct((rows, D_MODEL), F32),
        input_output_aliases={3: 0},
        compiler_params=pltpu.CompilerParams(dimension_semantics=("parallel",), vmem_limit_bytes=56 * MIB),
        name="ffn",
    )(ret, att, w_out, h, gain, wg, wu, wd)


ROUTE_G1, ROUTE_G2, ROUTE_D1, ROUTE_D2 = range(4)
TILE_BASE, TILE_COUNT, TILE_START = range(3)


def _lane_pick(x, lane, j):
    return jnp.sum(jnp.where(lane == j, x, 0.0), axis=-1, keepdims=True)


def _lane_scalar(vec, lane_row, e):
    return jnp.sum(jnp.where(lane_row == e, vec, 0.0)).astype(I32)


RUN_ALIGN_BITS = 3
RUN_ALIGN = 1 << RUN_ALIGN_BITS


def _run_piece_count(tm):
    return tm.bit_length() - RUN_ALIGN_BITS


def _run_copies(count, src0, dst0, src_ref, dst_ref, sems):
    copies = []
    for j in range(sems.shape[0]):
        k = j + RUN_ALIGN_BITS
        size = 1 << k
        start = (count >> (k + 1)) << (k + 1)
        src = src_ref.at[pl.ds(pl.multiple_of(src0 + start, RUN_ALIGN), size)]
        dst = dst_ref.at[pl.ds(pl.multiple_of(dst0 + start, RUN_ALIGN), size)]
        copies.append((((count >> k) & 1) == 1, pltpu.make_async_copy(src, dst, sems.at[j])))
    return copies


def _start_then_wait(copies):
    for cond, cp in copies:
        pl.when(cond)(cp.start)
    for cond, cp in copies:
        pl.when(cond)(cp.wait)


def _one_hot_rows(dest, width):
    col = lax.broadcasted_iota(I32, (dest.shape[0], width), 1)
    return jnp.where(col == dest.astype(I32), 1.0, 0.0)


def _router_kernel(*refs, region_rows, finalize, aliased):
    ret_ref, att_ref, wo_ref, h_ref, gain_ref, wr_ref, base_ref = refs[:7]
    refs = refs[7 + (1 if aliased else 0):]
    hout_ref, xs_ref, route_ref, tile_ref, cnt_ref, sorted_ref, zero_ref, sems, zsem = refs
    i = pl.program_id(0)

    @pl.when(i == 0)
    def _():
        cnt_ref[...] = base_ref[...]

    h_mixed = _mixed_residual(ret_ref, att_ref, wo_ref, h_ref)
    hout_ref[...] = h_mixed
    f = _rms(h_mixed, gain_ref[...])
    f_hi = f.astype(BF16)
    f_lo = (f - f_hi.astype(F32)).astype(BF16)
    both = (jnp.dot(f_hi, wr_ref[...], preferred_element_type=F32)
            + jnp.dot(f_lo, wr_ref[...], preferred_element_type=F32))
    logits = both[:, :LANES] + both[:, LANES:]
    tm = logits.shape[0]
    lane = lax.broadcasted_iota(I32, logits.shape, 1)
    lg = jnp.where(lane < N_EXPERTS, logits, -jnp.inf)
    m1 = jnp.max(lg, axis=-1, keepdims=True)
    i1 = jnp.min(jnp.where(lg == m1, lane, LANES), axis=-1, keepdims=True)
    lg2 = jnp.where(lane == i1, -jnp.inf, lg)
    m2 = jnp.max(lg2, axis=-1, keepdims=True)
    i2 = jnp.min(jnp.where(lg2 == m2, lane, LANES), axis=-1, keepdims=True)
    e2 = jnp.exp(m2 - m1)
    g1 = 1.0 / (1.0 + e2)
    g2 = e2 * g1

    active = jnp.max(jnp.abs(f), axis=-1, keepdims=True) > 0.0
    sel = jnp.where(active, jnp.where(lane == i1, 1.0, 0.0) + jnp.where(lane == i2, 1.0, 0.0), 0.0)
    sel_b = sel.astype(BF16)
    ri = lax.broadcasted_iota(I32, (tm, tm), 0)
    ci = lax.broadcasted_iota(I32, (tm, tm), 1)
    earlier_tokens = jnp.where(ri > ci, 1.0, 0.0).astype(BF16)
    rank = jnp.dot(earlier_tokens, sel_b, preferred_element_type=F32)
    li = lax.broadcasted_iota(I32, (LANES, LANES), 0)
    lj = lax.broadcasted_iota(I32, (LANES, LANES), 1)
    earlier_experts = jnp.where(li < lj, 1.0, 0.0).astype(BF16)
    tiles_per_run = jnp.floor((jnp.sum(sel, axis=0, keepdims=True) + (RUN_ALIGN - 1.0)) * (1.0 / RUN_ALIGN))
    count = tiles_per_run * RUN_ALIGN
    start = RUN_ALIGN * jnp.dot(jnp.broadcast_to(tiles_per_run, (SUBLANES, LANES)).astype(BF16), earlier_experts,
                                preferred_element_type=F32)[0:1, :]
    dest = start + rank
    no_slot = -1.0
    d1 = jnp.where(active, _lane_pick(dest, lane, i1), no_slot)
    d2 = jnp.where(active, _lane_pick(dest, lane, i2), no_slot)
    route = jnp.zeros(logits.shape, F32)
    for j, val in ((ROUTE_G1, g1), (ROUTE_G2, g2), (ROUTE_D1, d1), (ROUTE_D2, d2)):
        route = jnp.where(lane == j, val, route)
    route_ref[...] = route

    n_sorted = sorted_ref.shape[0]

    def as_row(col):
        wide = jnp.broadcast_to(col, (tm, LANES))
        return jnp.concatenate([wide[c * LANES:(c + 1) * LANES, :].T for c in range(tm // LANES)], axis=1)[0:1, :]

    slot = lax.broadcasted_iota(I32, (n_sorted, tm), 0)
    onehot_t = (jnp.where(slot == as_row(d1).astype(I32), 1.0, 0.0)
                + jnp.where(slot == as_row(d2).astype(I32), 1.0, 0.0))
    sorted_ref[...] = jnp.dot(onehot_t.astype(BF16), f_hi, preferred_element_type=F32)

    base = cnt_ref[0:1, :]
    sub = lax.broadcasted_iota(I32, (SUBLANES, LANES), 0)
    tile_ref[...] = jnp.where(sub == TILE_BASE, base, jnp.where(sub == TILE_COUNT, count,
                                                                jnp.where(sub == TILE_START, start, 0.0)))
    lane_row = lane[0:1, :]
    copies = []
    for e in range(N_EXPERTS):
        n_e = _lane_scalar(count, lane_row, e)
        src0 = _lane_scalar(start, lane_row, e)
        dst0 = e * region_rows + _lane_scalar(base, lane_row, e)
        copies += _run_copies(n_e, src0, dst0, sorted_ref, xs_ref, sems.at[e])
    _start_then_wait(copies)
    cnt_ref[...] = cnt_ref[...] + count

    if finalize:
        @pl.when(i == pl.num_programs(0) - 1)
        def _():
            zero_ref[...] = jnp.zeros(zero_ref.shape, F32)
            total = cnt_ref[0:1, :]
            for e in range(N_EXPERTS):
                end = pl.multiple_of(e * region_rows + _lane_scalar(total, lane_row, e), RUN_ALIGN)
                cp = pltpu.make_async_copy(zero_ref, xs_ref.at[pl.ds(end, EXPERT_ROW_TILE)], zsem)
                cp.start()
                cp.wait()


def _router_tile(rows):
    return _row_tile(rows, (512, 640, 384, 256, 128))


def _sorted_rows(tm):
    return TOP_K * tm + pl.cdiv(N_EXPERTS * (RUN_ALIGN - 1), LANES) * LANES


def _router(mix, gain, w_router, base_counts, xs, region_rows, finalize):
    rows = mix[3].shape[0]
    tm = _router_tile(rows)
    n_tiles = rows // tm
    aliased = xs is not None
    row_spec = lambda width: pl.BlockSpec((tm, width), lambda i: (i, 0))
    in_specs = _mix_specs(tm) + [_resident((1, D_MODEL)), _resident((D_MODEL, 2 * LANES)),
                                 _resident((SUBLANES, LANES))]
    args = [*mix, gain, w_router, base_counts]
    aliases = {3: 0}
    if aliased:
        aliases[len(args)] = 1
        in_specs.append(pl.BlockSpec(memory_space=pl.ANY))
        args.append(xs)
    n_pieces = _run_piece_count(tm)
    return pl.pallas_call(
        functools.partial(_router_kernel, region_rows=region_rows, finalize=finalize, aliased=aliased),
        grid=(n_tiles,),
        in_specs=in_specs,
        out_specs=[row_spec(D_MODEL), pl.BlockSpec(memory_space=pl.ANY), row_spec(LANES),
                   pl.BlockSpec((None, SUBLANES, LANES), lambda i: (i, 0, 0)),
                   pl.BlockSpec((SUBLANES, LANES), lambda i: (0, 0))],
        out_shape=[jax.ShapeDtypeStruct((rows, D_MODEL), F32),
                   jax.ShapeDtypeStruct((N_EXPERTS * region_rows, D_MODEL), F32),
                   jax.ShapeDtypeStruct((rows, LANES), F32),
                   jax.ShapeDtypeStruct((n_tiles, SUBLANES, LANES), F32),
                   jax.ShapeDtypeStruct((SUBLANES, LANES), F32)],
        scratch_shapes=[pltpu.VMEM((_sorted_rows(tm), D_MODEL), F32), pltpu.VMEM((EXPERT_ROW_TILE, D_MODEL), F32),
                        pltpu.SemaphoreType.DMA((N_EXPERTS, n_pieces)), pltpu.SemaphoreType.DMA(())],
        input_output_aliases=aliases,
        compiler_params=pltpu.CompilerParams(dimension_semantics=("arbitrary",), vmem_limit_bytes=48 * MIB),
        name="router",
    )(*args)


def _experts_kernel(blk_ref, exp_ref, nvalid_ref, x_ref, wg_ref, wu_ref, wd_ref, *y_refs):
    y_ref = y_refs[-1]
    partial_ref = y_refs[0] if len(y_refs) == 2 else None
    valid = pl.program_id(0) < nvalid_ref[0]

    @pl.when(valid)
    def _():
        x = x_ref[...].astype(BF16)
        gate = jnp.dot(x, wg_ref[...], preferred_element_type=F32)
        up = jnp.dot(x, wu_ref[...], preferred_element_type=F32)
        part = jnp.dot((_silu(gate) * up).astype(BF16), wd_ref[...], preferred_element_type=F32)
        y_ref[...] = part if partial_ref is None else partial_ref[...] + part

    @pl.when(jnp.logical_not(valid))
    def _():
        y_ref[...] = jnp.zeros(y_ref.shape, F32) if partial_ref is None else partial_ref[...]


def _experts(xs, tile_block, tile_expert, n_valid, wg, wu, wd):
    n_tiles = tile_block.shape[0]
    d_ff = wg.shape[2]
    ff_chunk = d_ff // 2
    assert ff_chunk % MXU_V7X_COLUMNS == 0
    tg = EXPERT_ROW_TILE
    y = None
    for c in range(d_ff // ff_chunk):
        in_specs = [pl.BlockSpec((tg, D_MODEL), lambda i, blk, ex, nv: (blk[i], 0)),
                    pl.BlockSpec((None, D_MODEL, ff_chunk), lambda i, blk, ex, nv, c=c: (ex[i], 0, c)),
                    pl.BlockSpec((None, D_MODEL, ff_chunk), lambda i, blk, ex, nv, c=c: (ex[i], 0, c)),
                    pl.BlockSpec((None, ff_chunk, D_MODEL), lambda i, blk, ex, nv, c=c: (ex[i], c, 0))]
        args = [tile_block, tile_expert, n_valid, xs, wg, wu, wd]
        y_spec = pl.BlockSpec((tg, D_MODEL), lambda i, blk, ex, nv: (i, 0))
        if y is not None:
            in_specs.append(y_spec)
            args.append(y)
        y = pl.pallas_call(
            _experts_kernel,
            grid_spec=pltpu.PrefetchScalarGridSpec(num_scalar_prefetch=3, grid=(n_tiles,), in_specs=in_specs,
                                                   out_specs=y_spec),
            out_shape=jax.ShapeDtypeStruct((n_tiles * tg, D_MODEL), F32),
            input_output_aliases={len(args) - 1: 0} if len(args) == 8 else {},
            compiler_params=pltpu.CompilerParams(dimension_semantics=("arbitrary",), vmem_limit_bytes=56 * MIB),
            name="experts",
        )(*args)
    return y


def _combine_kernel(tinfo_ref, yoff_ref, h_ref, route_ref, ys_ref, o_ref, sorted_ref, sems):
    i = pl.program_id(0)
    n_steps = pl.num_programs(0)
    n_sorted = sorted_ref.shape[1]
    slot = i % 2

    def tile_copies(tile, buf, live):
        copies = []
        for e in range(N_EXPERTS):
            rec = (tile * 3) * N_EXPERTS + e
            base = tinfo_ref[rec + TILE_BASE * N_EXPERTS]
            n_e = jnp.where(live, tinfo_ref[rec + TILE_COUNT * N_EXPERTS], 0)
            dst0 = tinfo_ref[rec + TILE_START * N_EXPERTS]
            copies += _run_copies(n_e, yoff_ref[e] + base, dst0, ys_ref, sorted_ref.at[buf], sems.at[buf, e])
        return copies

    @pl.when(i == 0)
    def _():
        sorted_ref[...] = jnp.zeros(sorted_ref.shape, F32)
        for cond, cp in tile_copies(0, 0, True):
            pl.when(cond)(cp.start)

    for cond, cp in tile_copies(jnp.minimum(i + 1, n_steps - 1), 1 - slot, i + 1 < n_steps):
        pl.when(cond)(cp.start)
    for cond, cp in tile_copies(i, slot, True):
        pl.when(cond)(cp.wait)
    y = sorted_ref[slot].astype(BF16)
    route = route_ref[...]
    lane = lax.broadcasted_iota(I32, route.shape, 1)
    pick = lambda j: _lane_pick(route, lane, j)
    y1 = jnp.dot(_one_hot_rows(pick(ROUTE_D1), n_sorted).astype(BF16), y, preferred_element_type=F32)
    y2 = jnp.dot(_one_hot_rows(pick(ROUTE_D2), n_sorted).astype(BF16), y, preferred_element_type=F32)
    o_ref[...] = h_ref[...] + pick(ROUTE_G1) * y1 + pick(ROUTE_G2) * y2


def _combine(h, route, tile_info, y_offsets, ys):
    rows = h.shape[0]
    tm = rows // tile_info.shape[0]
    n_pieces = _run_piece_count(tm)
    tinfo = tile_info[:, :3, :N_EXPERTS].astype(I32).reshape(-1)
    grid_spec = pltpu.PrefetchScalarGridSpec(
        num_scalar_prefetch=2,
        grid=(rows // tm,),
        in_specs=[pl.BlockSpec((tm, D_MODEL), lambda i, *_: (i, 0)),
                  pl.BlockSpec((tm, LANES), lambda i, *_: (i, 0)),
                  pl.BlockSpec(memory_space=pl.ANY)],
        out_specs=pl.BlockSpec((tm, D_MODEL), lambda i, *_: (i, 0)),
        scratch_shapes=[pltpu.VMEM((2, _sorted_rows(tm), D_MODEL), F32),
                        pltpu.SemaphoreType.DMA((2, N_EXPERTS, n_pieces))],
    )
    return pl.pallas_call(
        _combine_kernel,
        grid_spec=grid_spec,
        out_shape=jax.ShapeDtypeStruct((rows, D_MODEL), F32),
        input_output_aliases={2: 0},
        compiler_params=pltpu.CompilerParams(dimension_semantics=("arbitrary",), vmem_limit_bytes=48 * MIB),
        name="combine",
    )(tinfo, y_offsets, h, route, ys)


def _routed_experts(mixes, gain, router_w, expert_w):
    tg = EXPERT_ROW_TILE
    group_rows = [mix[3].shape[0] for mix in mixes]
    total_rows = sum(group_rows)
    run_pad = (RUN_ALIGN - 1) * sum(rows // _router_tile(rows) for rows in group_rows)
    region_rows = (pl.cdiv(total_rows + run_pad, tg) + 1) * tg
    counts = jnp.zeros((SUBLANES, LANES), F32)
    xs, hs, routes, tiles = None, [], [], []
    for g, mix in enumerate(mixes):
        h, xs, route, tile_info, counts = _router(mix, gain, router_w, counts, xs, region_rows,
                                                  finalize=(g == len(mixes) - 1))
        hs.append(h)
        routes.append(route)
        tiles.append(tile_info)
    n_e = counts[0, :N_EXPERTS].astype(I32)
    tiles_e = (n_e + tg - 1) // tg
    first_tile = jnp.cumsum(tiles_e) - tiles_e
    n_valid = jnp.sum(tiles_e)
    n_tiles = (TOP_K * total_rows + N_EXPERTS * (run_pad + tg - 1)) // tg + 1
    t = jnp.minimum(jnp.arange(n_tiles, dtype=I32), n_valid - 1)
    tile_expert = jnp.sum((t[:, None] >= (first_tile + tiles_e)[None, :]).astype(I32), axis=1)
    tile_block = tile_expert * (region_rows // tg) + (t - first_tile[tile_expert])
    ys = _experts(xs, tile_block.astype(I32), tile_expert.astype(I32), n_valid.reshape(1).astype(I32), *expert_w)
    y_offsets = (first_tile * tg).astype(I32)
    return [_combine(h, route, tile_info, y_offsets, ys) for h, route, tile_info in zip(hs, routes, tiles)]


def _rotary_tables(batch, seq_rows):
    pos = (jnp.arange(seq_rows, dtype=I32) - PAD_FRONT).astype(F32)

    def cos_sin(n_rot, theta):
        half = n_rot // 2
        inv = theta ** (-jnp.arange(half, dtype=F32) * 2.0 / n_rot)
        ang = pos[:, None] * inv[None, :]
        return jnp.cos(ang), jnp.sin(ang)

    cos, sin = cos_sin(HEAD_DIM, RET_ROPE_THETA)
    cr = jnp.tile(cos, (1, 2 * LANES // HEAD_DIM))
    sr = jnp.tile(jnp.concatenate([-sin, sin], axis=1), (1, LANES // HEAD_DIM))
    cos, sin = cos_sin(ROPE_DIMS, ROPE_THETA)
    rest = HEAD_DIM - ROPE_DIMS
    ca = jnp.tile(jnp.concatenate([cos, cos, jnp.ones((seq_rows, rest), F32)], axis=1), (1, LANES // HEAD_DIM))
    sa = jnp.tile(jnp.concatenate([-sin, sin, jnp.zeros((seq_rows, rest), F32)], axis=1), (1, LANES // HEAD_DIM))
    return tuple(jnp.tile(t, (batch, 1)) for t in (cr, sr, ca, sa))


def _retention_params(log_gf, log_gb, gain):
    n_pairs = RET_WIDTH // LANES
    per_lane = lambda t: jnp.repeat(t.astype(F32).reshape(n_pairs, 2), HEAD_DIM, axis=1)
    per_head = lambda t, j: jnp.broadcast_to(t.astype(F32).reshape(n_pairs, 2)[:, j:j + 1], (n_pairs, LANES))
    rows = [per_lane(log_gf), per_lane(log_gb), per_head(log_gf, 0), per_head(log_gf, 1),
            per_head(log_gb, 0), per_head(log_gb, 1), gain.astype(F32).reshape(n_pairs, LANES),
            jnp.zeros((n_pairs, LANES), F32)]
    return jnp.stack(rows, axis=1)


def _to_padded_rows(x, meta_tokens):
    batch, seq, _ = x.shape
    h = jnp.concatenate([jnp.zeros((batch, PAD_FRONT, D_MODEL), F32),
                         jnp.broadcast_to(meta_tokens.astype(F32)[None], (batch, N_META, D_MODEL)),
                         x.astype(F32)], axis=1)
    return h.reshape(batch * (seq + CHUNK), D_MODEL)


def _token_mixer(h, lp, tabs, batch, seq_rows):
    rq, rk, rv, rg, aq, ak, av = _inproj(h, lp["norm_mix"], lp["w_in"], tabs, lp["q_gain"], lp["k_gain"],
                                         lp["ones_bd"])
    ret = _retention(rq, rk, rv, rg, lp["ret_dec"], batch, seq_rows)
    att = _attention(aq, ak, av, lp["sink"], lp["att_gain"], batch, seq_rows)
    return ret, att, lp["w_out"], h


def kernel(x_prompt, x_sample, meta_tokens, norm_mix, w_in, ret_log_decay_fwd, ret_log_decay_bwd, ret_out_gain,
           q_norm_gain, k_norm_gain, attn_sink, attn_out_gain, w_out, norm_ffn, ffn_w_gate, ffn_w_up, ffn_w_down,
           moe_router, moe_w_gate, moe_w_up, moe_w_down):
    depth = w_in.shape[0]
    ri = lax.broadcasted_iota(I32, (LANES, LANES), 0)
    ci = lax.broadcasted_iota(I32, (LANES, LANES), 1)
    ones_bd = ((ri < HEAD_DIM) == (ci < HEAD_DIM)).astype(BF16)
    row = lambda t: t.astype(F32).reshape(1, -1)

    xs = (x_prompt, x_sample)
    shapes = [(x.shape[0], x.shape[1] + CHUNK) for x in xs]
    hs = [_to_padded_rows(x, meta_tokens) for x in xs]
    tabs = [_rotary_tables(b, r) for b, r in shapes]

    for l in range(depth):
        lp = {
            "norm_mix": row(norm_mix[l]),
            "w_in": w_in[l].astype(BF16),
            "q_gain": jnp.tile(row(q_norm_gain[l]), (1, LANES // HEAD_DIM)),
            "k_gain": jnp.tile(row(k_norm_gain[l]), (1, LANES // HEAD_DIM)),
            "ones_bd": ones_bd,
            "ret_dec": _retention_params(ret_log_decay_fwd[l], ret_log_decay_bwd[l], ret_out_gain[l]),
            "sink": jnp.broadcast_to(attn_sink[l].astype(F32)[:, None] * LOG2_E, (N_ATT_HEADS, LANES)),
            "att_gain": row(attn_out_gain[l]),
            "w_out": w_out[l].astype(BF16),
        }
        mixes = [_token_mixer(h, lp, tab, b, r) for h, tab, (b, r) in zip(hs, tabs, shapes)]
        i = l // 2
        if l % 2 == 0:
            w = (ffn_w_gate[i].astype(BF16), ffn_w_up[i].astype(BF16), ffn_w_down[i].astype(BF16))
            hs = [_ffn(*mix, row(norm_ffn[l]), *w) for mix in mixes]
        else:
            wr = jnp.pad(moe_router[i].astype(F32), ((0, 0), (0, LANES - N_EXPERTS)))
            wr_hi = wr.astype(BF16)
            router_w = jnp.concatenate([wr_hi, (wr - wr_hi.astype(F32)).astype(BF16)], axis=1)
            expert_w = (moe_w_gate[i].astype(BF16), moe_w_up[i].astype(BF16), moe_w_down[i].astype(BF16))
            hs = _routed_experts(mixes, row(norm_ffn[l]), router_w, expert_w)
    return tuple(h.reshape(b, r, D_MODEL)[:, CHUNK:].astype(x.dtype) for h, x, (b, r) in zip(hs, xs, shapes))
```

```python
import functools

import jax
import jax.numpy as jnp
from jax import lax
from jax.experimental import pallas as pl
from jax.experimental.pallas import tpu as pltpu

F32 = jnp.float32
BF16 = jnp.bfloat16
I32 = jnp.int32

D_MODEL = 1024
HEAD_DIM = 64
N_RET_HEADS = 8
N_ATT_HEADS = 8
N_KV_HEADS = 2
RET_WIDTH = N_RET_HEADS * HEAD_DIM
ATT_WIDTH = N_ATT_HEADS * HEAD_DIM
KV_WIDTH = N_KV_HEADS * HEAD_DIM
IN_WIDTH = 4 * RET_WIDTH + ATT_WIDTH + 2 * KV_WIDTH
CHUNK = 128
LANES = 128
SUBLANES = 8
N_META = 16
PAD_FRONT = CHUNK - N_META
ROPE_THETA = 500000.0
ROPE_DIMS = HEAD_DIM // 4
RET_ROPE_THETA = 10000.0
N_EXPERTS = 8
TOP_K = 2
EXPERT_ROW_TILE = 512
MXU_V7X_COLUMNS = 256
EPS = 1e-6
NEG = -1e30
QK_SCALE = HEAD_DIM ** -0.5
LOG2_E = 1.4426950408889634
MIB = 1024 * 1024


def _row_tile(rows, prefs):
    for t in prefs:
        if rows % t == 0:
            return t
    raise ValueError(f"no row tile for {rows} rows among {prefs}")


def _resident(shape):
    return pl.BlockSpec(shape, lambda *_: (0,) * len(shape), pipeline_mode=pl.Buffered(1))


def _rms(x, gain):
    ms = jnp.mean(x * x, axis=-1, keepdims=True)
    return x * lax.rsqrt(ms + EPS) * gain


def _silu(x):
    return x * jax.nn.sigmoid(x)


def _inproj_kernel(h_ref, gain_ref, w_ref, cr_ref, sr_ref, ca_ref, sa_ref, qg_ref, kg_ref, ones_ref,
                   rq_ref, rk_ref, rv_ref, rg_ref, aq_ref, ak_ref, av_ref):
    a = _rms(h_ref[...], gain_ref[...]).astype(BF16)
    tm = a.shape[0]
    lane = lax.broadcasted_iota(I32, (tm, LANES), 1)
    in_head = lane & (HEAD_DIM - 1)
    low_head = lane < HEAD_DIM

    def proj(c0, c1):
        return jnp.dot(a, w_ref[:, c0:c1], preferred_element_type=F32)

    def rotate(x, cos, sin_signed, half):
        partner = jnp.where(in_head < half, pltpu.roll(x, LANES - half, 1), pltpu.roll(x, half, 1))
        return x * cos + partner * sin_signed

    def head_norm(x, g):
        ss = jnp.dot((x * x).astype(BF16), ones_ref[...], preferred_element_type=F32)
        return x * lax.rsqrt(ss * (1.0 / HEAD_DIM) + EPS) * g

    cr, sr, ca, sa = cr_ref[...], sr_ref[...], ca_ref[...], sa_ref[...]
    cols = lambda j: slice(j * LANES, (j + 1) * LANES)
    n_lane_tiles = RET_WIDTH // LANES

    def store_keys(kt_ref, j, x):
        for c in range(tm // CHUNK):
            kt_ref[c, cols(j), :] = x[c * CHUNK:(c + 1) * CHUNK, :].T.astype(BF16)

    base = 4 * RET_WIDTH
    p_aq = proj(base, base + ATT_WIDTH)
    p_kv = proj(base + ATT_WIDTH, IN_WIDTH)
    p_rq = proj(0, RET_WIDTH)
    p_rk = proj(RET_WIDTH, 2 * RET_WIDTH)
    rv_ref[...] = proj(2 * RET_WIDTH, 3 * RET_WIDTH).astype(BF16)
    rg_ref[...] = proj(3 * RET_WIDTH, 4 * RET_WIDTH).astype(BF16)

    for j in range(n_lane_tiles):
        rq_ref[:, cols(j)] = rotate(p_rq[:, cols(j)], cr, sr, HEAD_DIM // 2).astype(BF16)
    for j in range(n_lane_tiles):
        store_keys(rk_ref, j, rotate(p_rk[:, cols(j)], cr, sr, HEAD_DIM // 2) * QK_SCALE)
    normed = [head_norm(p_aq[:, cols(j)], qg_ref[...]) for j in range(ATT_WIDTH // LANES)]
    normed_k = head_norm(p_kv[:, :KV_WIDTH], kg_ref[...])
    for j, n in enumerate(normed):
        aq_ref[:, cols(j)] = (rotate(n, ca, sa, ROPE_DIMS // 2) * (QK_SCALE * LOG2_E)).astype(BF16)

    k = rotate(normed_k, ca, sa, ROPE_DIMS // 2)
    v = p_kv[:, KV_WIDTH:]
    k_sw = pltpu.roll(k, HEAD_DIM, 1)
    v_sw = pltpu.roll(v, HEAD_DIM, 1)
    store_keys(ak_ref, 0, jnp.where(low_head, k, k_sw))
    store_keys(ak_ref, 1, jnp.where(low_head, k_sw, k))
    av_ref[:, 0:LANES] = jnp.where(low_head, v, v_sw).astype(BF16)
    av_ref[:, LANES:2 * LANES] = jnp.where(low_head, v_sw, v).astype(BF16)


def _inproj(h, gain, w, tabs, qg, kg, ones_bd):
    rows = h.shape[0]
    tm = _row_tile(rows, (1024, 640, 512, 384, 256, 128))
    row_spec = lambda width: pl.BlockSpec((tm, width), lambda i: (i, 0))
    outs = ((RET_WIDTH, False), (RET_WIDTH, True), (RET_WIDTH, False), (RET_WIDTH, False),
            (ATT_WIDTH, False), (2 * KV_WIDTH, True), (2 * KV_WIDTH, False))
    chunk_spec = lambda width: pl.BlockSpec((tm // CHUNK, width, CHUNK), lambda i: (i, 0, 0))
    return pl.pallas_call(
        _inproj_kernel,
        grid=(rows // tm,),
        in_specs=[row_spec(D_MODEL), _resident((1, D_MODEL)), _resident((D_MODEL, IN_WIDTH)),
                  row_spec(LANES), row_spec(LANES), row_spec(LANES), row_spec(LANES),
                  _resident((1, LANES)), _resident((1, LANES)), _resident((LANES, LANES))],
        out_specs=[chunk_spec(wd) if t else row_spec(wd) for wd, t in outs],
        out_shape=[jax.ShapeDtypeStruct((rows // CHUNK, wd, CHUNK) if t else (rows, wd), BF16) for wd, t in outs],
        compiler_params=pltpu.CompilerParams(dimension_semantics=("parallel",), vmem_limit_bytes=56 * MIB),
        name="inproj",
    )(h, gain, w, *tabs, qg, kg, ones_bd)


def _retention_kernel(q_ref, k_ref, v_ref, g_ref, dec_ref, o_ref, sb_ref, *, n_chunks):
    lgf, lgb = dec_ref[0:1, :], dec_ref[1:2, :]
    gain = dec_ref[6:7, :]
    ri = lax.broadcasted_iota(I32, (CHUNK, LANES), 0)
    ci = lax.broadcasted_iota(I32, (CHUNK, LANES), 1)
    r = ri.astype(F32)
    diff = (ri - ci).astype(F32)

    def decay_mask(lf, lb):
        return jnp.where(diff >= 0, jnp.exp(jnp.maximum(diff, 0.0) * lf), jnp.exp(jnp.maximum(-diff, 0.0) * lb))

    dm = jnp.concatenate([decay_mask(dec_ref[2:3, :], dec_ref[4:5, :]),
                          decay_mask(dec_ref[3:4, :], dec_ref[5:6, :])], axis=1)
    tok = ci.astype(F32)
    wf = jnp.exp((CHUNK - 1.0 - tok) * jnp.broadcast_to(lgf, (LANES, LANES)).T)
    wb = jnp.exp(tok * jnp.broadcast_to(lgb, (LANES, LANES)).T)
    qf = jnp.exp((r + 1.0) * lgf)
    qb = jnp.exp((CHUNK - r) * lgb)
    cf = jnp.exp(float(CHUNK) * lgf)
    cb = jnp.exp(float(CHUNK) * lgb)
    low_c = ci < HEAD_DIM
    same_head = (ri < HEAD_DIM) == low_c
    ones_bd = jnp.where(same_head, 1.0, 0.0).astype(BF16)
    m0 = jnp.where(low_c, 1.0, 0.0).astype(BF16)
    m1 = jnp.where(low_c, 0.0, 1.0).astype(BF16)
    low_r = ri < HEAD_DIM
    rm0 = jnp.where(low_r, 1.0, 0.0).astype(BF16)
    rm1 = jnp.where(low_r, 0.0, 1.0).astype(BF16)

    def chunk_rows(n):
        return pl.ds(pl.multiple_of(n * CHUNK, CHUNK), CHUNK)

    def pair_rows(x):
        return jnp.concatenate([x * m0, x * m1], axis=0)

    def pair_cols(xt):
        return jnp.concatenate([xt * rm0, xt * rm1], axis=1)

    def state_delta(kt, weights, v):
        u = jnp.dot((kt.astype(F32) * weights).astype(BF16), v, preferred_element_type=F32)
        return jnp.where(same_head, u, 0.0)

    group = next(u for u in (11, 13, 4, 3, 5, 2, 1) if n_chunks % u == 0)
    n_groups = n_chunks // group
    members = range(group)
    zero_state = jnp.zeros((LANES, LANES), F32)

    def backward(i, state):
        ns = [n_chunks - 1 - i * group - j for j in members]
        deltas = [state_delta(k_ref[n], wb, v_ref[chunk_rows(n), :]) for n in ns]
        for n, delta in zip(ns, deltas):
            sb_ref[n] = state.astype(BF16)
            state = cb * state + delta
        return state

    lax.fori_loop(0, n_groups, backward, zero_state)

    def forward(i, state):
        ns = [i * group + j for j in members]
        q = [q_ref[chunk_rows(n), :] for n in ns]
        kt = [k_ref[n] for n in ns]
        v = [v_ref[chunk_rows(n), :] for n in ns]
        deltas = [state_delta(kt[j], wf, v[j]) for j in members]
        s = [jnp.dot(q[j], pair_cols(kt[j]), preferred_element_type=F32) * dm for j in members]
        both = []
        for j in members:
            both.append(jnp.concatenate([state.astype(BF16), sb_ref[ns[j]]], axis=1))
            state = cf * state + deltas[j]
        intra = [jnp.dot(s[j].astype(BF16), pair_rows(v[j]), preferred_element_type=F32) for j in members]
        inter = [jnp.dot(q[j], both[j], preferred_element_type=F32) for j in members]
        o = [intra[j] + inter[j][:, :LANES] * qf + inter[j][:, LANES:] * qb for j in members]
        ss = [jnp.dot((o[j] * o[j]).astype(BF16), ones_bd, preferred_element_type=F32) for j in members]
        for j in members:
            gate = _silu(g_ref[chunk_rows(ns[j]), :].astype(F32))
            o_ref[chunk_rows(ns[j]), :] = (o[j] * lax.rsqrt(ss[j] * (1.0 / HEAD_DIM) + EPS) * gain * gate).astype(BF16)
        return state

    lax.fori_loop(0, n_groups, forward, zero_state)


def _retention(rq, rk, rv, rg, dec, batch, seq_rows):
    n_chunks = seq_rows // CHUNK
    n_pairs = RET_WIDTH // LANES
    view = lambda t: t.reshape(batch, seq_rows, RET_WIDTH)
    seq_spec = pl.BlockSpec((None, seq_rows, LANES), lambda b, p: (b, 0, p))
    key_spec = pl.BlockSpec((n_chunks, LANES, CHUNK), lambda b, p: (b, p, 0))
    out = pl.pallas_call(
        functools.partial(_retention_kernel, n_chunks=n_chunks),
        grid=(batch, n_pairs),
        in_specs=[seq_spec, key_spec, seq_spec, seq_spec,
                  pl.BlockSpec((None, 8, LANES), lambda b, p: (p, 0, 0))],
        out_specs=seq_spec,
        out_shape=jax.ShapeDtypeStruct((batch, seq_rows, RET_WIDTH), BF16),
        scratch_shapes=[pltpu.VMEM((n_chunks, LANES, LANES), BF16)],
        compiler_params=pltpu.CompilerParams(dimension_semantics=("parallel", "parallel"),
                                             vmem_limit_bytes=48 * MIB),
        name="retention",
    )(view(rq), rk, view(rv), view(rg), dec)
    return out.reshape(batch * seq_rows, RET_WIDTH)


def _attention_kernel(q_ref, kp_ref, kc_ref, kn_ref, km_ref, vp_ref, vc_ref, vn_ref, vm_ref,
                      sink_ref, gain_ref, o_ref, *, n_chunks, chunks_per_step):
    first_chunk = pl.program_id(1) * chunks_per_step
    ri = lax.broadcasted_iota(I32, (CHUNK, LANES), 0)
    ci = lax.broadcasted_iota(I32, (CHUNK, LANES), 1)
    never = 2 * CHUNK
    to_bias = lambda ok: jnp.where(ok, 0.0, NEG).astype(BF16)
    meta_bias = to_bias(ci >= PAD_FRONT)
    n_keys = 4 * CHUNK
    low_c = ci < HEAD_DIM
    m0 = jnp.where(low_c, 1.0, 0.0).astype(BF16)
    m1 = jnp.where(low_c, 0.0, 1.0).astype(BF16)
    low_r = ri < HEAD_DIM
    rm0 = jnp.where(low_r, 1.0, 0.0).astype(BF16)
    rm1 = jnp.where(low_r, 0.0, 1.0).astype(BF16)
    eye = jnp.where(ri == ci, 1.0, 0.0).astype(BF16)
    row_sums = jnp.concatenate([m0] * 4 + [m1] * 4, axis=0)
    pairs_per_kv = N_ATT_HEADS // N_KV_HEADS // 2
    rows = lambda j: slice(j * CHUNK, (j + 1) * CHUNK)

    def split_values(ref, blk, g):
        x = ref[blk, g * LANES:(g + 1) * LANES]
        return x * m0, x * m1

    def split_keys(ref, idx, g):
        xt = ref[idx, g * LANES:(g + 1) * LANES, :]
        return xt * rm0, xt * rm1

    def blocks(split, at, prev_ref, main_ref, next_ref, meta_ref, g):
        out = [split(prev_ref, at(0), g)]
        out += [split(main_ref, at(j), g) for j in range(chunks_per_step)]
        out += [split(next_ref, at(0), g), split(meta_ref, at(0), g)]
        return out

    k_blocks = [blocks(split_keys, lambda j: j, kp_ref, kc_ref, kn_ref, km_ref, g) for g in range(N_KV_HEADS)]
    v_blocks = [blocks(split_values, rows, vp_ref, vc_ref, vn_ref, vm_ref, g) for g in range(N_KV_HEADS)]

    def pair_blocks(blks, j, axis):
        use = [blks[j], blks[j + 1], blks[j + 2], blks[-1]]
        return jnp.concatenate([b[0] for b in use] + [b[1] for b in use], axis=axis)

    def pair_ids(g):
        return [g * pairs_per_kv + pp for pp in range(pairs_per_kv)]

    def scores(j, g):
        c = first_chunk + j
        prev_off = jnp.where(c >= 2, 0, never)
        cur_off = jnp.where(c >= 1, 0, never)
        next_off = jnp.where(c + 1 <= n_chunks - 1, 0, never)
        bias = jnp.concatenate([to_bias(ci >= ri + prev_off), to_bias(ci >= cur_off),
                                to_bias(ci + next_off <= ri), meta_bias] * 2, axis=1)
        k_ext = jnp.concatenate([pair_blocks(k_blocks[g], j, 1), bias], axis=0)
        q_ext = jnp.concatenate(
            [jnp.concatenate([q_ref[rows(j), p * LANES:(p + 1) * LANES], eye], axis=1) for p in pair_ids(g)], axis=0)
        return jnp.dot(q_ext, k_ext, preferred_element_type=F32)

    def softmax_parts(g, s):
        probs, sink_rows = [], []
        for pp, p in enumerate(pair_ids(g)):
            halves, sink_terms = [], []
            for hh in range(2):
                sh = s[rows(pp), hh * n_keys:(hh + 1) * n_keys]
                sk = sink_ref[2 * p + hh:2 * p + hh + 1, 0:1]
                m = jnp.maximum(jnp.max(sh, axis=-1, keepdims=True), sk)
                halves.append(jnp.exp2(sh - m).astype(BF16))
                sink_terms.append(jnp.exp2(sk - m))
            probs.append(jnp.concatenate(halves, axis=1))
            sink_rows.append(jnp.where(low_c, sink_terms[0], sink_terms[1]))
        return jnp.concatenate(probs, axis=0), sink_rows

    def weighted_values(j, g, probs, sink_rows):
        v_ext = jnp.concatenate([pair_blocks(v_blocks[g], j, 0), row_sums], axis=1)
        ol = jnp.dot(probs, v_ext, preferred_element_type=F32)
        return [ol[rows(pp), :LANES] / (ol[rows(pp), LANES:] + sink_rows[pp]) for pp in range(pairs_per_kv)]

    chunk_group = 4
    for j0 in range(0, chunks_per_step, chunk_group):
        units = [(j, g) for j in range(j0, min(j0 + chunk_group, chunks_per_step)) for g in range(N_KV_HEADS)]
        s = [scores(j, g) for j, g in units]
        parts = [softmax_parts(g, s[u]) for u, (j, g) in enumerate(units)]
        outs = [weighted_values(j, g, *parts[u]) for u, (j, g) in enumerate(units)]
        for j in sorted({j for j, _ in units}):
            att = jnp.concatenate([o for u, (ju, _) in enumerate(units) if ju == j for o in outs[u]], axis=1)
            row = lax.broadcasted_iota(I32, att.shape, 0)
            pad_rows = jnp.where(first_chunk + j == 0, PAD_FRONT, 0)
            att = jnp.where(row >= pad_rows, att, 0.0)
            o_ref[rows(j), :] = _rms(att, gain_ref[...]).astype(BF16)


def _attention(aq, ak, av, sink, gain, batch, seq_rows):
    n_chunks = seq_rows // CHUNK
    per_step = next(r for r in (11, 13, 3, 5, 4, 2, 1) if n_chunks % r == 0)
    n_steps = n_chunks // per_step
    q3 = aq.reshape(batch, seq_rows, ATT_WIDTH)
    v3 = av.reshape(batch, seq_rows, 2 * KV_WIDTH)
    prev_chunk = lambda s: jnp.maximum(s * per_step - 1, 0)
    next_chunk = lambda s: jnp.minimum((s + 1) * per_step, n_chunks - 1)
    v_one = lambda fn: pl.BlockSpec((None, CHUNK, 2 * KV_WIDTH), lambda b, s: (b, fn(s), 0))
    v_specs = [v_one(prev_chunk), pl.BlockSpec((None, per_step * CHUNK, 2 * KV_WIDTH), lambda b, s: (b, s, 0)),
               v_one(next_chunk), v_one(lambda s: 0)]
    k_one = lambda fn: pl.BlockSpec((1, 2 * KV_WIDTH, CHUNK), lambda b, s: (b * n_chunks + fn(s), 0, 0))
    k_specs = [k_one(prev_chunk),
               pl.BlockSpec((per_step, 2 * KV_WIDTH, CHUNK), lambda b, s: (b * n_steps + s, 0, 0)),
               k_one(next_chunk), k_one(lambda s: 0)]
    q_spec = pl.BlockSpec((None, per_step * CHUNK, ATT_WIDTH), lambda b, s: (b, s, 0))
    out = pl.pallas_call(
        functools.partial(_attention_kernel, n_chunks=n_chunks, chunks_per_step=per_step),
        grid=(batch, n_steps),
        in_specs=[q_spec] + k_specs + v_specs + [_resident((N_ATT_HEADS, LANES)), _resident((1, ATT_WIDTH))],
        out_specs=q_spec,
        out_shape=jax.ShapeDtypeStruct((batch, seq_rows, ATT_WIDTH), BF16),
        compiler_params=pltpu.CompilerParams(dimension_semantics=("parallel", "parallel"),
                                             vmem_limit_bytes=32 * MIB),
        name="attention",
    )(q3, ak, ak, ak, ak, v3, v3, v3, v3, sink, gain)
    return out.reshape(batch * seq_rows, ATT_WIDTH)


def _mixed_residual(ret_ref, att_ref, w_ref, h_ref, rows=slice(None)):
    acc = jnp.dot(ret_ref[rows, :], w_ref[0:RET_WIDTH, :], preferred_element_type=F32)
    acc = acc + jnp.dot(att_ref[rows, :], w_ref[RET_WIDTH:, :], preferred_element_type=F32)
    return h_ref[rows, :] + acc


def _mix_specs(tm):
    row_spec = lambda width: pl.BlockSpec((tm, width), lambda i, *_: (i, 0))
    return [row_spec(RET_WIDTH), row_spec(ATT_WIDTH), _resident((RET_WIDTH + ATT_WIDTH, D_MODEL)),
            row_spec(D_MODEL)]


def _ff_halves(d_ff):
    n_mxu_tiles = pl.cdiv(d_ff, MXU_V7X_COLUMNS)
    first = min(d_ff, pl.cdiv(n_mxu_tiles, 2) * MXU_V7X_COLUMNS)
    return tuple((c0, c1) for c0, c1 in ((0, first), (first, d_ff)) if c1 > c0)


def _ffn_kernel(ret_ref, att_ref, wo_ref, h_ref, gain_ref, wg_ref, wu_ref, wd_ref, o_ref):
    x = _mixed_residual(ret_ref, att_ref, wo_ref, h_ref)
    f = _rms(x, gain_ref[...]).astype(BF16)
    acc = x
    for c0, c1 in _ff_halves(wg_ref.shape[1]):
        gate = jnp.dot(f, wg_ref[:, c0:c1], preferred_element_type=F32)
        up = jnp.dot(f, wu_ref[:, c0:c1], preferred_element_type=F32)
        act = (_silu(gate) * up).astype(BF16)
        acc = acc + jnp.dot(act, wd_ref[c0:c1, :], preferred_element_type=F32)
    o_ref[...] = acc


def _ffn(ret, att, w_out, h, gain, wg, wu, wd):
    rows = h.shape[0]
    d_ff = wg.shape[1]
    tm = _row_tile(rows, (640, 512, 384, 256, 128))
    return pl.pallas_call(
        _ffn_kernel,
        grid=(rows // tm,),
        in_specs=_mix_specs(tm) + [_resident((1, D_MODEL)), _resident((D_MODEL, d_ff)), _resident((D_MODEL, d_ff)),
                                   _resident((d_ff, D_MODEL))],
        out_specs=pl.BlockSpec((tm, D_MODEL), lambda i: (i, 0)),
        out_shape=jax.ShapeDtypeStruct((rows, D_MODEL), F32),
        input_output_aliases={3: 0},
        compiler_params=pltpu.CompilerParams(dimension_semantics=("parallel",), vmem_limit_bytes=56 * MIB),
        name="ffn",
    )(ret, att, w_out, h, gain, wg, wu, wd)


ROUTE_G1, ROUTE_G2, ROUTE_D1, ROUTE_D2 = range(4)
TILE_BASE, TILE_COUNT, TILE_START = range(3)


def _lane_pick(x, lane, j):
    return jnp.sum(jnp.where(lane == j, x, 0.0), axis=-1, keepdims=True)


def _lane_scalar(vec, lane_row, e):
    return jnp.sum(jnp.where(lane_row == e, vec, 0.0)).astype(I32)


RUN_ALIGN_BITS = 3
RUN_ALIGN = 1 << RUN_ALIGN_BITS


def _run_piece_count(tm):
    return tm.bit_length() - RUN_ALIGN_BITS


def _run_copies(count, src0, dst0, src_ref, dst_ref, sems):
    copies = []
    for j in range(sems.shape[0]):
        k = j + RUN_ALIGN_BITS
        size = 1 << k
        start = (count >> (k + 1)) << (k + 1)
        src = src_ref.at[pl.ds(pl.multiple_of(src0 + start, RUN_ALIGN), size)]
        dst = dst_ref.at[pl.ds(pl.multiple_of(dst0 + start, RUN_ALIGN), size)]
        copies.append((((count >> k) & 1) == 1, pltpu.make_async_copy(src, dst, sems.at[j])))
    return copies


def _start_then_wait(copies):
    for cond, cp in copies:
        pl.when(cond)(cp.start)
    for cond, cp in copies:
        pl.when(cond)(cp.wait)


def _one_hot_rows(dest, width):
    col = lax.broadcasted_iota(I32, (dest.shape[0], width), 1)
    return jnp.where(col == dest.astype(I32), 1.0, 0.0)


def _router_kernel(*refs, region_rows, finalize, aliased, tiles_per_step):
    ret_ref, att_ref, wo_ref, h_ref, gain_ref, wr_ref, base_ref = refs[:7]
    refs = refs[7 + (1 if aliased else 0):]
    hout_ref, xs_ref, route_ref, tile_ref, cnt_ref, sorted_ref, zero_ref, sems, zsem = refs
    i = pl.program_id(0)
    tm = h_ref.shape[0] // tiles_per_step
    n_sorted = sorted_ref.shape[1]

    @pl.when(i == 0)
    def _():
        cnt_ref[...] = base_ref[...]

    lane = lax.broadcasted_iota(I32, (tm, LANES), 1)
    lane_row = lane[0:1, :]
    ri = lax.broadcasted_iota(I32, (tm, tm), 0)
    ci = lax.broadcasted_iota(I32, (tm, tm), 1)
    earlier_tokens = jnp.where(ri > ci, 1.0, 0.0).astype(BF16)
    li = lax.broadcasted_iota(I32, (LANES, LANES), 0)
    lj = lax.broadcasted_iota(I32, (LANES, LANES), 1)
    earlier_experts = jnp.where(li < lj, 1.0, 0.0).astype(BF16)
    slot = lax.broadcasted_iota(I32, (n_sorted, tm), 0)
    sub = lax.broadcasted_iota(I32, (SUBLANES, LANES), 0)

    def as_row(col):
        wide = jnp.broadcast_to(col, (tm, LANES))
        return jnp.concatenate([wide[c * LANES:(c + 1) * LANES, :].T for c in range(tm // LANES)], axis=1)[0:1, :]

    recs = [dict(t=t, rows=slice(t * tm, (t + 1) * tm)) for t in range(tiles_per_step)]

    def stage(fn):
        for rec in recs:
            rec.update(fn(rec))

    def mix(r):
        h_mixed = _mixed_residual(ret_ref, att_ref, wo_ref, h_ref, r["rows"])
        hout_ref[r["rows"], :] = h_mixed
        return dict(f=_rms(h_mixed, gain_ref[...]))

    def logits(r):
        f_hi = r["f"].astype(BF16)
        f_lo = (r["f"] - f_hi.astype(F32)).astype(BF16)
        both = (jnp.dot(f_hi, wr_ref[...], preferred_element_type=F32)
                + jnp.dot(f_lo, wr_ref[...], preferred_element_type=F32))
        lg = jnp.where(lane < N_EXPERTS, both[:, :LANES] + both[:, LANES:], -jnp.inf)
        return dict(f_hi=f_hi, lg=lg)

    def first_choice(r):
        m1 = jnp.max(r["lg"], axis=-1, keepdims=True)
        return dict(m1=m1, i1=jnp.min(jnp.where(r["lg"] == m1, lane, LANES), axis=-1, keepdims=True))

    def second_choice(r):
        lg2 = jnp.where(lane == r["i1"], -jnp.inf, r["lg"])
        m2 = jnp.max(lg2, axis=-1, keepdims=True)
        return dict(m2=m2, i2=jnp.min(jnp.where(lg2 == m2, lane, LANES), axis=-1, keepdims=True))

    def rank_tokens(r):
        active = jnp.max(jnp.abs(r["f"]), axis=-1, keepdims=True) > 0.0
        sel = jnp.where(active, jnp.where(lane == r["i1"], 1.0, 0.0) + jnp.where(lane == r["i2"], 1.0, 0.0), 0.0)
        rank = jnp.dot(earlier_tokens, sel.astype(BF16), preferred_element_type=F32)
        tiles_per_run = jnp.floor((jnp.sum(sel, axis=0, keepdims=True) + (RUN_ALIGN - 1.0)) * (1.0 / RUN_ALIGN))
        start = RUN_ALIGN * jnp.dot(jnp.broadcast_to(tiles_per_run, (SUBLANES, LANES)).astype(BF16),
                                    earlier_experts, preferred_element_type=F32)[0:1, :]
        return dict(active=active, rank=rank, count=tiles_per_run * RUN_ALIGN, start=start)

    def destinations(r):
        dest = r["start"] + r["rank"]
        no_slot = -1.0
        d1 = jnp.where(r["active"], _lane_pick(dest, lane, r["i1"]), no_slot)
        d2 = jnp.where(r["active"], _lane_pick(dest, lane, r["i2"]), no_slot)
        e2 = jnp.exp(r["m2"] - r["m1"])
        g1 = 1.0 / (1.0 + e2)
        route = jnp.zeros((tm, LANES), F32)
        for j, val in ((ROUTE_G1, g1), (ROUTE_G2, e2 * g1), (ROUTE_D1, d1), (ROUTE_D2, d2)):
            route = jnp.where(lane == j, val, route)
        route_ref[r["rows"], :] = route
        return dict(d1=d1, d2=d2)

    def sort_rows(r):
        onehot_t = (jnp.where(slot == as_row(r["d1"]).astype(I32), 1.0, 0.0)
                    + jnp.where(slot == as_row(r["d2"]).astype(I32), 1.0, 0.0))
        sorted_ref[r["t"]] = jnp.dot(onehot_t.astype(BF16), r["f_hi"], preferred_element_type=F32)
        return {}

    for fn in (mix, logits, first_choice, second_choice, rank_tokens, destinations, sort_rows):
        stage(fn)
    base = cnt_ref[0:1, :]
    for rec in recs:
        rec["base"] = base
        tile_ref[rec["t"]] = jnp.where(sub == TILE_BASE, base, jnp.where(
            sub == TILE_COUNT, rec["count"], jnp.where(sub == TILE_START, rec["start"], 0.0)))
        base = base + rec["count"]
    copies = []
    for t, rec in enumerate(recs):
        for e in range(N_EXPERTS):
            n_e = _lane_scalar(rec["count"], lane_row, e)
            src0 = _lane_scalar(rec["start"], lane_row, e)
            dst0 = e * region_rows + _lane_scalar(rec["base"], lane_row, e)
            copies += _run_copies(n_e, src0, dst0, sorted_ref.at[t], xs_ref, sems.at[t, e])
    _start_then_wait(copies)
    cnt_ref[...] = jnp.broadcast_to(base, cnt_ref.shape)

    if finalize:
        @pl.when(i == pl.num_programs(0) - 1)
        def _():
            zero_ref[...] = jnp.zeros(zero_ref.shape, F32)
            total = cnt_ref[0:1, :]
            for e in range(N_EXPERTS):
                end = pl.multiple_of(e * region_rows + _lane_scalar(total, lane_row, e), RUN_ALIGN)
                cp = pltpu.make_async_copy(zero_ref, xs_ref.at[pl.ds(end, EXPERT_ROW_TILE)], zsem)
                cp.start()
                cp.wait()


def _router_tile(rows):
    return _row_tile(rows, (512, 640, 384, 256, 128))


def _sorted_rows(tm):
    return TOP_K * tm + pl.cdiv(N_EXPERTS * (RUN_ALIGN - 1), LANES) * LANES


def _router(mix, gain, w_router, base_counts, xs, region_rows, finalize):
    rows = mix[3].shape[0]
    tm = _router_tile(rows)
    n_tiles = rows // tm
    per_step = 2 if n_tiles % 2 == 0 else 1
    step_rows = per_step * tm
    aliased = xs is not None
    row_spec = lambda width: pl.BlockSpec((step_rows, width), lambda i: (i, 0))
    in_specs = _mix_specs(step_rows) + [_resident((1, D_MODEL)), _resident((D_MODEL, 2 * LANES)),
                                        _resident((SUBLANES, LANES))]
    args = [*mix, gain, w_router, base_counts]
    aliases = {3: 0}
    if aliased:
        aliases[len(args)] = 1
        in_specs.append(pl.BlockSpec(memory_space=pl.ANY))
        args.append(xs)
    n_pieces = _run_piece_count(tm)
    return pl.pallas_call(
        functools.partial(_router_kernel, region_rows=region_rows, finalize=finalize, aliased=aliased,
                          tiles_per_step=per_step),
        grid=(n_tiles // per_step,),
        in_specs=in_specs,
        out_specs=[row_spec(D_MODEL), pl.BlockSpec(memory_space=pl.ANY), row_spec(LANES),
                   pl.BlockSpec((per_step, SUBLANES, LANES), lambda i: (i, 0, 0)),
                   pl.BlockSpec((SUBLANES, LANES), lambda i: (0, 0))],
        out_shape=[jax.ShapeDtypeStruct((rows, D_MODEL), F32),
                   jax.ShapeDtypeStruct((N_EXPERTS * region_rows, D_MODEL), F32),
                   jax.ShapeDtypeStruct((rows, LANES), F32),
                   jax.ShapeDtypeStruct((n_tiles, SUBLANES, LANES), F32),
                   jax.ShapeDtypeStruct((SUBLANES, LANES), F32)],
        scratch_shapes=[pltpu.VMEM((per_step, _sorted_rows(tm), D_MODEL), F32),
                        pltpu.VMEM((EXPERT_ROW_TILE, D_MODEL), F32),
                        pltpu.SemaphoreType.DMA((per_step, N_EXPERTS, n_pieces)), pltpu.SemaphoreType.DMA(())],
        input_output_aliases=aliases,
        compiler_params=pltpu.CompilerParams(dimension_semantics=("arbitrary",), vmem_limit_bytes=56 * MIB),
        name="router",
    )(*args)


def _experts_kernel(blk_ref, exp_ref, nvalid_ref, x_ref, wg_ref, wu_ref, wd_ref, *y_refs):
    y_ref = y_refs[-1]
    partial_ref = y_refs[0] if len(y_refs) == 2 else None
    valid = pl.program_id(0) < nvalid_ref[0]

    @pl.when(valid)
    def _():
        x = x_ref[...].astype(BF16)
        gate = jnp.dot(x, wg_ref[...], preferred_element_type=F32)
        up = jnp.dot(x, wu_ref[...], preferred_element_type=F32)
        part = jnp.dot((_silu(gate) * up).astype(BF16), wd_ref[...], preferred_element_type=F32)
        y_ref[...] = part if partial_ref is None else partial_ref[...] + part

    @pl.when(jnp.logical_not(valid))
    def _():
        y_ref[...] = jnp.zeros(y_ref.shape, F32) if partial_ref is None else partial_ref[...]


def _experts(xs, tile_block, tile_expert, n_valid, wg, wu, wd):
    n_tiles = tile_block.shape[0]
    d_ff = wg.shape[2]
    ff_chunk = d_ff // 2
    assert ff_chunk % MXU_V7X_COLUMNS == 0
    tg = EXPERT_ROW_TILE
    y = None
    for c in range(d_ff // ff_chunk):
        in_specs = [pl.BlockSpec((tg, D_MODEL), lambda i, blk, ex, nv: (blk[i], 0)),
                    pl.BlockSpec((None, D_MODEL, ff_chunk), lambda i, blk, ex, nv, c=c: (ex[i], 0, c)),
                    pl.BlockSpec((None, D_MODEL, ff_chunk), lambda i, blk, ex, nv, c=c: (ex[i], 0, c)),
                    pl.BlockSpec((None, ff_chunk, D_MODEL), lambda i, blk, ex, nv, c=c: (ex[i], c, 0))]
        args = [tile_block, tile_expert, n_valid, xs, wg, wu, wd]
        y_spec = pl.BlockSpec((tg, D_MODEL), lambda i, blk, ex, nv: (i, 0))
        if y is not None:
            in_specs.append(y_spec)
            args.append(y)
        y = pl.pallas_call(
            _experts_kernel,
            grid_spec=pltpu.PrefetchScalarGridSpec(num_scalar_prefetch=3, grid=(n_tiles,), in_specs=in_specs,
                                                   out_specs=y_spec),
            out_shape=jax.ShapeDtypeStruct((n_tiles * tg, D_MODEL), F32),
            input_output_aliases={len(args) - 1: 0} if len(args) == 8 else {},
            compiler_params=pltpu.CompilerParams(dimension_semantics=("arbitrary",), vmem_limit_bytes=56 * MIB),
            name="experts",
        )(*args)
    return y


def _combine_kernel(tinfo_ref, yoff_ref, h_ref, route_ref, ys_ref, o_ref, sorted_ref, sems):
    i = pl.program_id(0)
    n_steps = pl.num_programs(0)
    n_sorted = sorted_ref.shape[1]
    slot = i % 2

    def tile_copies(tile, buf, live):
        copies = []
        for e in range(N_EXPERTS):
            rec = (tile * 3) * N_EXPERTS + e
            base = tinfo_ref[rec + TILE_BASE * N_EXPERTS]
            n_e = jnp.where(live, tinfo_ref[rec + TILE_COUNT * N_EXPERTS], 0)
            dst0 = tinfo_ref[rec + TILE_START * N_EXPERTS]
            copies += _run_copies(n_e, yoff_ref[e] + base, dst0, ys_ref, sorted_ref.at[buf], sems.at[buf, e])
        return copies

    @pl.when(i == 0)
    def _():
        sorted_ref[...] = jnp.zeros(sorted_ref.shape, F32)
        for cond, cp in tile_copies(0, 0, True):
            pl.when(cond)(cp.start)

    for cond, cp in tile_copies(jnp.minimum(i + 1, n_steps - 1), 1 - slot, i + 1 < n_steps):
        pl.when(cond)(cp.start)
    for cond, cp in tile_copies(i, slot, True):
        pl.when(cond)(cp.wait)
    y = sorted_ref[slot].astype(BF16)
    route = route_ref[...]
    lane = lax.broadcasted_iota(I32, route.shape, 1)
    pick = lambda j: _lane_pick(route, lane, j)
    y1 = jnp.dot(_one_hot_rows(pick(ROUTE_D1), n_sorted).astype(BF16), y, preferred_element_type=F32)
    y2 = jnp.dot(_one_hot_rows(pick(ROUTE_D2), n_sorted).astype(BF16), y, preferred_element_type=F32)
    o_ref[...] = h_ref[...] + pick(ROUTE_G1) * y1 + pick(ROUTE_G2) * y2


def _combine(h, route, tile_info, y_offsets, ys):
    rows = h.shape[0]
    tm = rows // tile_info.shape[0]
    n_pieces = _run_piece_count(tm)
    tinfo = tile_info[:, :3, :N_EXPERTS].astype(I32).reshape(-1)
    grid_spec = pltpu.PrefetchScalarGridSpec(
        num_scalar_prefetch=2,
        grid=(rows // tm,),
        in_specs=[pl.BlockSpec((tm, D_MODEL), lambda i, *_: (i, 0)),
                  pl.BlockSpec((tm, LANES), lambda i, *_: (i, 0)),
                  pl.BlockSpec(memory_space=pl.ANY)],
        out_specs=pl.BlockSpec((tm, D_MODEL), lambda i, *_: (i, 0)),
        scratch_shapes=[pltpu.VMEM((2, _sorted_rows(tm), D_MODEL), F32),
                        pltpu.SemaphoreType.DMA((2, N_EXPERTS, n_pieces))],
    )
    return pl.pallas_call(
        _combine_kernel,
        grid_spec=grid_spec,
        out_shape=jax.ShapeDtypeStruct((rows, D_MODEL), F32),
        input_output_aliases={2: 0},
        compiler_params=pltpu.CompilerParams(dimension_semantics=("arbitrary",), vmem_limit_bytes=48 * MIB),
        name="combine",
    )(tinfo, y_offsets, h, route, ys)


def _routed_experts(mixes, gain, router_w, expert_w):
    tg = EXPERT_ROW_TILE
    group_rows = [mix[3].shape[0] for mix in mixes]
    total_rows = sum(group_rows)
    run_pad = (RUN_ALIGN - 1) * sum(rows // _router_tile(rows) for rows in group_rows)
    region_rows = (pl.cdiv(total_rows + run_pad, tg) + 1) * tg
    counts = jnp.zeros((SUBLANES, LANES), F32)
    xs, hs, routes, tiles = None, [], [], []
    for g, mix in enumerate(mixes):
        h, xs, route, tile_info, counts = _router(mix, gain, router_w, counts, xs, region_rows,
                                                  finalize=(g == len(mixes) - 1))
        hs.append(h)
        routes.append(route)
        tiles.append(tile_info)
    n_e = counts[0, :N_EXPERTS].astype(I32)
    tiles_e = (n_e + tg - 1) // tg
    first_tile = jnp.cumsum(tiles_e) - tiles_e
    n_valid = jnp.sum(tiles_e)
    n_tiles = (TOP_K * total_rows + N_EXPERTS * (run_pad + tg - 1)) // tg + 1
    t = jnp.minimum(jnp.arange(n_tiles, dtype=I32), n_valid - 1)
    tile_expert = jnp.sum((t[:, None] >= (first_tile + tiles_e)[None, :]).astype(I32), axis=1)
    tile_block = tile_expert * (region_rows // tg) + (t - first_tile[tile_expert])
    ys = _experts(xs, tile_block.astype(I32), tile_expert.astype(I32), n_valid.reshape(1).astype(I32), *expert_w)
    y_offsets = (first_tile * tg).astype(I32)
    return [_combine(h, route, tile_info, y_offsets, ys) for h, route, tile_info in zip(hs, routes, tiles)]


def _rotary_tables(batch, seq_rows):
    pos = (jnp.arange(seq_rows, dtype=I32) - PAD_FRONT).astype(F32)

    def cos_sin(n_rot, theta):
        half = n_rot // 2
        inv = theta ** (-jnp.arange(half, dtype=F32) * 2.0 / n_rot)
        ang = pos[:, None] * inv[None, :]
        return jnp.cos(ang), jnp.sin(ang)

    cos, sin = cos_sin(HEAD_DIM, RET_ROPE_THETA)
    cr = jnp.tile(cos, (1, 2 * LANES // HEAD_DIM))
    sr = jnp.tile(jnp.concatenate([-sin, sin], axis=1), (1, LANES // HEAD_DIM))
    cos, sin = cos_sin(ROPE_DIMS, ROPE_THETA)
    rest = HEAD_DIM - ROPE_DIMS
    ca = jnp.tile(jnp.concatenate([cos, cos, jnp.ones((seq_rows, rest), F32)], axis=1), (1, LANES // HEAD_DIM))
    sa = jnp.tile(jnp.concatenate([-sin, sin, jnp.zeros((seq_rows, rest), F32)], axis=1), (1, LANES // HEAD_DIM))
    return tuple(jnp.tile(t, (batch, 1)) for t in (cr, sr, ca, sa))


def _retention_params(log_gf, log_gb, gain):
    n_pairs = RET_WIDTH // LANES
    per_lane = lambda t: jnp.repeat(t.astype(F32).reshape(n_pairs, 2), HEAD_DIM, axis=1)
    per_head = lambda t, j: jnp.broadcast_to(t.astype(F32).reshape(n_pairs, 2)[:, j:j + 1], (n_pairs, LANES))
    rows = [per_lane(log_gf), per_lane(log_gb), per_head(log_gf, 0), per_head(log_gf, 1),
            per_head(log_gb, 0), per_head(log_gb, 1), gain.astype(F32).reshape(n_pairs, LANES),
            jnp.zeros((n_pairs, LANES), F32)]
    return jnp.stack(rows, axis=1)


def _to_padded_rows(x, meta_tokens):
    batch, seq, _ = x.shape
    h = jnp.concatenate([jnp.zeros((batch, PAD_FRONT, D_MODEL), F32),
                         jnp.broadcast_to(meta_tokens.astype(F32)[None], (batch, N_META, D_MODEL)),
                         x.astype(F32)], axis=1)
    return h.reshape(batch * (seq + CHUNK), D_MODEL)


def _token_mixer(h, lp, tabs, batch, seq_rows):
    rq, rk, rv, rg, aq, ak, av = _inproj(h, lp["norm_mix"], lp["w_in"], tabs, lp["q_gain"], lp["k_gain"],
                                         lp["ones_bd"])
    ret = _retention(rq, rk, rv, rg, lp["ret_dec"], batch, seq_rows)
    att = _attention(aq, ak, av, lp["sink"], lp["att_gain"], batch, seq_rows)
    return ret, att, lp["w_out"], h


def kernel(x_prompt, x_sample, meta_tokens, norm_mix, w_in, ret_log_decay_fwd, ret_log_decay_bwd, ret_out_gain,
           q_norm_gain, k_norm_gain, attn_sink, attn_out_gain, w_out, norm_ffn, ffn_w_gate, ffn_w_up, ffn_w_down,
           moe_router, moe_w_gate, moe_w_up, moe_w_down):
    depth = w_in.shape[0]
    ri = lax.broadcasted_iota(I32, (LANES, LANES), 0)
    ci = lax.broadcasted_iota(I32, (LANES, LANES), 1)
    ones_bd = ((ri < HEAD_DIM) == (ci < HEAD_DIM)).astype(BF16)
    row = lambda t: t.astype(F32).reshape(1, -1)

    xs = (x_prompt, x_sample)
    shapes = [(x.shape[0], x.shape[1] + CHUNK) for x in xs]
    hs = [_to_padded_rows(x, meta_tokens) for x in xs]
    tabs = [_rotary_tables(b, r) for b, r in shapes]

    for l in range(depth):
        lp = {
            "norm_mix": row(norm_mix[l]),
            "w_in": w_in[l].astype(BF16),
            "q_gain": jnp.tile(row(q_norm_gain[l]), (1, LANES // HEAD_DIM)),
            "k_gain": jnp.tile(row(k_norm_gain[l]), (1, LANES // HEAD_DIM)),
            "ones_bd": ones_bd,
            "ret_dec": _retention_params(ret_log_decay_fwd[l], ret_log_decay_bwd[l], ret_out_gain[l]),
            "sink": jnp.broadcast_to(attn_sink[l].astype(F32)[:, None] * LOG2_E, (N_ATT_HEADS, LANES)),
            "att_gain": row(attn_out_gain[l]),
            "w_out": w_out[l].astype(BF16),
        }
        mixes = [_token_mixer(h, lp, tab, b, r) for h, tab, (b, r) in zip(hs, tabs, shapes)]
        i = l // 2
        if l % 2 == 0:
            w = (ffn_w_gate[i].astype(BF16), ffn_w_up[i].astype(BF16), ffn_w_down[i].astype(BF16))
            hs = [_ffn(*mix, row(norm_ffn[l]), *w) for mix in mixes]
        else:
            wr = jnp.pad(moe_router[i].astype(F32), ((0, 0), (0, LANES - N_EXPERTS)))
            wr_hi = wr.astype(BF16)
            router_w = jnp.concatenate([wr_hi, (wr - wr_hi.astype(F32)).astype(BF16)], axis=1)
            expert_w = (moe_w_gate[i].astype(BF16), moe_w_up[i].astype(BF16), moe_w_down[i].astype(BF16))
            hs = _routed_experts(mixes, row(norm_ffn[l]), router_w, expert_w)
    return tuple(h.reshape(b, r, D_MODEL)[:, CHUNK:].astype(x.dtype) for h, x, (b, r) in zip(hs, xs, shapes))
```

```python
import functools

import jax
import jax.numpy as jnp
from jax import lax
from jax.experimental import pallas as pl
from jax.experimental.pallas import tpu as pltpu

F32 = jnp.float32
BF16 = jnp.bfloat16
I32 = jnp.int32

D_MODEL = 1024
HEAD_DIM = 64
N_RET_HEADS = 8
N_ATT_HEADS = 8
N_KV_HEADS = 2
RET_WIDTH = N_RET_HEADS * HEAD_DIM
ATT_WIDTH = N_ATT_HEADS * HEAD_DIM
KV_WIDTH = N_KV_HEADS * HEAD_DIM
IN_WIDTH = 4 * RET_WIDTH + ATT_WIDTH + 2 * KV_WIDTH
CHUNK = 128
LANES = 128
SUBLANES = 8
N_META = 16
PAD_FRONT = CHUNK - N_META
ROPE_THETA = 500000.0
ROPE_DIMS = HEAD_DIM // 4
RET_ROPE_THETA = 10000.0
N_EXPERTS = 8
TOP_K = 2
EXPERT_ROW_TILE = 512
MXU_V7X_COLUMNS = 256
EPS = 1e-6
NEG = -1e30
QK_SCALE = HEAD_DIM ** -0.5
LOG2_E = 1.4426950408889634
MIB = 1024 * 1024


def _row_tile(rows, prefs):
    for t in prefs:
        if rows % t == 0:
            return t
    raise ValueError(f"no row tile for {rows} rows among {prefs}")


def _resident(shape):
    return pl.BlockSpec(shape, lambda *_: (0,) * len(shape), pipeline_mode=pl.Buffered(1))


def _rms(x, gain):
    ms = jnp.mean(x * x, axis=-1, keepdims=True)
    return x * lax.rsqrt(ms + EPS) * gain


def _silu(x):
    return x * jax.nn.sigmoid(x)


def _inproj_kernel(h_ref, gain_ref, w_ref, cr_ref, sr_ref, ca_ref, sa_ref, qg_ref, kg_ref, ones_ref,
                   rq_ref, rk_ref, rv_ref, rg_ref, aq_ref, ak_ref, av_ref):
    a = _rms(h_ref[...], gain_ref[...]).astype(BF16)
    tm = a.shape[0]
    lane = lax.broadcasted_iota(I32, (tm, LANES), 1)
    in_head = lane & (HEAD_DIM - 1)
    low_head = lane < HEAD_DIM

    def proj(c0, c1):
        return jnp.dot(a, w_ref[:, c0:c1], preferred_element_type=F32)

    def rotate(x, cos, sin_signed, half):
        partner = jnp.where(in_head < half, pltpu.roll(x, LANES - half, 1), pltpu.roll(x, half, 1))
        return x * cos + partner * sin_signed

    def head_norm(x, g):
        ss = jnp.dot((x * x).astype(BF16), ones_ref[...], preferred_element_type=F32)
        return x * lax.rsqrt(ss * (1.0 / HEAD_DIM) + EPS) * g

    cr, sr, ca, sa = cr_ref[...], sr_ref[...], ca_ref[...], sa_ref[...]
    cols = lambda j: slice(j * LANES, (j + 1) * LANES)
    n_lane_tiles = RET_WIDTH // LANES

    def store_keys(kt_ref, j, x):
        for c in range(tm // CHUNK):
            kt_ref[c, cols(j), :] = x[c * CHUNK:(c + 1) * CHUNK, :].T.astype(BF16)

    base = 4 * RET_WIDTH
    p_aq = proj(base, base + ATT_WIDTH)
    p_kv = proj(base + ATT_WIDTH, IN_WIDTH)
    p_rq = proj(0, RET_WIDTH)
    p_rk = proj(RET_WIDTH, 2 * RET_WIDTH)
    rv_ref[...] = proj(2 * RET_WIDTH, 3 * RET_WIDTH).astype(BF16)
    rg_ref[...] = proj(3 * RET_WIDTH, 4 * RET_WIDTH).astype(BF16)

    for j in range(n_lane_tiles):
        rq_ref[:, cols(j)] = rotate(p_rq[:, cols(j)], cr, sr, HEAD_DIM // 2).astype(BF16)
    for j in range(n_lane_tiles):
        store_keys(rk_ref, j, rotate(p_rk[:, cols(j)], cr, sr, HEAD_DIM // 2) * QK_SCALE)
    normed = [head_norm(p_aq[:, cols(j)], qg_ref[...]) for j in range(ATT_WIDTH // LANES)]
    normed_k = head_norm(p_kv[:, :KV_WIDTH], kg_ref[...])
    for j, n in enumerate(normed):
        aq_ref[:, cols(j)] = (rotate(n, ca, sa, ROPE_DIMS // 2) * (QK_SCALE * LOG2_E)).astype(BF16)

    k = rotate(normed_k, ca, sa, ROPE_DIMS // 2)
    v = p_kv[:, KV_WIDTH:]
    k_sw = pltpu.roll(k, HEAD_DIM, 1)
    v_sw = pltpu.roll(v, HEAD_DIM, 1)
    store_keys(ak_ref, 0, jnp.where(low_head, k, k_sw))
    store_keys(ak_ref, 1, jnp.where(low_head, k_sw, k))
    av_ref[:, 0:LANES] = jnp.where(low_head, v, v_sw).astype(BF16)
    av_ref[:, LANES:2 * LANES] = jnp.where(low_head, v_sw, v).astype(BF16)


def _inproj(h, gain, w, tabs, qg, kg, ones_bd):
    rows = h.shape[0]
    tm = _row_tile(rows, (1024, 640, 512, 384, 256, 128))
    row_spec = lambda width: pl.BlockSpec((tm, width), lambda i: (i, 0))
    outs = ((RET_WIDTH, False), (RET_WIDTH, True), (RET_WIDTH, False), (RET_WIDTH, False),
            (ATT_WIDTH, False), (2 * KV_WIDTH, True), (2 * KV_WIDTH, False))
    chunk_spec = lambda width: pl.BlockSpec((tm // CHUNK, width, CHUNK), lambda i: (i, 0, 0))
    return pl.pallas_call(
        _inproj_kernel,
        grid=(rows // tm,),
        in_specs=[row_spec(D_MODEL), _resident((1, D_MODEL)), _resident((D_MODEL, IN_WIDTH)),
                  row_spec(LANES), row_spec(LANES), row_spec(LANES), row_spec(LANES),
                  _resident((1, LANES)), _resident((1, LANES)), _resident((LANES, LANES))],
        out_specs=[chunk_spec(wd) if t else row_spec(wd) for wd, t in outs],
        out_shape=[jax.ShapeDtypeStruct((rows // CHUNK, wd, CHUNK) if t else (rows, wd), BF16) for wd, t in outs],
        compiler_params=pltpu.CompilerParams(dimension_semantics=("parallel",), vmem_limit_bytes=56 * MIB),
        name="inproj",
    )(h, gain, w, *tabs, qg, kg, ones_bd)


def _retention_kernel(q_ref, k_ref, v_ref, g_ref, dec_ref, o_ref, sb_ref, *, n_chunks):
    lgf, lgb = dec_ref[0:1, :], dec_ref[1:2, :]
    gain = dec_ref[6:7, :]
    ri = lax.broadcasted_iota(I32, (CHUNK, LANES), 0)
    ci = lax.broadcasted_iota(I32, (CHUNK, LANES), 1)
    r = ri.astype(F32)
    diff = (ri - ci).astype(F32)

    def decay_mask(lf, lb):
        return jnp.where(diff >= 0, jnp.exp(jnp.maximum(diff, 0.0) * lf), jnp.exp(jnp.maximum(-diff, 0.0) * lb))

    dm = jnp.concatenate([decay_mask(dec_ref[2:3, :], dec_ref[4:5, :]),
                          decay_mask(dec_ref[3:4, :], dec_ref[5:6, :])], axis=1)
    tok = ci.astype(F32)
    wf = jnp.exp((CHUNK - 1.0 - tok) * jnp.broadcast_to(lgf, (LANES, LANES)).T)
    wb = jnp.exp(tok * jnp.broadcast_to(lgb, (LANES, LANES)).T)
    qf = jnp.exp((r + 1.0) * lgf)
    qb = jnp.exp((CHUNK - r) * lgb)
    cf = jnp.exp(float(CHUNK) * lgf)
    cb = jnp.exp(float(CHUNK) * lgb)
    low_c = ci < HEAD_DIM
    same_head = (ri < HEAD_DIM) == low_c
    ones_bd = jnp.where(same_head, 1.0, 0.0).astype(BF16)
    m0 = jnp.where(low_c, 1.0, 0.0).astype(BF16)
    m1 = jnp.where(low_c, 0.0, 1.0).astype(BF16)
    low_r = ri < HEAD_DIM
    rm0 = jnp.where(low_r, 1.0, 0.0).astype(BF16)
    rm1 = jnp.where(low_r, 0.0, 1.0).astype(BF16)

    def chunk_rows(n):
        return pl.ds(pl.multiple_of(n * CHUNK, CHUNK), CHUNK)

    def pair_rows(x):
        return jnp.concatenate([x * m0, x * m1], axis=0)

    def pair_cols(xt):
        return jnp.concatenate([xt * rm0, xt * rm1], axis=1)

    def state_delta(kt, weights, v):
        u = jnp.dot((kt.astype(F32) * weights).astype(BF16), v, preferred_element_type=F32)
        return jnp.where(same_head, u, 0.0)

    group = next(u for u in (11, 13, 4, 3, 5, 2, 1) if n_chunks % u == 0)
    n_groups = n_chunks // group
    members = range(group)
    zero_state = jnp.zeros((LANES, LANES), F32)

    def backward(i, state):
        ns = [n_chunks - 1 - i * group - j for j in members]
        deltas = [state_delta(k_ref[n], wb, v_ref[chunk_rows(n), :]) for n in ns]
        for n, delta in zip(ns, deltas):
            sb_ref[n] = state.astype(BF16)
            state = cb * state + delta
        return state

    lax.fori_loop(0, n_groups, backward, zero_state)

    def forward(i, state):
        ns = [i * group + j for j in members]
        q = [q_ref[chunk_rows(n), :] for n in ns]
        kt = [k_ref[n] for n in ns]
        v = [v_ref[chunk_rows(n), :] for n in ns]
        deltas = [state_delta(kt[j], wf, v[j]) for j in members]
        s = [jnp.dot(q[j], pair_cols(kt[j]), preferred_element_type=F32) * dm for j in members]
        both = []
        for j in members:
            both.append(jnp.concatenate([state.astype(BF16), sb_ref[ns[j]]], axis=1))
            state = cf * state + deltas[j]
        intra = [jnp.dot(s[j].astype(BF16), pair_rows(v[j]), preferred_element_type=F32) for j in members]
        inter = [jnp.dot(q[j], both[j], preferred_element_type=F32) for j in members]
        o = [intra[j] + inter[j][:, :LANES] * qf + inter[j][:, LANES:] * qb for j in members]
        ss = [jnp.dot((o[j] * o[j]).astype(BF16), ones_bd, preferred_element_type=F32) for j in members]
        for j in members:
            gate = _silu(g_ref[chunk_rows(ns[j]), :].astype(F32))
            o_ref[chunk_rows(ns[j]), :] = (o[j] * lax.rsqrt(ss[j] * (1.0 / HEAD_DIM) + EPS) * gain * gate).astype(BF16)
        return state

    lax.fori_loop(0, n_groups, forward, zero_state)


def _retention(rq, rk, rv, rg, dec, batch, seq_rows):
    n_chunks = seq_rows // CHUNK
    n_pairs = RET_WIDTH // LANES
    view = lambda t: t.reshape(batch, seq_rows, RET_WIDTH)
    seq_spec = pl.BlockSpec((None, seq_rows, LANES), lambda b, p: (b, 0, p))
    key_spec = pl.BlockSpec((n_chunks, LANES, CHUNK), lambda b, p: (b, p, 0))
    out = pl.pallas_call(
        functools.partial(_retention_kernel, n_chunks=n_chunks),
        grid=(batch, n_pairs),
        in_specs=[seq_spec, key_spec, seq_spec, seq_spec,
                  pl.BlockSpec((None, 8, LANES), lambda b, p: (p, 0, 0))],
        out_specs=seq_spec,
        out_shape=jax.ShapeDtypeStruct((batch, seq_rows, RET_WIDTH), BF16),
        scratch_shapes=[pltpu.VMEM((n_chunks, LANES, LANES), BF16)],
        compiler_params=pltpu.CompilerParams(dimension_semantics=("parallel", "parallel"),
                                             vmem_limit_bytes=48 * MIB),
        name="retention",
    )(view(rq), rk, view(rv), view(rg), dec)
    return out.reshape(batch * seq_rows, RET_WIDTH)


def _attention_kernel(q_ref, kp_ref, kc_ref, kn_ref, km_ref, vp_ref, vc_ref, vn_ref, vm_ref,
                      sink_ref, gain_ref, o_ref, *, n_chunks, chunks_per_step):
    first_chunk = pl.program_id(1) * chunks_per_step
    ri = lax.broadcasted_iota(I32, (CHUNK, LANES), 0)
    ci = lax.broadcasted_iota(I32, (CHUNK, LANES), 1)
    never = 2 * CHUNK
    to_bias = lambda ok: jnp.where(ok, 0.0, NEG).astype(BF16)
    meta_bias = to_bias(ci >= PAD_FRONT)
    n_keys = 4 * CHUNK
    low_c = ci < HEAD_DIM
    m0 = jnp.where(low_c, 1.0, 0.0).astype(BF16)
    m1 = jnp.where(low_c, 0.0, 1.0).astype(BF16)
    low_r = ri < HEAD_DIM
    rm0 = jnp.where(low_r, 1.0, 0.0).astype(BF16)
    rm1 = jnp.where(low_r, 0.0, 1.0).astype(BF16)
    eye = jnp.where(ri == ci, 1.0, 0.0).astype(BF16)
    row_sums = jnp.concatenate([m0] * 4 + [m1] * 4, axis=0)
    pairs_per_kv = N_ATT_HEADS // N_KV_HEADS // 2
    rows = lambda j: slice(j * CHUNK, (j + 1) * CHUNK)

    def split_values(ref, blk, g):
        x = ref[blk, g * LANES:(g + 1) * LANES]
        return x * m0, x * m1

    def split_keys(ref, idx, g):
        xt = ref[idx, g * LANES:(g + 1) * LANES, :]
        return xt * rm0, xt * rm1

    def blocks(split, at, prev_ref, main_ref, next_ref, meta_ref, g):
        out = [split(prev_ref, at(0), g)]
        out += [split(main_ref, at(j), g) for j in range(chunks_per_step)]
        out += [split(next_ref, at(0), g), split(meta_ref, at(0), g)]
        return out

    k_blocks = [blocks(split_keys, lambda j: j, kp_ref, kc_ref, kn_ref, km_ref, g) for g in range(N_KV_HEADS)]
    v_blocks = [blocks(split_values, rows, vp_ref, vc_ref, vn_ref, vm_ref, g) for g in range(N_KV_HEADS)]

    def pair_blocks(blks, j, axis):
        use = [blks[j], blks[j + 1], blks[j + 2], blks[-1]]
        return jnp.concatenate([b[0] for b in use] + [b[1] for b in use], axis=axis)

    def pair_ids(g):
        return [g * pairs_per_kv + pp for pp in range(pairs_per_kv)]

    def scores(j, g):
        c = first_chunk + j
        prev_off = jnp.where(c >= 2, 0, never)
        cur_off = jnp.where(c >= 1, 0, never)
        next_off = jnp.where(c + 1 <= n_chunks - 1, 0, never)
        bias = jnp.concatenate([to_bias(ci >= ri + prev_off), to_bias(ci >= cur_off),
                                to_bias(ci + next_off <= ri), meta_bias] * 2, axis=1)
        k_ext = jnp.concatenate([pair_blocks(k_blocks[g], j, 1), bias], axis=0)
        q_ext = jnp.concatenate(
            [jnp.concatenate([q_ref[rows(j), p * LANES:(p + 1) * LANES], eye], axis=1) for p in pair_ids(g)], axis=0)
        return jnp.dot(q_ext, k_ext, preferred_element_type=F32)

    def softmax_parts(g, s):
        probs, sink_rows = [], []
        for pp, p in enumerate(pair_ids(g)):
            halves, sink_terms = [], []
            for hh in range(2):
                sh = s[rows(pp), hh * n_keys:(hh + 1) * n_keys]
                sk = sink_ref[2 * p + hh:2 * p + hh + 1, 0:1]
                m = jnp.maximum(jnp.max(sh, axis=-1, keepdims=True), sk)
                halves.append(jnp.exp2(sh - m).astype(BF16))
                sink_terms.append(jnp.exp2(sk - m))
            probs.append(jnp.concatenate(halves, axis=1))
            sink_rows.append(jnp.where(low_c, sink_terms[0], sink_terms[1]))
        return jnp.concatenate(probs, axis=0), sink_rows

    def weighted_values(j, g, probs, sink_rows):
        v_ext = jnp.concatenate([pair_blocks(v_blocks[g], j, 0), row_sums], axis=1)
        ol = jnp.dot(probs, v_ext, preferred_element_type=F32)
        return [ol[rows(pp), :LANES] / (ol[rows(pp), LANES:] + sink_rows[pp]) for pp in range(pairs_per_kv)]

    chunk_group = 4
    for j0 in range(0, chunks_per_step, chunk_group):
        units = [(j, g) for j in range(j0, min(j0 + chunk_group, chunks_per_step)) for g in range(N_KV_HEADS)]
        s = [scores(j, g) for j, g in units]
        parts = [softmax_parts(g, s[u]) for u, (j, g) in enumerate(units)]
        outs = [weighted_values(j, g, *parts[u]) for u, (j, g) in enumerate(units)]
        for j in sorted({j for j, _ in units}):
            att = jnp.concatenate([o for u, (ju, _) in enumerate(units) if ju == j for o in outs[u]], axis=1)
            row = lax.broadcasted_iota(I32, att.shape, 0)
            pad_rows = jnp.where(first_chunk + j == 0, PAD_FRONT, 0)
            att = jnp.where(row >= pad_rows, att, 0.0)
            o_ref[rows(j), :] = _rms(att, gain_ref[...]).astype(BF16)


def _attention(aq, ak, av, sink, gain, batch, seq_rows):
    n_chunks = seq_rows // CHUNK
    per_step = next(r for r in (11, 13, 3, 5, 4, 2, 1) if n_chunks % r == 0)
    n_steps = n_chunks // per_step
    q3 = aq.reshape(batch, seq_rows, ATT_WIDTH)
    v3 = av.reshape(batch, seq_rows, 2 * KV_WIDTH)
    prev_chunk = lambda s: jnp.maximum(s * per_step - 1, 0)
    next_chunk = lambda s: jnp.minimum((s + 1) * per_step, n_chunks - 1)
    v_one = lambda fn: pl.BlockSpec((None, CHUNK, 2 * KV_WIDTH), lambda b, s: (b, fn(s), 0))
    v_specs = [v_one(prev_chunk), pl.BlockSpec((None, per_step * CHUNK, 2 * KV_WIDTH), lambda b, s: (b, s, 0)),
               v_one(next_chunk), v_one(lambda s: 0)]
    k_one = lambda fn: pl.BlockSpec((1, 2 * KV_WIDTH, CHUNK), lambda b, s: (b * n_chunks + fn(s), 0, 0))
    k_specs = [k_one(prev_chunk),
               pl.BlockSpec((per_step, 2 * KV_WIDTH, CHUNK), lambda b, s: (b * n_steps + s, 0, 0)),
               k_one(next_chunk), k_one(lambda s: 0)]
    q_spec = pl.BlockSpec((None, per_step * CHUNK, ATT_WIDTH), lambda b, s: (b, s, 0))
    out = pl.pallas_call(
        functools.partial(_attention_kernel, n_chunks=n_chunks, chunks_per_step=per_step),
        grid=(batch, n_steps),
        in_specs=[q_spec] + k_specs + v_specs + [_resident((N_ATT_HEADS, LANES)), _resident((1, ATT_WIDTH))],
        out_specs=q_spec,
        out_shape=jax.ShapeDtypeStruct((batch, seq_rows, ATT_WIDTH), BF16),
        compiler_params=pltpu.CompilerParams(dimension_semantics=("parallel", "parallel"),
                                             vmem_limit_bytes=32 * MIB),
        name="attention",
    )(q3, ak, ak, ak, ak, v3, v3, v3, v3, sink, gain)
    return out.reshape(batch * seq_rows, ATT_WIDTH)


def _mixed_residual(ret_ref, att_ref, w_ref, h_ref, rows=slice(None)):
    acc = jnp.dot(ret_ref[rows, :], w_ref[0:RET_WIDTH, :], preferred_element_type=F32)
    acc = acc + jnp.dot(att_ref[rows, :], w_ref[RET_WIDTH:, :], preferred_element_type=F32)
    return h_ref[rows, :] + acc


def _mix_specs(tm):
    row_spec = lambda width: pl.BlockSpec((tm, width), lambda i, *_: (i, 0))
    return [row_spec(RET_WIDTH), row_spec(ATT_WIDTH), _resident((RET_WIDTH + ATT_WIDTH, D_MODEL)),
            row_spec(D_MODEL)]


def _ff_halves(d_ff):
    n_mxu_tiles = pl.cdiv(d_ff, MXU_V7X_COLUMNS)
    first = min(d_ff, pl.cdiv(n_mxu_tiles, 2) * MXU_V7X_COLUMNS)
    return tuple((c0, c1) for c0, c1 in ((0, first), (first, d_ff)) if c1 > c0)


def _ffn_kernel(ret_ref, att_ref, wo_ref, h_ref, gain_ref, wg_ref, wu_ref, wd_ref, o_ref):
    x = _mixed_residual(ret_ref, att_ref, wo_ref, h_ref)
    f = _rms(x, gain_ref[...]).astype(BF16)
    acc = x
    for c0, c1 in _ff_halves(wg_ref.shape[1]):
        gate = jnp.dot(f, wg_ref[:, c0:c1], preferred_element_type=F32)
        up = jnp.dot(f, wu_ref[:, c0:c1], preferred_element_type=F32)
        act = (_silu(gate) * up).astype(BF16)
        acc = acc + jnp.dot(act, wd_ref[c0:c1, :], preferred_element_type=F32)
    o_ref[...] = acc


def _ffn(ret, att, w_out, h, gain, wg, wu, wd):
    rows = h.shape[0]
    d_ff = wg.shape[1]
    tm = _row_tile(rows, (640, 512, 384, 256, 128))
    return pl.pallas_call(
        _ffn_kernel,
        grid=(rows // tm,),
        in_specs=_mix_specs(tm) + [_resident((1, D_MODEL)), _resident((D_MODEL, d_ff)), _resident((D_MODEL, d_ff)),
                                   _resident((d_ff, D_MODEL))],
        out_specs=pl.BlockSpec((tm, D_MODEL), lambda i: (i, 0)),
        out_shape=jax.ShapeDtypeStruct((rows, D_MODEL), F32),
        input_output_aliases={3: 0},
        compiler_params=pltpu.CompilerParams(dimension_semantics=("parallel",), vmem_limit_bytes=56 * MIB),
        name="ffn",
    )(ret, att, w_out, h, gain, wg, wu, wd)


ROUTE_G1, ROUTE_G2, ROUTE_D1, ROUTE_D2 = range(4)
TILE_BASE, TILE_COUNT, TILE_START = range(3)


def _lane_pick(x, lane, j):
    return jnp.sum(jnp.where(lane == j, x, 0.0), axis=-1, keepdims=True)


def _lane_scalar(vec, lane_row, e):
    return jnp.sum(jnp.where(lane_row == e, vec, 0.0)).astype(I32)


RUN_ALIGN_BITS = 3
RUN_ALIGN = 1 << RUN_ALIGN_BITS


def _run_piece_count(tm):
    return tm.bit_length() - RUN_ALIGN_BITS


def _run_copies(count, src0, dst0, src_ref, dst_ref, sems):
    copies = []
    for j in range(sems.shape[0]):
        k = j + RUN_ALIGN_BITS
        size = 1 << k
        start = (count >> (k + 1)) << (k + 1)
        src = src_ref.at[pl.ds(pl.multiple_of(src0 + start, RUN_ALIGN), size)]
        dst = dst_ref.at[pl.ds(pl.multiple_of(dst0 + start, RUN_ALIGN), size)]
        copies.append((((count >> k) & 1) == 1, pltpu.make_async_copy(src, dst, sems.at[j])))
    return copies


def _start_then_wait(copies):
    for cond, cp in copies:
        pl.when(cond)(cp.start)
    for cond, cp in copies:
        pl.when(cond)(cp.wait)


def _one_hot_rows(dest, width):
    col = lax.broadcasted_iota(I32, (dest.shape[0], width), 1)
    return jnp.where(col == dest.astype(I32), 1.0, 0.0)


def _router_kernel(*refs, region_rows, finalize, aliased, tiles_per_step):
    ret_ref, att_ref, wo_ref, h_ref, gain_ref, wr_ref, base_ref = refs[:7]
    refs = refs[7 + (1 if aliased else 0):]
    hout_ref, xs_ref, route_ref, tile_ref, cnt_ref, sorted_ref, zero_ref, sems, zsem = refs
    i = pl.program_id(0)
    tm = h_ref.shape[0] // tiles_per_step
    n_sorted = sorted_ref.shape[1]

    @pl.when(i == 0)
    def _():
        cnt_ref[...] = base_ref[...]

    lane = lax.broadcasted_iota(I32, (tm, LANES), 1)
    lane_row = lane[0:1, :]
    ri = lax.broadcasted_iota(I32, (tm, tm), 0)
    ci = lax.broadcasted_iota(I32, (tm, tm), 1)
    earlier_tokens = jnp.where(ri > ci, 1.0, 0.0).astype(BF16)
    li = lax.broadcasted_iota(I32, (LANES, LANES), 0)
    lj = lax.broadcasted_iota(I32, (LANES, LANES), 1)
    earlier_experts = jnp.where(li < lj, 1.0, 0.0).astype(BF16)
    slot = lax.broadcasted_iota(I32, (n_sorted, tm), 0)
    sub = lax.broadcasted_iota(I32, (SUBLANES, LANES), 0)

    def as_row(col):
        wide = jnp.broadcast_to(col, (tm, LANES))
        return jnp.concatenate([wide[c * LANES:(c + 1) * LANES, :].T for c in range(tm // LANES)], axis=1)[0:1, :]

    recs = [dict(t=t, rows=slice(t * tm, (t + 1) * tm)) for t in range(tiles_per_step)]

    def stage(fn):
        for rec in recs:
            rec.update(fn(rec))

    def mix(r):
        h_mixed = _mixed_residual(ret_ref, att_ref, wo_ref, h_ref, r["rows"])
        hout_ref[r["rows"], :] = h_mixed
        return dict(f=_rms(h_mixed, gain_ref[...]))

    def logits(r):
        f_hi = r["f"].astype(BF16)
        f_lo = (r["f"] - f_hi.astype(F32)).astype(BF16)
        both = (jnp.dot(f_hi, wr_ref[...], preferred_element_type=F32)
                + jnp.dot(f_lo, wr_ref[...], preferred_element_type=F32))
        lg = jnp.where(lane < N_EXPERTS, both[:, :LANES] + both[:, LANES:], -jnp.inf)
        return dict(f_hi=f_hi, lg=lg)

    def first_choice(r):
        m1 = jnp.max(r["lg"], axis=-1, keepdims=True)
        return dict(m1=m1, i1=jnp.min(jnp.where(r["lg"] == m1, lane, LANES), axis=-1, keepdims=True))

    def second_choice(r):
        lg2 = jnp.where(lane == r["i1"], -jnp.inf, r["lg"])
        m2 = jnp.max(lg2, axis=-1, keepdims=True)
        return dict(m2=m2, i2=jnp.min(jnp.where(lg2 == m2, lane, LANES), axis=-1, keepdims=True))

    def rank_tokens(r):
        active = jnp.max(jnp.abs(r["f"]), axis=-1, keepdims=True) > 0.0
        sel = jnp.where(active, jnp.where(lane == r["i1"], 1.0, 0.0) + jnp.where(lane == r["i2"], 1.0, 0.0), 0.0)
        rank = jnp.dot(earlier_tokens, sel.astype(BF16), preferred_element_type=F32)
        tiles_per_run = jnp.floor((jnp.sum(sel, axis=0, keepdims=True) + (RUN_ALIGN - 1.0)) * (1.0 / RUN_ALIGN))
        start = RUN_ALIGN * jnp.dot(jnp.broadcast_to(tiles_per_run, (SUBLANES, LANES)).astype(BF16),
                                    earlier_experts, preferred_element_type=F32)[0:1, :]
        return dict(active=active, rank=rank, count=tiles_per_run * RUN_ALIGN, start=start)

    def destinations(r):
        dest = r["start"] + r["rank"]
        no_slot = -1.0
        d1 = jnp.where(r["active"], _lane_pick(dest, lane, r["i1"]), no_slot)
        d2 = jnp.where(r["active"], _lane_pick(dest, lane, r["i2"]), no_slot)
        e2 = jnp.exp(r["m2"] - r["m1"])
        g1 = 1.0 / (1.0 + e2)
        route = jnp.zeros((tm, LANES), F32)
        for j, val in ((ROUTE_G1, g1), (ROUTE_G2, e2 * g1), (ROUTE_D1, d1), (ROUTE_D2, d2)):
            route = jnp.where(lane == j, val, route)
        route_ref[r["rows"], :] = route
        return dict(d1=d1, d2=d2)

    def sort_rows(r):
        onehot_t = (jnp.where(slot == as_row(r["d1"]).astype(I32), 1.0, 0.0)
                    + jnp.where(slot == as_row(r["d2"]).astype(I32), 1.0, 0.0))
        sorted_ref[r["t"]] = jnp.dot(onehot_t.astype(BF16), r["f_hi"], preferred_element_type=F32)
        return {}

    for fn in (mix, logits, first_choice, second_choice, rank_tokens, destinations, sort_rows):
        stage(fn)
    base = cnt_ref[0:1, :]
    for rec in recs:
        rec["base"] = base
        tile_ref[rec["t"]] = jnp.where(sub == TILE_BASE, base, jnp.where(
            sub == TILE_COUNT, rec["count"], jnp.where(sub == TILE_START, rec["start"], 0.0)))
        base = base + rec["count"]
    copies = []
    for t, rec in enumerate(recs):
        for e in range(N_EXPERTS):
            n_e = _lane_scalar(rec["count"], lane_row, e)
            src0 = _lane_scalar(rec["start"], lane_row, e)
            dst0 = e * region_rows + _lane_scalar(rec["base"], lane_row, e)
            copies += _run_copies(n_e, src0, dst0, sorted_ref.at[t], xs_ref, sems.at[t, e])
    _start_then_wait(copies)
    cnt_ref[...] = jnp.broadcast_to(base, cnt_ref.shape)

    if finalize:
        @pl.when(i == pl.num_programs(0) - 1)
        def _():
            zero_ref[...] = jnp.zeros(zero_ref.shape, F32)
            total = cnt_ref[0:1, :]
            for e in range(N_EXPERTS):
                end = pl.multiple_of(e * region_rows + _lane_scalar(total, lane_row, e), RUN_ALIGN)
                cp = pltpu.make_async_copy(zero_ref, xs_ref.at[pl.ds(end, EXPERT_ROW_TILE)], zsem)
                cp.start()
                cp.wait()


def _router_tile(rows):
    return _row_tile(rows, (512, 640, 384, 256, 128))


def _sorted_rows(tm):
    return TOP_K * tm + pl.cdiv(N_EXPERTS * (RUN_ALIGN - 1), LANES) * LANES


def _router(mix, gain, w_router, base_counts, xs, region_rows, finalize):
    rows = mix[3].shape[0]
    tm = _router_tile(rows)
    n_tiles = rows // tm
    per_step = 2 if n_tiles % 2 == 0 else 1
    step_rows = per_step * tm
    aliased = xs is not None
    row_spec = lambda width: pl.BlockSpec((step_rows, width), lambda i: (i, 0))
    in_specs = _mix_specs(step_rows) + [_resident((1, D_MODEL)), _resident((D_MODEL, 2 * LANES)),
                                        _resident((SUBLANES, LANES))]
    args = [*mix, gain, w_router, base_counts]
    aliases = {3: 0}
    if aliased:
        aliases[len(args)] = 1
        in_specs.append(pl.BlockSpec(memory_space=pl.ANY))
        args.append(xs)
    n_pieces = _run_piece_count(tm)
    return pl.pallas_call(
        functools.partial(_router_kernel, region_rows=region_rows, finalize=finalize, aliased=aliased,
                          tiles_per_step=per_step),
        grid=(n_tiles // per_step,),
        in_specs=in_specs,
        out_specs=[row_spec(D_MODEL), pl.BlockSpec(memory_space=pl.ANY), row_spec(LANES),
                   pl.BlockSpec((per_step, SUBLANES, LANES), lambda i: (i, 0, 0)),
                   pl.BlockSpec((SUBLANES, LANES), lambda i: (0, 0))],
        out_shape=[jax.ShapeDtypeStruct((rows, D_MODEL), F32),
                   jax.ShapeDtypeStruct((N_EXPERTS * region_rows, D_MODEL), F32),
                   jax.ShapeDtypeStruct((rows, LANES), F32),
                   jax.ShapeDtypeStruct((n_tiles, SUBLANES, LANES), F32),
                   jax.ShapeDtypeStruct((SUBLANES, LANES), F32)],
        scratch_shapes=[pltpu.VMEM((per_step, _sorted_rows(tm), D_MODEL), F32),
                        pltpu.VMEM((EXPERT_ROW_TILE, D_MODEL), F32),
                        pltpu.SemaphoreType.DMA((per_step, N_EXPERTS, n_pieces)), pltpu.SemaphoreType.DMA(())],
        input_output_aliases=aliases,
        compiler_params=pltpu.CompilerParams(dimension_semantics=("arbitrary",), vmem_limit_bytes=56 * MIB),
        name="router",
    )(*args)


def _experts_kernel(blk_ref, exp_ref, nvalid_ref, x_ref, wg_ref, wu_ref, wd_ref, *y_refs):
    y_ref = y_refs[-1]
    partial_ref = y_refs[0] if len(y_refs) == 2 else None
    valid = pl.program_id(0) < nvalid_ref[0]

    @pl.when(valid)
    def _():
        x = x_ref[...].astype(BF16)
        gate = jnp.dot(x, wg_ref[...], preferred_element_type=F32)
        up = jnp.dot(x, wu_ref[...], preferred_element_type=F32)
        part = jnp.dot((_silu(gate) * up).astype(BF16), wd_ref[...], preferred_element_type=F32)
        y_ref[...] = part if partial_ref is None else partial_ref[...] + part

    @pl.when(jnp.logical_not(valid))
    def _():
        y_ref[...] = jnp.zeros(y_ref.shape, F32) if partial_ref is None else partial_ref[...]


def _experts(xs, tile_block, tile_expert, n_valid, wg, wu, wd):
    n_tiles = tile_block.shape[0]
    d_ff = wg.shape[2]
    ff_chunk = d_ff // 2
    assert ff_chunk % MXU_V7X_COLUMNS == 0
    tg = EXPERT_ROW_TILE
    y = None
    for c in range(d_ff // ff_chunk):
        in_specs = [pl.BlockSpec((tg, D_MODEL), lambda i, blk, ex, nv: (blk[i], 0)),
                    pl.BlockSpec((None, D_MODEL, ff_chunk), lambda i, blk, ex, nv, c=c: (ex[i], 0, c)),
                    pl.BlockSpec((None, D_MODEL, ff_chunk), lambda i, blk, ex, nv, c=c: (ex[i], 0, c)),
                    pl.BlockSpec((None, ff_chunk, D_MODEL), lambda i, blk, ex, nv, c=c: (ex[i], c, 0))]
        args = [tile_block, tile_expert, n_valid, xs, wg, wu, wd]
        y_spec = pl.BlockSpec((tg, D_MODEL), lambda i, blk, ex, nv: (i, 0))
        if y is not None:
            in_specs.append(y_spec)
            args.append(y)
        y = pl.pallas_call(
            _experts_kernel,
            grid_spec=pltpu.PrefetchScalarGridSpec(num_scalar_prefetch=3, grid=(n_tiles,), in_specs=in_specs,
                                                   out_specs=y_spec),
            out_shape=jax.ShapeDtypeStruct((n_tiles * tg, D_MODEL), F32),
            input_output_aliases={len(args) - 1: 0} if len(args) == 8 else {},
            compiler_params=pltpu.CompilerParams(dimension_semantics=("arbitrary",), vmem_limit_bytes=56 * MIB),
            name="experts",
        )(*args)
    return y


def _combine_kernel(tinfo_ref, yoff_ref, h_ref, route_ref, ys_ref, o_ref, sorted_ref, sems, *unpad_scratch,
                    seq_chunks):
    i = pl.program_id(0)
    n_steps = pl.num_programs(0)
    n_sorted = sorted_ref.shape[1]
    slot = i % 2

    def tile_copies(tile, buf, live):
        copies = []
        for e in range(N_EXPERTS):
            rec = (tile * 3) * N_EXPERTS + e
            base = tinfo_ref[rec + TILE_BASE * N_EXPERTS]
            n_e = jnp.where(live, tinfo_ref[rec + TILE_COUNT * N_EXPERTS], 0)
            dst0 = tinfo_ref[rec + TILE_START * N_EXPERTS]
            copies += _run_copies(n_e, yoff_ref[e] + base, dst0, ys_ref, sorted_ref.at[buf], sems.at[buf, e])
        return copies

    @pl.when(i == 0)
    def _():
        sorted_ref[...] = jnp.zeros(sorted_ref.shape, F32)
        for cond, cp in tile_copies(0, 0, True):
            pl.when(cond)(cp.start)

    for cond, cp in tile_copies(jnp.minimum(i + 1, n_steps - 1), 1 - slot, i + 1 < n_steps):
        pl.when(cond)(cp.start)
    for cond, cp in tile_copies(i, slot, True):
        pl.when(cond)(cp.wait)
    y = sorted_ref[slot].astype(BF16)
    route = route_ref[...]
    lane = lax.broadcasted_iota(I32, route.shape, 1)
    pick = lambda j: _lane_pick(route, lane, j)
    y1 = jnp.dot(_one_hot_rows(pick(ROUTE_D1), n_sorted).astype(BF16), y, preferred_element_type=F32)
    y2 = jnp.dot(_one_hot_rows(pick(ROUTE_D2), n_sorted).astype(BF16), y, preferred_element_type=F32)
    result = h_ref[...] + pick(ROUTE_G1) * y1 + pick(ROUTE_G2) * y2
    if seq_chunks is None:
        o_ref[...] = result
        return

    res_ref, out_sems = unpad_scratch
    chunks_per_tile = h_ref.shape[0] // CHUNK

    def chunk_copies(step, buf):
        copies = []
        for j in range(chunks_per_tile):
            chunk = step * chunks_per_tile + j
            seq = chunk // seq_chunks
            in_seq = chunk - seq * seq_chunks
            dst = pl.multiple_of(seq * ((seq_chunks - 1) * CHUNK) + (in_seq - 1) * CHUNK, CHUNK)
            cp = pltpu.make_async_copy(res_ref.at[buf, pl.ds(j * CHUNK, CHUNK)], o_ref.at[pl.ds(dst, CHUNK)],
                                       out_sems.at[buf, j])
            copies.append((jnp.logical_and(step >= 0, in_seq >= 1), cp))
        return copies

    res_ref[slot] = result
    for cond, cp in chunk_copies(i, slot):
        pl.when(cond)(cp.start)
    for cond, cp in chunk_copies(i - 1, 1 - slot):
        pl.when(cond)(cp.wait)

    @pl.when(i == n_steps - 1)
    def _():
        for cond, cp in chunk_copies(i, slot):
            pl.when(cond)(cp.wait)


def _combine(h, route, tile_info, y_offsets, ys, unpad_seq_rows=None):
    rows = h.shape[0]
    tm = rows // tile_info.shape[0]
    n_pieces = _run_piece_count(tm)
    tinfo = tile_info[:, :3, :N_EXPERTS].astype(I32).reshape(-1)
    scratch = [pltpu.VMEM((2, _sorted_rows(tm), D_MODEL), F32), pltpu.SemaphoreType.DMA((2, N_EXPERTS, n_pieces))]
    if unpad_seq_rows is None:
        seq_chunks, out_rows = None, rows
        out_spec = pl.BlockSpec((tm, D_MODEL), lambda i, *_: (i, 0))
    else:
        seq_chunks = unpad_seq_rows // CHUNK
        out_rows = rows // unpad_seq_rows * (unpad_seq_rows - CHUNK)
        out_spec = pl.BlockSpec(memory_space=pl.ANY)
        scratch += [pltpu.VMEM((2, tm, D_MODEL), F32), pltpu.SemaphoreType.DMA((2, tm // CHUNK))]
    grid_spec = pltpu.PrefetchScalarGridSpec(
        num_scalar_prefetch=2,
        grid=(rows // tm,),
        in_specs=[pl.BlockSpec((tm, D_MODEL), lambda i, *_: (i, 0)),
                  pl.BlockSpec((tm, LANES), lambda i, *_: (i, 0)),
                  pl.BlockSpec(memory_space=pl.ANY)],
        out_specs=out_spec,
        scratch_shapes=scratch,
    )
    return pl.pallas_call(
        functools.partial(_combine_kernel, seq_chunks=seq_chunks),
        grid_spec=grid_spec,
        out_shape=jax.ShapeDtypeStruct((out_rows, D_MODEL), F32),
        input_output_aliases={2: 0} if unpad_seq_rows is None else {},
        compiler_params=pltpu.CompilerParams(dimension_semantics=("arbitrary",), vmem_limit_bytes=48 * MIB),
        name="combine",
    )(tinfo, y_offsets, h, route, ys)


def _routed_experts(mixes, gain, router_w, expert_w, unpad_seq_rows):
    tg = EXPERT_ROW_TILE
    group_rows = [mix[3].shape[0] for mix in mixes]
    total_rows = sum(group_rows)
    run_pad = (RUN_ALIGN - 1) * sum(rows // _router_tile(rows) for rows in group_rows)
    region_rows = (pl.cdiv(total_rows + run_pad, tg) + 1) * tg
    counts = jnp.zeros((SUBLANES, LANES), F32)
    xs, hs, routes, tiles = None, [], [], []
    for g, mix in enumerate(mixes):
        h, xs, route, tile_info, counts = _router(mix, gain, router_w, counts, xs, region_rows,
                                                  finalize=(g == len(mixes) - 1))
        hs.append(h)
        routes.append(route)
        tiles.append(tile_info)
    n_e = counts[0, :N_EXPERTS].astype(I32)
    tiles_e = (n_e + tg - 1) // tg
    first_tile = jnp.cumsum(tiles_e) - tiles_e
    n_valid = jnp.sum(tiles_e)
    n_tiles = (TOP_K * total_rows + N_EXPERTS * (run_pad + tg - 1)) // tg + 1
    t = jnp.minimum(jnp.arange(n_tiles, dtype=I32), n_valid - 1)
    tile_expert = jnp.sum((t[:, None] >= (first_tile + tiles_e)[None, :]).astype(I32), axis=1)
    tile_block = tile_expert * (region_rows // tg) + (t - first_tile[tile_expert])
    ys = _experts(xs, tile_block.astype(I32), tile_expert.astype(I32), n_valid.reshape(1).astype(I32), *expert_w)
    y_offsets = (first_tile * tg).astype(I32)
    return [_combine(h, route, tile_info, y_offsets, ys, seq_rows)
            for h, route, tile_info, seq_rows in zip(hs, routes, tiles, unpad_seq_rows)]


def _rotary_tables(batch, seq_rows):
    pos = (jnp.arange(seq_rows, dtype=I32) - PAD_FRONT).astype(F32)

    def cos_sin(n_rot, theta):
        half = n_rot // 2
        inv = theta ** (-jnp.arange(half, dtype=F32) * 2.0 / n_rot)
        ang = pos[:, None] * inv[None, :]
        return jnp.cos(ang), jnp.sin(ang)

    cos, sin = cos_sin(HEAD_DIM, RET_ROPE_THETA)
    cr = jnp.tile(cos, (1, 2 * LANES // HEAD_DIM))
    sr = jnp.tile(jnp.concatenate([-sin, sin], axis=1), (1, LANES // HEAD_DIM))
    cos, sin = cos_sin(ROPE_DIMS, ROPE_THETA)
    rest = HEAD_DIM - ROPE_DIMS
    ca = jnp.tile(jnp.concatenate([cos, cos, jnp.ones((seq_rows, rest), F32)], axis=1), (1, LANES // HEAD_DIM))
    sa = jnp.tile(jnp.concatenate([-sin, sin, jnp.zeros((seq_rows, rest), F32)], axis=1), (1, LANES // HEAD_DIM))
    return tuple(jnp.tile(t, (batch, 1)) for t in (cr, sr, ca, sa))


def _retention_params(log_gf, log_gb, gain):
    n_pairs = RET_WIDTH // LANES
    per_lane = lambda t: jnp.repeat(t.astype(F32).reshape(n_pairs, 2), HEAD_DIM, axis=1)
    per_head = lambda t, j: jnp.broadcast_to(t.astype(F32).reshape(n_pairs, 2)[:, j:j + 1], (n_pairs, LANES))
    rows = [per_lane(log_gf), per_lane(log_gb), per_head(log_gf, 0), per_head(log_gf, 1),
            per_head(log_gb, 0), per_head(log_gb, 1), gain.astype(F32).reshape(n_pairs, LANES),
            jnp.zeros((n_pairs, LANES), F32)]
    return jnp.stack(rows, axis=1)


def _to_padded_rows(x, meta_tokens):
    batch, seq, _ = x.shape
    h = jnp.concatenate([jnp.zeros((batch, PAD_FRONT, D_MODEL), F32),
                         jnp.broadcast_to(meta_tokens.astype(F32)[None], (batch, N_META, D_MODEL)),
                         x.astype(F32)], axis=1)
    return h.reshape(batch * (seq + CHUNK), D_MODEL)


def _token_mixer(h, lp, tabs, batch, seq_rows):
    rq, rk, rv, rg, aq, ak, av = _inproj(h, lp["norm_mix"], lp["w_in"], tabs, lp["q_gain"], lp["k_gain"],
                                         lp["ones_bd"])
    ret = _retention(rq, rk, rv, rg, lp["ret_dec"], batch, seq_rows)
    att = _attention(aq, ak, av, lp["sink"], lp["att_gain"], batch, seq_rows)
    return ret, att, lp["w_out"], h


def kernel(x_prompt, x_sample, meta_tokens, norm_mix, w_in, ret_log_decay_fwd, ret_log_decay_bwd, ret_out_gain,
           q_norm_gain, k_norm_gain, attn_sink, attn_out_gain, w_out, norm_ffn, ffn_w_gate, ffn_w_up, ffn_w_down,
           moe_router, moe_w_gate, moe_w_up, moe_w_down):
    depth = w_in.shape[0]
    ri = lax.broadcasted_iota(I32, (LANES, LANES), 0)
    ci = lax.broadcasted_iota(I32, (LANES, LANES), 1)
    ones_bd = ((ri < HEAD_DIM) == (ci < HEAD_DIM)).astype(BF16)
    row = lambda t: t.astype(F32).reshape(1, -1)

    xs = (x_prompt, x_sample)
    shapes = [(x.shape[0], x.shape[1] + CHUNK) for x in xs]
    hs = [_to_padded_rows(x, meta_tokens) for x in xs]
    tabs = [_rotary_tables(b, r) for b, r in shapes]

    for l in range(depth):
        lp = {
            "norm_mix": row(norm_mix[l]),
            "w_in": w_in[l].astype(BF16),
            "q_gain": jnp.tile(row(q_norm_gain[l]), (1, LANES // HEAD_DIM)),
            "k_gain": jnp.tile(row(k_norm_gain[l]), (1, LANES // HEAD_DIM)),
            "ones_bd": ones_bd,
            "ret_dec": _retention_params(ret_log_decay_fwd[l], ret_log_decay_bwd[l], ret_out_gain[l]),
            "sink": jnp.broadcast_to(attn_sink[l].astype(F32)[:, None] * LOG2_E, (N_ATT_HEADS, LANES)),
            "att_gain": row(attn_out_gain[l]),
            "w_out": w_out[l].astype(BF16),
        }
        mixes = [_token_mixer(h, lp, tab, b, r) for h, tab, (b, r) in zip(hs, tabs, shapes)]
        i = l // 2
        if l % 2 == 0:
            w = (ffn_w_gate[i].astype(BF16), ffn_w_up[i].astype(BF16), ffn_w_down[i].astype(BF16))
            hs = [_ffn(*mix, row(norm_ffn[l]), *w) for mix in mixes]
        else:
            wr = jnp.pad(moe_router[i].astype(F32), ((0, 0), (0, LANES - N_EXPERTS)))
            wr_hi = wr.astype(BF16)
            router_w = jnp.concatenate([wr_hi, (wr - wr_hi.astype(F32)).astype(BF16)], axis=1)
            expert_w = (moe_w_gate[i].astype(BF16), moe_w_up[i].astype(BF16), moe_w_down[i].astype(BF16))
            last = l == depth - 1
            hs = _routed_experts(mixes, row(norm_ffn[l]), router_w, expert_w,
                                 [r if last else None for _, r in shapes])
    if depth % 2 == 0:
        return tuple(h.reshape(x.shape).astype(x.dtype) for h, x in zip(hs, xs))
    return tuple(h.reshape(b, r, D_MODEL)[:, CHUNK:].astype(x.dtype) for h, x, (b, r) in zip(hs, xs, shapes))
```

```python
import functools

import jax
import jax.numpy as jnp
from jax import lax
from jax.experimental import pallas as pl
from jax.experimental.pallas import tpu as pltpu

F32 = jnp.float32
BF16 = jnp.bfloat16
I32 = jnp.int32

D_MODEL = 1024
HEAD_DIM = 64
N_RET_HEADS = 8
N_ATT_HEADS = 8
N_KV_HEADS = 2
RET_WIDTH = N_RET_HEADS * HEAD_DIM
ATT_WIDTH = N_ATT_HEADS * HEAD_DIM
KV_WIDTH = N_KV_HEADS * HEAD_DIM
IN_WIDTH = 4 * RET_WIDTH + ATT_WIDTH + 2 * KV_WIDTH
CHUNK = 128
LANES = 128
SUBLANES = 8
N_META = 16
PAD_FRONT = CHUNK - N_META
ROPE_THETA = 500000.0
ROPE_DIMS = HEAD_DIM // 4
RET_ROPE_THETA = 10000.0
N_EXPERTS = 8
TOP_K = 2
EXPERT_ROW_TILE = 512
MXU_V7X_COLUMNS = 256
EPS = 1e-6
NEG = -1e30
QK_SCALE = HEAD_DIM ** -0.5
LOG2_E = 1.4426950408889634
MIB = 1024 * 1024


def _row_tile(rows, prefs):
    for t in prefs:
        if rows % t == 0:
            return t
    raise ValueError(f"no row tile for {rows} rows among {prefs}")


def _resident(shape):
    return pl.BlockSpec(shape, lambda *_: (0,) * len(shape), pipeline_mode=pl.Buffered(1))


def _rms(x, gain):
    ms = jnp.mean(x * x, axis=-1, keepdims=True)
    return x * lax.rsqrt(ms + EPS) * gain


def _silu(x):
    return x * jax.nn.sigmoid(x)


def _embed_kernel(x_ref, meta_ref, *refs, seq_chunks):
    h0_ref, hbuf_ref, sems = refs[-10], refs[-2], refs[-1]
    i = pl.program_id(0)
    n_steps = pl.num_programs(0)
    tm = h0_ref.shape[0]
    chunks_per_tile = tm // CHUNK
    slot = i % 2

    def chunk_place(step, j):
        chunk = step * chunks_per_tile + j
        seq = chunk // seq_chunks
        return seq, chunk - seq * seq_chunks

    def chunk_copies(step, buf):
        copies = []
        for j in range(chunks_per_tile):
            seq, in_seq = chunk_place(step, j)
            src = pl.multiple_of(seq * ((seq_chunks - 1) * CHUNK) + (in_seq - 1) * CHUNK, CHUNK)
            cp = pltpu.make_async_copy(x_ref.at[pl.ds(src, CHUNK)], hbuf_ref.at[buf, pl.ds(j * CHUNK, CHUNK)],
                                       sems.at[buf, j])
            copies.append((jnp.logical_and(step < n_steps, in_seq >= 1), cp))
        return copies

    @pl.when(i == 0)
    def _():
        for cond, cp in chunk_copies(0, 0):
            pl.when(cond)(cp.start)

    for cond, cp in chunk_copies(i + 1, 1 - slot):
        pl.when(cond)(cp.start)
    for cond, cp in chunk_copies(i, slot):
        pl.when(cond)(cp.wait)
    lead = jnp.concatenate([jnp.zeros((PAD_FRONT, D_MODEL), F32), meta_ref[...]], axis=0)
    for j in range(chunks_per_tile):
        @pl.when(chunk_place(i, j)[1] == 0)
        def _():
            hbuf_ref[slot, j * CHUNK:(j + 1) * CHUNK, :] = lead

    h0_ref[...] = hbuf_ref[slot]
    _inproj_kernel(hbuf_ref.at[slot], *refs[:-10], *refs[-9:-2])


def _inproj_kernel(h_ref, gain_ref, w_ref, cr_ref, sr_ref, ca_ref, sa_ref, qg_ref, kg_ref, ones_ref,
                   rq_ref, rk_ref, rv_ref, rg_ref, aq_ref, ak_ref, av_ref):
    tm = h_ref.shape[0]
    part = tm
    lane = lax.broadcasted_iota(I32, (part, LANES), 1)
    in_head = lane & (HEAD_DIM - 1)
    low_head = lane < HEAD_DIM
    cols = lambda j: slice(j * LANES, (j + 1) * LANES)
    n_lane_tiles = RET_WIDTH // LANES
    base = 4 * RET_WIDTH

    def project(r):
        a = _rms(h_ref[r, :], gain_ref[...]).astype(BF16)
        proj = lambda c0, c1: jnp.dot(a, w_ref[:, c0:c1], preferred_element_type=F32)
        return dict(aq=proj(base, base + ATT_WIDTH), kv=proj(base + ATT_WIDTH, IN_WIDTH),
                    rq=proj(0, RET_WIDTH), rk=proj(RET_WIDTH, 2 * RET_WIDTH),
                    rv=proj(2 * RET_WIDTH, 3 * RET_WIDTH), rg=proj(3 * RET_WIDTH, 4 * RET_WIDTH))

    def rotate(x, cos, sin_signed, half):
        partner = jnp.where(in_head < half, pltpu.roll(x, LANES - half, 1), pltpu.roll(x, half, 1))
        return x * cos + partner * sin_signed

    def head_norm(x, g):
        ss = jnp.dot((x * x).astype(BF16), ones_ref[...], preferred_element_type=F32)
        return x * lax.rsqrt(ss * (1.0 / HEAD_DIM) + EPS) * g

    def epilogue(r, first_chunk, p):
        cr, sr, ca, sa = cr_ref[r, :], sr_ref[r, :], ca_ref[r, :], sa_ref[r, :]

        def store_keys(kt_ref, j, x):
            for c in range(part // CHUNK):
                kt_ref[first_chunk + c, cols(j), :] = x[c * CHUNK:(c + 1) * CHUNK, :].T.astype(BF16)

        rv_ref[r, :] = p["rv"].astype(BF16)
        rg_ref[r, :] = p["rg"].astype(BF16)
        for j in range(n_lane_tiles):
            rq_ref[r, cols(j)] = rotate(p["rq"][:, cols(j)], cr, sr, HEAD_DIM // 2).astype(BF16)
        for j in range(n_lane_tiles):
            store_keys(rk_ref, j, rotate(p["rk"][:, cols(j)], cr, sr, HEAD_DIM // 2) * QK_SCALE)
        normed = [head_norm(p["aq"][:, cols(j)], qg_ref[...]) for j in range(ATT_WIDTH // LANES)]
        normed_k = head_norm(p["kv"][:, :KV_WIDTH], kg_ref[...])
        for j, n in enumerate(normed):
            aq_ref[r, cols(j)] = (rotate(n, ca, sa, ROPE_DIMS // 2) * (QK_SCALE * LOG2_E)).astype(BF16)
        k = rotate(normed_k, ca, sa, ROPE_DIMS // 2)
        v = p["kv"][:, KV_WIDTH:]
        k_sw = pltpu.roll(k, HEAD_DIM, 1)
        v_sw = pltpu.roll(v, HEAD_DIM, 1)
        store_keys(ak_ref, 0, jnp.where(low_head, k, k_sw))
        store_keys(ak_ref, 1, jnp.where(low_head, k_sw, k))
        av_ref[r, 0:LANES] = jnp.where(low_head, v, v_sw).astype(BF16)
        av_ref[r, LANES:2 * LANES] = jnp.where(low_head, v_sw, v).astype(BF16)

    whole_tile = slice(0, tm)
    epilogue(whole_tile, 0, project(whole_tile))


def _inproj(h, gain, w, tabs, qg, kg, ones_bd, embed=None):
    rows = tabs[0].shape[0]
    tm = _row_tile(rows, (1024, 640, 512, 384, 256, 128) if embed is None else (512, 640, 384, 256, 128))
    row_spec = lambda width: pl.BlockSpec((tm, width), lambda i: (i, 0))
    outs = ((RET_WIDTH, False), (RET_WIDTH, True), (RET_WIDTH, False), (RET_WIDTH, False),
            (ATT_WIDTH, False), (2 * KV_WIDTH, True), (2 * KV_WIDTH, False))
    chunk_spec = lambda width: pl.BlockSpec((tm // CHUNK, width, CHUNK), lambda i: (i, 0, 0))
    in_specs = [_resident((1, D_MODEL)), _resident((D_MODEL, IN_WIDTH)),
                row_spec(LANES), row_spec(LANES), row_spec(LANES), row_spec(LANES),
                _resident((1, LANES)), _resident((1, LANES)), _resident((LANES, LANES))]
    out_specs = [chunk_spec(wd) if t else row_spec(wd) for wd, t in outs]
    out_shape = [jax.ShapeDtypeStruct((rows // CHUNK, wd, CHUNK) if t else (rows, wd), BF16) for wd, t in outs]
    if embed is None:
        return pl.pallas_call(
            _inproj_kernel,
            grid=(rows // tm,),
            in_specs=[row_spec(D_MODEL)] + in_specs,
            out_specs=out_specs,
            out_shape=out_shape,
            compiler_params=pltpu.CompilerParams(dimension_semantics=("parallel",), vmem_limit_bytes=56 * MIB),
            name="inproj",
        )(h, gain, w, *tabs, qg, kg, ones_bd)
    x, meta, seq_rows = embed
    return pl.pallas_call(
        functools.partial(_embed_kernel, seq_chunks=seq_rows // CHUNK),
        grid=(rows // tm,),
        in_specs=[pl.BlockSpec(memory_space=pl.ANY), _resident((N_META, D_MODEL))] + in_specs,
        out_specs=[row_spec(D_MODEL)] + out_specs,
        out_shape=[jax.ShapeDtypeStruct((rows, D_MODEL), F32)] + out_shape,
        scratch_shapes=[pltpu.VMEM((2, tm, D_MODEL), F32), pltpu.SemaphoreType.DMA((2, tm // CHUNK))],
        compiler_params=pltpu.CompilerParams(dimension_semantics=("arbitrary",), vmem_limit_bytes=56 * MIB),
        name="embed",
    )(x, meta, gain, w, *tabs, qg, kg, ones_bd)


def _retention_kernel(q_ref, k_ref, v_ref, g_ref, dec_ref, o_ref, sb_ref, *, n_chunks):
    lgf, lgb = dec_ref[0:1, :], dec_ref[1:2, :]
    gain = dec_ref[6:7, :]
    ri = lax.broadcasted_iota(I32, (CHUNK, LANES), 0)
    ci = lax.broadcasted_iota(I32, (CHUNK, LANES), 1)
    r = ri.astype(F32)
    diff = (ri - ci).astype(F32)

    def decay_mask(lf, lb):
        return jnp.where(diff >= 0, jnp.exp(jnp.maximum(diff, 0.0) * lf), jnp.exp(jnp.maximum(-diff, 0.0) * lb))

    dm = jnp.concatenate([decay_mask(dec_ref[2:3, :], dec_ref[4:5, :]),
                          decay_mask(dec_ref[3:4, :], dec_ref[5:6, :])], axis=1)
    tok = ci.astype(F32)
    wf = jnp.exp((CHUNK - 1.0 - tok) * jnp.broadcast_to(lgf, (LANES, LANES)).T)
    wb = jnp.exp(tok * jnp.broadcast_to(lgb, (LANES, LANES)).T)
    qf = jnp.exp((r + 1.0) * lgf)
    qb = jnp.exp((CHUNK - r) * lgb)
    cf = jnp.exp(float(CHUNK) * lgf)
    cb = jnp.exp(float(CHUNK) * lgb)
    low_c = ci < HEAD_DIM
    same_head = (ri < HEAD_DIM) == low_c
    ones_bd = jnp.where(same_head, 1.0, 0.0).astype(BF16)
    m0 = jnp.where(low_c, 1.0, 0.0).astype(BF16)
    m1 = jnp.where(low_c, 0.0, 1.0).astype(BF16)
    low_r = ri < HEAD_DIM
    rm0 = jnp.where(low_r, 1.0, 0.0).astype(BF16)
    rm1 = jnp.where(low_r, 0.0, 1.0).astype(BF16)

    def chunk_rows(n):
        return pl.ds(pl.multiple_of(n * CHUNK, CHUNK), CHUNK)

    def pair_rows(x):
        return jnp.concatenate([x * m0, x * m1], axis=0)

    def pair_cols(xt):
        return jnp.concatenate([xt * rm0, xt * rm1], axis=1)

    def state_delta(kt, weights, v):
        u = jnp.dot((kt.astype(F32) * weights).astype(BF16), v, preferred_element_type=F32)
        return jnp.where(same_head, u, 0.0)

    group = next(u for u in (11, 13, 4, 3, 5, 2, 1) if n_chunks % u == 0)
    n_groups = n_chunks // group
    members = range(group)
    zero_state = jnp.zeros((LANES, LANES), F32)

    def backward(i, state):
        ns = [n_chunks - 1 - i * group - j for j in members]
        deltas = [state_delta(k_ref[n], wb, v_ref[chunk_rows(n), :]) for n in ns]
        for n, delta in zip(ns, deltas):
            sb_ref[n] = state.astype(BF16)
            state = cb * state + delta
        return state

    lax.fori_loop(0, n_groups, backward, zero_state)

    def forward(i, state):
        ns = [i * group + j for j in members]
        q = [q_ref[chunk_rows(n), :] for n in ns]
        kt = [k_ref[n] for n in ns]
        v = [v_ref[chunk_rows(n), :] for n in ns]
        deltas = [state_delta(kt[j], wf, v[j]) for j in members]
        s = [jnp.dot(q[j], pair_cols(kt[j]), preferred_element_type=F32) * dm for j in members]
        both = []
        for j in members:
            both.append(jnp.concatenate([state.astype(BF16), sb_ref[ns[j]]], axis=1))
            state = cf * state + deltas[j]
        intra = [jnp.dot(s[j].astype(BF16), pair_rows(v[j]), preferred_element_type=F32) for j in members]
        inter = [jnp.dot(q[j], both[j], preferred_element_type=F32) for j in members]
        o = [intra[j] + inter[j][:, :LANES] * qf + inter[j][:, LANES:] * qb for j in members]
        ss = [jnp.dot((o[j] * o[j]).astype(BF16), ones_bd, preferred_element_type=F32) for j in members]
        for j in members:
            gate = _silu(g_ref[chunk_rows(ns[j]), :].astype(F32))
            o_ref[chunk_rows(ns[j]), :] = (o[j] * lax.rsqrt(ss[j] * (1.0 / HEAD_DIM) + EPS) * gain * gate).astype(BF16)
        return state

    lax.fori_loop(0, n_groups, forward, zero_state)


def _retention(rq, rk, rv, rg, dec, batch, seq_rows):
    n_chunks = seq_rows // CHUNK
    n_pairs = RET_WIDTH // LANES
    view = lambda t: t.reshape(batch, seq_rows, RET_WIDTH)
    seq_spec = pl.BlockSpec((None, seq_rows, LANES), lambda b, p: (b, 0, p))
    key_spec = pl.BlockSpec((n_chunks, LANES, CHUNK), lambda b, p: (b, p, 0))
    out = pl.pallas_call(
        functools.partial(_retention_kernel, n_chunks=n_chunks),
        grid=(batch, n_pairs),
        in_specs=[seq_spec, key_spec, seq_spec, seq_spec,
                  pl.BlockSpec((None, 8, LANES), lambda b, p: (p, 0, 0))],
        out_specs=seq_spec,
        out_shape=jax.ShapeDtypeStruct((batch, seq_rows, RET_WIDTH), BF16),
        scratch_shapes=[pltpu.VMEM((n_chunks, LANES, LANES), BF16)],
        compiler_params=pltpu.CompilerParams(dimension_semantics=("parallel", "parallel"),
                                             vmem_limit_bytes=48 * MIB),
        name="retention",
    )(view(rq), rk, view(rv), view(rg), dec)
    return out.reshape(batch * seq_rows, RET_WIDTH)


def _attention_kernel(q_ref, kp_ref, kc_ref, kn_ref, km_ref, vp_ref, vc_ref, vn_ref, vm_ref,
                      sink_ref, gain_ref, o_ref, *, n_chunks, chunks_per_step):
    first_chunk = pl.program_id(1) * chunks_per_step
    ri = lax.broadcasted_iota(I32, (CHUNK, LANES), 0)
    ci = lax.broadcasted_iota(I32, (CHUNK, LANES), 1)
    never = 2 * CHUNK
    to_bias = lambda ok: jnp.where(ok, 0.0, NEG).astype(BF16)
    meta_bias = to_bias(ci >= PAD_FRONT)
    n_keys = 4 * CHUNK
    low_c = ci < HEAD_DIM
    m0 = jnp.where(low_c, 1.0, 0.0).astype(BF16)
    m1 = jnp.where(low_c, 0.0, 1.0).astype(BF16)
    low_r = ri < HEAD_DIM
    rm0 = jnp.where(low_r, 1.0, 0.0).astype(BF16)
    rm1 = jnp.where(low_r, 0.0, 1.0).astype(BF16)
    eye = jnp.where(ri == ci, 1.0, 0.0).astype(BF16)
    row_sums = jnp.concatenate([m0] * 4 + [m1] * 4, axis=0)
    pairs_per_kv = N_ATT_HEADS // N_KV_HEADS // 2
    rows = lambda j: slice(j * CHUNK, (j + 1) * CHUNK)

    def split_values(ref, blk, g):
        x = ref[blk, g * LANES:(g + 1) * LANES]
        return x * m0, x * m1

    def split_keys(ref, idx, g):
        xt = ref[idx, g * LANES:(g + 1) * LANES, :]
        return xt * rm0, xt * rm1

    def blocks(split, at, prev_ref, main_ref, next_ref, meta_ref, g):
        out = [split(prev_ref, at(0), g)]
        out += [split(main_ref, at(j), g) for j in range(chunks_per_step)]
        out += [split(next_ref, at(0), g), split(meta_ref, at(0), g)]
        return out

    k_blocks = [blocks(split_keys, lambda j: j, kp_ref, kc_ref, kn_ref, km_ref, g) for g in range(N_KV_HEADS)]
    v_blocks = [blocks(split_values, rows, vp_ref, vc_ref, vn_ref, vm_ref, g) for g in range(N_KV_HEADS)]

    def pair_blocks(blks, j, axis):
        use = [blks[j], blks[j + 1], blks[j + 2], blks[-1]]
        return jnp.concatenate([b[0] for b in use] + [b[1] for b in use], axis=axis)

    def pair_ids(g):
        return [g * pairs_per_kv + pp for pp in range(pairs_per_kv)]

    def scores(j, g):
        c = first_chunk + j
        prev_off = jnp.where(c >= 2, 0, never)
        cur_off = jnp.where(c >= 1, 0, never)
        next_off = jnp.where(c + 1 <= n_chunks - 1, 0, never)
        bias = jnp.concatenate([to_bias(ci >= ri + prev_off), to_bias(ci >= cur_off),
                                to_bias(ci + next_off <= ri), meta_bias] * 2, axis=1)
        k_ext = jnp.concatenate([pair_blocks(k_blocks[g], j, 1), bias], axis=0)
        q_ext = jnp.concatenate(
            [jnp.concatenate([q_ref[rows(j), p * LANES:(p + 1) * LANES], eye], axis=1) for p in pair_ids(g)], axis=0)
        return jnp.dot(q_ext, k_ext, preferred_element_type=F32)

    def softmax_parts(g, s):
        probs, sink_rows = [], []
        for pp, p in enumerate(pair_ids(g)):
            halves, sink_terms = [], []
            for hh in range(2):
                sh = s[rows(pp), hh * n_keys:(hh + 1) * n_keys]
                sk = sink_ref[2 * p + hh:2 * p + hh + 1, 0:1]
                m = jnp.maximum(jnp.max(sh, axis=-1, keepdims=True), sk)
                halves.append(jnp.exp2(sh - m).astype(BF16))
                sink_terms.append(jnp.exp2(sk - m))
            probs.append(jnp.concatenate(halves, axis=1))
            sink_rows.append(jnp.where(low_c, sink_terms[0], sink_terms[1]))
        return jnp.concatenate(probs, axis=0), sink_rows

    def weighted_values(j, g, probs, sink_rows):
        v_ext = jnp.concatenate([pair_blocks(v_blocks[g], j, 0), row_sums], axis=1)
        ol = jnp.dot(probs, v_ext, preferred_element_type=F32)
        return [ol[rows(pp), :LANES] / (ol[rows(pp), LANES:] + sink_rows[pp]) for pp in range(pairs_per_kv)]

    chunk_group = 4
    for j0 in range(0, chunks_per_step, chunk_group):
        units = [(j, g) for j in range(j0, min(j0 + chunk_group, chunks_per_step)) for g in range(N_KV_HEADS)]
        s = [scores(j, g) for j, g in units]
        parts = [softmax_parts(g, s[u]) for u, (j, g) in enumerate(units)]
        outs = [weighted_values(j, g, *parts[u]) for u, (j, g) in enumerate(units)]
        for j in sorted({j for j, _ in units}):
            att = jnp.concatenate([o for u, (ju, _) in enumerate(units) if ju == j for o in outs[u]], axis=1)
            row = lax.broadcasted_iota(I32, att.shape, 0)
            pad_rows = jnp.where(first_chunk + j == 0, PAD_FRONT, 0)
            att = jnp.where(row >= pad_rows, att, 0.0)
            o_ref[rows(j), :] = _rms(att, gain_ref[...]).astype(BF16)


def _attention(aq, ak, av, sink, gain, batch, seq_rows):
    n_chunks = seq_rows // CHUNK
    per_step = next(r for r in (11, 13, 3, 5, 4, 2, 1) if n_chunks % r == 0)
    n_steps = n_chunks // per_step
    q3 = aq.reshape(batch, seq_rows, ATT_WIDTH)
    v3 = av.reshape(batch, seq_rows, 2 * KV_WIDTH)
    prev_chunk = lambda s: jnp.maximum(s * per_step - 1, 0)
    next_chunk = lambda s: jnp.minimum((s + 1) * per_step, n_chunks - 1)
    v_one = lambda fn: pl.BlockSpec((None, CHUNK, 2 * KV_WIDTH), lambda b, s: (b, fn(s), 0))
    v_specs = [v_one(prev_chunk), pl.BlockSpec((None, per_step * CHUNK, 2 * KV_WIDTH), lambda b, s: (b, s, 0)),
               v_one(next_chunk), v_one(lambda s: 0)]
    k_one = lambda fn: pl.BlockSpec((1, 2 * KV_WIDTH, CHUNK), lambda b, s: (b * n_chunks + fn(s), 0, 0))
    k_specs = [k_one(prev_chunk),
               pl.BlockSpec((per_step, 2 * KV_WIDTH, CHUNK), lambda b, s: (b * n_steps + s, 0, 0)),
               k_one(next_chunk), k_one(lambda s: 0)]
    q_spec = pl.BlockSpec((None, per_step * CHUNK, ATT_WIDTH), lambda b, s: (b, s, 0))
    out = pl.pallas_call(
        functools.partial(_attention_kernel, n_chunks=n_chunks, chunks_per_step=per_step),
        grid=(batch, n_steps),
        in_specs=[q_spec] + k_specs + v_specs + [_resident((N_ATT_HEADS, LANES)), _resident((1, ATT_WIDTH))],
        out_specs=q_spec,
        out_shape=jax.ShapeDtypeStruct((batch, seq_rows, ATT_WIDTH), BF16),
        compiler_params=pltpu.CompilerParams(dimension_semantics=("parallel", "parallel"),
                                             vmem_limit_bytes=32 * MIB),
        name="attention",
    )(q3, ak, ak, ak, ak, v3, v3, v3, v3, sink, gain)
    return out.reshape(batch * seq_rows, ATT_WIDTH)


def _mixed_residual(ret_ref, att_ref, w_ref, h_ref, rows=slice(None)):
    acc = jnp.dot(ret_ref[rows, :], w_ref[0:RET_WIDTH, :], preferred_element_type=F32)
    acc = acc + jnp.dot(att_ref[rows, :], w_ref[RET_WIDTH:, :], preferred_element_type=F32)
    return h_ref[rows, :] + acc


def _mix_specs(tm):
    row_spec = lambda width: pl.BlockSpec((tm, width), lambda i, *_: (i, 0))
    return [row_spec(RET_WIDTH), row_spec(ATT_WIDTH), _resident((RET_WIDTH + ATT_WIDTH, D_MODEL)),
            row_spec(D_MODEL)]


def _ff_halves(d_ff):
    n_mxu_tiles = pl.cdiv(d_ff, MXU_V7X_COLUMNS)
    first = min(d_ff, pl.cdiv(n_mxu_tiles, 2) * MXU_V7X_COLUMNS)
    return tuple((c0, c1) for c0, c1 in ((0, first), (first, d_ff)) if c1 > c0)


def _ffn_kernel(ret_ref, att_ref, wo_ref, h_ref, gain_ref, wg_ref, wu_ref, wd_ref, o_ref):
    x = _mixed_residual(ret_ref, att_ref, wo_ref, h_ref)
    f = _rms(x, gain_ref[...]).astype(BF16)
    acc = x
    for c0, c1 in _ff_halves(wg_ref.shape[1]):
        gate = jnp.dot(f, wg_ref[:, c0:c1], preferred_element_type=F32)
        up = jnp.dot(f, wu_ref[:, c0:c1], preferred_element_type=F32)
        act = (_silu(gate) * up).astype(BF16)
        acc = acc + jnp.dot(act, wd_ref[c0:c1, :], preferred_element_type=F32)
    o_ref[...] = acc


def _ffn(ret, att, w_out, h, gain, wg, wu, wd):
    rows = h.shape[0]
    d_ff = wg.shape[1]
    tm = _row_tile(rows, (640, 512, 384, 256, 128))
    return pl.pallas_call(
        _ffn_kernel,
        grid=(rows // tm,),
        in_specs=_mix_specs(tm) + [_resident((1, D_MODEL)), _resident((D_MODEL, d_ff)), _resident((D_MODEL, d_ff)),
                                   _resident((d_ff, D_MODEL))],
        out_specs=pl.BlockSpec((tm, D_MODEL), lambda i: (i, 0)),
        out_shape=jax.ShapeDtypeStruct((rows, D_MODEL), F32),
        input_output_aliases={3: 0},
        compiler_params=pltpu.CompilerParams(dimension_semantics=("parallel",), vmem_limit_bytes=56 * MIB),
        name="ffn",
    )(ret, att, w_out, h, gain, wg, wu, wd)


ROUTE_G1, ROUTE_G2, ROUTE_D1, ROUTE_D2 = range(4)
TILE_BASE, TILE_COUNT, TILE_START = range(3)


def _lane_pick(x, lane, j):
    return jnp.sum(jnp.where(lane == j, x, 0.0), axis=-1, keepdims=True)


def _lane_scalar(vec, lane_row, e):
    return jnp.sum(jnp.where(lane_row == e, vec, 0.0)).astype(I32)


RUN_ALIGN_BITS = 3
RUN_ALIGN = 1 << RUN_ALIGN_BITS


def _run_piece_count(tm):
    return tm.bit_length() - RUN_ALIGN_BITS


def _run_copies(count, src0, dst0, src_ref, dst_ref, sems):
    copies = []
    for j in range(sems.shape[0]):
        k = j + RUN_ALIGN_BITS
        size = 1 << k
        start = (count >> (k + 1)) << (k + 1)
        src = src_ref.at[pl.ds(pl.multiple_of(src0 + start, RUN_ALIGN), size)]
        dst = dst_ref.at[pl.ds(pl.multiple_of(dst0 + start, RUN_ALIGN), size)]
        copies.append((((count >> k) & 1) == 1, pltpu.make_async_copy(src, dst, sems.at[j])))
    return copies


def _start_then_wait(copies):
    for cond, cp in copies:
        pl.when(cond)(cp.start)
    for cond, cp in copies:
        pl.when(cond)(cp.wait)


def _one_hot_rows(dest, width):
    col = lax.broadcasted_iota(I32, (dest.shape[0], width), 1)
    return jnp.where(col == dest.astype(I32), 1.0, 0.0)


def _router_kernel(*refs, region_rows, finalize, aliased, tiles_per_step):
    ret_ref, att_ref, wo_ref, h_ref, gain_ref, wr_ref, base_ref = refs[:7]
    refs = refs[7 + (1 if aliased else 0):]
    hout_ref, xs_ref, route_ref, tile_ref, cnt_ref, sorted_ref, zero_ref, sems, zsem = refs
    i = pl.program_id(0)
    tm = h_ref.shape[0] // tiles_per_step
    n_sorted = sorted_ref.shape[1]

    @pl.when(i == 0)
    def _():
        cnt_ref[...] = base_ref[...]

    lane = lax.broadcasted_iota(I32, (tm, LANES), 1)
    lane_row = lane[0:1, :]
    ri = lax.broadcasted_iota(I32, (tm, tm), 0)
    ci = lax.broadcasted_iota(I32, (tm, tm), 1)
    earlier_tokens = jnp.where(ri > ci, 1.0, 0.0).astype(BF16)
    li = lax.broadcasted_iota(I32, (LANES, LANES), 0)
    lj = lax.broadcasted_iota(I32, (LANES, LANES), 1)
    earlier_experts = jnp.where(li < lj, 1.0, 0.0).astype(BF16)
    slot = lax.broadcasted_iota(I32, (n_sorted, tm), 0)
    sub = lax.broadcasted_iota(I32, (SUBLANES, LANES), 0)

    def as_row(col):
        wide = jnp.broadcast_to(col, (tm, LANES))
        return jnp.concatenate([wide[c * LANES:(c + 1) * LANES, :].T for c in range(tm // LANES)], axis=1)[0:1, :]

    recs = [dict(t=t, rows=slice(t * tm, (t + 1) * tm)) for t in range(tiles_per_step)]

    def stage(fn):
        for rec in recs:
            rec.update(fn(rec))

    def mix(r):
        h_mixed = _mixed_residual(ret_ref, att_ref, wo_ref, h_ref, r["rows"])
        hout_ref[r["rows"], :] = h_mixed
        return dict(f=_rms(h_mixed, gain_ref[...]))

    def logits(r):
        f_hi = r["f"].astype(BF16)
        f_lo = (r["f"] - f_hi.astype(F32)).astype(BF16)
        both = (jnp.dot(f_hi, wr_ref[...], preferred_element_type=F32)
                + jnp.dot(f_lo, wr_ref[...], preferred_element_type=F32))
        lg = jnp.where(lane < N_EXPERTS, both[:, :LANES] + both[:, LANES:], -jnp.inf)
        return dict(f_hi=f_hi, lg=lg)

    def first_choice(r):
        m1 = jnp.max(r["lg"], axis=-1, keepdims=True)
        return dict(m1=m1, i1=jnp.min(jnp.where(r["lg"] == m1, lane, LANES), axis=-1, keepdims=True))

    def second_choice(r):
        lg2 = jnp.where(lane == r["i1"], -jnp.inf, r["lg"])
        m2 = jnp.max(lg2, axis=-1, keepdims=True)
        return dict(m2=m2, i2=jnp.min(jnp.where(lg2 == m2, lane, LANES), axis=-1, keepdims=True))

    def rank_tokens(r):
        active = jnp.max(jnp.abs(r["f"]), axis=-1, keepdims=True) > 0.0
        sel = jnp.where(active, jnp.where(lane == r["i1"], 1.0, 0.0) + jnp.where(lane == r["i2"], 1.0, 0.0), 0.0)
        rank = jnp.dot(earlier_tokens, sel.astype(BF16), preferred_element_type=F32)
        tiles_per_run = jnp.floor((jnp.sum(sel, axis=0, keepdims=True) + (RUN_ALIGN - 1.0)) * (1.0 / RUN_ALIGN))
        start = RUN_ALIGN * jnp.dot(jnp.broadcast_to(tiles_per_run, (SUBLANES, LANES)).astype(BF16),
                                    earlier_experts, preferred_element_type=F32)[0:1, :]
        return dict(active=active, rank=rank, count=tiles_per_run * RUN_ALIGN, start=start)

    def destinations(r):
        dest = r["start"] + r["rank"]
        no_slot = -1.0
        d1 = jnp.where(r["active"], _lane_pick(dest, lane, r["i1"]), no_slot)
        d2 = jnp.where(r["active"], _lane_pick(dest, lane, r["i2"]), no_slot)
        e2 = jnp.exp(r["m2"] - r["m1"])
        g1 = 1.0 / (1.0 + e2)
        route = jnp.zeros((tm, LANES), F32)
        for j, val in ((ROUTE_G1, g1), (ROUTE_G2, e2 * g1), (ROUTE_D1, d1), (ROUTE_D2, d2)):
            route = jnp.where(lane == j, val, route)
        route_ref[r["rows"], :] = route
        return dict(d1=d1, d2=d2)

    def sort_rows(r):
        onehot_t = (jnp.where(slot == as_row(r["d1"]).astype(I32), 1.0, 0.0)
                    + jnp.where(slot == as_row(r["d2"]).astype(I32), 1.0, 0.0))
        sorted_ref[r["t"]] = jnp.dot(onehot_t.astype(BF16), r["f_hi"], preferred_element_type=F32)
        return {}

    for fn in (mix, logits, first_choice, second_choice, rank_tokens, destinations, sort_rows):
        stage(fn)
    base = cnt_ref[0:1, :]
    for rec in recs:
        rec["base"] = base
        tile_ref[rec["t"]] = jnp.where(sub == TILE_BASE, base, jnp.where(
            sub == TILE_COUNT, rec["count"], jnp.where(sub == TILE_START, rec["start"], 0.0)))
        base = base + rec["count"]
    copies = []
    for t, rec in enumerate(recs):
        for e in range(N_EXPERTS):
            n_e = _lane_scalar(rec["count"], lane_row, e)
            src0 = _lane_scalar(rec["start"], lane_row, e)
            dst0 = e * region_rows + _lane_scalar(rec["base"], lane_row, e)
            copies += _run_copies(n_e, src0, dst0, sorted_ref.at[t], xs_ref, sems.at[t, e])
    _start_then_wait(copies)
    cnt_ref[...] = jnp.broadcast_to(base, cnt_ref.shape)

    if finalize:
        @pl.when(i == pl.num_programs(0) - 1)
        def _():
            zero_ref[...] = jnp.zeros(zero_ref.shape, F32)
            total = cnt_ref[0:1, :]
            for e in range(N_EXPERTS):
                end = pl.multiple_of(e * region_rows + _lane_scalar(total, lane_row, e), RUN_ALIGN)
                cp = pltpu.make_async_copy(zero_ref, xs_ref.at[pl.ds(end, EXPERT_ROW_TILE)], zsem)
                cp.start()
                cp.wait()


def _router_tile(rows):
    return _row_tile(rows, (512, 640, 384, 256, 128))


def _sorted_rows(tm):
    return TOP_K * tm + pl.cdiv(N_EXPERTS * (RUN_ALIGN - 1), LANES) * LANES


def _router(mix, gain, w_router, base_counts, xs, region_rows, finalize):
    rows = mix[3].shape[0]
    tm = _router_tile(rows)
    n_tiles = rows // tm
    per_step = 2 if n_tiles % 2 == 0 else 1
    step_rows = per_step * tm
    aliased = xs is not None
    row_spec = lambda width: pl.BlockSpec((step_rows, width), lambda i: (i, 0))
    in_specs = _mix_specs(step_rows) + [_resident((1, D_MODEL)), _resident((D_MODEL, 2 * LANES)),
                                        _resident((SUBLANES, LANES))]
    args = [*mix, gain, w_router, base_counts]
    aliases = {3: 0}
    if aliased:
        aliases[len(args)] = 1
        in_specs.append(pl.BlockSpec(memory_space=pl.ANY))
        args.append(xs)
    n_pieces = _run_piece_count(tm)
    return pl.pallas_call(
        functools.partial(_router_kernel, region_rows=region_rows, finalize=finalize, aliased=aliased,
                          tiles_per_step=per_step),
        grid=(n_tiles // per_step,),
        in_specs=in_specs,
        out_specs=[row_spec(D_MODEL), pl.BlockSpec(memory_space=pl.ANY), row_spec(LANES),
                   pl.BlockSpec((per_step, SUBLANES, LANES), lambda i: (i, 0, 0)),
                   pl.BlockSpec((SUBLANES, LANES), lambda i: (0, 0))],
        out_shape=[jax.ShapeDtypeStruct((rows, D_MODEL), F32),
                   jax.ShapeDtypeStruct((N_EXPERTS * region_rows, D_MODEL), F32),
                   jax.ShapeDtypeStruct((rows, LANES), F32),
                   jax.ShapeDtypeStruct((n_tiles, SUBLANES, LANES), F32),
                   jax.ShapeDtypeStruct((SUBLANES, LANES), F32)],
        scratch_shapes=[pltpu.VMEM((per_step, _sorted_rows(tm), D_MODEL), F32),
                        pltpu.VMEM((EXPERT_ROW_TILE, D_MODEL), F32),
                        pltpu.SemaphoreType.DMA((per_step, N_EXPERTS, n_pieces)), pltpu.SemaphoreType.DMA(())],
        input_output_aliases=aliases,
        compiler_params=pltpu.CompilerParams(dimension_semantics=("arbitrary",), vmem_limit_bytes=56 * MIB),
        name="router",
    )(*args)


def _experts_kernel(blk_ref, exp_ref, nvalid_ref, x_ref, wg_ref, wu_ref, wd_ref, *y_refs):
    y_ref = y_refs[-1]
    partial_ref = y_refs[0] if len(y_refs) == 2 else None
    valid = pl.program_id(0) < nvalid_ref[0]

    @pl.when(valid)
    def _():
        x = x_ref[...].astype(BF16)
        gate = jnp.dot(x, wg_ref[...], preferred_element_type=F32)
        up = jnp.dot(x, wu_ref[...], preferred_element_type=F32)
        part = jnp.dot((_silu(gate) * up).astype(BF16), wd_ref[...], preferred_element_type=F32)
        y_ref[...] = part if partial_ref is None else partial_ref[...] + part

    @pl.when(jnp.logical_not(valid))
    def _():
        y_ref[...] = jnp.zeros(y_ref.shape, F32) if partial_ref is None else partial_ref[...]


def _experts(xs, tile_block, tile_expert, n_valid, wg, wu, wd):
    n_tiles = tile_block.shape[0]
    d_ff = wg.shape[2]
    ff_chunk = d_ff // 2
    assert ff_chunk % MXU_V7X_COLUMNS == 0
    tg = EXPERT_ROW_TILE
    y = None
    for c in range(d_ff // ff_chunk):
        in_specs = [pl.BlockSpec((tg, D_MODEL), lambda i, blk, ex, nv: (blk[i], 0)),
                    pl.BlockSpec((None, D_MODEL, ff_chunk), lambda i, blk, ex, nv, c=c: (ex[i], 0, c)),
                    pl.BlockSpec((None, D_MODEL, ff_chunk), lambda i, blk, ex, nv, c=c: (ex[i], 0, c)),
                    pl.BlockSpec((None, ff_chunk, D_MODEL), lambda i, blk, ex, nv, c=c: (ex[i], c, 0))]
        args = [tile_block, tile_expert, n_valid, xs, wg, wu, wd]
        y_spec = pl.BlockSpec((tg, D_MODEL), lambda i, blk, ex, nv: (i, 0))
        if y is not None:
            in_specs.append(y_spec)
            args.append(y)
        y = pl.pallas_call(
            _experts_kernel,
            grid_spec=pltpu.PrefetchScalarGridSpec(num_scalar_prefetch=3, grid=(n_tiles,), in_specs=in_specs,
                                                   out_specs=y_spec),
            out_shape=jax.ShapeDtypeStruct((n_tiles * tg, D_MODEL), F32),
            input_output_aliases={len(args) - 1: 0} if len(args) == 8 else {},
            compiler_params=pltpu.CompilerParams(dimension_semantics=("arbitrary",), vmem_limit_bytes=56 * MIB),
            name="experts",
        )(*args)
    return y


def _combine_kernel(tinfo_ref, yoff_ref, h_ref, route_ref, ys_ref, o_ref, sorted_ref, sems, *unpad_scratch,
                    seq_chunks):
    i = pl.program_id(0)
    n_steps = pl.num_programs(0)
    n_sorted = sorted_ref.shape[1]
    slot = i % 2

    def tile_copies(tile, buf, live):
        copies = []
        for e in range(N_EXPERTS):
            rec = (tile * 3) * N_EXPERTS + e
            base = tinfo_ref[rec + TILE_BASE * N_EXPERTS]
            n_e = jnp.where(live, tinfo_ref[rec + TILE_COUNT * N_EXPERTS], 0)
            dst0 = tinfo_ref[rec + TILE_START * N_EXPERTS]
            copies += _run_copies(n_e, yoff_ref[e] + base, dst0, ys_ref, sorted_ref.at[buf], sems.at[buf, e])
        return copies

    @pl.when(i == 0)
    def _():
        sorted_ref[...] = jnp.zeros(sorted_ref.shape, F32)
        for cond, cp in tile_copies(0, 0, True):
            pl.when(cond)(cp.start)

    for cond, cp in tile_copies(jnp.minimum(i + 1, n_steps - 1), 1 - slot, i + 1 < n_steps):
        pl.when(cond)(cp.start)
    for cond, cp in tile_copies(i, slot, True):
        pl.when(cond)(cp.wait)
    y = sorted_ref[slot].astype(BF16)
    route = route_ref[...]
    lane = lax.broadcasted_iota(I32, route.shape, 1)
    pick = lambda j: _lane_pick(route, lane, j)
    y1 = jnp.dot(_one_hot_rows(pick(ROUTE_D1), n_sorted).astype(BF16), y, preferred_element_type=F32)
    y2 = jnp.dot(_one_hot_rows(pick(ROUTE_D2), n_sorted).astype(BF16), y, preferred_element_type=F32)
    result = h_ref[...] + pick(ROUTE_G1) * y1 + pick(ROUTE_G2) * y2
    if seq_chunks is None:
        o_ref[...] = result
        return

    res_ref, out_sems = unpad_scratch
    chunks_per_tile = h_ref.shape[0] // CHUNK

    def chunk_copies(step, buf):
        copies = []
        for j in range(chunks_per_tile):
            chunk = step * chunks_per_tile + j
            seq = chunk // seq_chunks
            in_seq = chunk - seq * seq_chunks
            dst = pl.multiple_of(seq * ((seq_chunks - 1) * CHUNK) + (in_seq - 1) * CHUNK, CHUNK)
            cp = pltpu.make_async_copy(res_ref.at[buf, pl.ds(j * CHUNK, CHUNK)], o_ref.at[pl.ds(dst, CHUNK)],
                                       out_sems.at[buf, j])
            copies.append((jnp.logical_and(step >= 0, in_seq >= 1), cp))
        return copies

    res_ref[slot] = result
    for cond, cp in chunk_copies(i, slot):
        pl.when(cond)(cp.start)
    for cond, cp in chunk_copies(i - 1, 1 - slot):
        pl.when(cond)(cp.wait)

    @pl.when(i == n_steps - 1)
    def _():
        for cond, cp in chunk_copies(i, slot):
            pl.when(cond)(cp.wait)


def _combine(h, route, tile_info, y_offsets, ys, unpad_seq_rows=None):
    rows = h.shape[0]
    tm = rows // tile_info.shape[0]
    n_pieces = _run_piece_count(tm)
    tinfo = tile_info[:, :3, :N_EXPERTS].astype(I32).reshape(-1)
    scratch = [pltpu.VMEM((2, _sorted_rows(tm), D_MODEL), F32), pltpu.SemaphoreType.DMA((2, N_EXPERTS, n_pieces))]
    if unpad_seq_rows is None:
        seq_chunks, out_rows = None, rows
        out_spec = pl.BlockSpec((tm, D_MODEL), lambda i, *_: (i, 0))
    else:
        seq_chunks = unpad_seq_rows // CHUNK
        out_rows = rows // unpad_seq_rows * (unpad_seq_rows - CHUNK)
        out_spec = pl.BlockSpec(memory_space=pl.ANY)
        scratch += [pltpu.VMEM((2, tm, D_MODEL), F32), pltpu.SemaphoreType.DMA((2, tm // CHUNK))]
    grid_spec = pltpu.PrefetchScalarGridSpec(
        num_scalar_prefetch=2,
        grid=(rows // tm,),
        in_specs=[pl.BlockSpec((tm, D_MODEL), lambda i, *_: (i, 0)),
                  pl.BlockSpec((tm, LANES), lambda i, *_: (i, 0)),
                  pl.BlockSpec(memory_space=pl.ANY)],
        out_specs=out_spec,
        scratch_shapes=scratch,
    )
    return pl.pallas_call(
        functools.partial(_combine_kernel, seq_chunks=seq_chunks),
        grid_spec=grid_spec,
        out_shape=jax.ShapeDtypeStruct((out_rows, D_MODEL), F32),
        input_output_aliases={2: 0} if unpad_seq_rows is None else {},
        compiler_params=pltpu.CompilerParams(dimension_semantics=("arbitrary",), vmem_limit_bytes=48 * MIB),
        name="combine",
    )(tinfo, y_offsets, h, route, ys)


def _routed_experts(mixes, gain, router_w, expert_w, unpad_seq_rows):
    tg = EXPERT_ROW_TILE
    group_rows = [mix[3].shape[0] for mix in mixes]
    total_rows = sum(group_rows)
    run_pad = (RUN_ALIGN - 1) * sum(rows // _router_tile(rows) for rows in group_rows)
    region_rows = (pl.cdiv(total_rows + run_pad, tg) + 1) * tg
    counts = jnp.zeros((SUBLANES, LANES), F32)
    xs, hs, routes, tiles = None, [], [], []
    for g, mix in enumerate(mixes):
        h, xs, route, tile_info, counts = _router(mix, gain, router_w, counts, xs, region_rows,
                                                  finalize=(g == len(mixes) - 1))
        hs.append(h)
        routes.append(route)
        tiles.append(tile_info)
    n_e = counts[0, :N_EXPERTS].astype(I32)
    tiles_e = (n_e + tg - 1) // tg
    first_tile = jnp.cumsum(tiles_e) - tiles_e
    n_valid = jnp.sum(tiles_e)
    n_tiles = (TOP_K * total_rows + N_EXPERTS * (run_pad + tg - 1)) // tg + 1
    t = jnp.minimum(jnp.arange(n_tiles, dtype=I32), n_valid - 1)
    tile_expert = jnp.sum((t[:, None] >= (first_tile + tiles_e)[None, :]).astype(I32), axis=1)
    tile_block = tile_expert * (region_rows // tg) + (t - first_tile[tile_expert])
    ys = _experts(xs, tile_block.astype(I32), tile_expert.astype(I32), n_valid.reshape(1).astype(I32), *expert_w)
    y_offsets = (first_tile * tg).astype(I32)
    return [_combine(h, route, tile_info, y_offsets, ys, seq_rows)
            for h, route, tile_info, seq_rows in zip(hs, routes, tiles, unpad_seq_rows)]


def _rotary_tables(batch, seq_rows):
    pos = (jnp.arange(seq_rows, dtype=I32) - PAD_FRONT).astype(F32)

    def cos_sin(n_rot, theta):
        half = n_rot // 2
        inv = theta ** (-jnp.arange(half, dtype=F32) * 2.0 / n_rot)
        ang = pos[:, None] * inv[None, :]
        return jnp.cos(ang), jnp.sin(ang)

    cos, sin = cos_sin(HEAD_DIM, RET_ROPE_THETA)
    cr = jnp.tile(cos, (1, 2 * LANES // HEAD_DIM))
    sr = jnp.tile(jnp.concatenate([-sin, sin], axis=1), (1, LANES // HEAD_DIM))
    cos, sin = cos_sin(ROPE_DIMS, ROPE_THETA)
    rest = HEAD_DIM - ROPE_DIMS
    ca = jnp.tile(jnp.concatenate([cos, cos, jnp.ones((seq_rows, rest), F32)], axis=1), (1, LANES // HEAD_DIM))
    sa = jnp.tile(jnp.concatenate([-sin, sin, jnp.zeros((seq_rows, rest), F32)], axis=1), (1, LANES // HEAD_DIM))
    return tuple(jnp.tile(t, (batch, 1)) for t in (cr, sr, ca, sa))


def _retention_params(log_gf, log_gb, gain):
    n_pairs = RET_WIDTH // LANES
    per_lane = lambda t: jnp.repeat(t.astype(F32).reshape(n_pairs, 2), HEAD_DIM, axis=1)
    per_head = lambda t, j: jnp.broadcast_to(t.astype(F32).reshape(n_pairs, 2)[:, j:j + 1], (n_pairs, LANES))
    rows = [per_lane(log_gf), per_lane(log_gb), per_head(log_gf, 0), per_head(log_gf, 1),
            per_head(log_gb, 0), per_head(log_gb, 1), gain.astype(F32).reshape(n_pairs, LANES),
            jnp.zeros((n_pairs, LANES), F32)]
    return jnp.stack(rows, axis=1)


def _token_mixer(h, lp, tabs, batch, seq_rows, embed=None):
    outs = _inproj(h, lp["norm_mix"], lp["w_in"], tabs, lp["q_gain"], lp["k_gain"], lp["ones_bd"], embed)
    if embed is not None:
        h, outs = outs[0], outs[1:]
    rq, rk, rv, rg, aq, ak, av = outs
    ret = _retention(rq, rk, rv, rg, lp["ret_dec"], batch, seq_rows)
    att = _attention(aq, ak, av, lp["sink"], lp["att_gain"], batch, seq_rows)
    return ret, att, lp["w_out"], h


def kernel(x_prompt, x_sample, meta_tokens, norm_mix, w_in, ret_log_decay_fwd, ret_log_decay_bwd, ret_out_gain,
           q_norm_gain, k_norm_gain, attn_sink, attn_out_gain, w_out, norm_ffn, ffn_w_gate, ffn_w_up, ffn_w_down,
           moe_router, moe_w_gate, moe_w_up, moe_w_down):
    depth = w_in.shape[0]
    ri = lax.broadcasted_iota(I32, (LANES, LANES), 0)
    ci = lax.broadcasted_iota(I32, (LANES, LANES), 1)
    ones_bd = ((ri < HEAD_DIM) == (ci < HEAD_DIM)).astype(BF16)
    row = lambda t: t.astype(F32).reshape(1, -1)

    xs = (x_prompt, x_sample)
    shapes = [(x.shape[0], x.shape[1] + CHUNK) for x in xs]
    hs = [None] * len(xs)
    meta = meta_tokens.astype(F32)
    embeds = [(x.astype(F32).reshape(-1, D_MODEL), meta, r) for x, (_, r) in zip(xs, shapes)]
    tabs = [_rotary_tables(b, r) for b, r in shapes]

    for l in range(depth):
        lp = {
            "norm_mix": row(norm_mix[l]),
            "w_in": w_in[l].astype(BF16),
            "q_gain": jnp.tile(row(q_norm_gain[l]), (1, LANES // HEAD_DIM)),
            "k_gain": jnp.tile(row(k_norm_gain[l]), (1, LANES // HEAD_DIM)),
            "ones_bd": ones_bd,
            "ret_dec": _retention_params(ret_log_decay_fwd[l], ret_log_decay_bwd[l], ret_out_gain[l]),
            "sink": jnp.broadcast_to(attn_sink[l].astype(F32)[:, None] * LOG2_E, (N_ATT_HEADS, LANES)),
            "att_gain": row(attn_out_gain[l]),
            "w_out": w_out[l].astype(BF16),
        }
        mixes = [_token_mixer(h, lp, tab, b, r, embed if l == 0 else None)
                 for h, tab, (b, r), embed in zip(hs, tabs, shapes, embeds)]
        i = l // 2
        if l % 2 == 0:
            w = (ffn_w_gate[i].astype(BF16), ffn_w_up[i].astype(BF16), ffn_w_down[i].astype(BF16))
            hs = [_ffn(*mix, row(norm_ffn[l]), *w) for mix in mixes]
        else:
            wr = jnp.pad(moe_router[i].astype(F32), ((0, 0), (0, LANES - N_EXPERTS)))
            wr_hi = wr.astype(BF16)
            router_w = jnp.concatenate([wr_hi, (wr - wr_hi.astype(F32)).astype(BF16)], axis=1)
            expert_w = (moe_w_gate[i].astype(BF16), moe_w_up[i].astype(BF16), moe_w_down[i].astype(BF16))
            last = l == depth - 1
            hs = _routed_experts(mixes, row(norm_ffn[l]), router_w, expert_w,
                                 [r if last else None for _, r in shapes])
    if depth % 2 == 0:
        return tuple(h.reshape(x.shape).astype(x.dtype) for h, x in zip(hs, xs))
    return tuple(h.reshape(b, r, D_MODEL)[:, CHUNK:].astype(x.dtype) for h, x, (b, r) in zip(hs, xs, shapes))
```

```python
import functools

import jax
import jax.numpy as jnp
from jax import lax
from jax.experimental import pallas as pl
from jax.experimental.pallas import tpu as pltpu

F32 = jnp.float32
BF16 = jnp.bfloat16
I32 = jnp.int32

D_MODEL = 1024
HEAD_DIM = 64
N_RET_HEADS = 8
N_ATT_HEADS = 8
N_KV_HEADS = 2
RET_WIDTH = N_RET_HEADS * HEAD_DIM
ATT_WIDTH = N_ATT_HEADS * HEAD_DIM
KV_WIDTH = N_KV_HEADS * HEAD_DIM
IN_WIDTH = 4 * RET_WIDTH + ATT_WIDTH + 2 * KV_WIDTH
CHUNK = 128
LANES = 128
SUBLANES = 8
N_META = 16
PAD_FRONT = CHUNK - N_META
ROPE_THETA = 500000.0
ROPE_DIMS = HEAD_DIM // 4
RET_ROPE_THETA = 10000.0
N_EXPERTS = 8
TOP_K = 2
EXPERT_ROW_TILE = 512
MXU_V7X_COLUMNS = 256
EPS = 1e-6
NEG = -1e30
QK_SCALE = HEAD_DIM ** -0.5
LOG2_E = 1.4426950408889634
MIB = 1024 * 1024


def _row_tile(rows, prefs):
    for t in prefs:
        if rows % t == 0:
            return t
    raise ValueError(f"no row tile for {rows} rows among {prefs}")


def _resident(shape):
    return pl.BlockSpec(shape, lambda *_: (0,) * len(shape), pipeline_mode=pl.Buffered(1))


def _rms(x, gain):
    ms = jnp.mean(x * x, axis=-1, keepdims=True)
    return x * lax.rsqrt(ms + EPS) * gain


def _silu(x):
    return x * jax.nn.sigmoid(x)


def _embed_kernel(x_ref, meta_ref, *refs, seq_chunks):
    h0_ref, hbuf_ref, sems = refs[-10], refs[-2], refs[-1]
    i = pl.program_id(0)
    n_steps = pl.num_programs(0)
    tm = h0_ref.shape[0]
    chunks_per_tile = tm // CHUNK
    slot = i % 2

    def chunk_place(step, j):
        chunk = step * chunks_per_tile + j
        seq = chunk // seq_chunks
        return seq, chunk - seq * seq_chunks

    def chunk_copies(step, buf):
        copies = []
        for j in range(chunks_per_tile):
            seq, in_seq = chunk_place(step, j)
            src = pl.multiple_of(seq * ((seq_chunks - 1) * CHUNK) + (in_seq - 1) * CHUNK, CHUNK)
            cp = pltpu.make_async_copy(x_ref.at[pl.ds(src, CHUNK)], hbuf_ref.at[buf, pl.ds(j * CHUNK, CHUNK)],
                                       sems.at[buf, j])
            copies.append((jnp.logical_and(step < n_steps, in_seq >= 1), cp))
        return copies

    @pl.when(i == 0)
    def _():
        for cond, cp in chunk_copies(0, 0):
            pl.when(cond)(cp.start)

    for cond, cp in chunk_copies(i + 1, 1 - slot):
        pl.when(cond)(cp.start)
    for cond, cp in chunk_copies(i, slot):
        pl.when(cond)(cp.wait)
    lead = jnp.concatenate([jnp.zeros((PAD_FRONT, D_MODEL), F32), meta_ref[...]], axis=0)
    for j in range(chunks_per_tile):
        @pl.when(chunk_place(i, j)[1] == 0)
        def _():
            hbuf_ref[slot, j * CHUNK:(j + 1) * CHUNK, :] = lead

    h0_ref[...] = hbuf_ref[slot]
    _inproj_kernel(hbuf_ref.at[slot], *refs[:-10], *refs[-9:-2])


def _inproj_kernel(h_ref, gain_ref, w_ref, cr_ref, sr_ref, ca_ref, sa_ref, qg_ref, kg_ref, ones_ref,
                   rq_ref, rk_ref, rv_ref, rg_ref, aq_ref, ak_ref, av_ref):
    tm = h_ref.shape[0]
    part = tm
    lane = lax.broadcasted_iota(I32, (part, LANES), 1)
    in_head = lane & (HEAD_DIM - 1)
    low_head = lane < HEAD_DIM
    cols = lambda j: slice(j * LANES, (j + 1) * LANES)
    n_lane_tiles = RET_WIDTH // LANES
    base = 4 * RET_WIDTH

    def project(r):
        a = _rms(h_ref[r, :], gain_ref[...]).astype(BF16)
        proj = lambda c0, c1: jnp.dot(a, w_ref[:, c0:c1], preferred_element_type=F32)
        return dict(aq=proj(base, base + ATT_WIDTH), kv=proj(base + ATT_WIDTH, IN_WIDTH),
                    rq=proj(0, RET_WIDTH), rk=proj(RET_WIDTH, 2 * RET_WIDTH),
                    rv=proj(2 * RET_WIDTH, 3 * RET_WIDTH), rg=proj(3 * RET_WIDTH, 4 * RET_WIDTH))

    def rotate(x, cos, sin_signed, half):
        partner = jnp.where(in_head < half, pltpu.roll(x, LANES - half, 1), pltpu.roll(x, half, 1))
        return x * cos + partner * sin_signed

    def head_norm(x, g):
        ss = jnp.dot((x * x).astype(BF16), ones_ref[...], preferred_element_type=F32)
        return x * lax.rsqrt(ss * (1.0 / HEAD_DIM) + EPS) * g

    def epilogue(r, first_chunk, p):
        cr, sr, ca, sa = cr_ref[r, :], sr_ref[r, :], ca_ref[r, :], sa_ref[r, :]

        def store_keys(kt_ref, j, x):
            for c in range(part // CHUNK):
                kt_ref[first_chunk + c, cols(j), :] = x[c * CHUNK:(c + 1) * CHUNK, :].T.astype(BF16)

        rv_ref[r, :] = p["rv"].astype(BF16)
        rg_ref[r, :] = p["rg"].astype(BF16)
        for j in range(n_lane_tiles):
            rq_ref[r, cols(j)] = rotate(p["rq"][:, cols(j)], cr, sr, HEAD_DIM // 2).astype(BF16)
        for j in range(n_lane_tiles):
            store_keys(rk_ref, j, rotate(p["rk"][:, cols(j)], cr, sr, HEAD_DIM // 2) * QK_SCALE)
        normed = [head_norm(p["aq"][:, cols(j)], qg_ref[...]) for j in range(ATT_WIDTH // LANES)]
        normed_k = head_norm(p["kv"][:, :KV_WIDTH], kg_ref[...])
        for j, n in enumerate(normed):
            aq_ref[r, cols(j)] = (rotate(n, ca, sa, ROPE_DIMS // 2) * (QK_SCALE * LOG2_E)).astype(BF16)
        k = rotate(normed_k, ca, sa, ROPE_DIMS // 2)
        v = p["kv"][:, KV_WIDTH:]
        k_sw = pltpu.roll(k, HEAD_DIM, 1)
        v_sw = pltpu.roll(v, HEAD_DIM, 1)
        store_keys(ak_ref, 0, jnp.where(low_head, k, k_sw))
        store_keys(ak_ref, 1, jnp.where(low_head, k_sw, k))
        av_ref[r, 0:LANES] = jnp.where(low_head, v, v_sw).astype(BF16)
        av_ref[r, LANES:2 * LANES] = jnp.where(low_head, v_sw, v).astype(BF16)

    whole_tile = slice(0, tm)
    epilogue(whole_tile, 0, project(whole_tile))


def _inproj(h, gain, w, tabs, qg, kg, ones_bd, embed=None):
    rows = tabs[0].shape[0]
    tm = _row_tile(rows, (1024, 640, 512, 384, 256, 128) if embed is None else (512, 640, 384, 256, 128))
    row_spec = lambda width: pl.BlockSpec((tm, width), lambda i: (i, 0))
    outs = ((RET_WIDTH, False), (RET_WIDTH, True), (RET_WIDTH, False), (RET_WIDTH, False),
            (ATT_WIDTH, False), (2 * KV_WIDTH, True), (2 * KV_WIDTH, False))
    chunk_spec = lambda width: pl.BlockSpec((tm // CHUNK, width, CHUNK), lambda i: (i, 0, 0))
    in_specs = [_resident((1, D_MODEL)), _resident((D_MODEL, IN_WIDTH)),
                row_spec(LANES), row_spec(LANES), row_spec(LANES), row_spec(LANES),
                _resident((1, LANES)), _resident((1, LANES)), _resident((LANES, LANES))]
    out_specs = [chunk_spec(wd) if t else row_spec(wd) for wd, t in outs]
    out_shape = [jax.ShapeDtypeStruct((rows // CHUNK, wd, CHUNK) if t else (rows, wd), BF16) for wd, t in outs]
    if embed is None:
        return pl.pallas_call(
            _inproj_kernel,
            grid=(rows // tm,),
            in_specs=[row_spec(D_MODEL)] + in_specs,
            out_specs=out_specs,
            out_shape=out_shape,
            compiler_params=pltpu.CompilerParams(dimension_semantics=("parallel",), vmem_limit_bytes=56 * MIB),
            name="inproj",
        )(h, gain, w, *tabs, qg, kg, ones_bd)
    x, meta, seq_rows = embed
    return pl.pallas_call(
        functools.partial(_embed_kernel, seq_chunks=seq_rows // CHUNK),
        grid=(rows // tm,),
        in_specs=[pl.BlockSpec(memory_space=pl.ANY), _resident((N_META, D_MODEL))] + in_specs,
        out_specs=[row_spec(D_MODEL)] + out_specs,
        out_shape=[jax.ShapeDtypeStruct((rows, D_MODEL), F32)] + out_shape,
        scratch_shapes=[pltpu.VMEM((2, tm, D_MODEL), F32), pltpu.SemaphoreType.DMA((2, tm // CHUNK))],
        compiler_params=pltpu.CompilerParams(dimension_semantics=("arbitrary",), vmem_limit_bytes=56 * MIB),
        name="embed",
    )(x, meta, gain, w, *tabs, qg, kg, ones_bd)


def _retention_kernel(q_ref, k_ref, v_ref, g_ref, dec_ref, o_ref, sb_ref, *, n_chunks):
    lgf, lgb = dec_ref[0:1, :], dec_ref[1:2, :]
    gain = dec_ref[6:7, :]
    ri = lax.broadcasted_iota(I32, (CHUNK, LANES), 0)
    ci = lax.broadcasted_iota(I32, (CHUNK, LANES), 1)
    r = ri.astype(F32)
    diff = (ri - ci).astype(F32)

    def decay_mask(lf, lb):
        return jnp.where(diff >= 0, jnp.exp(jnp.maximum(diff, 0.0) * lf), jnp.exp(jnp.maximum(-diff, 0.0) * lb))

    dm = jnp.concatenate([decay_mask(dec_ref[2:3, :], dec_ref[4:5, :]),
                          decay_mask(dec_ref[3:4, :], dec_ref[5:6, :])], axis=1)
    tok = ci.astype(F32)
    wf = jnp.exp((CHUNK - 1.0 - tok) * jnp.broadcast_to(lgf, (LANES, LANES)).T)
    wb = jnp.exp(tok * jnp.broadcast_to(lgb, (LANES, LANES)).T)
    qf = jnp.exp((r + 1.0) * lgf)
    qb = jnp.exp((CHUNK - r) * lgb)
    cf = jnp.exp(float(CHUNK) * lgf)
    cb = jnp.exp(float(CHUNK) * lgb)
    low_c = ci < HEAD_DIM
    same_head = (ri < HEAD_DIM) == low_c
    ones_bd = jnp.where(same_head, 1.0, 0.0).astype(BF16)
    m0 = jnp.where(low_c, 1.0, 0.0).astype(BF16)
    m1 = jnp.where(low_c, 0.0, 1.0).astype(BF16)
    low_r = ri < HEAD_DIM
    rm0 = jnp.where(low_r, 1.0, 0.0).astype(BF16)
    rm1 = jnp.where(low_r, 0.0, 1.0).astype(BF16)

    def chunk_rows(n):
        return pl.ds(pl.multiple_of(n * CHUNK, CHUNK), CHUNK)

    def pair_rows(x):
        return jnp.concatenate([x * m0, x * m1], axis=0)

    def pair_cols(xt):
        return jnp.concatenate([xt * rm0, xt * rm1], axis=1)

    def state_delta(kt, weights, v):
        u = jnp.dot((kt.astype(F32) * weights).astype(BF16), v, preferred_element_type=F32)
        return jnp.where(same_head, u, 0.0)

    group = next(u for u in (11, 13, 4, 3, 5, 2, 1) if n_chunks % u == 0)
    n_groups = n_chunks // group
    members = range(group)
    zero_state = jnp.zeros((LANES, LANES), F32)

    def backward(i, state):
        ns = [n_chunks - 1 - i * group - j for j in members]
        deltas = [state_delta(k_ref[n], wb, v_ref[chunk_rows(n), :]) for n in ns]
        for n, delta in zip(ns, deltas):
            sb_ref[n] = state.astype(BF16)
            state = cb * state + delta
        return state

    lax.fori_loop(0, n_groups, backward, zero_state)

    def forward(i, state):
        ns = [i * group + j for j in members]
        q = [q_ref[chunk_rows(n), :] for n in ns]
        kt = [k_ref[n] for n in ns]
        v = [v_ref[chunk_rows(n), :] for n in ns]
        deltas = [state_delta(kt[j], wf, v[j]) for j in members]
        s = [jnp.dot(q[j], pair_cols(kt[j]), preferred_element_type=F32) * dm for j in members]
        both = []
        for j in members:
            both.append(jnp.concatenate([state.astype(BF16), sb_ref[ns[j]]], axis=1))
            state = cf * state + deltas[j]
        intra = [jnp.dot(s[j].astype(BF16), pair_rows(v[j]), preferred_element_type=F32) for j in members]
        inter = [jnp.dot(q[j], both[j], preferred_element_type=F32) for j in members]
        o = [intra[j] + inter[j][:, :LANES] * qf + inter[j][:, LANES:] * qb for j in members]
        ss = [jnp.dot((o[j] * o[j]).astype(BF16), ones_bd, preferred_element_type=F32) for j in members]
        for j in members:
            gate = _silu(g_ref[chunk_rows(ns[j]), :].astype(F32))
            o_ref[chunk_rows(ns[j]), :] = (o[j] * lax.rsqrt(ss[j] * (1.0 / HEAD_DIM) + EPS) * gain * gate).astype(BF16)
        return state

    lax.fori_loop(0, n_groups, forward, zero_state)


def _retention(rq, rk, rv, rg, dec, batch, seq_rows):
    n_chunks = seq_rows // CHUNK
    n_pairs = RET_WIDTH // LANES
    view = lambda t: t.reshape(batch, seq_rows, RET_WIDTH)
    seq_spec = pl.BlockSpec((None, seq_rows, LANES), lambda b, p: (b, 0, p))
    key_spec = pl.BlockSpec((n_chunks, LANES, CHUNK), lambda b, p: (b, p, 0))
    out = pl.pallas_call(
        functools.partial(_retention_kernel, n_chunks=n_chunks),
        grid=(batch, n_pairs),
        in_specs=[seq_spec, key_spec, seq_spec, seq_spec,
                  pl.BlockSpec((None, 8, LANES), lambda b, p: (p, 0, 0))],
        out_specs=seq_spec,
        out_shape=jax.ShapeDtypeStruct((batch, seq_rows, RET_WIDTH), BF16),
        scratch_shapes=[pltpu.VMEM((n_chunks, LANES, LANES), BF16)],
        compiler_params=pltpu.CompilerParams(dimension_semantics=("parallel", "parallel"),
                                             vmem_limit_bytes=48 * MIB),
        name="retention",
    )(view(rq), rk, view(rv), view(rg), dec)
    return out.reshape(batch * seq_rows, RET_WIDTH)


def _attention_kernel(q_ref, kp_ref, kc_ref, kn_ref, km_ref, vp_ref, vc_ref, vn_ref, vm_ref,
                      sink_ref, gain_ref, o_ref, *, n_chunks, chunks_per_step):
    first_chunk = pl.program_id(1) * chunks_per_step
    ri = lax.broadcasted_iota(I32, (CHUNK, LANES), 0)
    ci = lax.broadcasted_iota(I32, (CHUNK, LANES), 1)
    never = 2 * CHUNK
    to_bias = lambda ok: jnp.where(ok, 0.0, NEG).astype(BF16)
    meta_bias = to_bias(ci >= PAD_FRONT)
    n_keys = 4 * CHUNK
    low_c = ci < HEAD_DIM
    m0 = jnp.where(low_c, 1.0, 0.0).astype(BF16)
    m1 = jnp.where(low_c, 0.0, 1.0).astype(BF16)
    low_r = ri < HEAD_DIM
    rm0 = jnp.where(low_r, 1.0, 0.0).astype(BF16)
    rm1 = jnp.where(low_r, 0.0, 1.0).astype(BF16)
    eye = jnp.where(ri == ci, 1.0, 0.0).astype(BF16)
    row_sums = jnp.concatenate([m0] * 4 + [m1] * 4, axis=0)
    pairs_per_kv = N_ATT_HEADS // N_KV_HEADS // 2
    rows = lambda j: slice(j * CHUNK, (j + 1) * CHUNK)

    def split_values(ref, blk, g):
        x = ref[blk, g * LANES:(g + 1) * LANES]
        return x * m0, x * m1

    def split_keys(ref, idx, g):
        xt = ref[idx, g * LANES:(g + 1) * LANES, :]
        return xt * rm0, xt * rm1

    def blocks(split, at, prev_ref, main_ref, next_ref, meta_ref, g):
        out = [split(prev_ref, at(0), g)]
        out += [split(main_ref, at(j), g) for j in range(chunks_per_step)]
        out += [split(next_ref, at(0), g), split(meta_ref, at(0), g)]
        return out

    k_blocks = [blocks(split_keys, lambda j: j, kp_ref, kc_ref, kn_ref, km_ref, g) for g in range(N_KV_HEADS)]
    v_blocks = [blocks(split_values, rows, vp_ref, vc_ref, vn_ref, vm_ref, g) for g in range(N_KV_HEADS)]

    def pair_blocks(blks, j, axis):
        use = [blks[j], blks[j + 1], blks[j + 2], blks[-1]]
        return jnp.concatenate([b[0] for b in use] + [b[1] for b in use], axis=axis)

    def pair_ids(g):
        return [g * pairs_per_kv + pp for pp in range(pairs_per_kv)]

    def scores(j, g):
        c = first_chunk + j
        prev_off = jnp.where(c >= 2, 0, never)
        cur_off = jnp.where(c >= 1, 0, never)
        next_off = jnp.where(c + 1 <= n_chunks - 1, 0, never)
        bias = jnp.concatenate([to_bias(ci >= ri + prev_off), to_bias(ci >= cur_off),
                                to_bias(ci + next_off <= ri), meta_bias] * 2, axis=1)
        k_ext = jnp.concatenate([pair_blocks(k_blocks[g], j, 1), bias], axis=0)
        q_ext = jnp.concatenate(
            [jnp.concatenate([q_ref[rows(j), p * LANES:(p + 1) * LANES], eye], axis=1) for p in pair_ids(g)], axis=0)
        return jnp.dot(q_ext, k_ext, preferred_element_type=F32)

    def softmax_parts(g, s):
        probs, sink_rows = [], []
        for pp, p in enumerate(pair_ids(g)):
            halves, sink_terms = [], []
            for hh in range(2):
                sh = s[rows(pp), hh * n_keys:(hh + 1) * n_keys]
                sk = sink_ref[2 * p + hh:2 * p + hh + 1, 0:1]
                m = jnp.maximum(jnp.max(sh, axis=-1, keepdims=True), sk)
                halves.append(jnp.exp2(sh - m).astype(BF16))
                sink_terms.append(jnp.exp2(sk - m))
            probs.append(jnp.concatenate(halves, axis=1))
            sink_rows.append(jnp.where(low_c, sink_terms[0], sink_terms[1]))
        return jnp.concatenate(probs, axis=0), sink_rows

    def weighted_values(j, g, probs, sink_rows):
        v_ext = jnp.concatenate([pair_blocks(v_blocks[g], j, 0), row_sums], axis=1)
        ol = jnp.dot(probs, v_ext, preferred_element_type=F32)
        return [ol[rows(pp), :LANES] / (ol[rows(pp), LANES:] + sink_rows[pp]) for pp in range(pairs_per_kv)]

    chunk_group = 4
    for j0 in range(0, chunks_per_step, chunk_group):
        units = [(j, g) for j in range(j0, min(j0 + chunk_group, chunks_per_step)) for g in range(N_KV_HEADS)]
        s = [scores(j, g) for j, g in units]
        parts = [softmax_parts(g, s[u]) for u, (j, g) in enumerate(units)]
        outs = [weighted_values(j, g, *parts[u]) for u, (j, g) in enumerate(units)]
        for j in sorted({j for j, _ in units}):
            att = jnp.concatenate([o for u, (ju, _) in enumerate(units) if ju == j for o in outs[u]], axis=1)
            row = lax.broadcasted_iota(I32, att.shape, 0)
            pad_rows = jnp.where(first_chunk + j == 0, PAD_FRONT, 0)
            att = jnp.where(row >= pad_rows, att, 0.0)
            o_ref[rows(j), :] = _rms(att, gain_ref[...]).astype(BF16)


def _attention(aq, ak, av, sink, gain, batch, seq_rows):
    n_chunks = seq_rows // CHUNK
    per_step = next(r for r in (11, 13, 3, 5, 4, 2, 1) if n_chunks % r == 0)
    n_steps = n_chunks // per_step
    q3 = aq.reshape(batch, seq_rows, ATT_WIDTH)
    v3 = av.reshape(batch, seq_rows, 2 * KV_WIDTH)
    prev_chunk = lambda s: jnp.maximum(s * per_step - 1, 0)
    next_chunk = lambda s: jnp.minimum((s + 1) * per_step, n_chunks - 1)
    v_one = lambda fn: pl.BlockSpec((None, CHUNK, 2 * KV_WIDTH), lambda b, s: (b, fn(s), 0))
    v_specs = [v_one(prev_chunk), pl.BlockSpec((None, per_step * CHUNK, 2 * KV_WIDTH), lambda b, s: (b, s, 0)),
               v_one(next_chunk), v_one(lambda s: 0)]
    k_one = lambda fn: pl.BlockSpec((1, 2 * KV_WIDTH, CHUNK), lambda b, s: (b * n_chunks + fn(s), 0, 0))
    k_specs = [k_one(prev_chunk),
               pl.BlockSpec((per_step, 2 * KV_WIDTH, CHUNK), lambda b, s: (b * n_steps + s, 0, 0)),
               k_one(next_chunk), k_one(lambda s: 0)]
    q_spec = pl.BlockSpec((None, per_step * CHUNK, ATT_WIDTH), lambda b, s: (b, s, 0))
    out = pl.pallas_call(
        functools.partial(_attention_kernel, n_chunks=n_chunks, chunks_per_step=per_step),
        grid=(batch, n_steps),
        in_specs=[q_spec] + k_specs + v_specs + [_resident((N_ATT_HEADS, LANES)), _resident((1, ATT_WIDTH))],
        out_specs=q_spec,
        out_shape=jax.ShapeDtypeStruct((batch, seq_rows, ATT_WIDTH), BF16),
        compiler_params=pltpu.CompilerParams(dimension_semantics=("parallel", "parallel"),
                                             vmem_limit_bytes=32 * MIB),
        name="attention",
    )(q3, ak, ak, ak, ak, v3, v3, v3, v3, sink, gain)
    return out.reshape(batch * seq_rows, ATT_WIDTH)


def _mixed_residual(ret_ref, att_ref, w_ref, h_ref, rows=slice(None)):
    acc = jnp.dot(ret_ref[rows, :], w_ref[0:RET_WIDTH, :], preferred_element_type=F32)
    acc = acc + jnp.dot(att_ref[rows, :], w_ref[RET_WIDTH:, :], preferred_element_type=F32)
    return h_ref[rows, :] + acc


def _mix_specs(tm):
    row_spec = lambda width: pl.BlockSpec((tm, width), lambda i, *_: (i, 0))
    return [row_spec(RET_WIDTH), row_spec(ATT_WIDTH), _resident((RET_WIDTH + ATT_WIDTH, D_MODEL)),
            row_spec(D_MODEL)]


def _ff_halves(d_ff):
    n_mxu_tiles = pl.cdiv(d_ff, MXU_V7X_COLUMNS)
    first = min(d_ff, pl.cdiv(n_mxu_tiles, 2) * MXU_V7X_COLUMNS)
    return tuple((c0, c1) for c0, c1 in ((0, first), (first, d_ff)) if c1 > c0)


def _ffn_kernel(ret_ref, att_ref, wo_ref, h_ref, gain_ref, wg_ref, wu_ref, wd_ref, o_ref):
    x = _mixed_residual(ret_ref, att_ref, wo_ref, h_ref)
    f = _rms(x, gain_ref[...]).astype(BF16)
    acc = x
    for c0, c1 in _ff_halves(wg_ref.shape[1]):
        gate = jnp.dot(f, wg_ref[:, c0:c1], preferred_element_type=F32)
        up = jnp.dot(f, wu_ref[:, c0:c1], preferred_element_type=F32)
        act = (_silu(gate) * up).astype(BF16)
        acc = acc + jnp.dot(act, wd_ref[c0:c1, :], preferred_element_type=F32)
    o_ref[...] = acc


def _ffn(ret, att, w_out, h, gain, wg, wu, wd):
    rows = h.shape[0]
    d_ff = wg.shape[1]
    tm = _row_tile(rows, (640, 512, 384, 256, 128))
    return pl.pallas_call(
        _ffn_kernel,
        grid=(rows // tm,),
        in_specs=_mix_specs(tm) + [_resident((1, D_MODEL)), _resident((D_MODEL, d_ff)), _resident((D_MODEL, d_ff)),
                                   _resident((d_ff, D_MODEL))],
        out_specs=pl.BlockSpec((tm, D_MODEL), lambda i: (i, 0)),
        out_shape=jax.ShapeDtypeStruct((rows, D_MODEL), F32),
        input_output_aliases={3: 0},
        compiler_params=pltpu.CompilerParams(dimension_semantics=("parallel",), vmem_limit_bytes=56 * MIB),
        name="ffn",
    )(ret, att, w_out, h, gain, wg, wu, wd)


ROUTE_G1, ROUTE_G2, ROUTE_D1, ROUTE_D2 = range(4)
TILE_BASE, TILE_COUNT, TILE_START = range(3)


def _lane_pick(x, lane, j):
    return jnp.sum(jnp.where(lane == j, x, 0.0), axis=-1, keepdims=True)


def _lane_scalar(vec, lane_row, e):
    return jnp.sum(jnp.where(lane_row == e, vec, 0.0)).astype(I32)


RUN_ALIGN_BITS = 3
RUN_ALIGN = 1 << RUN_ALIGN_BITS


def _run_piece_count(tm):
    return tm.bit_length() - RUN_ALIGN_BITS


def _run_copies(count, src0, dst0, src_ref, dst_ref, sems):
    copies = []
    for j in range(sems.shape[0]):
        k = j + RUN_ALIGN_BITS
        size = 1 << k
        start = (count >> (k + 1)) << (k + 1)
        src = src_ref.at[pl.ds(pl.multiple_of(src0 + start, RUN_ALIGN), size)]
        dst = dst_ref.at[pl.ds(pl.multiple_of(dst0 + start, RUN_ALIGN), size)]
        copies.append((((count >> k) & 1) == 1, pltpu.make_async_copy(src, dst, sems.at[j])))
    return copies


def _start_then_wait(copies):
    for cond, cp in copies:
        pl.when(cond)(cp.start)
    for cond, cp in copies:
        pl.when(cond)(cp.wait)


def _one_hot_rows(dest, width):
    col = lax.broadcasted_iota(I32, (dest.shape[0], width), 1)
    return jnp.where(col == dest.astype(I32), 1.0, 0.0)


def _router_kernel(*refs, region_rows, finalize, aliased, tiles_per_step):
    ret_ref, att_ref, wo_ref, h_ref, gain_ref, wr_ref, base_ref = refs[:7]
    refs = refs[7 + (1 if aliased else 0):]
    hout_ref, xs_ref, route_ref, tile_ref, cnt_ref, sorted_ref, zero_ref, sems, zsem = refs
    i = pl.program_id(0)
    tm = h_ref.shape[0] // tiles_per_step
    n_sorted = sorted_ref.shape[1]

    @pl.when(i == 0)
    def _():
        cnt_ref[...] = base_ref[...]

    lane = lax.broadcasted_iota(I32, (tm, LANES), 1)
    lane_row = lane[0:1, :]
    ri = lax.broadcasted_iota(I32, (tm, tm), 0)
    ci = lax.broadcasted_iota(I32, (tm, tm), 1)
    earlier_tokens = jnp.where(ri > ci, 1.0, 0.0).astype(BF16)
    li = lax.broadcasted_iota(I32, (LANES, LANES), 0)
    lj = lax.broadcasted_iota(I32, (LANES, LANES), 1)
    earlier_experts = jnp.where(li < lj, 1.0, 0.0).astype(BF16)
    slot = lax.broadcasted_iota(I32, (n_sorted, tm), 0)
    sub = lax.broadcasted_iota(I32, (SUBLANES, LANES), 0)

    def as_row(col):
        wide = jnp.broadcast_to(col, (tm, LANES))
        return jnp.concatenate([wide[c * LANES:(c + 1) * LANES, :].T for c in range(tm // LANES)], axis=1)[0:1, :]

    recs = [dict(t=t, rows=slice(t * tm, (t + 1) * tm)) for t in range(tiles_per_step)]

    def stage(fn):
        for rec in recs:
            rec.update(fn(rec))

    def mix(r):
        h_mixed = _mixed_residual(ret_ref, att_ref, wo_ref, h_ref, r["rows"])
        hout_ref[r["rows"], :] = h_mixed
        return dict(f=_rms(h_mixed, gain_ref[...]))

    def logits(r):
        f_hi = r["f"].astype(BF16)
        f_lo = (r["f"] - f_hi.astype(F32)).astype(BF16)
        both = (jnp.dot(f_hi, wr_ref[...], preferred_element_type=F32)
                + jnp.dot(f_lo, wr_ref[...], preferred_element_type=F32))
        lg = jnp.where(lane < N_EXPERTS, both[:, :LANES] + both[:, LANES:], -jnp.inf)
        return dict(f_hi=f_hi, lg=lg)

    def first_choice(r):
        m1 = jnp.max(r["lg"], axis=-1, keepdims=True)
        return dict(m1=m1, i1=jnp.min(jnp.where(r["lg"] == m1, lane, LANES), axis=-1, keepdims=True))

    def second_choice(r):
        lg2 = jnp.where(lane == r["i1"], -jnp.inf, r["lg"])
        m2 = jnp.max(lg2, axis=-1, keepdims=True)
        return dict(m2=m2, i2=jnp.min(jnp.where(lg2 == m2, lane, LANES), axis=-1, keepdims=True))

    def rank_tokens(r):
        active = jnp.max(jnp.abs(r["f"]), axis=-1, keepdims=True) > 0.0
        sel = jnp.where(active, jnp.where(lane == r["i1"], 1.0, 0.0) + jnp.where(lane == r["i2"], 1.0, 0.0), 0.0)
        rank = jnp.dot(earlier_tokens, sel.astype(BF16), preferred_element_type=F32)
        tiles_per_run = jnp.floor((jnp.sum(sel, axis=0, keepdims=True) + (RUN_ALIGN - 1.0)) * (1.0 / RUN_ALIGN))
        start = RUN_ALIGN * jnp.dot(jnp.broadcast_to(tiles_per_run, (SUBLANES, LANES)).astype(BF16),
                                    earlier_experts, preferred_element_type=F32)[0:1, :]
        return dict(active=active, rank=rank, count=tiles_per_run * RUN_ALIGN, start=start)

    def destinations(r):
        dest = r["start"] + r["rank"]
        no_slot = -1.0
        d1 = jnp.where(r["active"], _lane_pick(dest, lane, r["i1"]), no_slot)
        d2 = jnp.where(r["active"], _lane_pick(dest, lane, r["i2"]), no_slot)
        e2 = jnp.exp(r["m2"] - r["m1"])
        g1 = 1.0 / (1.0 + e2)
        route = jnp.zeros((tm, LANES), F32)
        for j, val in ((ROUTE_G1, g1), (ROUTE_G2, e2 * g1), (ROUTE_D1, d1), (ROUTE_D2, d2)):
            route = jnp.where(lane == j, val, route)
        route_ref[r["rows"], :] = route
        return dict(d1=d1, d2=d2)

    def sort_rows(r):
        onehot_t = (jnp.where(slot == as_row(r["d1"]).astype(I32), 1.0, 0.0)
                    + jnp.where(slot == as_row(r["d2"]).astype(I32), 1.0, 0.0))
        sorted_ref[r["t"]] = jnp.dot(onehot_t.astype(BF16), r["f_hi"], preferred_element_type=F32)
        return {}

    for fn in (mix, logits, first_choice, second_choice, rank_tokens, destinations, sort_rows):
        stage(fn)
    base = cnt_ref[0:1, :]
    for rec in recs:
        rec["base"] = base
        tile_ref[rec["t"]] = jnp.where(sub == TILE_BASE, base, jnp.where(
            sub == TILE_COUNT, rec["count"], jnp.where(sub == TILE_START, rec["start"], 0.0)))
        base = base + rec["count"]
    copies = []
    for t, rec in enumerate(recs):
        for e in range(N_EXPERTS):
            n_e = _lane_scalar(rec["count"], lane_row, e)
            src0 = _lane_scalar(rec["start"], lane_row, e)
            dst0 = e * region_rows + _lane_scalar(rec["base"], lane_row, e)
            copies += _run_copies(n_e, src0, dst0, sorted_ref.at[t], xs_ref, sems.at[t, e])
    _start_then_wait(copies)
    cnt_ref[...] = jnp.broadcast_to(base, cnt_ref.shape)

    if finalize:
        @pl.when(i == pl.num_programs(0) - 1)
        def _():
            zero_ref[...] = jnp.zeros(zero_ref.shape, F32)
            total = cnt_ref[0:1, :]
            for e in range(N_EXPERTS):
                end = pl.multiple_of(e * region_rows + _lane_scalar(total, lane_row, e), RUN_ALIGN)
                cp = pltpu.make_async_copy(zero_ref, xs_ref.at[pl.ds(end, EXPERT_ROW_TILE)], zsem)
                cp.start()
                cp.wait()


def _router_tile(rows):
    return _row_tile(rows, (512, 640, 384, 256, 128))


def _sorted_rows(tm):
    return TOP_K * tm + pl.cdiv(N_EXPERTS * (RUN_ALIGN - 1), LANES) * LANES


def _router(mix, gain, w_router, base_counts, xs, region_rows, finalize):
    rows = mix[3].shape[0]
    tm = _router_tile(rows)
    n_tiles = rows // tm
    per_step = 2 if n_tiles % 2 == 0 else 1
    step_rows = per_step * tm
    aliased = xs is not None
    row_spec = lambda width: pl.BlockSpec((step_rows, width), lambda i: (i, 0))
    in_specs = _mix_specs(step_rows) + [_resident((1, D_MODEL)), _resident((D_MODEL, 2 * LANES)),
                                        _resident((SUBLANES, LANES))]
    args = [*mix, gain, w_router, base_counts]
    aliases = {3: 0}
    if aliased:
        aliases[len(args)] = 1
        in_specs.append(pl.BlockSpec(memory_space=pl.ANY))
        args.append(xs)
    n_pieces = _run_piece_count(tm)
    return pl.pallas_call(
        functools.partial(_router_kernel, region_rows=region_rows, finalize=finalize, aliased=aliased,
                          tiles_per_step=per_step),
        grid=(n_tiles // per_step,),
        in_specs=in_specs,
        out_specs=[row_spec(D_MODEL), pl.BlockSpec(memory_space=pl.ANY), row_spec(LANES),
                   pl.BlockSpec((per_step, SUBLANES, LANES), lambda i: (i, 0, 0)),
                   pl.BlockSpec((SUBLANES, LANES), lambda i: (0, 0))],
        out_shape=[jax.ShapeDtypeStruct((rows, D_MODEL), F32),
                   jax.ShapeDtypeStruct((N_EXPERTS * region_rows, D_MODEL), F32),
                   jax.ShapeDtypeStruct((rows, LANES), F32),
                   jax.ShapeDtypeStruct((n_tiles, SUBLANES, LANES), F32),
                   jax.ShapeDtypeStruct((SUBLANES, LANES), F32)],
        scratch_shapes=[pltpu.VMEM((per_step, _sorted_rows(tm), D_MODEL), F32),
                        pltpu.VMEM((EXPERT_ROW_TILE, D_MODEL), F32),
                        pltpu.SemaphoreType.DMA((per_step, N_EXPERTS, n_pieces)), pltpu.SemaphoreType.DMA(())],
        input_output_aliases=aliases,
        compiler_params=pltpu.CompilerParams(dimension_semantics=("arbitrary",), vmem_limit_bytes=56 * MIB),
        name="router",
    )(*args)


def _experts_kernel(blk_ref, exp_ref, nvalid_ref, x_ref, wg_ref, wu_ref, wd_ref, *y_refs):
    y_ref = y_refs[-1]
    partial_ref = y_refs[0] if len(y_refs) == 2 else None
    valid = pl.program_id(0) < nvalid_ref[0]

    @pl.when(valid)
    def _():
        x = x_ref[...].astype(BF16)
        gate = jnp.dot(x, wg_ref[...], preferred_element_type=F32)
        up = jnp.dot(x, wu_ref[...], preferred_element_type=F32)
        part = jnp.dot((_silu(gate) * up).astype(BF16), wd_ref[...], preferred_element_type=F32)
        y_ref[...] = part if partial_ref is None else partial_ref[...] + part

    @pl.when(jnp.logical_not(valid))
    def _():
        y_ref[...] = jnp.zeros(y_ref.shape, F32) if partial_ref is None else partial_ref[...]


def _experts(xs, tile_block, tile_expert, n_valid, wg, wu, wd):
    n_tiles = tile_block.shape[0]
    d_ff = wg.shape[2]
    ff_chunk = d_ff // 2
    assert ff_chunk % MXU_V7X_COLUMNS == 0
    tg = EXPERT_ROW_TILE
    y = None
    for c in range(d_ff // ff_chunk):
        in_specs = [pl.BlockSpec((tg, D_MODEL), lambda i, blk, ex, nv: (blk[i], 0)),
                    pl.BlockSpec((None, D_MODEL, ff_chunk), lambda i, blk, ex, nv, c=c: (ex[i], 0, c)),
                    pl.BlockSpec((None, D_MODEL, ff_chunk), lambda i, blk, ex, nv, c=c: (ex[i], 0, c)),
                    pl.BlockSpec((None, ff_chunk, D_MODEL), lambda i, blk, ex, nv, c=c: (ex[i], c, 0))]
        args = [tile_block, tile_expert, n_valid, xs, wg, wu, wd]
        y_spec = pl.BlockSpec((tg, D_MODEL), lambda i, blk, ex, nv: (i, 0))
        if y is not None:
            in_specs.append(y_spec)
            args.append(y)
        y = pl.pallas_call(
            _experts_kernel,
            grid_spec=pltpu.PrefetchScalarGridSpec(num_scalar_prefetch=3, grid=(n_tiles,), in_specs=in_specs,
                                                   out_specs=y_spec),
            out_shape=jax.ShapeDtypeStruct((n_tiles * tg, D_MODEL), F32),
            input_output_aliases={len(args) - 1: 0} if len(args) == 8 else {},
            compiler_params=pltpu.CompilerParams(dimension_semantics=("arbitrary",), vmem_limit_bytes=56 * MIB),
            name="experts",
        )(*args)
    return y


def _combine_kernel(tinfo_ref, yoff_ref, h_ref, route_ref, ys_ref, o_ref, sorted_ref, sems, *unpad_scratch,
                    seq_chunks, tiles_per_step):
    i = pl.program_id(0)
    n_steps = pl.num_programs(0)
    n_sorted = sorted_ref.shape[2]
    tm = h_ref.shape[0] // tiles_per_step
    slot = i % 2

    def step_copies(step, buf, live):
        copies = []
        for t in range(tiles_per_step):
            for e in range(N_EXPERTS):
                rec = ((step * tiles_per_step + t) * 3) * N_EXPERTS + e
                base = tinfo_ref[rec + TILE_BASE * N_EXPERTS]
                n_e = jnp.where(live, tinfo_ref[rec + TILE_COUNT * N_EXPERTS], 0)
                dst0 = tinfo_ref[rec + TILE_START * N_EXPERTS]
                copies += _run_copies(n_e, yoff_ref[e] + base, dst0, ys_ref, sorted_ref.at[buf, t],
                                      sems.at[buf, t, e])
        return copies

    @pl.when(i == 0)
    def _():
        sorted_ref[...] = jnp.zeros(sorted_ref.shape, F32)
        for cond, cp in step_copies(0, 0, True):
            pl.when(cond)(cp.start)

    for cond, cp in step_copies(jnp.minimum(i + 1, n_steps - 1), 1 - slot, i + 1 < n_steps):
        pl.when(cond)(cp.start)
    for cond, cp in step_copies(i, slot, True):
        pl.when(cond)(cp.wait)

    lane = lax.broadcasted_iota(I32, (tm, LANES), 1)
    tiles = [slice(t * tm, (t + 1) * tm) for t in range(tiles_per_step)]
    ys = [sorted_ref[slot, t].astype(BF16) for t in range(tiles_per_step)]
    routes = [route_ref[r, :] for r in tiles]
    pick = lambda t, j: _lane_pick(routes[t], lane, j)
    hot = [[_one_hot_rows(pick(t, d), n_sorted).astype(BF16) for d in (ROUTE_D1, ROUTE_D2)]
           for t in range(tiles_per_step)]
    picked = [[jnp.dot(hot[t][k], ys[t], preferred_element_type=F32) for k in range(TOP_K)]
              for t in range(tiles_per_step)]
    results = [h_ref[r, :] + pick(t, ROUTE_G1) * picked[t][0] + pick(t, ROUTE_G2) * picked[t][1]
               for t, r in enumerate(tiles)]
    if seq_chunks is None:
        for r, result in zip(tiles, results):
            o_ref[r, :] = result
        return

    res_ref, out_sems = unpad_scratch
    chunks_per_tile = h_ref.shape[0] // CHUNK

    def chunk_copies(step, buf):
        copies = []
        for j in range(chunks_per_tile):
            chunk = step * chunks_per_tile + j
            seq = chunk // seq_chunks
            in_seq = chunk - seq * seq_chunks
            dst = pl.multiple_of(seq * ((seq_chunks - 1) * CHUNK) + (in_seq - 1) * CHUNK, CHUNK)
            cp = pltpu.make_async_copy(res_ref.at[buf, pl.ds(j * CHUNK, CHUNK)], o_ref.at[pl.ds(dst, CHUNK)],
                                       out_sems.at[buf, j])
            copies.append((jnp.logical_and(step >= 0, in_seq >= 1), cp))
        return copies

    for r, result in zip(tiles, results):
        res_ref[slot, r, :] = result
    for cond, cp in chunk_copies(i, slot):
        pl.when(cond)(cp.start)
    for cond, cp in chunk_copies(i - 1, 1 - slot):
        pl.when(cond)(cp.wait)

    @pl.when(i == n_steps - 1)
    def _():
        for cond, cp in chunk_copies(i, slot):
            pl.when(cond)(cp.wait)


def _combine(h, route, tile_info, y_offsets, ys, unpad_seq_rows=None):
    rows = h.shape[0]
    n_tiles = tile_info.shape[0]
    tm = rows // n_tiles
    per_step = 2 if n_tiles % 2 == 0 else 1
    step_rows = per_step * tm
    n_pieces = _run_piece_count(tm)
    tinfo = tile_info[:, :3, :N_EXPERTS].astype(I32).reshape(-1)
    scratch = [pltpu.VMEM((2, per_step, _sorted_rows(tm), D_MODEL), F32),
               pltpu.SemaphoreType.DMA((2, per_step, N_EXPERTS, n_pieces))]
    if unpad_seq_rows is None:
        seq_chunks, out_rows = None, rows
        out_spec = pl.BlockSpec((step_rows, D_MODEL), lambda i, *_: (i, 0))
    else:
        seq_chunks = unpad_seq_rows // CHUNK
        out_rows = rows // unpad_seq_rows * (unpad_seq_rows - CHUNK)
        out_spec = pl.BlockSpec(memory_space=pl.ANY)
        scratch += [pltpu.VMEM((2, step_rows, D_MODEL), F32), pltpu.SemaphoreType.DMA((2, step_rows // CHUNK))]
    grid_spec = pltpu.PrefetchScalarGridSpec(
        num_scalar_prefetch=2,
        grid=(n_tiles // per_step,),
        in_specs=[pl.BlockSpec((step_rows, D_MODEL), lambda i, *_: (i, 0)),
                  pl.BlockSpec((step_rows, LANES), lambda i, *_: (i, 0)),
                  pl.BlockSpec(memory_space=pl.ANY)],
        out_specs=out_spec,
        scratch_shapes=scratch,
    )
    return pl.pallas_call(
        functools.partial(_combine_kernel, seq_chunks=seq_chunks, tiles_per_step=per_step),
        grid_spec=grid_spec,
        out_shape=jax.ShapeDtypeStruct((out_rows, D_MODEL), F32),
        input_output_aliases={2: 0} if unpad_seq_rows is None else {},
        compiler_params=pltpu.CompilerParams(dimension_semantics=("arbitrary",), vmem_limit_bytes=56 * MIB),
        name="combine",
    )(tinfo, y_offsets, h, route, ys)


def _routed_experts(mixes, gain, router_w, expert_w, unpad_seq_rows):
    tg = EXPERT_ROW_TILE
    group_rows = [mix[3].shape[0] for mix in mixes]
    total_rows = sum(group_rows)
    run_pad = (RUN_ALIGN - 1) * sum(rows // _router_tile(rows) for rows in group_rows)
    region_rows = (pl.cdiv(total_rows + run_pad, tg) + 1) * tg
    counts = jnp.zeros((SUBLANES, LANES), F32)
    xs, hs, routes, tiles = None, [], [], []
    for g, mix in enumerate(mixes):
        h, xs, route, tile_info, counts = _router(mix, gain, router_w, counts, xs, region_rows,
                                                  finalize=(g == len(mixes) - 1))
        hs.append(h)
        routes.append(route)
        tiles.append(tile_info)
    n_e = counts[0, :N_EXPERTS].astype(I32)
    tiles_e = (n_e + tg - 1) // tg
    first_tile = jnp.cumsum(tiles_e) - tiles_e
    n_valid = jnp.sum(tiles_e)
    n_tiles = (TOP_K * total_rows + N_EXPERTS * (run_pad + tg - 1)) // tg + 1
    t = jnp.minimum(jnp.arange(n_tiles, dtype=I32), n_valid - 1)
    tile_expert = jnp.sum((t[:, None] >= (first_tile + tiles_e)[None, :]).astype(I32), axis=1)
    tile_block = tile_expert * (region_rows // tg) + (t - first_tile[tile_expert])
    ys = _experts(xs, tile_block.astype(I32), tile_expert.astype(I32), n_valid.reshape(1).astype(I32), *expert_w)
    y_offsets = (first_tile * tg).astype(I32)
    return [_combine(h, route, tile_info, y_offsets, ys, seq_rows)
            for h, route, tile_info, seq_rows in zip(hs, routes, tiles, unpad_seq_rows)]


def _rotary_tables(batch, seq_rows):
    pos = (jnp.arange(seq_rows, dtype=I32) - PAD_FRONT).astype(F32)

    def cos_sin(n_rot, theta):
        half = n_rot // 2
        inv = theta ** (-jnp.arange(half, dtype=F32) * 2.0 / n_rot)
        ang = pos[:, None] * inv[None, :]
        return jnp.cos(ang), jnp.sin(ang)

    cos, sin = cos_sin(HEAD_DIM, RET_ROPE_THETA)
    cr = jnp.tile(cos, (1, 2 * LANES // HEAD_DIM))
    sr = jnp.tile(jnp.concatenate([-sin, sin], axis=1), (1, LANES // HEAD_DIM))
    cos, sin = cos_sin(ROPE_DIMS, ROPE_THETA)
    rest = HEAD_DIM - ROPE_DIMS
    ca = jnp.tile(jnp.concatenate([cos, cos, jnp.ones((seq_rows, rest), F32)], axis=1), (1, LANES // HEAD_DIM))
    sa = jnp.tile(jnp.concatenate([-sin, sin, jnp.zeros((seq_rows, rest), F32)], axis=1), (1, LANES // HEAD_DIM))
    return tuple(jnp.tile(t, (batch, 1)) for t in (cr, sr, ca, sa))


def _retention_params(log_gf, log_gb, gain):
    n_pairs = RET_WIDTH // LANES
    per_lane = lambda t: jnp.repeat(t.astype(F32).reshape(n_pairs, 2), HEAD_DIM, axis=1)
    per_head = lambda t, j: jnp.broadcast_to(t.astype(F32).reshape(n_pairs, 2)[:, j:j + 1], (n_pairs, LANES))
    rows = [per_lane(log_gf), per_lane(log_gb), per_head(log_gf, 0), per_head(log_gf, 1),
            per_head(log_gb, 0), per_head(log_gb, 1), gain.astype(F32).reshape(n_pairs, LANES),
            jnp.zeros((n_pairs, LANES), F32)]
    return jnp.stack(rows, axis=1)


def _token_mixer(h, lp, tabs, batch, seq_rows, embed=None):
    outs = _inproj(h, lp["norm_mix"], lp["w_in"], tabs, lp["q_gain"], lp["k_gain"], lp["ones_bd"], embed)
    if embed is not None:
        h, outs = outs[0], outs[1:]
    rq, rk, rv, rg, aq, ak, av = outs
    ret = _retention(rq, rk, rv, rg, lp["ret_dec"], batch, seq_rows)
    att = _attention(aq, ak, av, lp["sink"], lp["att_gain"], batch, seq_rows)
    return ret, att, lp["w_out"], h


def kernel(x_prompt, x_sample, meta_tokens, norm_mix, w_in, ret_log_decay_fwd, ret_log_decay_bwd, ret_out_gain,
           q_norm_gain, k_norm_gain, attn_sink, attn_out_gain, w_out, norm_ffn, ffn_w_gate, ffn_w_up, ffn_w_down,
           moe_router, moe_w_gate, moe_w_up, moe_w_down):
    depth = w_in.shape[0]
    ri = lax.broadcasted_iota(I32, (LANES, LANES), 0)
    ci = lax.broadcasted_iota(I32, (LANES, LANES), 1)
    ones_bd = ((ri < HEAD_DIM) == (ci < HEAD_DIM)).astype(BF16)
    row = lambda t: t.astype(F32).reshape(1, -1)

    xs = (x_prompt, x_sample)
    shapes = [(x.shape[0], x.shape[1] + CHUNK) for x in xs]
    hs = [None] * len(xs)
    meta = meta_tokens.astype(F32)
    embeds = [(x.astype(F32).reshape(-1, D_MODEL), meta, r) for x, (_, r) in zip(xs, shapes)]
    tabs = [_rotary_tables(b, r) for b, r in shapes]

    for l in range(depth):
        lp = {
            "norm_mix": row(norm_mix[l]),
            "w_in": w_in[l].astype(BF16),
            "q_gain": jnp.tile(row(q_norm_gain[l]), (1, LANES // HEAD_DIM)),
            "k_gain": jnp.tile(row(k_norm_gain[l]), (1, LANES // HEAD_DIM)),
            "ones_bd": ones_bd,
            "ret_dec": _retention_params(ret_log_decay_fwd[l], ret_log_decay_bwd[l], ret_out_gain[l]),
            "sink": jnp.broadcast_to(attn_sink[l].astype(F32)[:, None] * LOG2_E, (N_ATT_HEADS, LANES)),
            "att_gain": row(attn_out_gain[l]),
            "w_out": w_out[l].astype(BF16),
        }
        mixes = [_token_mixer(h, lp, tab, b, r, embed if l == 0 else None)
                 for h, tab, (b, r), embed in zip(hs, tabs, shapes, embeds)]
        i = l // 2
        if l % 2 == 0:
            w = (ffn_w_gate[i].astype(BF16), ffn_w_up[i].astype(BF16), ffn_w_down[i].astype(BF16))
            hs = [_ffn(*mix, row(norm_ffn[l]), *w) for mix in mixes]
        else:
            wr = jnp.pad(moe_router[i].astype(F32), ((0, 0), (0, LANES - N_EXPERTS)))
            wr_hi = wr.astype(BF16)
            router_w = jnp.concatenate([wr_hi, (wr - wr_hi.astype(F32)).astype(BF16)], axis=1)
            expert_w = (moe_w_gate[i].astype(BF16), moe_w_up[i].astype(BF16), moe_w_down[i].astype(BF16))
            last = l == depth - 1
            hs = _routed_experts(mixes, row(norm_ffn[l]), router_w, expert_w,
                                 [r if last else None for _, r in shapes])
    if depth % 2 == 0:
        return tuple(h.reshape(x.shape).astype(x.dtype) for h, x in zip(hs, xs))
    return tuple(h.reshape(b, r, D_MODEL)[:, CHUNK:].astype(x.dtype) for h, x, (b, r) in zip(hs, xs, shapes))
```

```python
import functools

import jax
import jax.numpy as jnp
from jax import lax
from jax.experimental import pallas as pl
from jax.experimental.pallas import tpu as pltpu

F32 = jnp.float32
BF16 = jnp.bfloat16
I32 = jnp.int32

D_MODEL = 1024
HEAD_DIM = 64
N_RET_HEADS = 8
N_ATT_HEADS = 8
N_KV_HEADS = 2
RET_WIDTH = N_RET_HEADS * HEAD_DIM
ATT_WIDTH = N_ATT_HEADS * HEAD_DIM
KV_WIDTH = N_KV_HEADS * HEAD_DIM
IN_WIDTH = 4 * RET_WIDTH + ATT_WIDTH + 2 * KV_WIDTH
CHUNK = 128
LANES = 128
SUBLANES = 8
N_META = 16
PAD_FRONT = CHUNK - N_META
ROPE_THETA = 500000.0
ROPE_DIMS = HEAD_DIM // 4
RET_ROPE_THETA = 10000.0
N_EXPERTS = 8
TOP_K = 2
EXPERT_ROW_TILE = 512
MXU_V7X_COLUMNS = 256
EPS = 1e-6
NEG = -1e30
QK_SCALE = HEAD_DIM ** -0.5
LOG2_E = 1.4426950408889634
MIB = 1024 * 1024


def _row_tile(rows, prefs):
    for t in prefs:
        if rows % t == 0:
            return t
    raise ValueError(f"no row tile for {rows} rows among {prefs}")


def _resident(shape):
    return pl.BlockSpec(shape, lambda *_: (0,) * len(shape), pipeline_mode=pl.Buffered(1))


def _rms(x, gain):
    ms = jnp.mean(x * x, axis=-1, keepdims=True)
    return x * lax.rsqrt(ms + EPS) * gain


def _silu(x):
    return x * jax.nn.sigmoid(x)


def _embed_kernel(x_ref, meta_ref, *refs, seq_chunks):
    h0_ref, hbuf_ref, sems = refs[-10], refs[-2], refs[-1]
    i = pl.program_id(0)
    n_steps = pl.num_programs(0)
    tm = h0_ref.shape[0]
    chunks_per_tile = tm // CHUNK
    slot = i % 2

    def chunk_place(step, j):
        chunk = step * chunks_per_tile + j
        seq = chunk // seq_chunks
        return seq, chunk - seq * seq_chunks

    def chunk_copies(step, buf):
        copies = []
        for j in range(chunks_per_tile):
            seq, in_seq = chunk_place(step, j)
            src = pl.multiple_of(seq * ((seq_chunks - 1) * CHUNK) + (in_seq - 1) * CHUNK, CHUNK)
            cp = pltpu.make_async_copy(x_ref.at[pl.ds(src, CHUNK)], hbuf_ref.at[buf, pl.ds(j * CHUNK, CHUNK)],
                                       sems.at[buf, j])
            copies.append((jnp.logical_and(step < n_steps, in_seq >= 1), cp))
        return copies

    @pl.when(i == 0)
    def _():
        for cond, cp in chunk_copies(0, 0):
            pl.when(cond)(cp.start)

    for cond, cp in chunk_copies(i + 1, 1 - slot):
        pl.when(cond)(cp.start)
    for cond, cp in chunk_copies(i, slot):
        pl.when(cond)(cp.wait)
    lead = jnp.concatenate([jnp.zeros((PAD_FRONT, D_MODEL), F32), meta_ref[...]], axis=0)
    for j in range(chunks_per_tile):
        @pl.when(chunk_place(i, j)[1] == 0)
        def _():
            hbuf_ref[slot, j * CHUNK:(j + 1) * CHUNK, :] = lead

    h0_ref[...] = hbuf_ref[slot]
    _inproj_kernel(hbuf_ref.at[slot], *refs[:-10], *refs[-9:-2])


def _inproj_kernel(h_ref, gain_ref, w_ref, cr_ref, sr_ref, ca_ref, sa_ref, qg_ref, kg_ref, ones_ref,
                   rq_ref, rk_ref, rv_ref, rg_ref, aq_ref, ak_ref, av_ref):
    tm = h_ref.shape[0]
    part = tm
    lane = lax.broadcasted_iota(I32, (part, LANES), 1)
    in_head = lane & (HEAD_DIM - 1)
    low_head = lane < HEAD_DIM
    cols = lambda j: slice(j * LANES, (j + 1) * LANES)
    n_lane_tiles = RET_WIDTH // LANES
    base = 4 * RET_WIDTH

    def project(r):
        a = _rms(h_ref[r, :], gain_ref[...]).astype(BF16)
        proj = lambda c0, c1: jnp.dot(a, w_ref[:, c0:c1], preferred_element_type=F32)
        return dict(aq=proj(base, base + ATT_WIDTH), kv=proj(base + ATT_WIDTH, IN_WIDTH),
                    rq=proj(0, RET_WIDTH), rk=proj(RET_WIDTH, 2 * RET_WIDTH),
                    rv=proj(2 * RET_WIDTH, 3 * RET_WIDTH), rg=proj(3 * RET_WIDTH, 4 * RET_WIDTH))

    def rotate(x, cos, sin_signed, half):
        partner = jnp.where(in_head < half, pltpu.roll(x, LANES - half, 1), pltpu.roll(x, half, 1))
        return x * cos + partner * sin_signed

    def head_norm(x, g):
        ms = jnp.dot((x * x).astype(BF16), ones_ref[...], preferred_element_type=F32)
        return x * lax.rsqrt(ms + EPS) * g

    def epilogue(r, first_chunk, p):
        cr, sr, ca, sa = cr_ref[r, :], sr_ref[r, :], ca_ref[r, :], sa_ref[r, :]

        def store_keys(kt_ref, j, x):
            for c in range(part // CHUNK):
                kt_ref[first_chunk + c, cols(j), :] = x[c * CHUNK:(c + 1) * CHUNK, :].T.astype(BF16)

        rv_ref[r, :] = p["rv"].astype(BF16)
        rg_ref[r, :] = p["rg"].astype(BF16)
        for j in range(n_lane_tiles):
            rq_ref[r, cols(j)] = rotate(p["rq"][:, cols(j)], cr, sr, HEAD_DIM // 2).astype(BF16)
        for j in range(n_lane_tiles):
            store_keys(rk_ref, j, rotate(p["rk"][:, cols(j)], cr, sr, HEAD_DIM // 2) * QK_SCALE)
        normed = [head_norm(p["aq"][:, cols(j)], qg_ref[...]) for j in range(ATT_WIDTH // LANES)]
        normed_k = head_norm(p["kv"][:, :KV_WIDTH], kg_ref[...])
        for j, n in enumerate(normed):
            aq_ref[r, cols(j)] = (rotate(n, ca, sa, ROPE_DIMS // 2) * (QK_SCALE * LOG2_E)).astype(BF16)
        k = rotate(normed_k, ca, sa, ROPE_DIMS // 2)
        v = p["kv"][:, KV_WIDTH:]
        k_sw = pltpu.roll(k, HEAD_DIM, 1)
        v_sw = pltpu.roll(v, HEAD_DIM, 1)
        store_keys(ak_ref, 0, jnp.where(low_head, k, k_sw))
        store_keys(ak_ref, 1, jnp.where(low_head, k_sw, k))
        av_ref[r, 0:LANES] = jnp.where(low_head, v, v_sw).astype(BF16)
        av_ref[r, LANES:2 * LANES] = jnp.where(low_head, v_sw, v).astype(BF16)

    whole_tile = slice(0, tm)
    epilogue(whole_tile, 0, project(whole_tile))


def _inproj(h, gain, w, tabs, qg, kg, ones_bd, embed=None):
    rows = tabs[0].shape[0]
    tm = _row_tile(rows, (1024, 640, 512, 384, 256, 128) if embed is None else (768, 640, 512, 384, 256, 128))
    row_spec = lambda width: pl.BlockSpec((tm, width), lambda i: (i, 0))
    outs = ((RET_WIDTH, False), (RET_WIDTH, True), (RET_WIDTH, False), (RET_WIDTH, False),
            (ATT_WIDTH, False), (2 * KV_WIDTH, True), (2 * KV_WIDTH, False))
    chunk_spec = lambda width: pl.BlockSpec((tm // CHUNK, width, CHUNK), lambda i: (i, 0, 0))
    in_specs = [_resident((1, D_MODEL)), _resident((D_MODEL, IN_WIDTH)),
                row_spec(LANES), row_spec(LANES), row_spec(LANES), row_spec(LANES),
                _resident((1, LANES)), _resident((1, LANES)), _resident((LANES, LANES))]
    out_specs = [chunk_spec(wd) if t else row_spec(wd) for wd, t in outs]
    out_shape = [jax.ShapeDtypeStruct((rows // CHUNK, wd, CHUNK) if t else (rows, wd), BF16) for wd, t in outs]
    if embed is None:
        return pl.pallas_call(
            _inproj_kernel,
            grid=(rows // tm,),
            in_specs=[row_spec(D_MODEL)] + in_specs,
            out_specs=out_specs,
            out_shape=out_shape,
            compiler_params=pltpu.CompilerParams(dimension_semantics=("parallel",), vmem_limit_bytes=56 * MIB),
            name="inproj",
        )(h, gain, w, *tabs, qg, kg, ones_bd)
    x, meta, seq_rows = embed
    return pl.pallas_call(
        functools.partial(_embed_kernel, seq_chunks=seq_rows // CHUNK),
        grid=(rows // tm,),
        in_specs=[pl.BlockSpec(memory_space=pl.ANY), _resident((N_META, D_MODEL))] + in_specs,
        out_specs=[row_spec(D_MODEL)] + out_specs,
        out_shape=[jax.ShapeDtypeStruct((rows, D_MODEL), F32)] + out_shape,
        scratch_shapes=[pltpu.VMEM((2, tm, D_MODEL), F32), pltpu.SemaphoreType.DMA((2, tm // CHUNK))],
        compiler_params=pltpu.CompilerParams(dimension_semantics=("arbitrary",), vmem_limit_bytes=56 * MIB),
        name="embed",
    )(x, meta, gain, w, *tabs, qg, kg, ones_bd)


def _retention_kernel(q_ref, k_ref, v_ref, g_ref, dec_ref, o_ref, sb_ref, *, n_chunks):
    lgf, lgb = dec_ref[0:1, :], dec_ref[1:2, :]
    gain = dec_ref[6:7, :]
    ri = lax.broadcasted_iota(I32, (CHUNK, LANES), 0)
    ci = lax.broadcasted_iota(I32, (CHUNK, LANES), 1)
    r = ri.astype(F32)
    diff = (ri - ci).astype(F32)

    def decay_mask(lf, lb):
        return jnp.where(diff >= 0, jnp.exp(jnp.maximum(diff, 0.0) * lf), jnp.exp(jnp.maximum(-diff, 0.0) * lb))

    dm = jnp.concatenate([decay_mask(dec_ref[2:3, :], dec_ref[4:5, :]),
                          decay_mask(dec_ref[3:4, :], dec_ref[5:6, :])], axis=1)
    tok = ci.astype(F32)
    wf = jnp.exp((CHUNK - 1.0 - tok) * jnp.broadcast_to(lgf, (LANES, LANES)).T)
    wb = jnp.exp(tok * jnp.broadcast_to(lgb, (LANES, LANES)).T)
    qf = jnp.exp((r + 1.0) * lgf)
    qb = jnp.exp((CHUNK - r) * lgb)
    cf = jnp.exp(float(CHUNK) * lgf)
    cb = jnp.exp(float(CHUNK) * lgb)
    low_c = ci < HEAD_DIM
    same_head = (ri < HEAD_DIM) == low_c
    mean_bd = jnp.where(same_head, 1.0 / HEAD_DIM, 0.0).astype(BF16)
    m0 = jnp.where(low_c, 1.0, 0.0).astype(BF16)
    m1 = jnp.where(low_c, 0.0, 1.0).astype(BF16)
    low_r = ri < HEAD_DIM
    rm0 = jnp.where(low_r, 1.0, 0.0).astype(BF16)
    rm1 = jnp.where(low_r, 0.0, 1.0).astype(BF16)

    def chunk_rows(n):
        return pl.ds(pl.multiple_of(n * CHUNK, CHUNK), CHUNK)

    def pair_rows(x):
        return jnp.concatenate([x * m0, x * m1], axis=0)

    def pair_cols(xt):
        return jnp.concatenate([xt * rm0, xt * rm1], axis=1)

    def state_delta(kt, weights, v):
        u = jnp.dot((kt.astype(F32) * weights).astype(BF16), v, preferred_element_type=F32)
        return jnp.where(same_head, u, 0.0)

    group = next(u for u in (11, 13, 4, 3, 5, 2, 1) if n_chunks % u == 0)
    n_groups = n_chunks // group
    members = range(group)
    zero_state = jnp.zeros((LANES, LANES), F32)

    def backward(i, state):
        ns = [n_chunks - 1 - i * group - j for j in members]
        deltas = [state_delta(k_ref[n], wb, v_ref[chunk_rows(n), :]) for n in ns]
        for n, delta in zip(ns, deltas):
            sb_ref[n] = state.astype(BF16)
            state = cb * state + delta
        return state

    lax.fori_loop(0, n_groups, backward, zero_state)

    def forward(i, state):
        ns = [i * group + j for j in members]
        q = [q_ref[chunk_rows(n), :] for n in ns]
        kt = [k_ref[n] for n in ns]
        v = [v_ref[chunk_rows(n), :] for n in ns]
        deltas = [state_delta(kt[j], wf, v[j]) for j in members]
        s = [jnp.dot(q[j], pair_cols(kt[j]), preferred_element_type=F32) * dm for j in members]
        both = []
        for j in members:
            both.append(jnp.concatenate([state.astype(BF16), sb_ref[ns[j]]], axis=1))
            state = cf * state + deltas[j]
        intra = [jnp.dot(s[j].astype(BF16), pair_rows(v[j]), preferred_element_type=F32) for j in members]
        inter = [jnp.dot(q[j], both[j], preferred_element_type=F32) for j in members]
        o = [intra[j] + inter[j][:, :LANES] * qf + inter[j][:, LANES:] * qb for j in members]
        ms = [jnp.dot((o[j] * o[j]).astype(BF16), mean_bd, preferred_element_type=F32) for j in members]
        for j in members:
            gate = _silu(g_ref[chunk_rows(ns[j]), :].astype(F32))
            o_ref[chunk_rows(ns[j]), :] = (o[j] * lax.rsqrt(ms[j] + EPS) * gain * gate).astype(BF16)
        return state

    lax.fori_loop(0, n_groups, forward, zero_state)


def _retention(rq, rk, rv, rg, dec, batch, seq_rows):
    n_chunks = seq_rows // CHUNK
    n_pairs = RET_WIDTH // LANES
    view = lambda t: t.reshape(batch, seq_rows, RET_WIDTH)
    seq_spec = pl.BlockSpec((None, seq_rows, LANES), lambda b, p: (b, 0, p))
    key_spec = pl.BlockSpec((n_chunks, LANES, CHUNK), lambda b, p: (b, p, 0))
    out = pl.pallas_call(
        functools.partial(_retention_kernel, n_chunks=n_chunks),
        grid=(batch, n_pairs),
        in_specs=[seq_spec, key_spec, seq_spec, seq_spec,
                  pl.BlockSpec((None, 8, LANES), lambda b, p: (p, 0, 0))],
        out_specs=seq_spec,
        out_shape=jax.ShapeDtypeStruct((batch, seq_rows, RET_WIDTH), BF16),
        scratch_shapes=[pltpu.VMEM((n_chunks, LANES, LANES), BF16)],
        compiler_params=pltpu.CompilerParams(dimension_semantics=("parallel", "parallel"),
                                             vmem_limit_bytes=48 * MIB),
        name="retention",
    )(view(rq), rk, view(rv), view(rg), dec)
    return out.reshape(batch * seq_rows, RET_WIDTH)


def _attention_kernel(q_ref, kp_ref, kc_ref, kn_ref, km_ref, vp_ref, vc_ref, vn_ref, vm_ref,
                      sink_ref, gain_ref, o_ref, *, n_chunks, chunks_per_step):
    first_chunk = pl.program_id(1) * chunks_per_step
    ri = lax.broadcasted_iota(I32, (CHUNK, LANES), 0)
    ci = lax.broadcasted_iota(I32, (CHUNK, LANES), 1)
    never = 2 * CHUNK
    to_bias = lambda ok: jnp.where(ok, 0.0, NEG).astype(BF16)
    meta_bias = to_bias(ci >= PAD_FRONT)
    n_keys = 4 * CHUNK
    low_c = ci < HEAD_DIM
    m0 = jnp.where(low_c, 1.0, 0.0).astype(BF16)
    m1 = jnp.where(low_c, 0.0, 1.0).astype(BF16)
    low_r = ri < HEAD_DIM
    rm0 = jnp.where(low_r, 1.0, 0.0).astype(BF16)
    rm1 = jnp.where(low_r, 0.0, 1.0).astype(BF16)
    eye = jnp.where(ri == ci, 1.0, 0.0).astype(BF16)
    row_sums = jnp.concatenate([m0] * 4 + [m1] * 4, axis=0)
    pairs_per_kv = N_ATT_HEADS // N_KV_HEADS // 2
    rows = lambda j: slice(j * CHUNK, (j + 1) * CHUNK)

    def split_values(ref, blk, g):
        x = ref[blk, g * LANES:(g + 1) * LANES]
        return x * m0, x * m1

    def split_keys(ref, idx, g):
        xt = ref[idx, g * LANES:(g + 1) * LANES, :]
        return xt * rm0, xt * rm1

    def blocks(split, at, prev_ref, main_ref, next_ref, meta_ref, g):
        out = [split(prev_ref, at(0), g)]
        out += [split(main_ref, at(j), g) for j in range(chunks_per_step)]
        out += [split(next_ref, at(0), g), split(meta_ref, at(0), g)]
        return out

    k_blocks = [blocks(split_keys, lambda j: j, kp_ref, kc_ref, kn_ref, km_ref, g) for g in range(N_KV_HEADS)]
    v_blocks = [blocks(split_values, rows, vp_ref, vc_ref, vn_ref, vm_ref, g) for g in range(N_KV_HEADS)]

    def pair_blocks(blks, j, axis):
        use = [blks[j], blks[j + 1], blks[j + 2], blks[-1]]
        return jnp.concatenate([b[0] for b in use] + [b[1] for b in use], axis=axis)

    def pair_ids(g):
        return [g * pairs_per_kv + pp for pp in range(pairs_per_kv)]

    def scores(j, g):
        c = first_chunk + j
        prev_off = jnp.where(c >= 2, 0, never)
        cur_off = jnp.where(c >= 1, 0, never)
        next_off = jnp.where(c + 1 <= n_chunks - 1, 0, never)
        bias = jnp.concatenate([to_bias(ci >= ri + prev_off), to_bias(ci >= cur_off),
                                to_bias(ci + next_off <= ri), meta_bias] * 2, axis=1)
        k_ext = jnp.concatenate([pair_blocks(k_blocks[g], j, 1), bias], axis=0)
        q_ext = jnp.concatenate(
            [jnp.concatenate([q_ref[rows(j), p * LANES:(p + 1) * LANES], eye], axis=1) for p in pair_ids(g)], axis=0)
        return jnp.dot(q_ext, k_ext, preferred_element_type=F32)

    def softmax_parts(g, s):
        probs, sink_rows = [], []
        for pp, p in enumerate(pair_ids(g)):
            halves, sink_terms = [], []
            for hh in range(2):
                sh = s[rows(pp), hh * n_keys:(hh + 1) * n_keys]
                sk = sink_ref[2 * p + hh:2 * p + hh + 1, 0:1]
                m = jnp.maximum(jnp.max(sh, axis=-1, keepdims=True), sk)
                halves.append(jnp.exp2(sh - m).astype(BF16))
                sink_terms.append(jnp.exp2(sk - m))
            probs.append(jnp.concatenate(halves, axis=1))
            sink_rows.append(jnp.where(low_c, sink_terms[0], sink_terms[1]))
        return jnp.concatenate(probs, axis=0), sink_rows

    def weighted_values(j, g, probs, sink_rows):
        v_ext = jnp.concatenate([pair_blocks(v_blocks[g], j, 0), row_sums], axis=1)
        ol = jnp.dot(probs, v_ext, preferred_element_type=F32)
        return [ol[rows(pp), :LANES] / (ol[rows(pp), LANES:] + sink_rows[pp]) for pp in range(pairs_per_kv)]

    chunk_group = 4
    for j0 in range(0, chunks_per_step, chunk_group):
        units = [(j, g) for j in range(j0, min(j0 + chunk_group, chunks_per_step)) for g in range(N_KV_HEADS)]
        s = [scores(j, g) for j, g in units]
        parts = [softmax_parts(g, s[u]) for u, (j, g) in enumerate(units)]
        outs = [weighted_values(j, g, *parts[u]) for u, (j, g) in enumerate(units)]
        for j in sorted({j for j, _ in units}):
            att = jnp.concatenate([o for u, (ju, _) in enumerate(units) if ju == j for o in outs[u]], axis=1)
            row = lax.broadcasted_iota(I32, att.shape, 0)
            pad_rows = jnp.where(first_chunk + j == 0, PAD_FRONT, 0)
            att = jnp.where(row >= pad_rows, att, 0.0)
            o_ref[rows(j), :] = _rms(att, gain_ref[...]).astype(BF16)


def _attention(aq, ak, av, sink, gain, batch, seq_rows):
    n_chunks = seq_rows // CHUNK
    per_step = next(r for r in (11, 13, 3, 5, 4, 2, 1) if n_chunks % r == 0)
    n_steps = n_chunks // per_step
    q3 = aq.reshape(batch, seq_rows, ATT_WIDTH)
    v3 = av.reshape(batch, seq_rows, 2 * KV_WIDTH)
    prev_chunk = lambda s: jnp.maximum(s * per_step - 1, 0)
    next_chunk = lambda s: jnp.minimum((s + 1) * per_step, n_chunks - 1)
    v_one = lambda fn: pl.BlockSpec((None, CHUNK, 2 * KV_WIDTH), lambda b, s: (b, fn(s), 0))
    v_specs = [v_one(prev_chunk), pl.BlockSpec((None, per_step * CHUNK, 2 * KV_WIDTH), lambda b, s: (b, s, 0)),
               v_one(next_chunk), v_one(lambda s: 0)]
    k_one = lambda fn: pl.BlockSpec((1, 2 * KV_WIDTH, CHUNK), lambda b, s: (b * n_chunks + fn(s), 0, 0))
    k_specs = [k_one(prev_chunk),
               pl.BlockSpec((per_step, 2 * KV_WIDTH, CHUNK), lambda b, s: (b * n_steps + s, 0, 0)),
               k_one(next_chunk), k_one(lambda s: 0)]
    q_spec = pl.BlockSpec((None, per_step * CHUNK, ATT_WIDTH), lambda b, s: (b, s, 0))
    out = pl.pallas_call(
        functools.partial(_attention_kernel, n_chunks=n_chunks, chunks_per_step=per_step),
        grid=(batch, n_steps),
        in_specs=[q_spec] + k_specs + v_specs + [_resident((N_ATT_HEADS, LANES)), _resident((1, ATT_WIDTH))],
        out_specs=q_spec,
        out_shape=jax.ShapeDtypeStruct((batch, seq_rows, ATT_WIDTH), BF16),
        compiler_params=pltpu.CompilerParams(dimension_semantics=("parallel", "parallel"),
                                             vmem_limit_bytes=32 * MIB),
        name="attention",
    )(q3, ak, ak, ak, ak, v3, v3, v3, v3, sink, gain)
    return out.reshape(batch * seq_rows, ATT_WIDTH)


def _mixed_residual(ret_ref, att_ref, w_ref, h_ref, rows=slice(None)):
    acc = jnp.dot(ret_ref[rows, :], w_ref[0:RET_WIDTH, :], preferred_element_type=F32)
    acc = acc + jnp.dot(att_ref[rows, :], w_ref[RET_WIDTH:, :], preferred_element_type=F32)
    return h_ref[rows, :] + acc


def _mix_specs(tm):
    row_spec = lambda width: pl.BlockSpec((tm, width), lambda i, *_: (i, 0))
    return [row_spec(RET_WIDTH), row_spec(ATT_WIDTH), _resident((RET_WIDTH + ATT_WIDTH, D_MODEL)),
            row_spec(D_MODEL)]


def _ff_halves(d_ff):
    n_mxu_tiles = pl.cdiv(d_ff, MXU_V7X_COLUMNS)
    first = min(d_ff, pl.cdiv(n_mxu_tiles, 2) * MXU_V7X_COLUMNS)
    return tuple((c0, c1) for c0, c1 in ((0, first), (first, d_ff)) if c1 > c0)


def _ffn_kernel(ret_ref, att_ref, wo_ref, h_ref, gain_ref, wg_ref, wu_ref, wd_ref, o_ref):
    x = _mixed_residual(ret_ref, att_ref, wo_ref, h_ref)
    f = _rms(x, gain_ref[...]).astype(BF16)
    acc = x
    for c0, c1 in _ff_halves(wg_ref.shape[1]):
        gate = jnp.dot(f, wg_ref[:, c0:c1], preferred_element_type=F32)
        up = jnp.dot(f, wu_ref[:, c0:c1], preferred_element_type=F32)
        act = (_silu(gate) * up).astype(BF16)
        acc = acc + jnp.dot(act, wd_ref[c0:c1, :], preferred_element_type=F32)
    o_ref[...] = acc


def _ffn(ret, att, w_out, h, gain, wg, wu, wd):
    rows = h.shape[0]
    d_ff = wg.shape[1]
    tm = _row_tile(rows, (768, 640, 512, 384, 256, 128))
    return pl.pallas_call(
        _ffn_kernel,
        grid=(rows // tm,),
        in_specs=_mix_specs(tm) + [_resident((1, D_MODEL)), _resident((D_MODEL, d_ff)), _resident((D_MODEL, d_ff)),
                                   _resident((d_ff, D_MODEL))],
        out_specs=pl.BlockSpec((tm, D_MODEL), lambda i: (i, 0)),
        out_shape=jax.ShapeDtypeStruct((rows, D_MODEL), F32),
        input_output_aliases={3: 0},
        compiler_params=pltpu.CompilerParams(dimension_semantics=("parallel",), vmem_limit_bytes=56 * MIB),
        name="ffn",
    )(ret, att, w_out, h, gain, wg, wu, wd)


ROUTE_G1, ROUTE_G2, ROUTE_D1, ROUTE_D2 = range(4)
TILE_BASE, TILE_COUNT, TILE_START = range(3)


def _lane_pick(x, lane, j):
    return jnp.sum(jnp.where(lane == j, x, 0.0), axis=-1, keepdims=True)


def _lane_scalar(vec, lane_row, e):
    return jnp.sum(jnp.where(lane_row == e, vec, 0.0)).astype(I32)


RUN_ALIGN_BITS = 3
RUN_ALIGN = 1 << RUN_ALIGN_BITS


def _run_piece_count(tm):
    return tm.bit_length() - RUN_ALIGN_BITS


def _run_copies(count, src0, dst0, src_ref, dst_ref, sems):
    copies = []
    for j in range(sems.shape[0]):
        k = j + RUN_ALIGN_BITS
        size = 1 << k
        start = (count >> (k + 1)) << (k + 1)
        src = src_ref.at[pl.ds(pl.multiple_of(src0 + start, RUN_ALIGN), size)]
        dst = dst_ref.at[pl.ds(pl.multiple_of(dst0 + start, RUN_ALIGN), size)]
        copies.append((((count >> k) & 1) == 1, pltpu.make_async_copy(src, dst, sems.at[j])))
    return copies


def _start_then_wait(copies):
    for cond, cp in copies:
        pl.when(cond)(cp.start)
    for cond, cp in copies:
        pl.when(cond)(cp.wait)


def _router_kernel(*refs, region_rows, finalize, aliased, tiles_per_step):
    ret_ref, att_ref, wo_ref, h_ref, gain_ref, wr_ref, base_ref = refs[:7]
    refs = refs[7 + (1 if aliased else 0):]
    hout_ref, xs_ref, route_ref, tile_ref, cnt_ref, sorted_ref, zero_ref, sems, zsem = refs
    i = pl.program_id(0)
    tm = h_ref.shape[0] // tiles_per_step
    n_sorted = sorted_ref.shape[1]

    @pl.when(i == 0)
    def _():
        cnt_ref[...] = base_ref[...]

    lane = lax.broadcasted_iota(I32, (tm, LANES), 1)
    lane_row = lane[0:1, :]
    ri = lax.broadcasted_iota(I32, (tm, tm), 0)
    ci = lax.broadcasted_iota(I32, (tm, tm), 1)
    earlier_tokens = jnp.where(ri > ci, 1.0, 0.0).astype(BF16)
    li = lax.broadcasted_iota(I32, (LANES, LANES), 0)
    lj = lax.broadcasted_iota(I32, (LANES, LANES), 1)
    earlier_experts = jnp.where(li < lj, 1.0, 0.0).astype(BF16)
    slot = lax.broadcasted_iota(I32, (n_sorted, tm), 0)
    sub = lax.broadcasted_iota(I32, (SUBLANES, LANES), 0)

    def as_row(col):
        wide = jnp.broadcast_to(col, (tm, LANES))
        return jnp.concatenate([wide[c * LANES:(c + 1) * LANES, :].T for c in range(tm // LANES)], axis=1)[0:1, :]

    recs = [dict(t=t, rows=slice(t * tm, (t + 1) * tm)) for t in range(tiles_per_step)]

    def stage(fn):
        for rec in recs:
            rec.update(fn(rec))

    def mix(r):
        h_mixed = _mixed_residual(ret_ref, att_ref, wo_ref, h_ref, r["rows"])
        hout_ref[r["rows"], :] = h_mixed
        return dict(f=_rms(h_mixed, gain_ref[...]))

    def logits(r):
        f_hi = r["f"].astype(BF16)
        f_lo = (r["f"] - f_hi.astype(F32)).astype(BF16)
        both = (jnp.dot(f_hi, wr_ref[...], preferred_element_type=F32)
                + jnp.dot(f_lo, wr_ref[...], preferred_element_type=F32))
        lg = jnp.where(lane < N_EXPERTS, both[:, :LANES] + both[:, LANES:], -jnp.inf)
        return dict(f_hi=f_hi, lg=lg)

    def first_choice(r):
        m1 = jnp.max(r["lg"], axis=-1, keepdims=True)
        return dict(m1=m1, i1=jnp.min(jnp.where(r["lg"] == m1, lane, LANES), axis=-1, keepdims=True))

    def second_choice(r):
        lg2 = jnp.where(lane == r["i1"], -jnp.inf, r["lg"])
        m2 = jnp.max(lg2, axis=-1, keepdims=True)
        return dict(m2=m2, i2=jnp.min(jnp.where(lg2 == m2, lane, LANES), axis=-1, keepdims=True))

    def rank_tokens(r):
        active = jnp.max(jnp.abs(r["f"]), axis=-1, keepdims=True) > 0.0
        sel = jnp.where(active, jnp.where(lane == r["i1"], 1.0, 0.0) + jnp.where(lane == r["i2"], 1.0, 0.0), 0.0)
        rank = jnp.dot(earlier_tokens, sel.astype(BF16), preferred_element_type=F32)
        tiles_per_run = jnp.floor((jnp.sum(sel, axis=0, keepdims=True) + (RUN_ALIGN - 1.0)) * (1.0 / RUN_ALIGN))
        start = RUN_ALIGN * jnp.dot(jnp.broadcast_to(tiles_per_run, (SUBLANES, LANES)).astype(BF16),
                                    earlier_experts, preferred_element_type=F32)[0:1, :]
        return dict(active=active, rank=rank, count=tiles_per_run * RUN_ALIGN, start=start)

    def destinations(r):
        dest = r["start"] + r["rank"]
        no_slot = -1.0
        d1 = jnp.where(r["active"], _lane_pick(dest, lane, r["i1"]), no_slot)
        d2 = jnp.where(r["active"], _lane_pick(dest, lane, r["i2"]), no_slot)
        e2 = jnp.exp(r["m2"] - r["m1"])
        g1 = 1.0 / (1.0 + e2)
        route = jnp.zeros((tm, LANES), F32)
        for j, val in ((ROUTE_G1, g1), (ROUTE_G2, e2 * g1), (ROUTE_D1, d1), (ROUTE_D2, d2)):
            route = jnp.where(lane == j, val, route)
        route_ref[r["rows"], :] = route
        return dict(d1=d1, d2=d2)

    def sort_rows(r):
        onehot_t = (jnp.where(slot == as_row(r["d1"]).astype(I32), 1.0, 0.0)
                    + jnp.where(slot == as_row(r["d2"]).astype(I32), 1.0, 0.0))
        sorted_ref[r["t"]] = jnp.dot(onehot_t.astype(BF16), r["f_hi"], preferred_element_type=F32)
        return {}

    for fn in (mix, logits, first_choice, second_choice, rank_tokens, destinations, sort_rows):
        stage(fn)
    base = cnt_ref[0:1, :]
    for rec in recs:
        rec["base"] = base
        tile_ref[rec["t"]] = jnp.where(sub == TILE_BASE, base, jnp.where(
            sub == TILE_COUNT, rec["count"], jnp.where(sub == TILE_START, rec["start"], 0.0)))
        base = base + rec["count"]
    copies = []
    for t, rec in enumerate(recs):
        for e in range(N_EXPERTS):
            n_e = _lane_scalar(rec["count"], lane_row, e)
            src0 = _lane_scalar(rec["start"], lane_row, e)
            dst0 = e * region_rows + _lane_scalar(rec["base"], lane_row, e)
            copies += _run_copies(n_e, src0, dst0, sorted_ref.at[t], xs_ref, sems.at[t, e])
    _start_then_wait(copies)
    cnt_ref[...] = jnp.broadcast_to(base, cnt_ref.shape)

    if finalize:
        @pl.when(i == pl.num_programs(0) - 1)
        def _():
            zero_ref[...] = jnp.zeros(zero_ref.shape, F32)
            total = cnt_ref[0:1, :]
            for e in range(N_EXPERTS):
                end = pl.multiple_of(e * region_rows + _lane_scalar(total, lane_row, e), RUN_ALIGN)
                cp = pltpu.make_async_copy(zero_ref, xs_ref.at[pl.ds(end, EXPERT_ROW_TILE)], zsem)
                cp.start()
                cp.wait()


def _router_tile(rows):
    return _row_tile(rows, (512, 640, 384, 256, 128))


def _sorted_rows(tm):
    return TOP_K * tm + pl.cdiv(N_EXPERTS * (RUN_ALIGN - 1), LANES) * LANES


def _router(mix, gain, w_router, base_counts, xs, region_rows, finalize):
    rows = mix[3].shape[0]
    tm = _router_tile(rows)
    n_tiles = rows // tm
    per_step = 2 if n_tiles % 2 == 0 else 1
    step_rows = per_step * tm
    aliased = xs is not None
    row_spec = lambda width: pl.BlockSpec((step_rows, width), lambda i: (i, 0))
    in_specs = _mix_specs(step_rows) + [_resident((1, D_MODEL)), _resident((D_MODEL, 2 * LANES)),
                                        _resident((SUBLANES, LANES))]
    args = [*mix, gain, w_router, base_counts]
    aliases = {3: 0}
    if aliased:
        aliases[len(args)] = 1
        in_specs.append(pl.BlockSpec(memory_space=pl.ANY))
        args.append(xs)
    n_pieces = _run_piece_count(tm)
    return pl.pallas_call(
        functools.partial(_router_kernel, region_rows=region_rows, finalize=finalize, aliased=aliased,
                          tiles_per_step=per_step),
        grid=(n_tiles // per_step,),
        in_specs=in_specs,
        out_specs=[row_spec(D_MODEL), pl.BlockSpec(memory_space=pl.ANY), row_spec(LANES),
                   pl.BlockSpec((per_step, SUBLANES, LANES), lambda i: (i, 0, 0)),
                   pl.BlockSpec((SUBLANES, LANES), lambda i: (0, 0))],
        out_shape=[jax.ShapeDtypeStruct((rows, D_MODEL), F32),
                   jax.ShapeDtypeStruct((N_EXPERTS * region_rows, D_MODEL), F32),
                   jax.ShapeDtypeStruct((rows, LANES), F32),
                   jax.ShapeDtypeStruct((n_tiles, SUBLANES, LANES), F32),
                   jax.ShapeDtypeStruct((SUBLANES, LANES), F32)],
        scratch_shapes=[pltpu.VMEM((per_step, _sorted_rows(tm), D_MODEL), F32),
                        pltpu.VMEM((EXPERT_ROW_TILE, D_MODEL), F32),
                        pltpu.SemaphoreType.DMA((per_step, N_EXPERTS, n_pieces)), pltpu.SemaphoreType.DMA(())],
        input_output_aliases=aliases,
        compiler_params=pltpu.CompilerParams(dimension_semantics=("arbitrary",), vmem_limit_bytes=56 * MIB),
        name="router",
    )(*args)


def _experts_kernel(blk_ref, exp_ref, nvalid_ref, x_ref, wg_ref, wu_ref, wd_ref, *y_refs):
    y_ref = y_refs[-1]
    partial_ref = y_refs[0] if len(y_refs) == 2 else None
    valid = pl.program_id(0) < nvalid_ref[0]

    @pl.when(valid)
    def _():
        x = x_ref[...].astype(BF16)
        gate = jnp.dot(x, wg_ref[...], preferred_element_type=F32)
        up = jnp.dot(x, wu_ref[...], preferred_element_type=F32)
        part = jnp.dot((_silu(gate) * up).astype(BF16), wd_ref[...], preferred_element_type=F32)
        y_ref[...] = part if partial_ref is None else partial_ref[...] + part

    @pl.when(jnp.logical_not(valid))
    def _():
        y_ref[...] = jnp.zeros(y_ref.shape, F32) if partial_ref is None else partial_ref[...]


def _experts(xs, tile_block, tile_expert, n_valid, wg, wu, wd):
    n_tiles = tile_block.shape[0]
    d_ff = wg.shape[2]
    ff_chunk = d_ff // 2
    assert ff_chunk % MXU_V7X_COLUMNS == 0
    tg = EXPERT_ROW_TILE
    y = None
    for c in range(d_ff // ff_chunk):
        in_specs = [pl.BlockSpec((tg, D_MODEL), lambda i, blk, ex, nv: (blk[i], 0)),
                    pl.BlockSpec((None, D_MODEL, ff_chunk), lambda i, blk, ex, nv, c=c: (ex[i], 0, c)),
                    pl.BlockSpec((None, D_MODEL, ff_chunk), lambda i, blk, ex, nv, c=c: (ex[i], 0, c)),
                    pl.BlockSpec((None, ff_chunk, D_MODEL), lambda i, blk, ex, nv, c=c: (ex[i], c, 0))]
        args = [tile_block, tile_expert, n_valid, xs, wg, wu, wd]
        y_spec = pl.BlockSpec((tg, D_MODEL), lambda i, blk, ex, nv: (i, 0))
        if y is not None:
            in_specs.append(y_spec)
            args.append(y)
        y = pl.pallas_call(
            _experts_kernel,
            grid_spec=pltpu.PrefetchScalarGridSpec(num_scalar_prefetch=3, grid=(n_tiles,), in_specs=in_specs,
                                                   out_specs=y_spec),
            out_shape=jax.ShapeDtypeStruct((n_tiles * tg, D_MODEL), F32),
            input_output_aliases={len(args) - 1: 0} if len(args) == 8 else {},
            compiler_params=pltpu.CompilerParams(dimension_semantics=("arbitrary",), vmem_limit_bytes=56 * MIB),
            name="experts",
        )(*args)
    return y


def _combine_kernel(tinfo_ref, yoff_ref, h_ref, route_ref, ys_ref, o_ref, sorted_ref, sems, *unpad_scratch,
                    seq_chunks):
    i = pl.program_id(0)
    n_steps = pl.num_programs(0)
    n_sorted = sorted_ref.shape[1]
    slot = i % 2

    def tile_copies(tile, buf, live):
        copies = []
        for e in range(N_EXPERTS):
            rec = (tile * 3) * N_EXPERTS + e
            base = tinfo_ref[rec + TILE_BASE * N_EXPERTS]
            n_e = jnp.where(live, tinfo_ref[rec + TILE_COUNT * N_EXPERTS], 0)
            dst0 = tinfo_ref[rec + TILE_START * N_EXPERTS]
            copies += _run_copies(n_e, yoff_ref[e] + base, dst0, ys_ref, sorted_ref.at[buf], sems.at[buf, e])
        return copies

    @pl.when(i == 0)
    def _():
        sorted_ref[...] = jnp.zeros(sorted_ref.shape, F32)
        for cond, cp in tile_copies(0, 0, True):
            pl.when(cond)(cp.start)

    for cond, cp in tile_copies(jnp.minimum(i + 1, n_steps - 1), 1 - slot, i + 1 < n_steps):
        pl.when(cond)(cp.start)
    for cond, cp in tile_copies(i, slot, True):
        pl.when(cond)(cp.wait)
    y = sorted_ref[slot].astype(BF16)
    route = route_ref[...]
    lane = lax.broadcasted_iota(I32, route.shape, 1)
    pick = lambda j: _lane_pick(route, lane, j)
    col = lax.broadcasted_iota(I32, (route.shape[0], n_sorted), 1)
    gated = (jnp.where(col == pick(ROUTE_D1).astype(I32), pick(ROUTE_G1), 0.0)
             + jnp.where(col == pick(ROUTE_D2).astype(I32), pick(ROUTE_G2), 0.0))
    result = h_ref[...] + jnp.dot(gated.astype(BF16), y, preferred_element_type=F32)
    if seq_chunks is None:
        o_ref[...] = result
        return

    res_ref, out_sems = unpad_scratch
    chunks_per_tile = h_ref.shape[0] // CHUNK

    def chunk_copies(step, buf):
        copies = []
        for j in range(chunks_per_tile):
            chunk = step * chunks_per_tile + j
            seq = chunk // seq_chunks
            in_seq = chunk - seq * seq_chunks
            dst = pl.multiple_of(seq * ((seq_chunks - 1) * CHUNK) + (in_seq - 1) * CHUNK, CHUNK)
            cp = pltpu.make_async_copy(res_ref.at[buf, pl.ds(j * CHUNK, CHUNK)], o_ref.at[pl.ds(dst, CHUNK)],
                                       out_sems.at[buf, j])
            copies.append((jnp.logical_and(step >= 0, in_seq >= 1), cp))
        return copies

    res_ref[slot] = result
    for cond, cp in chunk_copies(i, slot):
        pl.when(cond)(cp.start)
    for cond, cp in chunk_copies(i - 1, 1 - slot):
        pl.when(cond)(cp.wait)

    @pl.when(i == n_steps - 1)
    def _():
        for cond, cp in chunk_copies(i, slot):
            pl.when(cond)(cp.wait)


def _combine(h, route, tile_info, y_offsets, ys, unpad_seq_rows=None):
    rows = h.shape[0]
    tm = rows // tile_info.shape[0]
    n_pieces = _run_piece_count(tm)
    tinfo = tile_info[:, :3, :N_EXPERTS].astype(I32).reshape(-1)
    scratch = [pltpu.VMEM((2, _sorted_rows(tm), D_MODEL), F32), pltpu.SemaphoreType.DMA((2, N_EXPERTS, n_pieces))]
    if unpad_seq_rows is None:
        seq_chunks, out_rows = None, rows
        out_spec = pl.BlockSpec((tm, D_MODEL), lambda i, *_: (i, 0))
    else:
        seq_chunks = unpad_seq_rows // CHUNK
        out_rows = rows // unpad_seq_rows * (unpad_seq_rows - CHUNK)
        out_spec = pl.BlockSpec(memory_space=pl.ANY)
        scratch += [pltpu.VMEM((2, tm, D_MODEL), F32), pltpu.SemaphoreType.DMA((2, tm // CHUNK))]
    grid_spec = pltpu.PrefetchScalarGridSpec(
        num_scalar_prefetch=2,
        grid=(rows // tm,),
        in_specs=[pl.BlockSpec((tm, D_MODEL), lambda i, *_: (i, 0)),
                  pl.BlockSpec((tm, LANES), lambda i, *_: (i, 0)),
                  pl.BlockSpec(memory_space=pl.ANY)],
        out_specs=out_spec,
        scratch_shapes=scratch,
    )
    return pl.pallas_call(
        functools.partial(_combine_kernel, seq_chunks=seq_chunks),
        grid_spec=grid_spec,
        out_shape=jax.ShapeDtypeStruct((out_rows, D_MODEL), F32),
        input_output_aliases={2: 0} if unpad_seq_rows is None else {},
        compiler_params=pltpu.CompilerParams(dimension_semantics=("arbitrary",), vmem_limit_bytes=48 * MIB),
        name="combine",
    )(tinfo, y_offsets, h, route, ys)


def _routed_experts(mixes, gain, router_w, expert_w, unpad_seq_rows):
    tg = EXPERT_ROW_TILE
    group_rows = [mix[3].shape[0] for mix in mixes]
    total_rows = sum(group_rows)
    run_pad = (RUN_ALIGN - 1) * sum(rows // _router_tile(rows) for rows in group_rows)
    region_rows = (pl.cdiv(total_rows + run_pad, tg) + 1) * tg
    counts = jnp.zeros((SUBLANES, LANES), F32)
    xs, hs, routes, tiles = None, [], [], []
    for g, mix in enumerate(mixes):
        h, xs, route, tile_info, counts = _router(mix, gain, router_w, counts, xs, region_rows,
                                                  finalize=(g == len(mixes) - 1))
        hs.append(h)
        routes.append(route)
        tiles.append(tile_info)
    n_e = counts[0, :N_EXPERTS].astype(I32)
    tiles_e = (n_e + tg - 1) // tg
    first_tile = jnp.cumsum(tiles_e) - tiles_e
    n_valid = jnp.sum(tiles_e)
    n_tiles = (TOP_K * total_rows + N_EXPERTS * (run_pad + tg - 1)) // tg + 1
    t = jnp.minimum(jnp.arange(n_tiles, dtype=I32), n_valid - 1)
    tile_expert = jnp.sum((t[:, None] >= (first_tile + tiles_e)[None, :]).astype(I32), axis=1)
    tile_block = tile_expert * (region_rows // tg) + (t - first_tile[tile_expert])
    ys = _experts(xs, tile_block.astype(I32), tile_expert.astype(I32), n_valid.reshape(1).astype(I32), *expert_w)
    y_offsets = (first_tile * tg).astype(I32)
    return [_combine(h, route, tile_info, y_offsets, ys, seq_rows)
            for h, route, tile_info, seq_rows in zip(hs, routes, tiles, unpad_seq_rows)]


def _rotary_tables(batch, seq_rows):
    pos = (jnp.arange(seq_rows, dtype=I32) - PAD_FRONT).astype(F32)

    def cos_sin(n_rot, theta):
        half = n_rot // 2
        inv = theta ** (-jnp.arange(half, dtype=F32) * 2.0 / n_rot)
        ang = pos[:, None] * inv[None, :]
        return jnp.cos(ang), jnp.sin(ang)

    cos, sin = cos_sin(HEAD_DIM, RET_ROPE_THETA)
    cr = jnp.tile(cos, (1, 2 * LANES // HEAD_DIM))
    sr = jnp.tile(jnp.concatenate([-sin, sin], axis=1), (1, LANES // HEAD_DIM))
    cos, sin = cos_sin(ROPE_DIMS, ROPE_THETA)
    rest = HEAD_DIM - ROPE_DIMS
    ca = jnp.tile(jnp.concatenate([cos, cos, jnp.ones((seq_rows, rest), F32)], axis=1), (1, LANES // HEAD_DIM))
    sa = jnp.tile(jnp.concatenate([-sin, sin, jnp.zeros((seq_rows, rest), F32)], axis=1), (1, LANES // HEAD_DIM))
    return tuple(jnp.tile(t, (batch, 1)) for t in (cr, sr, ca, sa))


def _retention_params(log_gf, log_gb, gain):
    n_pairs = RET_WIDTH // LANES
    per_lane = lambda t: jnp.repeat(t.astype(F32).reshape(n_pairs, 2), HEAD_DIM, axis=1)
    per_head = lambda t, j: jnp.broadcast_to(t.astype(F32).reshape(n_pairs, 2)[:, j:j + 1], (n_pairs, LANES))
    rows = [per_lane(log_gf), per_lane(log_gb), per_head(log_gf, 0), per_head(log_gf, 1),
            per_head(log_gb, 0), per_head(log_gb, 1), gain.astype(F32).reshape(n_pairs, LANES),
            jnp.zeros((n_pairs, LANES), F32)]
    return jnp.stack(rows, axis=1)


def _token_mixer(h, lp, tabs, batch, seq_rows, embed=None):
    outs = _inproj(h, lp["norm_mix"], lp["w_in"], tabs, lp["q_gain"], lp["k_gain"], lp["ones_bd"], embed)
    if embed is not None:
        h, outs = outs[0], outs[1:]
    rq, rk, rv, rg, aq, ak, av = outs
    ret = _retention(rq, rk, rv, rg, lp["ret_dec"], batch, seq_rows)
    att = _attention(aq, ak, av, lp["sink"], lp["att_gain"], batch, seq_rows)
    return ret, att, lp["w_out"], h


def kernel(x_prompt, x_sample, meta_tokens, norm_mix, w_in, ret_log_decay_fwd, ret_log_decay_bwd, ret_out_gain,
           q_norm_gain, k_norm_gain, attn_sink, attn_out_gain, w_out, norm_ffn, ffn_w_gate, ffn_w_up, ffn_w_down,
           moe_router, moe_w_gate, moe_w_up, moe_w_down):
    depth = w_in.shape[0]
    ri = lax.broadcasted_iota(I32, (LANES, LANES), 0)
    ci = lax.broadcasted_iota(I32, (LANES, LANES), 1)
    ones_bd = jnp.where((ri < HEAD_DIM) == (ci < HEAD_DIM), 1.0 / HEAD_DIM, 0.0).astype(BF16)
    row = lambda t: t.astype(F32).reshape(1, -1)

    xs = (x_prompt, x_sample)
    shapes = [(x.shape[0], x.shape[1] + CHUNK) for x in xs]
    hs = [None] * len(xs)
    meta = meta_tokens.astype(F32)
    embeds = [(x.astype(F32).reshape(-1, D_MODEL), meta, r) for x, (_, r) in zip(xs, shapes)]
    tabs = [_rotary_tables(b, r) for b, r in shapes]

    for l in range(depth):
        lp = {
            "norm_mix": row(norm_mix[l]),
            "w_in": w_in[l].astype(BF16),
            "q_gain": jnp.tile(row(q_norm_gain[l]), (1, LANES // HEAD_DIM)),
            "k_gain": jnp.tile(row(k_norm_gain[l]), (1, LANES // HEAD_DIM)),
            "ones_bd": ones_bd,
            "ret_dec": _retention_params(ret_log_decay_fwd[l], ret_log_decay_bwd[l], ret_out_gain[l]),
            "sink": jnp.broadcast_to(attn_sink[l].astype(F32)[:, None] * LOG2_E, (N_ATT_HEADS, LANES)),
            "att_gain": row(attn_out_gain[l]),
            "w_out": w_out[l].astype(BF16),
        }
        mixes = [_token_mixer(h, lp, tab, b, r, embed if l == 0 else None)
                 for h, tab, (b, r), embed in zip(hs, tabs, shapes, embeds)]
        i = l // 2
        if l % 2 == 0:
            w = (ffn_w_gate[i].astype(BF16), ffn_w_up[i].astype(BF16), ffn_w_down[i].astype(BF16))
            hs = [_ffn(*mix, row(norm_ffn[l]), *w) for mix in mixes]
        else:
            wr = jnp.pad(moe_router[i].astype(F32), ((0, 0), (0, LANES - N_EXPERTS)))
            wr_hi = wr.astype(BF16)
            router_w = jnp.concatenate([wr_hi, (wr - wr_hi.astype(F32)).astype(BF16)], axis=1)
            expert_w = (moe_w_gate[i].astype(BF16), moe_w_up[i].astype(BF16), moe_w_down[i].astype(BF16))
            last = l == depth - 1
            hs = _routed_experts(mixes, row(norm_ffn[l]), router_w, expert_w,
                                 [r if last else None for _, r in shapes])
    if depth % 2 == 0:
        return tuple(h.reshape(x.shape).astype(x.dtype) for h, x in zip(hs, xs))
    return tuple(h.reshape(b, r, D_MODEL)[:, CHUNK:].astype(x.dtype) for h, x, (b, r) in zip(hs, xs, shapes))
```

```python
import functools

import jax
import jax.numpy as jnp
from jax import lax
from jax.experimental import pallas as pl
from jax.experimental.pallas import tpu as pltpu

F32 = jnp.float32
BF16 = jnp.bfloat16
I32 = jnp.int32

D_MODEL = 1024
HEAD_DIM = 64
N_RET_HEADS = 8
N_ATT_HEADS = 8
N_KV_HEADS = 2
RET_WIDTH = N_RET_HEADS * HEAD_DIM
ATT_WIDTH = N_ATT_HEADS * HEAD_DIM
KV_WIDTH = N_KV_HEADS * HEAD_DIM
IN_WIDTH = 4 * RET_WIDTH + ATT_WIDTH + 2 * KV_WIDTH
CHUNK = 128
LANES = 128
SUBLANES = 8
N_META = 16
PAD_FRONT = CHUNK - N_META
ROPE_THETA = 500000.0
ROPE_DIMS = HEAD_DIM // 4
RET_ROPE_THETA = 10000.0
N_EXPERTS = 8
TOP_K = 2
EXPERT_ROW_TILE = 512
MXU_V7X_COLUMNS = 256
EPS = 1e-6
NEG = -1e30
QK_SCALE = HEAD_DIM ** -0.5
LOG2_E = 1.4426950408889634
MIB = 1024 * 1024


def _row_tile(rows, prefs):
    for t in prefs:
        if rows % t == 0:
            return t
    raise ValueError(f"no row tile for {rows} rows among {prefs}")


def _resident(shape):
    return pl.BlockSpec(shape, lambda *_: (0,) * len(shape), pipeline_mode=pl.Buffered(1))


def _rms(x, gain):
    ms = jnp.mean(x * x, axis=-1, keepdims=True)
    return x * lax.rsqrt(ms + EPS) * gain


def _silu(x):
    return x * jax.nn.sigmoid(x)


def _embed_kernel(x_ref, meta_ref, *refs, seq_chunks):
    h0_ref, hbuf_ref, sems = refs[-10], refs[-2], refs[-1]
    i = pl.program_id(0)
    n_steps = pl.num_programs(0)
    tm = h0_ref.shape[0]
    chunks_per_tile = tm // CHUNK
    slot = i % 2

    def chunk_place(step, j):
        chunk = step * chunks_per_tile + j
        seq = chunk // seq_chunks
        return seq, chunk - seq * seq_chunks

    def chunk_copies(step, buf):
        copies = []
        for j in range(chunks_per_tile):
            seq, in_seq = chunk_place(step, j)
            src = pl.multiple_of(seq * ((seq_chunks - 1) * CHUNK) + (in_seq - 1) * CHUNK, CHUNK)
            cp = pltpu.make_async_copy(x_ref.at[pl.ds(src, CHUNK)], hbuf_ref.at[buf, pl.ds(j * CHUNK, CHUNK)],
                                       sems.at[buf, j])
            copies.append((jnp.logical_and(step < n_steps, in_seq >= 1), cp))
        return copies

    @pl.when(i == 0)
    def _():
        for cond, cp in chunk_copies(0, 0):
            pl.when(cond)(cp.start)

    for cond, cp in chunk_copies(i + 1, 1 - slot):
        pl.when(cond)(cp.start)
    for cond, cp in chunk_copies(i, slot):
        pl.when(cond)(cp.wait)
    lead = jnp.concatenate([jnp.zeros((PAD_FRONT, D_MODEL), F32), meta_ref[...]], axis=0)
    for j in range(chunks_per_tile):
        @pl.when(chunk_place(i, j)[1] == 0)
        def _():
            hbuf_ref[slot, j * CHUNK:(j + 1) * CHUNK, :] = lead

    h0_ref[...] = hbuf_ref[slot]
    _inproj_kernel(hbuf_ref.at[slot], *refs[:-10], *refs[-9:-2])


def _inproj_kernel(h_ref, gain_ref, w_ref, cr_ref, sr_ref, ca_ref, sa_ref, qg_ref, kg_ref, ones_ref,
                   rq_ref, rk_ref, rv_ref, rg_ref, aq_ref, ak_ref, av_ref):
    tm = h_ref.shape[0]
    part = tm
    lane = lax.broadcasted_iota(I32, (part, LANES), 1)
    in_head = lane & (HEAD_DIM - 1)
    low_head = lane < HEAD_DIM
    cols = lambda j: slice(j * LANES, (j + 1) * LANES)
    n_lane_tiles = RET_WIDTH // LANES
    base = 4 * RET_WIDTH

    def project(r):
        a = _rms(h_ref[r, :], gain_ref[...]).astype(BF16)
        proj = lambda c0, c1: jnp.dot(a, w_ref[:, c0:c1], preferred_element_type=F32)
        return dict(aq=proj(base, base + ATT_WIDTH), kv=proj(base + ATT_WIDTH, IN_WIDTH),
                    rq=proj(0, RET_WIDTH), rk=proj(RET_WIDTH, 2 * RET_WIDTH),
                    rv=proj(2 * RET_WIDTH, 3 * RET_WIDTH), rg=proj(3 * RET_WIDTH, 4 * RET_WIDTH))

    def rotate(x, cos, sin_signed, half):
        partner = jnp.where(in_head < half, pltpu.roll(x, LANES - half, 1), pltpu.roll(x, half, 1))
        return x * cos + partner * sin_signed

    def head_norm(x, g):
        ms = jnp.dot((x * x).astype(BF16), ones_ref[...], preferred_element_type=F32)
        return x * lax.rsqrt(ms + EPS) * g

    def epilogue(r, first_chunk, p):
        cr, sr, ca, sa = cr_ref[r, :], sr_ref[r, :], ca_ref[r, :], sa_ref[r, :]

        def store_keys(kt_ref, j, x):
            for c in range(part // CHUNK):
                kt_ref[first_chunk + c, cols(j), :] = x[c * CHUNK:(c + 1) * CHUNK, :].T.astype(BF16)

        rv_ref[r, :] = p["rv"].astype(BF16)
        rg_ref[r, :] = p["rg"].astype(BF16)
        for j in range(n_lane_tiles):
            rq_ref[r, cols(j)] = rotate(p["rq"][:, cols(j)], cr, sr, HEAD_DIM // 2).astype(BF16)
        for j in range(n_lane_tiles):
            store_keys(rk_ref, j, rotate(p["rk"][:, cols(j)], cr, sr, HEAD_DIM // 2) * QK_SCALE)
        normed = [head_norm(p["aq"][:, cols(j)], qg_ref[...]) for j in range(ATT_WIDTH // LANES)]
        normed_k = head_norm(p["kv"][:, :KV_WIDTH], kg_ref[...])
        for j, n in enumerate(normed):
            aq_ref[r, cols(j)] = (rotate(n, ca, sa, ROPE_DIMS // 2) * (QK_SCALE * LOG2_E)).astype(BF16)
        k = rotate(normed_k, ca, sa, ROPE_DIMS // 2)
        v = p["kv"][:, KV_WIDTH:]
        k_sw = pltpu.roll(k, HEAD_DIM, 1)
        v_sw = pltpu.roll(v, HEAD_DIM, 1)
        store_keys(ak_ref, 0, jnp.where(low_head, k, k_sw))
        store_keys(ak_ref, 1, jnp.where(low_head, k_sw, k))
        av_ref[r, 0:LANES] = jnp.where(low_head, v, v_sw).astype(BF16)
        av_ref[r, LANES:2 * LANES] = jnp.where(low_head, v_sw, v).astype(BF16)

    whole_tile = slice(0, tm)
    epilogue(whole_tile, 0, project(whole_tile))


def _inproj(h, gain, w, tabs, qg, kg, ones_bd, embed=None):
    rows = tabs[0].shape[0]
    tm = _row_tile(rows, (1024, 640, 512, 384, 256, 128) if embed is None else (768, 640, 512, 384, 256, 128))
    row_spec = lambda width: pl.BlockSpec((tm, width), lambda i: (i, 0))
    outs = ((RET_WIDTH, False), (RET_WIDTH, True), (RET_WIDTH, False), (RET_WIDTH, False),
            (ATT_WIDTH, False), (2 * KV_WIDTH, True), (2 * KV_WIDTH, False))
    chunk_spec = lambda width: pl.BlockSpec((tm // CHUNK, width, CHUNK), lambda i: (i, 0, 0))
    in_specs = [_resident((1, D_MODEL)), _resident((D_MODEL, IN_WIDTH)),
                row_spec(LANES), row_spec(LANES), row_spec(LANES), row_spec(LANES),
                _resident((1, LANES)), _resident((1, LANES)), _resident((LANES, LANES))]
    out_specs = [chunk_spec(wd) if t else row_spec(wd) for wd, t in outs]
    out_shape = [jax.ShapeDtypeStruct((rows // CHUNK, wd, CHUNK) if t else (rows, wd), BF16) for wd, t in outs]
    if embed is None:
        return pl.pallas_call(
            _inproj_kernel,
            grid=(rows // tm,),
            in_specs=[row_spec(D_MODEL)] + in_specs,
            out_specs=out_specs,
            out_shape=out_shape,
            compiler_params=pltpu.CompilerParams(dimension_semantics=("parallel",), vmem_limit_bytes=56 * MIB),
            name="inproj",
        )(h, gain, w, *tabs, qg, kg, ones_bd)
    x, meta, seq_rows = embed
    return pl.pallas_call(
        functools.partial(_embed_kernel, seq_chunks=seq_rows // CHUNK),
        grid=(rows // tm,),
        in_specs=[pl.BlockSpec(memory_space=pl.ANY), _resident((N_META, D_MODEL))] + in_specs,
        out_specs=[row_spec(D_MODEL)] + out_specs,
        out_shape=[jax.ShapeDtypeStruct((rows, D_MODEL), F32)] + out_shape,
        scratch_shapes=[pltpu.VMEM((2, tm, D_MODEL), F32), pltpu.SemaphoreType.DMA((2, tm // CHUNK))],
        compiler_params=pltpu.CompilerParams(dimension_semantics=("arbitrary",), vmem_limit_bytes=56 * MIB),
        name="embed",
    )(x, meta, gain, w, *tabs, qg, kg, ones_bd)


def _retention_kernel(q_ref, k_ref, v_ref, g_ref, dec_ref, o_ref, sb_ref, *, n_chunks):
    lgf, lgb = dec_ref[0:1, :], dec_ref[1:2, :]
    gain = dec_ref[6:7, :]
    ri = lax.broadcasted_iota(I32, (CHUNK, LANES), 0)
    ci = lax.broadcasted_iota(I32, (CHUNK, LANES), 1)
    r = ri.astype(F32)
    diff = (ri - ci).astype(F32)

    def decay_mask(lf, lb):
        return jnp.where(diff >= 0, jnp.exp(jnp.maximum(diff, 0.0) * lf), jnp.exp(jnp.maximum(-diff, 0.0) * lb))

    dm = jnp.concatenate([decay_mask(dec_ref[2:3, :], dec_ref[4:5, :]),
                          decay_mask(dec_ref[3:4, :], dec_ref[5:6, :])], axis=1)
    tok = ci.astype(F32)
    wf = jnp.exp((CHUNK - 1.0 - tok) * jnp.broadcast_to(lgf, (LANES, LANES)).T)
    wb = jnp.exp(tok * jnp.broadcast_to(lgb, (LANES, LANES)).T)
    qf = jnp.exp((r + 1.0) * lgf)
    qb = jnp.exp((CHUNK - r) * lgb)
    cf = jnp.exp(float(CHUNK) * lgf)
    cb = jnp.exp(float(CHUNK) * lgb)
    low_c = ci < HEAD_DIM
    same_head = (ri < HEAD_DIM) == low_c
    mean_bd = jnp.where(same_head, 1.0 / HEAD_DIM, 0.0).astype(BF16)
    m0 = jnp.where(low_c, 1.0, 0.0).astype(BF16)
    m1 = jnp.where(low_c, 0.0, 1.0).astype(BF16)
    low_r = ri < HEAD_DIM
    rm0 = jnp.where(low_r, 1.0, 0.0).astype(BF16)
    rm1 = jnp.where(low_r, 0.0, 1.0).astype(BF16)

    def chunk_rows(n):
        return pl.ds(pl.multiple_of(n * CHUNK, CHUNK), CHUNK)

    def pair_rows(x):
        return jnp.concatenate([x * m0, x * m1], axis=0)

    def pair_cols(xt):
        return jnp.concatenate([xt * rm0, xt * rm1], axis=1)

    def state_delta(kt, weights, v):
        u = jnp.dot((kt.astype(F32) * weights).astype(BF16), v, preferred_element_type=F32)
        return jnp.where(same_head, u, 0.0)

    group = next(u for u in (11, 13, 4, 3, 5, 2, 1) if n_chunks % u == 0)
    n_groups = n_chunks // group
    members = range(group)
    zero_state = jnp.zeros((LANES, LANES), F32)

    def backward(i, state):
        ns = [n_chunks - 1 - i * group - j for j in members]
        deltas = [state_delta(k_ref[n], wb, v_ref[chunk_rows(n), :]) for n in ns]
        for n, delta in zip(ns, deltas):
            sb_ref[n] = state.astype(BF16)
            state = cb * state + delta
        return state

    lax.fori_loop(0, n_groups, backward, zero_state)

    def forward(i, state):
        ns = [i * group + j for j in members]
        q = [q_ref[chunk_rows(n), :] for n in ns]
        kt = [k_ref[n] for n in ns]
        v = [v_ref[chunk_rows(n), :] for n in ns]
        deltas = [state_delta(kt[j], wf, v[j]) for j in members]
        s = [jnp.dot(q[j], pair_cols(kt[j]), preferred_element_type=F32) * dm for j in members]
        both = []
        for j in members:
            both.append(jnp.concatenate([state.astype(BF16), sb_ref[ns[j]]], axis=1))
            state = cf * state + deltas[j]
        intra = [jnp.dot(s[j].astype(BF16), pair_rows(v[j]), preferred_element_type=F32) for j in members]
        inter = [jnp.dot(q[j], both[j], preferred_element_type=F32) for j in members]
        o = [intra[j] + inter[j][:, :LANES] * qf + inter[j][:, LANES:] * qb for j in members]
        ms = [jnp.dot((o[j] * o[j]).astype(BF16), mean_bd, preferred_element_type=F32) for j in members]
        for j in members:
            gate = _silu(g_ref[chunk_rows(ns[j]), :].astype(F32))
            o_ref[chunk_rows(ns[j]), :] = (o[j] * lax.rsqrt(ms[j] + EPS) * gain * gate).astype(BF16)
        return state

    lax.fori_loop(0, n_groups, forward, zero_state)


def _retention(rq, rk, rv, rg, dec, batch, seq_rows):
    n_chunks = seq_rows // CHUNK
    n_pairs = RET_WIDTH // LANES
    view = lambda t: t.reshape(batch, seq_rows, RET_WIDTH)
    seq_spec = pl.BlockSpec((None, seq_rows, LANES), lambda b, p: (b, 0, p))
    key_spec = pl.BlockSpec((n_chunks, LANES, CHUNK), lambda b, p: (b, p, 0))
    out = pl.pallas_call(
        functools.partial(_retention_kernel, n_chunks=n_chunks),
        grid=(batch, n_pairs),
        in_specs=[seq_spec, key_spec, seq_spec, seq_spec,
                  pl.BlockSpec((None, 8, LANES), lambda b, p: (p, 0, 0))],
        out_specs=seq_spec,
        out_shape=jax.ShapeDtypeStruct((batch, seq_rows, RET_WIDTH), BF16),
        scratch_shapes=[pltpu.VMEM((n_chunks, LANES, LANES), BF16)],
        compiler_params=pltpu.CompilerParams(dimension_semantics=("parallel", "parallel"),
                                             vmem_limit_bytes=48 * MIB),
        name="retention",
    )(view(rq), rk, view(rv), view(rg), dec)
    return out.reshape(batch * seq_rows, RET_WIDTH)


def _attention_kernel(q_ref, kp_ref, kc_ref, kn_ref, km_ref, vp_ref, vc_ref, vn_ref, vm_ref,
                      sink_ref, gain_ref, o_ref, *, n_chunks, chunks_per_step):
    first_chunk = pl.program_id(1) * chunks_per_step
    ri = lax.broadcasted_iota(I32, (CHUNK, LANES), 0)
    ci = lax.broadcasted_iota(I32, (CHUNK, LANES), 1)
    never = 2 * CHUNK
    to_bias = lambda ok: jnp.where(ok, 0.0, NEG).astype(BF16)
    meta_bias = to_bias(ci >= PAD_FRONT)
    n_keys = 4 * CHUNK
    low_c = ci < HEAD_DIM
    m0 = jnp.where(low_c, 1.0, 0.0).astype(BF16)
    m1 = jnp.where(low_c, 0.0, 1.0).astype(BF16)
    low_r = ri < HEAD_DIM
    rm0 = jnp.where(low_r, 1.0, 0.0).astype(BF16)
    rm1 = jnp.where(low_r, 0.0, 1.0).astype(BF16)
    eye = jnp.where(ri == ci, 1.0, 0.0).astype(BF16)
    row_sums = jnp.concatenate([m0] * 4 + [m1] * 4, axis=0)
    pairs_per_kv = N_ATT_HEADS // N_KV_HEADS // 2
    rows = lambda j: slice(j * CHUNK, (j + 1) * CHUNK)

    def split_values(ref, blk, g):
        x = ref[blk, g * LANES:(g + 1) * LANES]
        return x * m0, x * m1

    def split_keys(ref, idx, g):
        xt = ref[idx, g * LANES:(g + 1) * LANES, :]
        return xt * rm0, xt * rm1

    def blocks(split, at, prev_ref, main_ref, next_ref, meta_ref, g):
        out = [split(prev_ref, at(0), g)]
        out += [split(main_ref, at(j), g) for j in range(chunks_per_step)]
        out += [split(next_ref, at(0), g), split(meta_ref, at(0), g)]
        return out

    k_blocks = [blocks(split_keys, lambda j: j, kp_ref, kc_ref, kn_ref, km_ref, g) for g in range(N_KV_HEADS)]
    v_blocks = [blocks(split_values, rows, vp_ref, vc_ref, vn_ref, vm_ref, g) for g in range(N_KV_HEADS)]

    def pair_blocks(blks, j, axis):
        use = [blks[j], blks[j + 1], blks[j + 2], blks[-1]]
        return jnp.concatenate([b[0] for b in use] + [b[1] for b in use], axis=axis)

    def pair_ids(g):
        return [g * pairs_per_kv + pp for pp in range(pairs_per_kv)]

    def scores(j, g):
        c = first_chunk + j
        prev_off = jnp.where(c >= 2, 0, never)
        cur_off = jnp.where(c >= 1, 0, never)
        next_off = jnp.where(c + 1 <= n_chunks - 1, 0, never)
        bias = jnp.concatenate([to_bias(ci >= ri + prev_off), to_bias(ci >= cur_off),
                                to_bias(ci + next_off <= ri), meta_bias] * 2, axis=1)
        k_ext = jnp.concatenate([pair_blocks(k_blocks[g], j, 1), bias], axis=0)
        q_ext = jnp.concatenate(
            [jnp.concatenate([q_ref[rows(j), p * LANES:(p + 1) * LANES], eye], axis=1) for p in pair_ids(g)], axis=0)
        return jnp.dot(q_ext, k_ext, preferred_element_type=F32)

    def softmax_parts(g, s):
        probs, sink_rows = [], []
        for pp, p in enumerate(pair_ids(g)):
            halves, sink_terms = [], []
            for hh in range(2):
                sh = s[rows(pp), hh * n_keys:(hh + 1) * n_keys]
                sk = sink_ref[2 * p + hh:2 * p + hh + 1, 0:1]
                m = jnp.maximum(jnp.max(sh, axis=-1, keepdims=True), sk)
                halves.append(jnp.exp2(sh - m).astype(BF16))
                sink_terms.append(jnp.exp2(sk - m))
            probs.append(jnp.concatenate(halves, axis=1))
            sink_rows.append(jnp.where(low_c, sink_terms[0], sink_terms[1]))
        return jnp.concatenate(probs, axis=0), sink_rows

    def weighted_values(j, g, probs, sink_rows):
        v_ext = jnp.concatenate([pair_blocks(v_blocks[g], j, 0), row_sums], axis=1)
        ol = jnp.dot(probs, v_ext, preferred_element_type=F32)
        return [ol[rows(pp), :LANES] / (ol[rows(pp), LANES:] + sink_rows[pp]) for pp in range(pairs_per_kv)]

    chunk_group = 4
    for j0 in range(0, chunks_per_step, chunk_group):
        units = [(j, g) for j in range(j0, min(j0 + chunk_group, chunks_per_step)) for g in range(N_KV_HEADS)]
        s = [scores(j, g) for j, g in units]
        parts = [softmax_parts(g, s[u]) for u, (j, g) in enumerate(units)]
        outs = [weighted_values(j, g, *parts[u]) for u, (j, g) in enumerate(units)]
        for j in sorted({j for j, _ in units}):
            att = jnp.concatenate([o for u, (ju, _) in enumerate(units) if ju == j for o in outs[u]], axis=1)
            row = lax.broadcasted_iota(I32, att.shape, 0)
            pad_rows = jnp.where(first_chunk + j == 0, PAD_FRONT, 0)
            att = jnp.where(row >= pad_rows, att, 0.0)
            o_ref[rows(j), :] = _rms(att, gain_ref[...]).astype(BF16)


def _attention(aq, ak, av, sink, gain, batch, seq_rows):
    n_chunks = seq_rows // CHUNK
    per_step = next(r for r in (11, 13, 3, 5, 4, 2, 1) if n_chunks % r == 0)
    n_steps = n_chunks // per_step
    q3 = aq.reshape(batch, seq_rows, ATT_WIDTH)
    v3 = av.reshape(batch, seq_rows, 2 * KV_WIDTH)
    prev_chunk = lambda s: jnp.maximum(s * per_step - 1, 0)
    next_chunk = lambda s: jnp.minimum((s + 1) * per_step, n_chunks - 1)
    v_one = lambda fn: pl.BlockSpec((None, CHUNK, 2 * KV_WIDTH), lambda b, s: (b, fn(s), 0))
    v_specs = [v_one(prev_chunk), pl.BlockSpec((None, per_step * CHUNK, 2 * KV_WIDTH), lambda b, s: (b, s, 0)),
               v_one(next_chunk), v_one(lambda s: 0)]
    k_one = lambda fn: pl.BlockSpec((1, 2 * KV_WIDTH, CHUNK), lambda b, s: (b * n_chunks + fn(s), 0, 0))
    k_specs = [k_one(prev_chunk),
               pl.BlockSpec((per_step, 2 * KV_WIDTH, CHUNK), lambda b, s: (b * n_steps + s, 0, 0)),
               k_one(next_chunk), k_one(lambda s: 0)]
    q_spec = pl.BlockSpec((None, per_step * CHUNK, ATT_WIDTH), lambda b, s: (b, s, 0))
    out = pl.pallas_call(
        functools.partial(_attention_kernel, n_chunks=n_chunks, chunks_per_step=per_step),
        grid=(batch, n_steps),
        in_specs=[q_spec] + k_specs + v_specs + [_resident((N_ATT_HEADS, LANES)), _resident((1, ATT_WIDTH))],
        out_specs=q_spec,
        out_shape=jax.ShapeDtypeStruct((batch, seq_rows, ATT_WIDTH), BF16),
        compiler_params=pltpu.CompilerParams(dimension_semantics=("parallel", "parallel"),
                                             vmem_limit_bytes=32 * MIB),
        name="attention",
    )(q3, ak, ak, ak, ak, v3, v3, v3, v3, sink, gain)
    return out.reshape(batch * seq_rows, ATT_WIDTH)


def _mixed_residual(ret_ref, att_ref, w_ref, h_ref, rows=slice(None)):
    acc = jnp.dot(ret_ref[rows, :], w_ref[0:RET_WIDTH, :], preferred_element_type=F32)
    acc = acc + jnp.dot(att_ref[rows, :], w_ref[RET_WIDTH:, :], preferred_element_type=F32)
    return h_ref[rows, :] + acc


def _mix_specs(tm):
    row_spec = lambda width: pl.BlockSpec((tm, width), lambda i, *_: (i, 0))
    return [row_spec(RET_WIDTH), row_spec(ATT_WIDTH), _resident((RET_WIDTH + ATT_WIDTH, D_MODEL)),
            row_spec(D_MODEL)]


def _ff_halves(d_ff):
    n_mxu_tiles = pl.cdiv(d_ff, MXU_V7X_COLUMNS)
    first = min(d_ff, pl.cdiv(n_mxu_tiles, 2) * MXU_V7X_COLUMNS)
    return tuple((c0, c1) for c0, c1 in ((0, first), (first, d_ff)) if c1 > c0)


def _ffn_kernel(ret_ref, att_ref, wo_ref, h_ref, gain_ref, wg_ref, wu_ref, wd_ref, o_ref):
    x = _mixed_residual(ret_ref, att_ref, wo_ref, h_ref)
    f = _rms(x, gain_ref[...]).astype(BF16)
    acc = x
    for c0, c1 in _ff_halves(wg_ref.shape[1]):
        gate = jnp.dot(f, wg_ref[:, c0:c1], preferred_element_type=F32)
        up = jnp.dot(f, wu_ref[:, c0:c1], preferred_element_type=F32)
        act = (_silu(gate) * up).astype(BF16)
        acc = acc + jnp.dot(act, wd_ref[c0:c1, :], preferred_element_type=F32)
    o_ref[...] = acc


def _ffn(ret, att, w_out, h, gain, wg, wu, wd):
    rows = h.shape[0]
    d_ff = wg.shape[1]
    tm = _row_tile(rows, (768, 640, 512, 384, 256, 128))
    return pl.pallas_call(
        _ffn_kernel,
        grid=(rows // tm,),
        in_specs=_mix_specs(tm) + [_resident((1, D_MODEL)), _resident((D_MODEL, d_ff)), _resident((D_MODEL, d_ff)),
                                   _resident((d_ff, D_MODEL))],
        out_specs=pl.BlockSpec((tm, D_MODEL), lambda i: (i, 0)),
        out_shape=jax.ShapeDtypeStruct((rows, D_MODEL), F32),
        input_output_aliases={3: 0},
        compiler_params=pltpu.CompilerParams(dimension_semantics=("parallel",), vmem_limit_bytes=56 * MIB),
        name="ffn",
    )(ret, att, w_out, h, gain, wg, wu, wd)


ROUTE_G1, ROUTE_G2, ROUTE_D1, ROUTE_D2 = range(4)
TILE_BASE, TILE_COUNT, TILE_START = range(3)


def _lane_pick(x, lane, j):
    return jnp.sum(jnp.where(lane == j, x, 0.0), axis=-1, keepdims=True)


def _lane_scalar(vec, lane_row, e):
    return jnp.sum(jnp.where(lane_row == e, vec, 0.0)).astype(I32)


RUN_ALIGN_BITS = 3
RUN_ALIGN = 1 << RUN_ALIGN_BITS


def _run_piece_count(tm):
    return tm.bit_length() - RUN_ALIGN_BITS


def _run_copies(count, src0, dst0, src_ref, dst_ref, sems):
    copies = []
    for j in range(sems.shape[0]):
        k = j + RUN_ALIGN_BITS
        size = 1 << k
        start = (count >> (k + 1)) << (k + 1)
        src = src_ref.at[pl.ds(pl.multiple_of(src0 + start, RUN_ALIGN), size)]
        dst = dst_ref.at[pl.ds(pl.multiple_of(dst0 + start, RUN_ALIGN), size)]
        copies.append((((count >> k) & 1) == 1, pltpu.make_async_copy(src, dst, sems.at[j])))
    return copies


def _router_kernel(*refs, region_rows, finalize, aliased, tiles_per_step):
    ret_ref, att_ref, wo_ref, h_ref, gain_ref, wr_ref, base_ref = refs[:7]
    refs = refs[7 + (1 if aliased else 0):]
    hout_ref, xs_ref, route_ref, tile_ref, cnt_ref, sorted_ref, zero_ref, sems, zsem, prev_ref = refs
    i = pl.program_id(0)
    tm = h_ref.shape[0] // tiles_per_step
    n_sorted = sorted_ref.shape[1]

    @pl.when(i == 0)
    def _():
        cnt_ref[...] = base_ref[...]

    lane = lax.broadcasted_iota(I32, (tm, LANES), 1)
    lane_row = lane[0:1, :]
    ri = lax.broadcasted_iota(I32, (tm, tm), 0)
    ci = lax.broadcasted_iota(I32, (tm, tm), 1)
    earlier_tokens = jnp.where(ri > ci, 1.0, 0.0).astype(BF16)
    li = lax.broadcasted_iota(I32, (LANES, LANES), 0)
    lj = lax.broadcasted_iota(I32, (LANES, LANES), 1)
    earlier_experts = jnp.where(li < lj, 1.0, 0.0).astype(BF16)
    slot = lax.broadcasted_iota(I32, (n_sorted, tm), 0)
    sub = lax.broadcasted_iota(I32, (SUBLANES, LANES), 0)

    def as_row(col):
        wide = jnp.broadcast_to(col, (tm, LANES))
        return jnp.concatenate([wide[c * LANES:(c + 1) * LANES, :].T for c in range(tm // LANES)], axis=1)[0:1, :]

    recs = [dict(t=t, rows=slice(t * tm, (t + 1) * tm)) for t in range(tiles_per_step)]

    def stage(fn):
        for rec in recs:
            rec.update(fn(rec))

    def mix(r):
        h_mixed = _mixed_residual(ret_ref, att_ref, wo_ref, h_ref, r["rows"])
        hout_ref[r["rows"], :] = h_mixed
        return dict(f=_rms(h_mixed, gain_ref[...]))

    def logits(r):
        f_hi = r["f"].astype(BF16)
        f_lo = (r["f"] - f_hi.astype(F32)).astype(BF16)
        both = (jnp.dot(f_hi, wr_ref[...], preferred_element_type=F32)
                + jnp.dot(f_lo, wr_ref[...], preferred_element_type=F32))
        lg = jnp.where(lane < N_EXPERTS, both[:, :LANES] + both[:, LANES:], -jnp.inf)
        return dict(f_hi=f_hi, lg=lg)

    def first_choice(r):
        m1 = jnp.max(r["lg"], axis=-1, keepdims=True)
        return dict(m1=m1, i1=jnp.min(jnp.where(r["lg"] == m1, lane, LANES), axis=-1, keepdims=True))

    def second_choice(r):
        lg2 = jnp.where(lane == r["i1"], -jnp.inf, r["lg"])
        m2 = jnp.max(lg2, axis=-1, keepdims=True)
        return dict(m2=m2, i2=jnp.min(jnp.where(lg2 == m2, lane, LANES), axis=-1, keepdims=True))

    def rank_tokens(r):
        active = jnp.max(jnp.abs(r["f"]), axis=-1, keepdims=True) > 0.0
        sel = jnp.where(active, jnp.where(lane == r["i1"], 1.0, 0.0) + jnp.where(lane == r["i2"], 1.0, 0.0), 0.0)
        rank = jnp.dot(earlier_tokens, sel.astype(BF16), preferred_element_type=F32)
        tiles_per_run = jnp.floor((jnp.sum(sel, axis=0, keepdims=True) + (RUN_ALIGN - 1.0)) * (1.0 / RUN_ALIGN))
        start = RUN_ALIGN * jnp.dot(jnp.broadcast_to(tiles_per_run, (SUBLANES, LANES)).astype(BF16),
                                    earlier_experts, preferred_element_type=F32)[0:1, :]
        return dict(active=active, rank=rank, count=tiles_per_run * RUN_ALIGN, start=start)

    def destinations(r):
        dest = r["start"] + r["rank"]
        no_slot = -1.0
        d1 = jnp.where(r["active"], _lane_pick(dest, lane, r["i1"]), no_slot)
        d2 = jnp.where(r["active"], _lane_pick(dest, lane, r["i2"]), no_slot)
        e2 = jnp.exp(r["m2"] - r["m1"])
        g1 = 1.0 / (1.0 + e2)
        route = jnp.zeros((tm, LANES), F32)
        for j, val in ((ROUTE_G1, g1), (ROUTE_G2, e2 * g1), (ROUTE_D1, d1), (ROUTE_D2, d2)):
            route = jnp.where(lane == j, val, route)
        route_ref[r["rows"], :] = route
        return dict(d1=d1, d2=d2)

    def sort_rows(r):
        onehot_t = (jnp.where(slot == as_row(r["d1"]).astype(I32), 1.0, 0.0)
                    + jnp.where(slot == as_row(r["d2"]).astype(I32), 1.0, 0.0))
        sorted_ref[r["t"]] = jnp.dot(onehot_t.astype(BF16), r["f_hi"], preferred_element_type=F32)
        return {}

    def run_copies(records, live):
        copies = []
        for t, rec in enumerate(records):
            for e in range(N_EXPERTS):
                n_e = jnp.where(live, _lane_scalar(rec[TILE_COUNT:TILE_COUNT + 1, :], lane_row, e), 0)
                src0 = _lane_scalar(rec[TILE_START:TILE_START + 1, :], lane_row, e)
                dst0 = e * region_rows + _lane_scalar(rec[TILE_BASE:TILE_BASE + 1, :], lane_row, e)
                copies += _run_copies(n_e, src0, dst0, sorted_ref.at[t], xs_ref, sems.at[t, e])
        return copies

    @pl.when(i == 0)
    def _():
        prev_ref[...] = jnp.zeros(prev_ref.shape, F32)

    for fn in (mix, logits, first_choice, second_choice, rank_tokens, destinations):
        stage(fn)
    for cond, cp in run_copies([prev_ref[t] for t in range(tiles_per_step)], i > 0):
        pl.when(cond)(cp.wait)
    stage(sort_rows)
    base = cnt_ref[0:1, :]
    for rec in recs:
        record = jnp.where(sub == TILE_BASE, base, jnp.where(
            sub == TILE_COUNT, rec["count"], jnp.where(sub == TILE_START, rec["start"], 0.0)))
        tile_ref[rec["t"]] = record
        prev_ref[rec["t"]] = record
        rec["record"] = record
        base = base + rec["count"]
    copies = run_copies([rec["record"] for rec in recs], True)
    for cond, cp in copies:
        pl.when(cond)(cp.start)
    cnt_ref[...] = jnp.broadcast_to(base, cnt_ref.shape)

    @pl.when(i == pl.num_programs(0) - 1)
    def _():
        for cond, cp in copies:
            pl.when(cond)(cp.wait)

    if finalize:
        @pl.when(i == pl.num_programs(0) - 1)
        def _():
            zero_ref[...] = jnp.zeros(zero_ref.shape, F32)
            total = cnt_ref[0:1, :]
            for e in range(N_EXPERTS):
                end = pl.multiple_of(e * region_rows + _lane_scalar(total, lane_row, e), RUN_ALIGN)
                cp = pltpu.make_async_copy(zero_ref, xs_ref.at[pl.ds(end, EXPERT_ROW_TILE)], zsem)
                cp.start()
                cp.wait()


def _router_tile(rows):
    return _row_tile(rows, (512, 640, 384, 256, 128))


def _sorted_rows(tm):
    return TOP_K * tm + pl.cdiv(N_EXPERTS * (RUN_ALIGN - 1), LANES) * LANES


def _router(mix, gain, w_router, base_counts, xs, region_rows, finalize):
    rows = mix[3].shape[0]
    tm = _router_tile(rows)
    n_tiles = rows // tm
    per_step = 2 if n_tiles % 2 == 0 else 1
    step_rows = per_step * tm
    aliased = xs is not None
    row_spec = lambda width: pl.BlockSpec((step_rows, width), lambda i: (i, 0))
    in_specs = _mix_specs(step_rows) + [_resident((1, D_MODEL)), _resident((D_MODEL, 2 * LANES)),
                                        _resident((SUBLANES, LANES))]
    args = [*mix, gain, w_router, base_counts]
    aliases = {3: 0}
    if aliased:
        aliases[len(args)] = 1
        in_specs.append(pl.BlockSpec(memory_space=pl.ANY))
        args.append(xs)
    n_pieces = _run_piece_count(tm)
    return pl.pallas_call(
        functools.partial(_router_kernel, region_rows=region_rows, finalize=finalize, aliased=aliased,
                          tiles_per_step=per_step),
        grid=(n_tiles // per_step,),
        in_specs=in_specs,
        out_specs=[row_spec(D_MODEL), pl.BlockSpec(memory_space=pl.ANY), row_spec(LANES),
                   pl.BlockSpec((per_step, SUBLANES, LANES), lambda i: (i, 0, 0)),
                   pl.BlockSpec((SUBLANES, LANES), lambda i: (0, 0))],
        out_shape=[jax.ShapeDtypeStruct((rows, D_MODEL), F32),
                   jax.ShapeDtypeStruct((N_EXPERTS * region_rows, D_MODEL), F32),
                   jax.ShapeDtypeStruct((rows, LANES), F32),
                   jax.ShapeDtypeStruct((n_tiles, SUBLANES, LANES), F32),
                   jax.ShapeDtypeStruct((SUBLANES, LANES), F32)],
        scratch_shapes=[pltpu.VMEM((per_step, _sorted_rows(tm), D_MODEL), F32),
                        pltpu.VMEM((EXPERT_ROW_TILE, D_MODEL), F32),
                        pltpu.SemaphoreType.DMA((per_step, N_EXPERTS, n_pieces)), pltpu.SemaphoreType.DMA(()),
                        pltpu.VMEM((per_step, SUBLANES, LANES), F32)],
        input_output_aliases=aliases,
        compiler_params=pltpu.CompilerParams(dimension_semantics=("arbitrary",), vmem_limit_bytes=56 * MIB),
        name="router",
    )(*args)


def _experts_kernel(blk_ref, exp_ref, nvalid_ref, x_ref, wg_ref, wu_ref, wd_ref, *y_refs):
    y_ref = y_refs[-1]
    partial_ref = y_refs[0] if len(y_refs) == 2 else None
    valid = pl.program_id(0) < nvalid_ref[0]

    @pl.when(valid)
    def _():
        x = x_ref[...].astype(BF16)
        gate = jnp.dot(x, wg_ref[...], preferred_element_type=F32)
        up = jnp.dot(x, wu_ref[...], preferred_element_type=F32)
        part = jnp.dot((_silu(gate) * up).astype(BF16), wd_ref[...], preferred_element_type=F32)
        y_ref[...] = part if partial_ref is None else partial_ref[...] + part

    @pl.when(jnp.logical_not(valid))
    def _():
        y_ref[...] = jnp.zeros(y_ref.shape, F32) if partial_ref is None else partial_ref[...]


def _experts(xs, tile_block, tile_expert, n_valid, wg, wu, wd):
    n_tiles = tile_block.shape[0]
    d_ff = wg.shape[2]
    ff_chunk = d_ff // 2
    assert ff_chunk % MXU_V7X_COLUMNS == 0
    tg = EXPERT_ROW_TILE
    y = None
    for c in range(d_ff // ff_chunk):
        in_specs = [pl.BlockSpec((tg, D_MODEL), lambda i, blk, ex, nv: (blk[i], 0)),
                    pl.BlockSpec((None, D_MODEL, ff_chunk), lambda i, blk, ex, nv, c=c: (ex[i], 0, c)),
                    pl.BlockSpec((None, D_MODEL, ff_chunk), lambda i, blk, ex, nv, c=c: (ex[i], 0, c)),
                    pl.BlockSpec((None, ff_chunk, D_MODEL), lambda i, blk, ex, nv, c=c: (ex[i], c, 0))]
        args = [tile_block, tile_expert, n_valid, xs, wg, wu, wd]
        y_spec = pl.BlockSpec((tg, D_MODEL), lambda i, blk, ex, nv: (i, 0))
        if y is not None:
            in_specs.append(y_spec)
            args.append(y)
        y = pl.pallas_call(
            _experts_kernel,
            grid_spec=pltpu.PrefetchScalarGridSpec(num_scalar_prefetch=3, grid=(n_tiles,), in_specs=in_specs,
                                                   out_specs=y_spec),
            out_shape=jax.ShapeDtypeStruct((n_tiles * tg, D_MODEL), F32),
            input_output_aliases={len(args) - 1: 0} if len(args) == 8 else {},
            compiler_params=pltpu.CompilerParams(dimension_semantics=("arbitrary",), vmem_limit_bytes=56 * MIB),
            name="experts",
        )(*args)
    return y


def _combine_kernel(tinfo_ref, yoff_ref, h_ref, route_ref, ys_ref, o_ref, sorted_ref, sems, *unpad_scratch,
                    seq_chunks):
    i = pl.program_id(0)
    n_steps = pl.num_programs(0)
    n_sorted = sorted_ref.shape[1]
    slot = i % 2

    def tile_copies(tile, buf, live):
        copies = []
        for e in range(N_EXPERTS):
            rec = (tile * 3) * N_EXPERTS + e
            base = tinfo_ref[rec + TILE_BASE * N_EXPERTS]
            n_e = jnp.where(live, tinfo_ref[rec + TILE_COUNT * N_EXPERTS], 0)
            dst0 = tinfo_ref[rec + TILE_START * N_EXPERTS]
            copies += _run_copies(n_e, yoff_ref[e] + base, dst0, ys_ref, sorted_ref.at[buf], sems.at[buf, e])
        return copies

    @pl.when(i == 0)
    def _():
        sorted_ref[...] = jnp.zeros(sorted_ref.shape, F32)
        for cond, cp in tile_copies(0, 0, True):
            pl.when(cond)(cp.start)

    for cond, cp in tile_copies(jnp.minimum(i + 1, n_steps - 1), 1 - slot, i + 1 < n_steps):
        pl.when(cond)(cp.start)
    for cond, cp in tile_copies(i, slot, True):
        pl.when(cond)(cp.wait)
    y = sorted_ref[slot].astype(BF16)
    route = route_ref[...]
    lane = lax.broadcasted_iota(I32, route.shape, 1)
    pick = lambda j: _lane_pick(route, lane, j)
    col = lax.broadcasted_iota(I32, (route.shape[0], n_sorted), 1)
    gated = (jnp.where(col == pick(ROUTE_D1).astype(I32), pick(ROUTE_G1), 0.0)
             + jnp.where(col == pick(ROUTE_D2).astype(I32), pick(ROUTE_G2), 0.0))
    result = h_ref[...] + jnp.dot(gated.astype(BF16), y, preferred_element_type=F32)
    if seq_chunks is None:
        o_ref[...] = result
        return

    res_ref, out_sems = unpad_scratch
    chunks_per_tile = h_ref.shape[0] // CHUNK

    def chunk_copies(step, buf):
        copies = []
        for j in range(chunks_per_tile):
            chunk = step * chunks_per_tile + j
            seq = chunk // seq_chunks
            in_seq = chunk - seq * seq_chunks
            dst = pl.multiple_of(seq * ((seq_chunks - 1) * CHUNK) + (in_seq - 1) * CHUNK, CHUNK)
            cp = pltpu.make_async_copy(res_ref.at[buf, pl.ds(j * CHUNK, CHUNK)], o_ref.at[pl.ds(dst, CHUNK)],
                                       out_sems.at[buf, j])
            copies.append((jnp.logical_and(step >= 0, in_seq >= 1), cp))
        return copies

    res_ref[slot] = result
    for cond, cp in chunk_copies(i, slot):
        pl.when(cond)(cp.start)
    for cond, cp in chunk_copies(i - 1, 1 - slot):
        pl.when(cond)(cp.wait)

    @pl.when(i == n_steps - 1)
    def _():
        for cond, cp in chunk_copies(i, slot):
            pl.when(cond)(cp.wait)


def _combine(h, route, tile_info, y_offsets, ys, unpad_seq_rows=None):
    rows = h.shape[0]
    tm = rows // tile_info.shape[0]
    n_pieces = _run_piece_count(tm)
    tinfo = tile_info[:, :3, :N_EXPERTS].astype(I32).reshape(-1)
    scratch = [pltpu.VMEM((2, _sorted_rows(tm), D_MODEL), F32), pltpu.SemaphoreType.DMA((2, N_EXPERTS, n_pieces))]
    if unpad_seq_rows is None:
        seq_chunks, out_rows = None, rows
        out_spec = pl.BlockSpec((tm, D_MODEL), lambda i, *_: (i, 0))
    else:
        seq_chunks = unpad_seq_rows // CHUNK
        out_rows = rows // unpad_seq_rows * (unpad_seq_rows - CHUNK)
        out_spec = pl.BlockSpec(memory_space=pl.ANY)
        scratch += [pltpu.VMEM((2, tm, D_MODEL), F32), pltpu.SemaphoreType.DMA((2, tm // CHUNK))]
    grid_spec = pltpu.PrefetchScalarGridSpec(
        num_scalar_prefetch=2,
        grid=(rows // tm,),
        in_specs=[pl.BlockSpec((tm, D_MODEL), lambda i, *_: (i, 0)),
                  pl.BlockSpec((tm, LANES), lambda i, *_: (i, 0)),
                  pl.BlockSpec(memory_space=pl.ANY)],
        out_specs=out_spec,
        scratch_shapes=scratch,
    )
    return pl.pallas_call(
        functools.partial(_combine_kernel, seq_chunks=seq_chunks),
        grid_spec=grid_spec,
        out_shape=jax.ShapeDtypeStruct((out_rows, D_MODEL), F32),
        input_output_aliases={2: 0} if unpad_seq_rows is None else {},
        compiler_params=pltpu.CompilerParams(dimension_semantics=("arbitrary",), vmem_limit_bytes=48 * MIB),
        name="combine",
    )(tinfo, y_offsets, h, route, ys)


def _routed_experts(mixes, gain, router_w, expert_w, unpad_seq_rows):
    tg = EXPERT_ROW_TILE
    group_rows = [mix[3].shape[0] for mix in mixes]
    total_rows = sum(group_rows)
    run_pad = (RUN_ALIGN - 1) * sum(rows // _router_tile(rows) for rows in group_rows)
    region_rows = (pl.cdiv(total_rows + run_pad, tg) + 1) * tg
    counts = jnp.zeros((SUBLANES, LANES), F32)
    xs, hs, routes, tiles = None, [], [], []
    for g, mix in enumerate(mixes):
        h, xs, route, tile_info, counts = _router(mix, gain, router_w, counts, xs, region_rows,
                                                  finalize=(g == len(mixes) - 1))
        hs.append(h)
        routes.append(route)
        tiles.append(tile_info)
    n_e = counts[0, :N_EXPERTS].astype(I32)
    tiles_e = (n_e + tg - 1) // tg
    first_tile = jnp.cumsum(tiles_e) - tiles_e
    n_valid = jnp.sum(tiles_e)
    n_tiles = (TOP_K * total_rows + N_EXPERTS * (run_pad + tg - 1)) // tg + 1
    t = jnp.minimum(jnp.arange(n_tiles, dtype=I32), n_valid - 1)
    tile_expert = jnp.sum((t[:, None] >= (first_tile + tiles_e)[None, :]).astype(I32), axis=1)
    tile_block = tile_expert * (region_rows // tg) + (t - first_tile[tile_expert])
    ys = _experts(xs, tile_block.astype(I32), tile_expert.astype(I32), n_valid.reshape(1).astype(I32), *expert_w)
    y_offsets = (first_tile * tg).astype(I32)
    return [_combine(h, route, tile_info, y_offsets, ys, seq_rows)
            for h, route, tile_info, seq_rows in zip(hs, routes, tiles, unpad_seq_rows)]


def _rotary_tables(batch, seq_rows):
    pos = (jnp.arange(seq_rows, dtype=I32) - PAD_FRONT).astype(F32)

    def cos_sin(n_rot, theta):
        half = n_rot // 2
        inv = theta ** (-jnp.arange(half, dtype=F32) * 2.0 / n_rot)
        ang = pos[:, None] * inv[None, :]
        return jnp.cos(ang), jnp.sin(ang)

    cos, sin = cos_sin(HEAD_DIM, RET_ROPE_THETA)
    cr = jnp.tile(cos, (1, 2 * LANES // HEAD_DIM))
    sr = jnp.tile(jnp.concatenate([-sin, sin], axis=1), (1, LANES // HEAD_DIM))
    cos, sin = cos_sin(ROPE_DIMS, ROPE_THETA)
    rest = HEAD_DIM - ROPE_DIMS
    ca = jnp.tile(jnp.concatenate([cos, cos, jnp.ones((seq_rows, rest), F32)], axis=1), (1, LANES // HEAD_DIM))
    sa = jnp.tile(jnp.concatenate([-sin, sin, jnp.zeros((seq_rows, rest), F32)], axis=1), (1, LANES // HEAD_DIM))
    return tuple(jnp.tile(t, (batch, 1)) for t in (cr, sr, ca, sa))


def _retention_params(log_gf, log_gb, gain):
    n_pairs = RET_WIDTH // LANES
    per_lane = lambda t: jnp.repeat(t.astype(F32).reshape(n_pairs, 2), HEAD_DIM, axis=1)
    per_head = lambda t, j: jnp.broadcast_to(t.astype(F32).reshape(n_pairs, 2)[:, j:j + 1], (n_pairs, LANES))
    rows = [per_lane(log_gf), per_lane(log_gb), per_head(log_gf, 0), per_head(log_gf, 1),
            per_head(log_gb, 0), per_head(log_gb, 1), gain.astype(F32).reshape(n_pairs, LANES),
            jnp.zeros((n_pairs, LANES), F32)]
    return jnp.stack(rows, axis=1)


def _token_mixer(h, lp, tabs, batch, seq_rows, embed=None):
    outs = _inproj(h, lp["norm_mix"], lp["w_in"], tabs, lp["q_gain"], lp["k_gain"], lp["ones_bd"], embed)
    if embed is not None:
        h, outs = outs[0], outs[1:]
    rq, rk, rv, rg, aq, ak, av = outs
    ret = _retention(rq, rk, rv, rg, lp["ret_dec"], batch, seq_rows)
    att = _attention(aq, ak, av, lp["sink"], lp["att_gain"], batch, seq_rows)
    return ret, att, lp["w_out"], h


def kernel(x_prompt, x_sample, meta_tokens, norm_mix, w_in, ret_log_decay_fwd, ret_log_decay_bwd, ret_out_gain,
           q_norm_gain, k_norm_gain, attn_sink, attn_out_gain, w_out, norm_ffn, ffn_w_gate, ffn_w_up, ffn_w_down,
           moe_router, moe_w_gate, moe_w_up, moe_w_down):
    depth = w_in.shape[0]
    ri = lax.broadcasted_iota(I32, (LANES, LANES), 0)
    ci = lax.broadcasted_iota(I32, (LANES, LANES), 1)
    ones_bd = jnp.where((ri < HEAD_DIM) == (ci < HEAD_DIM), 1.0 / HEAD_DIM, 0.0).astype(BF16)
    row = lambda t: t.astype(F32).reshape(1, -1)

    xs = (x_prompt, x_sample)
    shapes = [(x.shape[0], x.shape[1] + CHUNK) for x in xs]
    hs = [None] * len(xs)
    meta = meta_tokens.astype(F32)
    embeds = [(x.astype(F32).reshape(-1, D_MODEL), meta, r) for x, (_, r) in zip(xs, shapes)]
    tabs = [_rotary_tables(b, r) for b, r in shapes]

    for l in range(depth):
        lp = {
            "norm_mix": row(norm_mix[l]),
            "w_in": w_in[l].astype(BF16),
            "q_gain": jnp.tile(row(q_norm_gain[l]), (1, LANES // HEAD_DIM)),
            "k_gain": jnp.tile(row(k_norm_gain[l]), (1, LANES // HEAD_DIM)),
            "ones_bd": ones_bd,
            "ret_dec": _retention_params(ret_log_decay_fwd[l], ret_log_decay_bwd[l], ret_out_gain[l]),
            "sink": jnp.broadcast_to(attn_sink[l].astype(F32)[:, None] * LOG2_E, (N_ATT_HEADS, LANES)),
            "att_gain": row(attn_out_gain[l]),
            "w_out": w_out[l].astype(BF16),
        }
        mixes = [_token_mixer(h, lp, tab, b, r, embed if l == 0 else None)
                 for h, tab, (b, r), embed in zip(hs, tabs, shapes, embeds)]
        i = l // 2
        if l % 2 == 0:
            w = (ffn_w_gate[i].astype(BF16), ffn_w_up[i].astype(BF16), ffn_w_down[i].astype(BF16))
            hs = [_ffn(*mix, row(norm_ffn[l]), *w) for mix in mixes]
        else:
            wr = jnp.pad(moe_router[i].astype(F32), ((0, 0), (0, LANES - N_EXPERTS)))
            wr_hi = wr.astype(BF16)
            router_w = jnp.concatenate([wr_hi, (wr - wr_hi.astype(F32)).astype(BF16)], axis=1)
            expert_w = (moe_w_gate[i].astype(BF16), moe_w_up[i].astype(BF16), moe_w_down[i].astype(BF16))
            last = l == depth - 1
            hs = _routed_experts(mixes, row(norm_ffn[l]), router_w, expert_w,
                                 [r if last else None for _, r in shapes])
    if depth % 2 == 0:
        return tuple(h.reshape(x.shape).astype(x.dtype) for h, x in zip(hs, xs))
    return tuple(h.reshape(b, r, D_MODEL)[:, CHUNK:].astype(x.dtype) for h, x, (b, r) in zip(hs, xs, shapes))
```

```python
import functools

import jax
import jax.numpy as jnp
from jax import lax
from jax.experimental import pallas as pl
from jax.experimental.pallas import tpu as pltpu

F32 = jnp.float32
BF16 = jnp.bfloat16
I32 = jnp.int32

D_MODEL = 1024
HEAD_DIM = 64
N_RET_HEADS = 8
N_ATT_HEADS = 8
N_KV_HEADS = 2
RET_WIDTH = N_RET_HEADS * HEAD_DIM
ATT_WIDTH = N_ATT_HEADS * HEAD_DIM
KV_WIDTH = N_KV_HEADS * HEAD_DIM
IN_WIDTH = 4 * RET_WIDTH + ATT_WIDTH + 2 * KV_WIDTH
CHUNK = 128
LANES = 128
SUBLANES = 8
N_META = 16
PAD_FRONT = CHUNK - N_META
ROPE_THETA = 500000.0
ROPE_DIMS = HEAD_DIM // 4
RET_ROPE_THETA = 10000.0
N_EXPERTS = 8
TOP_K = 2
EXPERT_ROW_TILE = 512
MXU_V7X_COLUMNS = 256
EPS = 1e-6
NEG = -1e30
QK_SCALE = HEAD_DIM ** -0.5
LOG2_E = 1.4426950408889634
MIB = 1024 * 1024


def _row_tile(rows, prefs):
    for t in prefs:
        if rows % t == 0:
            return t
    raise ValueError(f"no row tile for {rows} rows among {prefs}")


def _resident(shape):
    return pl.BlockSpec(shape, lambda *_: (0,) * len(shape), pipeline_mode=pl.Buffered(1))


def _rms(x, gain):
    ms = jnp.mean(x * x, axis=-1, keepdims=True)
    return x * lax.rsqrt(ms + EPS) * gain


def _silu(x):
    return x * jax.nn.sigmoid(x)


def _embed_kernel(x_ref, meta_ref, *refs, seq_chunks):
    h0_ref, hbuf_ref, sems = refs[-10], refs[-2], refs[-1]
    i = pl.program_id(0)
    n_steps = pl.num_programs(0)
    tm = h0_ref.shape[0]
    chunks_per_tile = tm // CHUNK
    slot = i % 2

    def chunk_place(step, j):
        chunk = step * chunks_per_tile + j
        seq = chunk // seq_chunks
        return seq, chunk - seq * seq_chunks

    def chunk_copies(step, buf):
        copies = []
        for j in range(chunks_per_tile):
            seq, in_seq = chunk_place(step, j)
            src = pl.multiple_of(seq * ((seq_chunks - 1) * CHUNK) + (in_seq - 1) * CHUNK, CHUNK)
            cp = pltpu.make_async_copy(x_ref.at[pl.ds(src, CHUNK)], hbuf_ref.at[buf, pl.ds(j * CHUNK, CHUNK)],
                                       sems.at[buf, j])
            copies.append((jnp.logical_and(step < n_steps, in_seq >= 1), cp))
        return copies

    @pl.when(i == 0)
    def _():
        for cond, cp in chunk_copies(0, 0):
            pl.when(cond)(cp.start)

    for cond, cp in chunk_copies(i + 1, 1 - slot):
        pl.when(cond)(cp.start)
    for cond, cp in chunk_copies(i, slot):
        pl.when(cond)(cp.wait)
    lead = jnp.concatenate([jnp.zeros((PAD_FRONT, D_MODEL), F32), meta_ref[...]], axis=0)
    for j in range(chunks_per_tile):
        @pl.when(chunk_place(i, j)[1] == 0)
        def _():
            hbuf_ref[slot, j * CHUNK:(j + 1) * CHUNK, :] = lead

    h0_ref[...] = hbuf_ref[slot]
    _inproj_kernel(hbuf_ref.at[slot], *refs[:-10], *refs[-9:-2])


def _inproj_kernel(h_ref, gain_ref, w_ref, cr_ref, sr_ref, ca_ref, sa_ref, qg_ref, kg_ref, ones_ref,
                   rq_ref, rk_ref, rv_ref, rg_ref, aq_ref, ak_ref, av_ref):
    tm = h_ref.shape[0]
    part = tm
    lane = lax.broadcasted_iota(I32, (part, LANES), 1)
    in_head = lane & (HEAD_DIM - 1)
    low_head = lane < HEAD_DIM
    cols = lambda j: slice(j * LANES, (j + 1) * LANES)
    n_lane_tiles = RET_WIDTH // LANES
    base = 4 * RET_WIDTH

    def project(r):
        a = _rms(h_ref[r, :], gain_ref[...]).astype(BF16)
        proj = lambda c0, c1: jnp.dot(a, w_ref[:, c0:c1], preferred_element_type=F32)
        return dict(aq=proj(base, base + ATT_WIDTH), kv=proj(base + ATT_WIDTH, IN_WIDTH),
                    rq=proj(0, RET_WIDTH), rk=proj(RET_WIDTH, 2 * RET_WIDTH),
                    rv=proj(2 * RET_WIDTH, 3 * RET_WIDTH), rg=proj(3 * RET_WIDTH, 4 * RET_WIDTH))

    def rotate(x, cos, sin_signed, half):
        partner = jnp.where(in_head < half, pltpu.roll(x, LANES - half, 1), pltpu.roll(x, half, 1))
        return x * cos + partner * sin_signed

    def head_norm(x, g):
        ms = jnp.dot((x * x).astype(BF16), ones_ref[...], preferred_element_type=F32)
        return x * lax.rsqrt(ms + EPS) * g

    def epilogue(r, first_chunk, p):
        cr, sr, ca, sa = cr_ref[r, :], sr_ref[r, :], ca_ref[r, :], sa_ref[r, :]

        def store_keys(kt_ref, j, x):
            for c in range(part // CHUNK):
                kt_ref[first_chunk + c, cols(j), :] = x[c * CHUNK:(c + 1) * CHUNK, :].T.astype(BF16)

        rv_ref[r, :] = p["rv"].astype(BF16)
        rg_ref[r, :] = p["rg"].astype(BF16)
        for j in range(n_lane_tiles):
            rq_ref[r, cols(j)] = rotate(p["rq"][:, cols(j)], cr, sr, HEAD_DIM // 2).astype(BF16)
        for j in range(n_lane_tiles):
            store_keys(rk_ref, j, rotate(p["rk"][:, cols(j)], cr, sr, HEAD_DIM // 2) * QK_SCALE)
        normed = [head_norm(p["aq"][:, cols(j)], qg_ref[...]) for j in range(ATT_WIDTH // LANES)]
        normed_k = head_norm(p["kv"][:, :KV_WIDTH], kg_ref[...])
        for j, n in enumerate(normed):
            aq_ref[r, cols(j)] = (rotate(n, ca, sa, ROPE_DIMS // 2) * (QK_SCALE * LOG2_E)).astype(BF16)
        k = rotate(normed_k, ca, sa, ROPE_DIMS // 2)
        v = p["kv"][:, KV_WIDTH:]
        k_sw = pltpu.roll(k, HEAD_DIM, 1)
        v_sw = pltpu.roll(v, HEAD_DIM, 1)
        store_keys(ak_ref, 0, jnp.where(low_head, k, k_sw))
        store_keys(ak_ref, 1, jnp.where(low_head, k_sw, k))
        av_ref[r, 0:LANES] = jnp.where(low_head, v, v_sw).astype(BF16)
        av_ref[r, LANES:2 * LANES] = jnp.where(low_head, v_sw, v).astype(BF16)

    whole_tile = slice(0, tm)
    epilogue(whole_tile, 0, project(whole_tile))


def _inproj(h, gain, w, tabs, qg, kg, ones_bd, embed=None):
    rows = tabs[0].shape[0]
    tm = _row_tile(rows, (1024, 768, 640, 512, 384, 256, 128))
    row_spec = lambda width: pl.BlockSpec((tm, width), lambda i: (i, 0))
    outs = ((RET_WIDTH, False), (RET_WIDTH, True), (RET_WIDTH, False), (RET_WIDTH, False),
            (ATT_WIDTH, False), (2 * KV_WIDTH, True), (2 * KV_WIDTH, False))
    chunk_spec = lambda width: pl.BlockSpec((tm // CHUNK, width, CHUNK), lambda i: (i, 0, 0))
    in_specs = [_resident((1, D_MODEL)), _resident((D_MODEL, IN_WIDTH)),
                row_spec(LANES), row_spec(LANES), row_spec(LANES), row_spec(LANES),
                _resident((1, LANES)), _resident((1, LANES)), _resident((LANES, LANES))]
    out_specs = [chunk_spec(wd) if t else row_spec(wd) for wd, t in outs]
    out_shape = [jax.ShapeDtypeStruct((rows // CHUNK, wd, CHUNK) if t else (rows, wd), BF16) for wd, t in outs]
    if embed is None:
        return pl.pallas_call(
            _inproj_kernel,
            grid=(rows // tm,),
            in_specs=[row_spec(D_MODEL)] + in_specs,
            out_specs=out_specs,
            out_shape=out_shape,
            compiler_params=pltpu.CompilerParams(dimension_semantics=("parallel",), vmem_limit_bytes=56 * MIB),
            name="inproj",
        )(h, gain, w, *tabs, qg, kg, ones_bd)
    x, meta, seq_rows = embed
    return pl.pallas_call(
        functools.partial(_embed_kernel, seq_chunks=seq_rows // CHUNK),
        grid=(rows // tm,),
        in_specs=[pl.BlockSpec(memory_space=pl.ANY), _resident((N_META, D_MODEL))] + in_specs,
        out_specs=[row_spec(D_MODEL)] + out_specs,
        out_shape=[jax.ShapeDtypeStruct((rows, D_MODEL), F32)] + out_shape,
        scratch_shapes=[pltpu.VMEM((2, tm, D_MODEL), F32), pltpu.SemaphoreType.DMA((2, tm // CHUNK))],
        compiler_params=pltpu.CompilerParams(dimension_semantics=("arbitrary",), vmem_limit_bytes=56 * MIB),
        name="embed",
    )(x, meta, gain, w, *tabs, qg, kg, ones_bd)


def _retention_kernel(q_ref, k_ref, v_ref, g_ref, dec_ref, o_ref, sb_ref, *, n_chunks):
    lgf, lgb = dec_ref[0:1, :], dec_ref[1:2, :]
    gain = dec_ref[6:7, :]
    ri = lax.broadcasted_iota(I32, (CHUNK, LANES), 0)
    ci = lax.broadcasted_iota(I32, (CHUNK, LANES), 1)
    r = ri.astype(F32)
    diff = (ri - ci).astype(F32)

    def decay_mask(lf, lb):
        return jnp.where(diff >= 0, jnp.exp(jnp.maximum(diff, 0.0) * lf), jnp.exp(jnp.maximum(-diff, 0.0) * lb))

    dm = jnp.concatenate([decay_mask(dec_ref[2:3, :], dec_ref[4:5, :]),
                          decay_mask(dec_ref[3:4, :], dec_ref[5:6, :])], axis=1)
    tok = ci.astype(F32)
    wf = jnp.exp((CHUNK - 1.0 - tok) * jnp.broadcast_to(lgf, (LANES, LANES)).T)
    wb = jnp.exp(tok * jnp.broadcast_to(lgb, (LANES, LANES)).T)
    qf = jnp.exp((r + 1.0) * lgf)
    qb = jnp.exp((CHUNK - r) * lgb)
    cf = jnp.exp(float(CHUNK) * lgf)
    cb = jnp.exp(float(CHUNK) * lgb)
    low_c = ci < HEAD_DIM
    same_head = (ri < HEAD_DIM) == low_c
    mean_bd = jnp.where(same_head, 1.0 / HEAD_DIM, 0.0).astype(BF16)
    m0 = jnp.where(low_c, 1.0, 0.0).astype(BF16)
    m1 = jnp.where(low_c, 0.0, 1.0).astype(BF16)
    low_r = ri < HEAD_DIM
    rm0 = jnp.where(low_r, 1.0, 0.0).astype(BF16)
    rm1 = jnp.where(low_r, 0.0, 1.0).astype(BF16)

    def chunk_rows(n):
        return pl.ds(pl.multiple_of(n * CHUNK, CHUNK), CHUNK)

    def pair_rows(x):
        return jnp.concatenate([x * m0, x * m1], axis=0)

    def pair_cols(xt):
        return jnp.concatenate([xt * rm0, xt * rm1], axis=1)

    def state_delta(kt, weights, v):
        u = jnp.dot((kt.astype(F32) * weights).astype(BF16), v, preferred_element_type=F32)
        return jnp.where(same_head, u, 0.0)

    group = next(u for u in (33, 11, 13, 4, 3, 5, 2, 1) if n_chunks % u == 0)
    n_groups = n_chunks // group
    members = range(group)
    zero_state = jnp.zeros((LANES, LANES), F32)

    def backward(i, state):
        ns = [n_chunks - 1 - i * group - j for j in members]
        deltas = [state_delta(k_ref[n], wb, v_ref[chunk_rows(n), :]) for n in ns]
        for n, delta in zip(ns, deltas):
            sb_ref[n] = state.astype(BF16)
            state = cb * state + delta
        return state

    lax.fori_loop(0, n_groups, backward, zero_state)

    def forward(i, state):
        ns = [i * group + j for j in members]
        q = [q_ref[chunk_rows(n), :] for n in ns]
        kt = [k_ref[n] for n in ns]
        v = [v_ref[chunk_rows(n), :] for n in ns]
        deltas = [state_delta(kt[j], wf, v[j]) for j in members]
        s = [jnp.dot(q[j], pair_cols(kt[j]), preferred_element_type=F32) * dm for j in members]
        both = []
        for j in members:
            both.append(jnp.concatenate([state.astype(BF16), sb_ref[ns[j]]], axis=1))
            state = cf * state + deltas[j]
        intra = [jnp.dot(s[j].astype(BF16), pair_rows(v[j]), preferred_element_type=F32) for j in members]
        inter = [jnp.dot(q[j], both[j], preferred_element_type=F32) for j in members]
        o = [intra[j] + inter[j][:, :LANES] * qf + inter[j][:, LANES:] * qb for j in members]
        ms = [jnp.dot((o[j] * o[j]).astype(BF16), mean_bd, preferred_element_type=F32) for j in members]
        for j in members:
            gate = _silu(g_ref[chunk_rows(ns[j]), :].astype(F32))
            o_ref[chunk_rows(ns[j]), :] = (o[j] * lax.rsqrt(ms[j] + EPS) * gain * gate).astype(BF16)
        return state

    lax.fori_loop(0, n_groups, forward, zero_state)


def _retention(rq, rk, rv, rg, dec, batch, seq_rows):
    n_chunks = seq_rows // CHUNK
    n_pairs = RET_WIDTH // LANES
    view = lambda t: t.reshape(batch, seq_rows, RET_WIDTH)
    seq_spec = pl.BlockSpec((None, seq_rows, LANES), lambda b, p: (b, 0, p))
    key_spec = pl.BlockSpec((n_chunks, LANES, CHUNK), lambda b, p: (b, p, 0))
    out = pl.pallas_call(
        functools.partial(_retention_kernel, n_chunks=n_chunks),
        grid=(batch, n_pairs),
        in_specs=[seq_spec, key_spec, seq_spec, seq_spec,
                  pl.BlockSpec((None, 8, LANES), lambda b, p: (p, 0, 0))],
        out_specs=seq_spec,
        out_shape=jax.ShapeDtypeStruct((batch, seq_rows, RET_WIDTH), BF16),
        scratch_shapes=[pltpu.VMEM((n_chunks, LANES, LANES), BF16)],
        compiler_params=pltpu.CompilerParams(dimension_semantics=("parallel", "parallel"),
                                             vmem_limit_bytes=48 * MIB),
        name="retention",
    )(view(rq), rk, view(rv), view(rg), dec)
    return out.reshape(batch * seq_rows, RET_WIDTH)


def _attention_kernel(q_ref, kp_ref, kc_ref, kn_ref, km_ref, vp_ref, vc_ref, vn_ref, vm_ref,
                      sink_ref, gain_ref, o_ref, *, n_chunks, chunks_per_step):
    first_chunk = pl.program_id(1) * chunks_per_step
    ri = lax.broadcasted_iota(I32, (CHUNK, LANES), 0)
    ci = lax.broadcasted_iota(I32, (CHUNK, LANES), 1)
    never = 2 * CHUNK
    to_bias = lambda ok: jnp.where(ok, 0.0, NEG).astype(BF16)
    meta_bias = to_bias(ci >= PAD_FRONT)
    n_keys = 4 * CHUNK
    low_c = ci < HEAD_DIM
    m0 = jnp.where(low_c, 1.0, 0.0).astype(BF16)
    m1 = jnp.where(low_c, 0.0, 1.0).astype(BF16)
    low_r = ri < HEAD_DIM
    rm0 = jnp.where(low_r, 1.0, 0.0).astype(BF16)
    rm1 = jnp.where(low_r, 0.0, 1.0).astype(BF16)
    eye = jnp.where(ri == ci, 1.0, 0.0).astype(BF16)
    row_sums = jnp.concatenate([m0] * 4 + [m1] * 4, axis=0)
    pairs_per_kv = N_ATT_HEADS // N_KV_HEADS // 2
    rows = lambda j: slice(j * CHUNK, (j + 1) * CHUNK)

    def split_values(ref, blk, g):
        x = ref[blk, g * LANES:(g + 1) * LANES]
        return x * m0, x * m1

    def split_keys(ref, idx, g):
        xt = ref[idx, g * LANES:(g + 1) * LANES, :]
        return xt * rm0, xt * rm1

    def blocks(split, at, prev_ref, main_ref, next_ref, meta_ref, g):
        out = [split(prev_ref, at(0), g)]
        out += [split(main_ref, at(j), g) for j in range(chunks_per_step)]
        out += [split(next_ref, at(0), g), split(meta_ref, at(0), g)]
        return out

    k_blocks = [blocks(split_keys, lambda j: j, kp_ref, kc_ref, kn_ref, km_ref, g) for g in range(N_KV_HEADS)]
    v_blocks = [blocks(split_values, rows, vp_ref, vc_ref, vn_ref, vm_ref, g) for g in range(N_KV_HEADS)]

    def pair_blocks(blks, j, axis):
        use = [blks[j], blks[j + 1], blks[j + 2], blks[-1]]
        return jnp.concatenate([b[0] for b in use] + [b[1] for b in use], axis=axis)

    def pair_ids(g):
        return [g * pairs_per_kv + pp for pp in range(pairs_per_kv)]

    def scores(j, g):
        c = first_chunk + j
        prev_off = jnp.where(c >= 2, 0, never)
        cur_off = jnp.where(c >= 1, 0, never)
        next_off = jnp.where(c + 1 <= n_chunks - 1, 0, never)
        bias = jnp.concatenate([to_bias(ci >= ri + prev_off), to_bias(ci >= cur_off),
                                to_bias(ci + next_off <= ri), meta_bias] * 2, axis=1)
        k_ext = jnp.concatenate([pair_blocks(k_blocks[g], j, 1), bias], axis=0)
        q_ext = jnp.concatenate(
            [jnp.concatenate([q_ref[rows(j), p * LANES:(p + 1) * LANES], eye], axis=1) for p in pair_ids(g)], axis=0)
        return jnp.dot(q_ext, k_ext, preferred_element_type=F32)

    def softmax_parts(g, s):
        probs, sink_rows = [], []
        for pp, p in enumerate(pair_ids(g)):
            halves, sink_terms = [], []
            for hh in range(2):
                sh = s[rows(pp), hh * n_keys:(hh + 1) * n_keys]
                sk = sink_ref[2 * p + hh:2 * p + hh + 1, 0:1]
                m = jnp.maximum(jnp.max(sh, axis=-1, keepdims=True), sk)
                halves.append(jnp.exp2(sh - m).astype(BF16))
                sink_terms.append(jnp.exp2(sk - m))
            probs.append(jnp.concatenate(halves, axis=1))
            sink_rows.append(jnp.where(low_c, sink_terms[0], sink_terms[1]))
        return jnp.concatenate(probs, axis=0), sink_rows

    def weighted_values(j, g, probs, sink_rows):
        v_ext = jnp.concatenate([pair_blocks(v_blocks[g], j, 0), row_sums], axis=1)
        ol = jnp.dot(probs, v_ext, preferred_element_type=F32)
        return [ol[rows(pp), :LANES] / (ol[rows(pp), LANES:] + sink_rows[pp]) for pp in range(pairs_per_kv)]

    chunk_group = 7
    for j0 in range(0, chunks_per_step, chunk_group):
        units = [(j, g) for j in range(j0, min(j0 + chunk_group, chunks_per_step)) for g in range(N_KV_HEADS)]
        s = [scores(j, g) for j, g in units]
        parts = [softmax_parts(g, s[u]) for u, (j, g) in enumerate(units)]
        outs = [weighted_values(j, g, *parts[u]) for u, (j, g) in enumerate(units)]
        for j in sorted({j for j, _ in units}):
            att = jnp.concatenate([o for u, (ju, _) in enumerate(units) if ju == j for o in outs[u]], axis=1)
            row = lax.broadcasted_iota(I32, att.shape, 0)
            pad_rows = jnp.where(first_chunk + j == 0, PAD_FRONT, 0)
            att = jnp.where(row >= pad_rows, att, 0.0)
            o_ref[rows(j), :] = _rms(att, gain_ref[...]).astype(BF16)


def _attention(aq, ak, av, sink, gain, batch, seq_rows):
    n_chunks = seq_rows // CHUNK
    per_step = next(r for r in (11, 13, 3, 5, 4, 2, 1) if n_chunks % r == 0)
    n_steps = n_chunks // per_step
    q3 = aq.reshape(batch, seq_rows, ATT_WIDTH)
    v3 = av.reshape(batch, seq_rows, 2 * KV_WIDTH)
    prev_chunk = lambda s: jnp.maximum(s * per_step - 1, 0)
    next_chunk = lambda s: jnp.minimum((s + 1) * per_step, n_chunks - 1)
    v_one = lambda fn: pl.BlockSpec((None, CHUNK, 2 * KV_WIDTH), lambda b, s: (b, fn(s), 0))
    v_specs = [v_one(prev_chunk), pl.BlockSpec((None, per_step * CHUNK, 2 * KV_WIDTH), lambda b, s: (b, s, 0)),
               v_one(next_chunk), v_one(lambda s: 0)]
    k_one = lambda fn: pl.BlockSpec((1, 2 * KV_WIDTH, CHUNK), lambda b, s: (b * n_chunks + fn(s), 0, 0))
    k_specs = [k_one(prev_chunk),
               pl.BlockSpec((per_step, 2 * KV_WIDTH, CHUNK), lambda b, s: (b * n_steps + s, 0, 0)),
               k_one(next_chunk), k_one(lambda s: 0)]
    q_spec = pl.BlockSpec((None, per_step * CHUNK, ATT_WIDTH), lambda b, s: (b, s, 0))
    out = pl.pallas_call(
        functools.partial(_attention_kernel, n_chunks=n_chunks, chunks_per_step=per_step),
        grid=(batch, n_steps),
        in_specs=[q_spec] + k_specs + v_specs + [_resident((N_ATT_HEADS, LANES)), _resident((1, ATT_WIDTH))],
        out_specs=q_spec,
        out_shape=jax.ShapeDtypeStruct((batch, seq_rows, ATT_WIDTH), BF16),
        compiler_params=pltpu.CompilerParams(dimension_semantics=("parallel", "parallel"),
                                             vmem_limit_bytes=32 * MIB),
        name="attention",
    )(q3, ak, ak, ak, ak, v3, v3, v3, v3, sink, gain)
    return out.reshape(batch * seq_rows, ATT_WIDTH)


def _mixed_residual(ret_ref, att_ref, w_ref, h_ref, rows=slice(None)):
    acc = jnp.dot(ret_ref[rows, :], w_ref[0:RET_WIDTH, :], preferred_element_type=F32)
    acc = acc + jnp.dot(att_ref[rows, :], w_ref[RET_WIDTH:, :], preferred_element_type=F32)
    return h_ref[rows, :] + acc


def _mix_specs(tm):
    row_spec = lambda width: pl.BlockSpec((tm, width), lambda i, *_: (i, 0))
    return [row_spec(RET_WIDTH), row_spec(ATT_WIDTH), _resident((RET_WIDTH + ATT_WIDTH, D_MODEL)),
            row_spec(D_MODEL)]


def _ff_halves(d_ff):
    n_mxu_tiles = pl.cdiv(d_ff, MXU_V7X_COLUMNS)
    first = min(d_ff, pl.cdiv(n_mxu_tiles, 2) * MXU_V7X_COLUMNS)
    return tuple((c0, c1) for c0, c1 in ((0, first), (first, d_ff)) if c1 > c0)


def _ffn_kernel(ret_ref, att_ref, wo_ref, h_ref, gain_ref, wg_ref, wu_ref, wd_ref, o_ref):
    x = _mixed_residual(ret_ref, att_ref, wo_ref, h_ref)
    f = _rms(x, gain_ref[...]).astype(BF16)
    acc = x
    for c0, c1 in _ff_halves(wg_ref.shape[1]):
        gate = jnp.dot(f, wg_ref[:, c0:c1], preferred_element_type=F32)
        up = jnp.dot(f, wu_ref[:, c0:c1], preferred_element_type=F32)
        act = (_silu(gate) * up).astype(BF16)
        acc = acc + jnp.dot(act, wd_ref[c0:c1, :], preferred_element_type=F32)
    o_ref[...] = acc


def _ffn(ret, att, w_out, h, gain, wg, wu, wd):
    rows = h.shape[0]
    d_ff = wg.shape[1]
    tm = _row_tile(rows, (768, 640, 512, 384, 256, 128))
    return pl.pallas_call(
        _ffn_kernel,
        grid=(rows // tm,),
        in_specs=_mix_specs(tm) + [_resident((1, D_MODEL)), _resident((D_MODEL, d_ff)), _resident((D_MODEL, d_ff)),
                                   _resident((d_ff, D_MODEL))],
        out_specs=pl.BlockSpec((tm, D_MODEL), lambda i: (i, 0)),
        out_shape=jax.ShapeDtypeStruct((rows, D_MODEL), F32),
        input_output_aliases={3: 0},
        compiler_params=pltpu.CompilerParams(dimension_semantics=("parallel",), vmem_limit_bytes=56 * MIB),
        name="ffn",
    )(ret, att, w_out, h, gain, wg, wu, wd)


ROUTE_G1, ROUTE_G2, ROUTE_D1, ROUTE_D2 = range(4)
TILE_BASE, TILE_COUNT, TILE_START = range(3)


def _lane_pick(x, lane, j):
    return jnp.sum(jnp.where(lane == j, x, 0.0), axis=-1, keepdims=True)


def _lane_scalar(vec, lane_row, e):
    return jnp.sum(jnp.where(lane_row == e, vec, 0.0)).astype(I32)


RUN_ALIGN_BITS = 3
RUN_ALIGN = 1 << RUN_ALIGN_BITS


def _run_piece_count(tm):
    return tm.bit_length() - RUN_ALIGN_BITS


def _run_copies(count, src0, dst0, src_ref, dst_ref, sems):
    copies = []
    for j in range(sems.shape[0]):
        k = j + RUN_ALIGN_BITS
        size = 1 << k
        start = (count >> (k + 1)) << (k + 1)
        src = src_ref.at[pl.ds(pl.multiple_of(src0 + start, RUN_ALIGN), size)]
        dst = dst_ref.at[pl.ds(pl.multiple_of(dst0 + start, RUN_ALIGN), size)]
        copies.append((((count >> k) & 1) == 1, pltpu.make_async_copy(src, dst, sems.at[j])))
    return copies


def _router_kernel(*refs, region_rows, finalize, aliased, tiles_per_step):
    ret_ref, att_ref, wo_ref, h_ref, gain_ref, wr_ref, base_ref = refs[:7]
    refs = refs[7 + (1 if aliased else 0):]
    hout_ref, xs_ref, route_ref, tile_ref, cnt_ref, sorted_ref, zero_ref, sems, zsem, prev_ref = refs
    i = pl.program_id(0)
    tm = h_ref.shape[0] // tiles_per_step
    n_sorted = sorted_ref.shape[1]

    @pl.when(i == 0)
    def _():
        cnt_ref[...] = base_ref[...]

    lane = lax.broadcasted_iota(I32, (tm, LANES), 1)
    lane_row = lane[0:1, :]
    ri = lax.broadcasted_iota(I32, (tm, tm), 0)
    ci = lax.broadcasted_iota(I32, (tm, tm), 1)
    earlier_tokens = jnp.where(ri > ci, 1.0, 0.0).astype(BF16)
    li = lax.broadcasted_iota(I32, (LANES, LANES), 0)
    lj = lax.broadcasted_iota(I32, (LANES, LANES), 1)
    earlier_experts = jnp.where(li < lj, 1.0, 0.0).astype(BF16)
    slot = lax.broadcasted_iota(I32, (n_sorted, tm), 0)
    sub = lax.broadcasted_iota(I32, (SUBLANES, LANES), 0)

    def as_row(col):
        wide = jnp.broadcast_to(col, (tm, LANES))
        return jnp.concatenate([wide[c * LANES:(c + 1) * LANES, :].T for c in range(tm // LANES)], axis=1)[0:1, :]

    recs = [dict(t=t, rows=slice(t * tm, (t + 1) * tm)) for t in range(tiles_per_step)]

    def stage(fn):
        for rec in recs:
            rec.update(fn(rec))

    def mix(r):
        h_mixed = _mixed_residual(ret_ref, att_ref, wo_ref, h_ref, r["rows"])
        hout_ref[r["rows"], :] = h_mixed
        return dict(f=_rms(h_mixed, gain_ref[...]))

    def logits(r):
        f_hi = r["f"].astype(BF16)
        f_lo = (r["f"] - f_hi.astype(F32)).astype(BF16)
        both = (jnp.dot(f_hi, wr_ref[...], preferred_element_type=F32)
                + jnp.dot(f_lo, wr_ref[...], preferred_element_type=F32))
        lg = jnp.where(lane < N_EXPERTS, both[:, :LANES] + both[:, LANES:], -jnp.inf)
        return dict(f_hi=f_hi, lg=lg)

    def first_choice(r):
        m1 = jnp.max(r["lg"], axis=-1, keepdims=True)
        return dict(m1=m1, i1=jnp.min(jnp.where(r["lg"] == m1, lane, LANES), axis=-1, keepdims=True))

    def second_choice(r):
        lg2 = jnp.where(lane == r["i1"], -jnp.inf, r["lg"])
        m2 = jnp.max(lg2, axis=-1, keepdims=True)
        return dict(m2=m2, i2=jnp.min(jnp.where(lg2 == m2, lane, LANES), axis=-1, keepdims=True))

    def rank_tokens(r):
        active = jnp.max(jnp.abs(r["f"]), axis=-1, keepdims=True) > 0.0
        sel = jnp.where(active, jnp.where(lane == r["i1"], 1.0, 0.0) + jnp.where(lane == r["i2"], 1.0, 0.0), 0.0)
        rank = jnp.dot(earlier_tokens, sel.astype(BF16), preferred_element_type=F32)
        tiles_per_run = jnp.floor((jnp.sum(sel, axis=0, keepdims=True) + (RUN_ALIGN - 1.0)) * (1.0 / RUN_ALIGN))
        start = RUN_ALIGN * jnp.dot(jnp.broadcast_to(tiles_per_run, (SUBLANES, LANES)).astype(BF16),
                                    earlier_experts, preferred_element_type=F32)[0:1, :]
        return dict(active=active, rank=rank, count=tiles_per_run * RUN_ALIGN, start=start)

    def destinations(r):
        dest = r["start"] + r["rank"]
        no_slot = -1.0
        d1 = jnp.where(r["active"], _lane_pick(dest, lane, r["i1"]), no_slot)
        d2 = jnp.where(r["active"], _lane_pick(dest, lane, r["i2"]), no_slot)
        e2 = jnp.exp(r["m2"] - r["m1"])
        g1 = 1.0 / (1.0 + e2)
        route = jnp.zeros((tm, LANES), F32)
        for j, val in ((ROUTE_G1, g1), (ROUTE_G2, e2 * g1), (ROUTE_D1, d1), (ROUTE_D2, d2)):
            route = jnp.where(lane == j, val, route)
        route_ref[r["rows"], :] = route
        return dict(d1=d1, d2=d2)

    def sort_rows(r):
        onehot_t = (jnp.where(slot == as_row(r["d1"]).astype(I32), 1.0, 0.0)
                    + jnp.where(slot == as_row(r["d2"]).astype(I32), 1.0, 0.0))
        sorted_ref[r["t"]] = jnp.dot(onehot_t.astype(BF16), r["f_hi"], preferred_element_type=F32)
        return {}

    def run_copies(records, live):
        copies = []
        for t, rec in enumerate(records):
            for e in range(N_EXPERTS):
                n_e = jnp.where(live, _lane_scalar(rec[TILE_COUNT:TILE_COUNT + 1, :], lane_row, e), 0)
                src0 = _lane_scalar(rec[TILE_START:TILE_START + 1, :], lane_row, e)
                dst0 = e * region_rows + _lane_scalar(rec[TILE_BASE:TILE_BASE + 1, :], lane_row, e)
                copies += _run_copies(n_e, src0, dst0, sorted_ref.at[t], xs_ref, sems.at[t, e])
        return copies

    @pl.when(i == 0)
    def _():
        prev_ref[...] = jnp.zeros(prev_ref.shape, F32)

    for fn in (mix, logits, first_choice, second_choice, rank_tokens, destinations):
        stage(fn)
    for cond, cp in run_copies([prev_ref[t] for t in range(tiles_per_step)], i > 0):
        pl.when(cond)(cp.wait)
    stage(sort_rows)
    base = cnt_ref[0:1, :]
    for rec in recs:
        record = jnp.where(sub == TILE_BASE, base, jnp.where(
            sub == TILE_COUNT, rec["count"], jnp.where(sub == TILE_START, rec["start"], 0.0)))
        tile_ref[rec["t"]] = record
        prev_ref[rec["t"]] = record
        rec["record"] = record
        base = base + rec["count"]
    copies = run_copies([rec["record"] for rec in recs], True)
    for cond, cp in copies:
        pl.when(cond)(cp.start)
    cnt_ref[...] = jnp.broadcast_to(base, cnt_ref.shape)

    @pl.when(i == pl.num_programs(0) - 1)
    def _():
        for cond, cp in copies:
            pl.when(cond)(cp.wait)

    if finalize:
        @pl.when(i == pl.num_programs(0) - 1)
        def _():
            zero_ref[...] = jnp.zeros(zero_ref.shape, F32)
            total = cnt_ref[0:1, :]
            for e in range(N_EXPERTS):
                end = pl.multiple_of(e * region_rows + _lane_scalar(total, lane_row, e), RUN_ALIGN)
                cp = pltpu.make_async_copy(zero_ref, xs_ref.at[pl.ds(end, EXPERT_ROW_TILE)], zsem)
                cp.start()
                cp.wait()


def _router_tile(rows):
    return _row_tile(rows, (512, 640, 384, 256, 128))


def _sorted_rows(tm):
    return TOP_K * tm + pl.cdiv(N_EXPERTS * (RUN_ALIGN - 1), LANES) * LANES


def _router(mix, gain, w_router, base_counts, xs, region_rows, finalize):
    rows = mix[3].shape[0]
    tm = _router_tile(rows)
    n_tiles = rows // tm
    per_step = 2 if n_tiles % 2 == 0 else 1
    step_rows = per_step * tm
    aliased = xs is not None
    row_spec = lambda width: pl.BlockSpec((step_rows, width), lambda i: (i, 0))
    in_specs = _mix_specs(step_rows) + [_resident((1, D_MODEL)), _resident((D_MODEL, 2 * LANES)),
                                        _resident((SUBLANES, LANES))]
    args = [*mix, gain, w_router, base_counts]
    aliases = {3: 0}
    if aliased:
        aliases[len(args)] = 1
        in_specs.append(pl.BlockSpec(memory_space=pl.ANY))
        args.append(xs)
    n_pieces = _run_piece_count(tm)
    return pl.pallas_call(
        functools.partial(_router_kernel, region_rows=region_rows, finalize=finalize, aliased=aliased,
                          tiles_per_step=per_step),
        grid=(n_tiles // per_step,),
        in_specs=in_specs,
        out_specs=[row_spec(D_MODEL), pl.BlockSpec(memory_space=pl.ANY), row_spec(LANES),
                   pl.BlockSpec((per_step, SUBLANES, LANES), lambda i: (i, 0, 0)),
                   pl.BlockSpec((SUBLANES, LANES), lambda i: (0, 0))],
        out_shape=[jax.ShapeDtypeStruct((rows, D_MODEL), F32),
                   jax.ShapeDtypeStruct((N_EXPERTS * region_rows, D_MODEL), F32),
                   jax.ShapeDtypeStruct((rows, LANES), F32),
                   jax.ShapeDtypeStruct((n_tiles, SUBLANES, LANES), F32),
                   jax.ShapeDtypeStruct((SUBLANES, LANES), F32)],
        scratch_shapes=[pltpu.VMEM((per_step, _sorted_rows(tm), D_MODEL), F32),
                        pltpu.VMEM((EXPERT_ROW_TILE, D_MODEL), F32),
                        pltpu.SemaphoreType.DMA((per_step, N_EXPERTS, n_pieces)), pltpu.SemaphoreType.DMA(()),
                        pltpu.VMEM((per_step, SUBLANES, LANES), F32)],
        input_output_aliases=aliases,
        compiler_params=pltpu.CompilerParams(dimension_semantics=("arbitrary",), vmem_limit_bytes=56 * MIB),
        name="router",
    )(*args)


def _experts_kernel(blk_ref, exp_ref, nvalid_ref, x_ref, wg_ref, wu_ref, wd_ref, *y_refs):
    y_ref = y_refs[-1]
    partial_ref = y_refs[0] if len(y_refs) == 2 else None
    valid = pl.program_id(0) < nvalid_ref[0]

    @pl.when(valid)
    def _():
        x = x_ref[...].astype(BF16)
        gate = jnp.dot(x, wg_ref[...], preferred_element_type=F32)
        up = jnp.dot(x, wu_ref[...], preferred_element_type=F32)
        part = jnp.dot((_silu(gate) * up).astype(BF16), wd_ref[...], preferred_element_type=F32)
        y_ref[...] = part if partial_ref is None else partial_ref[...] + part

    @pl.when(jnp.logical_not(valid))
    def _():
        y_ref[...] = jnp.zeros(y_ref.shape, F32) if partial_ref is None else partial_ref[...]


def _experts(xs, tile_block, tile_expert, n_valid, wg, wu, wd):
    n_tiles = tile_block.shape[0]
    d_ff = wg.shape[2]
    ff_chunk = d_ff // 2
    assert ff_chunk % MXU_V7X_COLUMNS == 0
    tg = EXPERT_ROW_TILE
    y = None
    for c in range(d_ff // ff_chunk):
        in_specs = [pl.BlockSpec((tg, D_MODEL), lambda i, blk, ex, nv: (blk[i], 0)),
                    pl.BlockSpec((None, D_MODEL, ff_chunk), lambda i, blk, ex, nv, c=c: (ex[i], 0, c)),
                    pl.BlockSpec((None, D_MODEL, ff_chunk), lambda i, blk, ex, nv, c=c: (ex[i], 0, c)),
                    pl.BlockSpec((None, ff_chunk, D_MODEL), lambda i, blk, ex, nv, c=c: (ex[i], c, 0))]
        args = [tile_block, tile_expert, n_valid, xs, wg, wu, wd]
        y_spec = pl.BlockSpec((tg, D_MODEL), lambda i, blk, ex, nv: (i, 0))
        if y is not None:
            in_specs.append(y_spec)
            args.append(y)
        y = pl.pallas_call(
            _experts_kernel,
            grid_spec=pltpu.PrefetchScalarGridSpec(num_scalar_prefetch=3, grid=(n_tiles,), in_specs=in_specs,
                                                   out_specs=y_spec),
            out_shape=jax.ShapeDtypeStruct((n_tiles * tg, D_MODEL), F32),
            input_output_aliases={len(args) - 1: 0} if len(args) == 8 else {},
            compiler_params=pltpu.CompilerParams(dimension_semantics=("arbitrary",), vmem_limit_bytes=56 * MIB),
            name="experts",
        )(*args)
    return y


def _combine_kernel(tinfo_ref, yoff_ref, h_ref, route_ref, ys_ref, o_ref, sorted_ref, sems, *unpad_scratch,
                    seq_chunks):
    i = pl.program_id(0)
    n_steps = pl.num_programs(0)
    n_sorted = sorted_ref.shape[1]
    slot = i % 2

    def tile_copies(tile, buf, live):
        copies = []
        for e in range(N_EXPERTS):
            rec = (tile * 3) * N_EXPERTS + e
            base = tinfo_ref[rec + TILE_BASE * N_EXPERTS]
            n_e = jnp.where(live, tinfo_ref[rec + TILE_COUNT * N_EXPERTS], 0)
            dst0 = tinfo_ref[rec + TILE_START * N_EXPERTS]
            copies += _run_copies(n_e, yoff_ref[e] + base, dst0, ys_ref, sorted_ref.at[buf], sems.at[buf, e])
        return copies

    @pl.when(i == 0)
    def _():
        sorted_ref[...] = jnp.zeros(sorted_ref.shape, F32)
        for cond, cp in tile_copies(0, 0, True):
            pl.when(cond)(cp.start)

    for cond, cp in tile_copies(jnp.minimum(i + 1, n_steps - 1), 1 - slot, i + 1 < n_steps):
        pl.when(cond)(cp.start)
    for cond, cp in tile_copies(i, slot, True):
        pl.when(cond)(cp.wait)
    y = sorted_ref[slot].astype(BF16)
    route = route_ref[...]
    lane = lax.broadcasted_iota(I32, route.shape, 1)
    pick = lambda j: _lane_pick(route, lane, j)
    col = lax.broadcasted_iota(I32, (route.shape[0], n_sorted), 1)
    gated = (jnp.where(col == pick(ROUTE_D1).astype(I32), pick(ROUTE_G1), 0.0)
             + jnp.where(col == pick(ROUTE_D2).astype(I32), pick(ROUTE_G2), 0.0))
    result = h_ref[...] + jnp.dot(gated.astype(BF16), y, preferred_element_type=F32)
    if seq_chunks is None:
        o_ref[...] = result
        return

    res_ref, out_sems = unpad_scratch
    chunks_per_tile = h_ref.shape[0] // CHUNK

    def chunk_copies(step, buf):
        copies = []
        for j in range(chunks_per_tile):
            chunk = step * chunks_per_tile + j
            seq = chunk // seq_chunks
            in_seq = chunk - seq * seq_chunks
            dst = pl.multiple_of(seq * ((seq_chunks - 1) * CHUNK) + (in_seq - 1) * CHUNK, CHUNK)
            cp = pltpu.make_async_copy(res_ref.at[buf, pl.ds(j * CHUNK, CHUNK)], o_ref.at[pl.ds(dst, CHUNK)],
                                       out_sems.at[buf, j])
            copies.append((jnp.logical_and(step >= 0, in_seq >= 1), cp))
        return copies

    res_ref[slot] = result
    for cond, cp in chunk_copies(i, slot):
        pl.when(cond)(cp.start)
    for cond, cp in chunk_copies(i - 1, 1 - slot):
        pl.when(cond)(cp.wait)

    @pl.when(i == n_steps - 1)
    def _():
        for cond, cp in chunk_copies(i, slot):
            pl.when(cond)(cp.wait)


def _combine(h, route, tile_info, y_offsets, ys, unpad_seq_rows=None):
    rows = h.shape[0]
    tm = rows // tile_info.shape[0]
    n_pieces = _run_piece_count(tm)
    tinfo = tile_info[:, :3, :N_EXPERTS].astype(I32).reshape(-1)
    scratch = [pltpu.VMEM((2, _sorted_rows(tm), D_MODEL), F32), pltpu.SemaphoreType.DMA((2, N_EXPERTS, n_pieces))]
    if unpad_seq_rows is None:
        seq_chunks, out_rows = None, rows
        out_spec = pl.BlockSpec((tm, D_MODEL), lambda i, *_: (i, 0))
    else:
        seq_chunks = unpad_seq_rows // CHUNK
        out_rows = rows // unpad_seq_rows * (unpad_seq_rows - CHUNK)
        out_spec = pl.BlockSpec(memory_space=pl.ANY)
        scratch += [pltpu.VMEM((2, tm, D_MODEL), F32), pltpu.SemaphoreType.DMA((2, tm // CHUNK))]
    grid_spec = pltpu.PrefetchScalarGridSpec(
        num_scalar_prefetch=2,
        grid=(rows // tm,),
        in_specs=[pl.BlockSpec((tm, D_MODEL), lambda i, *_: (i, 0)),
                  pl.BlockSpec((tm, LANES), lambda i, *_: (i, 0)),
                  pl.BlockSpec(memory_space=pl.ANY)],
        out_specs=out_spec,
        scratch_shapes=scratch,
    )
    return pl.pallas_call(
        functools.partial(_combine_kernel, seq_chunks=seq_chunks),
        grid_spec=grid_spec,
        out_shape=jax.ShapeDtypeStruct((out_rows, D_MODEL), F32),
        input_output_aliases={2: 0} if unpad_seq_rows is None else {},
        compiler_params=pltpu.CompilerParams(dimension_semantics=("arbitrary",), vmem_limit_bytes=48 * MIB),
        name="combine",
    )(tinfo, y_offsets, h, route, ys)


def _routed_experts(mixes, gain, router_w, expert_w, unpad_seq_rows):
    tg = EXPERT_ROW_TILE
    group_rows = [mix[3].shape[0] for mix in mixes]
    total_rows = sum(group_rows)
    run_pad = (RUN_ALIGN - 1) * sum(rows // _router_tile(rows) for rows in group_rows)
    region_rows = (pl.cdiv(total_rows + run_pad, tg) + 1) * tg
    counts = jnp.zeros((SUBLANES, LANES), F32)
    xs, hs, routes, tiles = None, [], [], []
    for g, mix in enumerate(mixes):
        h, xs, route, tile_info, counts = _router(mix, gain, router_w, counts, xs, region_rows,
                                                  finalize=(g == len(mixes) - 1))
        hs.append(h)
        routes.append(route)
        tiles.append(tile_info)
    n_e = counts[0, :N_EXPERTS].astype(I32)
    tiles_e = (n_e + tg - 1) // tg
    first_tile = jnp.cumsum(tiles_e) - tiles_e
    n_valid = jnp.sum(tiles_e)
    n_tiles = (TOP_K * total_rows + N_EXPERTS * (run_pad + tg - 1)) // tg + 1
    t = jnp.minimum(jnp.arange(n_tiles, dtype=I32), n_valid - 1)
    tile_expert = jnp.sum((t[:, None] >= (first_tile + tiles_e)[None, :]).astype(I32), axis=1)
    tile_block = tile_expert * (region_rows // tg) + (t - first_tile[tile_expert])
    ys = _experts(xs, tile_block.astype(I32), tile_expert.astype(I32), n_valid.reshape(1).astype(I32), *expert_w)
    y_offsets = (first_tile * tg).astype(I32)
    return [_combine(h, route, tile_info, y_offsets, ys, seq_rows)
            for h, route, tile_info, seq_rows in zip(hs, routes, tiles, unpad_seq_rows)]


def _rotary_tables(batch, seq_rows):
    pos = (jnp.arange(seq_rows, dtype=I32) - PAD_FRONT).astype(F32)

    def cos_sin(n_rot, theta):
        half = n_rot // 2
        inv = theta ** (-jnp.arange(half, dtype=F32) * 2.0 / n_rot)
        ang = pos[:, None] * inv[None, :]
        return jnp.cos(ang), jnp.sin(ang)

    cos, sin = cos_sin(HEAD_DIM, RET_ROPE_THETA)
    cr = jnp.tile(cos, (1, 2 * LANES // HEAD_DIM))
    sr = jnp.tile(jnp.concatenate([-sin, sin], axis=1), (1, LANES // HEAD_DIM))
    cos, sin = cos_sin(ROPE_DIMS, ROPE_THETA)
    rest = HEAD_DIM - ROPE_DIMS
    ca = jnp.tile(jnp.concatenate([cos, cos, jnp.ones((seq_rows, rest), F32)], axis=1), (1, LANES // HEAD_DIM))
    sa = jnp.tile(jnp.concatenate([-sin, sin, jnp.zeros((seq_rows, rest), F32)], axis=1), (1, LANES // HEAD_DIM))
    return tuple(jnp.tile(t, (batch, 1)) for t in (cr, sr, ca, sa))


def _retention_params(log_gf, log_gb, gain):
    n_pairs = RET_WIDTH // LANES
    per_lane = lambda t: jnp.repeat(t.astype(F32).reshape(n_pairs, 2), HEAD_DIM, axis=1)
    per_head = lambda t, j: jnp.broadcast_to(t.astype(F32).reshape(n_pairs, 2)[:, j:j + 1], (n_pairs, LANES))
    rows = [per_lane(log_gf), per_lane(log_gb), per_head(log_gf, 0), per_head(log_gf, 1),
            per_head(log_gb, 0), per_head(log_gb, 1), gain.astype(F32).reshape(n_pairs, LANES),
            jnp.zeros((n_pairs, LANES), F32)]
    return jnp.stack(rows, axis=1)


def _token_mixer(h, lp, tabs, batch, seq_rows, embed=None):
    outs = _inproj(h, lp["norm_mix"], lp["w_in"], tabs, lp["q_gain"], lp["k_gain"], lp["ones_bd"], embed)
    if embed is not None:
        h, outs = outs[0], outs[1:]
    rq, rk, rv, rg, aq, ak, av = outs
    ret = _retention(rq, rk, rv, rg, lp["ret_dec"], batch, seq_rows)
    att = _attention(aq, ak, av, lp["sink"], lp["att_gain"], batch, seq_rows)
    return ret, att, lp["w_out"], h


def kernel(x_prompt, x_sample, meta_tokens, norm_mix, w_in, ret_log_decay_fwd, ret_log_decay_bwd, ret_out_gain,
           q_norm_gain, k_norm_gain, attn_sink, attn_out_gain, w_out, norm_ffn, ffn_w_gate, ffn_w_up, ffn_w_down,
           moe_router, moe_w_gate, moe_w_up, moe_w_down):
    depth = w_in.shape[0]
    ri = lax.broadcasted_iota(I32, (LANES, LANES), 0)
    ci = lax.broadcasted_iota(I32, (LANES, LANES), 1)
    ones_bd = jnp.where((ri < HEAD_DIM) == (ci < HEAD_DIM), 1.0 / HEAD_DIM, 0.0).astype(BF16)
    row = lambda t: t.astype(F32).reshape(1, -1)

    xs = (x_prompt, x_sample)
    shapes = [(x.shape[0], x.shape[1] + CHUNK) for x in xs]
    hs = [None] * len(xs)
    meta = meta_tokens.astype(F32)
    embeds = [(x.astype(F32).reshape(-1, D_MODEL), meta, r) for x, (_, r) in zip(xs, shapes)]
    tabs = [_rotary_tables(b, r) for b, r in shapes]

    for l in range(depth):
        lp = {
            "norm_mix": row(norm_mix[l]),
            "w_in": w_in[l].astype(BF16),
            "q_gain": jnp.tile(row(q_norm_gain[l]), (1, LANES // HEAD_DIM)),
            "k_gain": jnp.tile(row(k_norm_gain[l]), (1, LANES // HEAD_DIM)),
            "ones_bd": ones_bd,
            "ret_dec": _retention_params(ret_log_decay_fwd[l], ret_log_decay_bwd[l], ret_out_gain[l]),
            "sink": jnp.broadcast_to(attn_sink[l].astype(F32)[:, None] * LOG2_E, (N_ATT_HEADS, LANES)),
            "att_gain": row(attn_out_gain[l]),
            "w_out": w_out[l].astype(BF16),
        }
        mixes = [_token_mixer(h, lp, tab, b, r, embed if l == 0 else None)
                 for h, tab, (b, r), embed in zip(hs, tabs, shapes, embeds)]
        i = l // 2
        if l % 2 == 0:
            w = (ffn_w_gate[i].astype(BF16), ffn_w_up[i].astype(BF16), ffn_w_down[i].astype(BF16))
            hs = [_ffn(*mix, row(norm_ffn[l]), *w) for mix in mixes]
        else:
            wr = jnp.pad(moe_router[i].astype(F32), ((0, 0), (0, LANES - N_EXPERTS)))
            wr_hi = wr.astype(BF16)
            router_w = jnp.concatenate([wr_hi, (wr - wr_hi.astype(F32)).astype(BF16)], axis=1)
            expert_w = (moe_w_gate[i].astype(BF16), moe_w_up[i].astype(BF16), moe_w_down[i].astype(BF16))
            last = l == depth - 1
            hs = _routed_experts(mixes, row(norm_ffn[l]), router_w, expert_w,
                                 [r if last else None for _, r in shapes])
    if depth % 2 == 0:
        return tuple(h.reshape(x.shape).astype(x.dtype) for h, x in zip(hs, xs))
    return tuple(h.reshape(b, r, D_MODEL)[:, CHUNK:].astype(x.dtype) for h, x, (b, r) in zip(hs, xs, shapes))
```

```python
import functools

import jax
import jax.numpy as jnp
from jax import lax
from jax.experimental import pallas as pl
from jax.experimental.pallas import tpu as pltpu

F32 = jnp.float32
BF16 = jnp.bfloat16
I32 = jnp.int32

D_MODEL = 1024
HEAD_DIM = 64
N_RET_HEADS = 8
N_ATT_HEADS = 8
N_KV_HEADS = 2
RET_WIDTH = N_RET_HEADS * HEAD_DIM
ATT_WIDTH = N_ATT_HEADS * HEAD_DIM
KV_WIDTH = N_KV_HEADS * HEAD_DIM
IN_WIDTH = 4 * RET_WIDTH + ATT_WIDTH + 2 * KV_WIDTH
CHUNK = 128
LANES = 128
SUBLANES = 8
N_META = 16
PAD_FRONT = CHUNK - N_META
ROPE_THETA = 500000.0
ROPE_DIMS = HEAD_DIM // 4
RET_ROPE_THETA = 10000.0
N_EXPERTS = 8
TOP_K = 2
EXPERT_ROW_TILE = 512
MXU_V7X_COLUMNS = 256
EPS = 1e-6
NEG = -1e30
QK_SCALE = HEAD_DIM ** -0.5
LOG2_E = 1.4426950408889634
MIB = 1024 * 1024


def _row_tile(rows, prefs):
    for t in prefs:
        if rows % t == 0:
            return t
    raise ValueError(f"no row tile for {rows} rows among {prefs}")


def _resident(shape):
    return pl.BlockSpec(shape, lambda *_: (0,) * len(shape), pipeline_mode=pl.Buffered(1))


def _rms(x, gain):
    ms = jnp.mean(x * x, axis=-1, keepdims=True)
    return x * lax.rsqrt(ms + EPS) * gain


def _silu(x):
    return x * jax.nn.sigmoid(x)


def _embed_kernel(x_ref, meta_ref, *refs, seq_chunks):
    h0_ref, hbuf_ref, sems = refs[-10], refs[-2], refs[-1]
    i = pl.program_id(0)
    n_steps = pl.num_programs(0)
    tm = h0_ref.shape[0]
    chunks_per_tile = tm // CHUNK
    slot = i % 2

    def chunk_place(step, j):
        chunk = step * chunks_per_tile + j
        seq = chunk // seq_chunks
        return seq, chunk - seq * seq_chunks

    def chunk_copies(step, buf):
        copies = []
        for j in range(chunks_per_tile):
            seq, in_seq = chunk_place(step, j)
            src = pl.multiple_of(seq * ((seq_chunks - 1) * CHUNK) + (in_seq - 1) * CHUNK, CHUNK)
            cp = pltpu.make_async_copy(x_ref.at[pl.ds(src, CHUNK)], hbuf_ref.at[buf, pl.ds(j * CHUNK, CHUNK)],
                                       sems.at[buf, j])
            copies.append((jnp.logical_and(step < n_steps, in_seq >= 1), cp))
        return copies

    @pl.when(i == 0)
    def _():
        for cond, cp in chunk_copies(0, 0):
            pl.when(cond)(cp.start)

    for cond, cp in chunk_copies(i + 1, 1 - slot):
        pl.when(cond)(cp.start)
    for cond, cp in chunk_copies(i, slot):
        pl.when(cond)(cp.wait)
    lead = jnp.concatenate([jnp.zeros((PAD_FRONT, D_MODEL), F32), meta_ref[...]], axis=0)
    for j in range(chunks_per_tile):
        @pl.when(chunk_place(i, j)[1] == 0)
        def _():
            hbuf_ref[slot, j * CHUNK:(j + 1) * CHUNK, :] = lead

    h0_ref[...] = hbuf_ref[slot]
    _inproj_kernel(hbuf_ref.at[slot], *refs[:-10], *refs[-9:-2])


def _inproj_kernel(h_ref, gain_ref, w_ref, cr_ref, sr_ref, ca_ref, sa_ref, qg_ref, kg_ref, ones_ref,
                   rq_ref, rk_ref, rv_ref, rg_ref, aq_ref, ak_ref, av_ref):
    tm = h_ref.shape[0]
    part = tm
    lane = lax.broadcasted_iota(I32, (part, LANES), 1)
    in_head = lane & (HEAD_DIM - 1)
    low_head = lane < HEAD_DIM
    cols = lambda j: slice(j * LANES, (j + 1) * LANES)
    n_lane_tiles = RET_WIDTH // LANES
    base = 4 * RET_WIDTH

    def project(r):
        a = _rms(h_ref[r, :], gain_ref[...]).astype(BF16)
        proj = lambda c0, c1: jnp.dot(a, w_ref[:, c0:c1], preferred_element_type=F32)
        return dict(aq=proj(base, base + ATT_WIDTH), kv=proj(base + ATT_WIDTH, IN_WIDTH),
                    rq=proj(0, RET_WIDTH), rk=proj(RET_WIDTH, 2 * RET_WIDTH),
                    rv=proj(2 * RET_WIDTH, 3 * RET_WIDTH), rg=proj(3 * RET_WIDTH, 4 * RET_WIDTH))

    def rotate(x, cos, sin_signed, half):
        partner = jnp.where(in_head < half, pltpu.roll(x, LANES - half, 1), pltpu.roll(x, half, 1))
        return x * cos + partner * sin_signed

    def head_norm(x, g):
        ms = jnp.dot((x * x).astype(BF16), ones_ref[...], preferred_element_type=F32)
        return x * lax.rsqrt(ms + EPS) * g

    def epilogue(r, first_chunk, p):
        cr, sr, ca, sa = cr_ref[r, :], sr_ref[r, :], ca_ref[r, :], sa_ref[r, :]

        def store_keys(kt_ref, j, x):
            for c in range(part // CHUNK):
                kt_ref[first_chunk + c, cols(j), :] = x[c * CHUNK:(c + 1) * CHUNK, :].T.astype(BF16)

        rv_ref[r, :] = p["rv"].astype(BF16)
        rg_ref[r, :] = p["rg"].astype(BF16)
        for j in range(n_lane_tiles):
            rq_ref[r, cols(j)] = rotate(p["rq"][:, cols(j)], cr, sr, HEAD_DIM // 2).astype(BF16)
        for j in range(n_lane_tiles):
            store_keys(rk_ref, j, rotate(p["rk"][:, cols(j)], cr, sr, HEAD_DIM // 2) * QK_SCALE)
        normed = [head_norm(p["aq"][:, cols(j)], qg_ref[...]) for j in range(ATT_WIDTH // LANES)]
        normed_k = head_norm(p["kv"][:, :KV_WIDTH], kg_ref[...])
        for j, n in enumerate(normed):
            aq_ref[r, cols(j)] = (rotate(n, ca, sa, ROPE_DIMS // 2) * (QK_SCALE * LOG2_E)).astype(BF16)
        k = rotate(normed_k, ca, sa, ROPE_DIMS // 2)
        v = p["kv"][:, KV_WIDTH:]
        k_sw = pltpu.roll(k, HEAD_DIM, 1)
        v_sw = pltpu.roll(v, HEAD_DIM, 1)
        store_keys(ak_ref, 0, jnp.where(low_head, k, k_sw))
        store_keys(ak_ref, 1, jnp.where(low_head, k_sw, k))
        av_ref[r, 0:LANES] = jnp.where(low_head, v, v_sw).astype(BF16)
        av_ref[r, LANES:2 * LANES] = jnp.where(low_head, v_sw, v).astype(BF16)

    whole_tile = slice(0, tm)
    epilogue(whole_tile, 0, project(whole_tile))


def _inproj(h, gain, w, tabs, qg, kg, ones_bd, embed=None):
    rows = tabs[0].shape[0]
    tm = _row_tile(rows, (1024, 768, 640, 512, 384, 256, 128))
    row_spec = lambda width: pl.BlockSpec((tm, width), lambda i: (i, 0))
    outs = ((RET_WIDTH, False), (RET_WIDTH, True), (RET_WIDTH, False), (RET_WIDTH, False),
            (ATT_WIDTH, False), (2 * KV_WIDTH, True), (2 * KV_WIDTH, False))
    chunk_spec = lambda width: pl.BlockSpec((tm // CHUNK, width, CHUNK), lambda i: (i, 0, 0))
    in_specs = [_resident((1, D_MODEL)), _resident((D_MODEL, IN_WIDTH)),
                row_spec(LANES), row_spec(LANES), row_spec(LANES), row_spec(LANES),
                _resident((1, LANES)), _resident((1, LANES)), _resident((LANES, LANES))]
    out_specs = [chunk_spec(wd) if t else row_spec(wd) for wd, t in outs]
    out_shape = [jax.ShapeDtypeStruct((rows // CHUNK, wd, CHUNK) if t else (rows, wd), BF16) for wd, t in outs]
    if embed is None:
        return pl.pallas_call(
            _inproj_kernel,
            grid=(rows // tm,),
            in_specs=[row_spec(D_MODEL)] + in_specs,
            out_specs=out_specs,
            out_shape=out_shape,
            compiler_params=pltpu.CompilerParams(dimension_semantics=("parallel",), vmem_limit_bytes=56 * MIB),
            name="inproj",
        )(h, gain, w, *tabs, qg, kg, ones_bd)
    x, meta, seq_rows = embed
    return pl.pallas_call(
        functools.partial(_embed_kernel, seq_chunks=seq_rows // CHUNK),
        grid=(rows // tm,),
        in_specs=[pl.BlockSpec(memory_space=pl.ANY), _resident((N_META, D_MODEL))] + in_specs,
        out_specs=[row_spec(D_MODEL)] + out_specs,
        out_shape=[jax.ShapeDtypeStruct((rows, D_MODEL), F32)] + out_shape,
        scratch_shapes=[pltpu.VMEM((2, tm, D_MODEL), F32), pltpu.SemaphoreType.DMA((2, tm // CHUNK))],
        compiler_params=pltpu.CompilerParams(dimension_semantics=("arbitrary",), vmem_limit_bytes=56 * MIB),
        name="embed",
    )(x, meta, gain, w, *tabs, qg, kg, ones_bd)


def _retention_kernel(q_ref, k_ref, v_ref, g_ref, dec_ref, o_ref, sb_ref, *, n_chunks):
    lgf, lgb = dec_ref[0:1, :], dec_ref[1:2, :]
    gain = dec_ref[6:7, :]
    ri = lax.broadcasted_iota(I32, (CHUNK, LANES), 0)
    ci = lax.broadcasted_iota(I32, (CHUNK, LANES), 1)
    r = ri.astype(F32)
    diff = (ri - ci).astype(F32)

    def decay_mask(lf, lb):
        return jnp.where(diff >= 0, jnp.exp(jnp.maximum(diff, 0.0) * lf), jnp.exp(jnp.maximum(-diff, 0.0) * lb))

    dm = jnp.concatenate([decay_mask(dec_ref[2:3, :], dec_ref[4:5, :]),
                          decay_mask(dec_ref[3:4, :], dec_ref[5:6, :])], axis=1)
    tok = ci.astype(F32)
    wf = jnp.exp((CHUNK - 1.0 - tok) * jnp.broadcast_to(lgf, (LANES, LANES)).T)
    wb = jnp.exp(tok * jnp.broadcast_to(lgb, (LANES, LANES)).T)
    qf = jnp.exp((r + 1.0) * lgf)
    qb = jnp.exp((CHUNK - r) * lgb)
    cf = jnp.exp(float(CHUNK) * lgf)
    cb = jnp.exp(float(CHUNK) * lgb)
    low_c = ci < HEAD_DIM
    same_head = (ri < HEAD_DIM) == low_c
    mean_bd = jnp.where(same_head, 1.0 / HEAD_DIM, 0.0).astype(BF16)
    m0 = jnp.where(low_c, 1.0, 0.0).astype(BF16)
    m1 = jnp.where(low_c, 0.0, 1.0).astype(BF16)
    low_r = ri < HEAD_DIM
    rm0 = jnp.where(low_r, 1.0, 0.0).astype(BF16)
    rm1 = jnp.where(low_r, 0.0, 1.0).astype(BF16)

    def chunk_rows(n):
        return pl.ds(pl.multiple_of(n * CHUNK, CHUNK), CHUNK)

    def pair_rows(x):
        return jnp.concatenate([x * m0, x * m1], axis=0)

    def pair_cols(xt):
        return jnp.concatenate([xt * rm0, xt * rm1], axis=1)

    def state_delta(kt, weights, v):
        u = jnp.dot((kt.astype(F32) * weights).astype(BF16), v, preferred_element_type=F32)
        return jnp.where(same_head, u, 0.0)

    group = next(u for u in (33, 11, 13, 4, 3, 5, 2, 1) if n_chunks % u == 0)
    n_groups = n_chunks // group
    members = range(group)
    zero_state = jnp.zeros((LANES, LANES), F32)

    def backward(i, state):
        ns = [n_chunks - 1 - i * group - j for j in members]
        deltas = [state_delta(k_ref[n], wb, v_ref[chunk_rows(n), :]) for n in ns]
        for n, delta in zip(ns, deltas):
            sb_ref[n] = state.astype(BF16)
            state = cb * state + delta
        return state

    lax.fori_loop(0, n_groups, backward, zero_state)

    def forward(i, state):
        ns = [i * group + j for j in members]
        q = [q_ref[chunk_rows(n), :] for n in ns]
        kt = [k_ref[n] for n in ns]
        v = [v_ref[chunk_rows(n), :] for n in ns]
        deltas = [state_delta(kt[j], wf, v[j]) for j in members]
        s = [jnp.dot(q[j], pair_cols(kt[j]), preferred_element_type=F32) * dm for j in members]
        both = []
        for j in members:
            both.append(jnp.concatenate([state.astype(BF16), sb_ref[ns[j]]], axis=1))
            state = cf * state + deltas[j]
        intra = [jnp.dot(s[j].astype(BF16), pair_rows(v[j]), preferred_element_type=F32) for j in members]
        inter = [jnp.dot(q[j], both[j], preferred_element_type=F32) for j in members]
        o = [intra[j] + inter[j][:, :LANES] * qf + inter[j][:, LANES:] * qb for j in members]
        ms = [jnp.dot((o[j] * o[j]).astype(BF16), mean_bd, preferred_element_type=F32) for j in members]
        for j in members:
            gate = _silu(g_ref[chunk_rows(ns[j]), :].astype(F32))
            o_ref[chunk_rows(ns[j]), :] = (o[j] * lax.rsqrt(ms[j] + EPS) * gain * gate).astype(BF16)
        return state

    lax.fori_loop(0, n_groups, forward, zero_state)


def _retention(rq, rk, rv, rg, dec, batch, seq_rows):
    n_chunks = seq_rows // CHUNK
    n_pairs = RET_WIDTH // LANES
    view = lambda t: t.reshape(batch, seq_rows, RET_WIDTH)
    seq_spec = pl.BlockSpec((None, seq_rows, LANES), lambda b, p: (b, 0, p))
    key_spec = pl.BlockSpec((n_chunks, LANES, CHUNK), lambda b, p: (b, p, 0))
    out = pl.pallas_call(
        functools.partial(_retention_kernel, n_chunks=n_chunks),
        grid=(batch, n_pairs),
        in_specs=[seq_spec, key_spec, seq_spec, seq_spec,
                  pl.BlockSpec((None, 8, LANES), lambda b, p: (p, 0, 0))],
        out_specs=seq_spec,
        out_shape=jax.ShapeDtypeStruct((batch, seq_rows, RET_WIDTH), BF16),
        scratch_shapes=[pltpu.VMEM((n_chunks, LANES, LANES), BF16)],
        compiler_params=pltpu.CompilerParams(dimension_semantics=("parallel", "parallel"),
                                             vmem_limit_bytes=48 * MIB),
        name="retention",
    )(view(rq), rk, view(rv), view(rg), dec)
    return out.reshape(batch * seq_rows, RET_WIDTH)


def _attention_kernel(q_ref, kp_ref, kc_ref, kn_ref, km_ref, vp_ref, vc_ref, vn_ref, vm_ref,
                      sink_ref, gain_ref, o_ref, *, n_chunks, chunks_per_step):
    first_chunk = pl.program_id(1) * chunks_per_step
    ri = lax.broadcasted_iota(I32, (CHUNK, LANES), 0)
    ci = lax.broadcasted_iota(I32, (CHUNK, LANES), 1)
    never = 2 * CHUNK
    to_bias = lambda ok: jnp.where(ok, 0.0, NEG).astype(BF16)
    meta_bias = to_bias(ci >= PAD_FRONT)
    n_keys = 4 * CHUNK
    low_c = ci < HEAD_DIM
    m0 = jnp.where(low_c, 1.0, 0.0).astype(BF16)
    m1 = jnp.where(low_c, 0.0, 1.0).astype(BF16)
    low_r = ri < HEAD_DIM
    rm0 = jnp.where(low_r, 1.0, 0.0).astype(BF16)
    rm1 = jnp.where(low_r, 0.0, 1.0).astype(BF16)
    eye = jnp.where(ri == ci, 1.0, 0.0).astype(BF16)
    row_sums = jnp.concatenate([m0] * 4 + [m1] * 4, axis=0)
    pairs_per_kv = N_ATT_HEADS // N_KV_HEADS // 2
    rows = lambda j: slice(j * CHUNK, (j + 1) * CHUNK)

    def split_values(ref, blk, g):
        x = ref[blk, g * LANES:(g + 1) * LANES]
        return x * m0, x * m1

    def split_keys(ref, idx, g):
        xt = ref[idx, g * LANES:(g + 1) * LANES, :]
        return xt * rm0, xt * rm1

    def blocks(split, at, prev_ref, main_ref, next_ref, meta_ref, g):
        out = [split(prev_ref, at(0), g)]
        out += [split(main_ref, at(j), g) for j in range(chunks_per_step)]
        out += [split(next_ref, at(0), g), split(meta_ref, at(0), g)]
        return out

    k_blocks = [blocks(split_keys, lambda j: j, kp_ref, kc_ref, kn_ref, km_ref, g) for g in range(N_KV_HEADS)]
    v_blocks = [blocks(split_values, rows, vp_ref, vc_ref, vn_ref, vm_ref, g) for g in range(N_KV_HEADS)]

    def pair_blocks(blks, j, axis):
        use = [blks[j], blks[j + 1], blks[j + 2], blks[-1]]
        return jnp.concatenate([b[0] for b in use] + [b[1] for b in use], axis=axis)

    def pair_ids(g):
        return [g * pairs_per_kv + pp for pp in range(pairs_per_kv)]

    def scores(j, g):
        c = first_chunk + j
        prev_off = jnp.where(c >= 2, 0, never)
        cur_off = jnp.where(c >= 1, 0, never)
        next_off = jnp.where(c + 1 <= n_chunks - 1, 0, never)
        bias = jnp.concatenate([to_bias(ci >= ri + prev_off), to_bias(ci >= cur_off),
                                to_bias(ci + next_off <= ri), meta_bias] * 2, axis=1)
        k_ext = jnp.concatenate([pair_blocks(k_blocks[g], j, 1), bias], axis=0)
        q_ext = jnp.concatenate(
            [jnp.concatenate([q_ref[rows(j), p * LANES:(p + 1) * LANES], eye], axis=1) for p in pair_ids(g)], axis=0)
        return jnp.dot(q_ext, k_ext, preferred_element_type=F32)

    def softmax_parts(g, s):
        probs, sink_rows = [], []
        for pp, p in enumerate(pair_ids(g)):
            halves, sink_terms = [], []
            for hh in range(2):
                sh = s[rows(pp), hh * n_keys:(hh + 1) * n_keys]
                sk = sink_ref[2 * p + hh:2 * p + hh + 1, 0:1]
                m = jnp.maximum(jnp.max(sh, axis=-1, keepdims=True), sk)
                halves.append(jnp.exp2(sh - m).astype(BF16))
                sink_terms.append(jnp.exp2(sk - m))
            probs.append(jnp.concatenate(halves, axis=1))
            sink_rows.append(jnp.where(low_c, sink_terms[0], sink_terms[1]))
        return jnp.concatenate(probs, axis=0), sink_rows

    def weighted_values(j, g, probs, sink_rows):
        v_ext = jnp.concatenate([pair_blocks(v_blocks[g], j, 0), row_sums], axis=1)
        ol = jnp.dot(probs, v_ext, preferred_element_type=F32)
        return [ol[rows(pp), :LANES] / (ol[rows(pp), LANES:] + sink_rows[pp]) for pp in range(pairs_per_kv)]

    chunk_group = 7
    for j0 in range(0, chunks_per_step, chunk_group):
        units = [(j, g) for j in range(j0, min(j0 + chunk_group, chunks_per_step)) for g in range(N_KV_HEADS)]
        s = [scores(j, g) for j, g in units]
        parts = [softmax_parts(g, s[u]) for u, (j, g) in enumerate(units)]
        outs = [weighted_values(j, g, *parts[u]) for u, (j, g) in enumerate(units)]
        for j in sorted({j for j, _ in units}):
            att = jnp.concatenate([o for u, (ju, _) in enumerate(units) if ju == j for o in outs[u]], axis=1)
            row = lax.broadcasted_iota(I32, att.shape, 0)
            pad_rows = jnp.where(first_chunk + j == 0, PAD_FRONT, 0)
            att = jnp.where(row >= pad_rows, att, 0.0)
            o_ref[rows(j), :] = _rms(att, gain_ref[...]).astype(BF16)


def _attention(aq, ak, av, sink, gain, batch, seq_rows):
    n_chunks = seq_rows // CHUNK
    per_step = next(r for r in (11, 13, 3, 5, 4, 2, 1) if n_chunks % r == 0)
    n_steps = n_chunks // per_step
    q3 = aq.reshape(batch, seq_rows, ATT_WIDTH)
    v3 = av.reshape(batch, seq_rows, 2 * KV_WIDTH)
    prev_chunk = lambda s: jnp.maximum(s * per_step - 1, 0)
    next_chunk = lambda s: jnp.minimum((s + 1) * per_step, n_chunks - 1)
    v_one = lambda fn: pl.BlockSpec((None, CHUNK, 2 * KV_WIDTH), lambda b, s: (b, fn(s), 0))
    v_specs = [v_one(prev_chunk), pl.BlockSpec((None, per_step * CHUNK, 2 * KV_WIDTH), lambda b, s: (b, s, 0)),
               v_one(next_chunk), v_one(lambda s: 0)]
    k_one = lambda fn: pl.BlockSpec((1, 2 * KV_WIDTH, CHUNK), lambda b, s: (b * n_chunks + fn(s), 0, 0))
    k_specs = [k_one(prev_chunk),
               pl.BlockSpec((per_step, 2 * KV_WIDTH, CHUNK), lambda b, s: (b * n_steps + s, 0, 0)),
               k_one(next_chunk), k_one(lambda s: 0)]
    q_spec = pl.BlockSpec((None, per_step * CHUNK, ATT_WIDTH), lambda b, s: (b, s, 0))
    out = pl.pallas_call(
        functools.partial(_attention_kernel, n_chunks=n_chunks, chunks_per_step=per_step),
        grid=(batch, n_steps),
        in_specs=[q_spec] + k_specs + v_specs + [_resident((N_ATT_HEADS, LANES)), _resident((1, ATT_WIDTH))],
        out_specs=q_spec,
        out_shape=jax.ShapeDtypeStruct((batch, seq_rows, ATT_WIDTH), BF16),
        compiler_params=pltpu.CompilerParams(dimension_semantics=("parallel", "parallel"),
                                             vmem_limit_bytes=32 * MIB),
        name="attention",
    )(q3, ak, ak, ak, ak, v3, v3, v3, v3, sink, gain)
    return out.reshape(batch * seq_rows, ATT_WIDTH)


def _mixed_residual(ret_ref, att_ref, w_ref, h_ref, rows=slice(None)):
    acc = jnp.dot(ret_ref[rows, :], w_ref[0:RET_WIDTH, :], preferred_element_type=F32)
    acc = acc + jnp.dot(att_ref[rows, :], w_ref[RET_WIDTH:, :], preferred_element_type=F32)
    return h_ref[rows, :] + acc


def _mix_specs(tm):
    row_spec = lambda width: pl.BlockSpec((tm, width), lambda i, *_: (i, 0))
    return [row_spec(RET_WIDTH), row_spec(ATT_WIDTH), _resident((RET_WIDTH + ATT_WIDTH, D_MODEL)),
            row_spec(D_MODEL)]


def _ff_halves(d_ff):
    n_mxu_tiles = pl.cdiv(d_ff, MXU_V7X_COLUMNS)
    first = min(d_ff, pl.cdiv(n_mxu_tiles, 2) * MXU_V7X_COLUMNS)
    return tuple((c0, c1) for c0, c1 in ((0, first), (first, d_ff)) if c1 > c0)


def _ffn_kernel(ret_ref, att_ref, wo_ref, h_ref, gain_ref, wg_ref, wu_ref, wd_ref, o_ref):
    x = _mixed_residual(ret_ref, att_ref, wo_ref, h_ref)
    f = _rms(x, gain_ref[...]).astype(BF16)
    acc = x
    for c0, c1 in _ff_halves(wg_ref.shape[1]):
        gate = jnp.dot(f, wg_ref[:, c0:c1], preferred_element_type=F32)
        up = jnp.dot(f, wu_ref[:, c0:c1], preferred_element_type=F32)
        act = (_silu(gate) * up).astype(BF16)
        acc = acc + jnp.dot(act, wd_ref[c0:c1, :], preferred_element_type=F32)
    o_ref[...] = acc


def _ffn(ret, att, w_out, h, gain, wg, wu, wd):
    rows = h.shape[0]
    d_ff = wg.shape[1]
    tm = _row_tile(rows, (768, 640, 512, 384, 256, 128))
    return pl.pallas_call(
        _ffn_kernel,
        grid=(rows // tm,),
        in_specs=_mix_specs(tm) + [_resident((1, D_MODEL)), _resident((D_MODEL, d_ff)), _resident((D_MODEL, d_ff)),
                                   _resident((d_ff, D_MODEL))],
        out_specs=pl.BlockSpec((tm, D_MODEL), lambda i: (i, 0)),
        out_shape=jax.ShapeDtypeStruct((rows, D_MODEL), F32),
        input_output_aliases={3: 0},
        compiler_params=pltpu.CompilerParams(dimension_semantics=("parallel",), vmem_limit_bytes=56 * MIB),
        name="ffn",
    )(ret, att, w_out, h, gain, wg, wu, wd)


ROUTE_G1, ROUTE_G2, ROUTE_D1, ROUTE_D2 = range(4)
TILE_BASE, TILE_COUNT, TILE_START = range(3)


def _lane_pick(x, lane, j):
    return jnp.sum(jnp.where(lane == j, x, 0.0), axis=-1, keepdims=True)


def _lane_scalar(vec, lane_row, e):
    return jnp.sum(jnp.where(lane_row == e, vec, 0.0)).astype(I32)


RUN_ALIGN_BITS = 3
RUN_ALIGN = 1 << RUN_ALIGN_BITS


def _run_piece_count(tm):
    return tm.bit_length() - RUN_ALIGN_BITS


def _run_copies(count, src0, dst0, src_ref, dst_ref, sems):
    copies = []
    for j in range(sems.shape[0]):
        k = j + RUN_ALIGN_BITS
        size = 1 << k
        start = (count >> (k + 1)) << (k + 1)
        src = src_ref.at[pl.ds(pl.multiple_of(src0 + start, RUN_ALIGN), size)]
        dst = dst_ref.at[pl.ds(pl.multiple_of(dst0 + start, RUN_ALIGN), size)]
        copies.append((((count >> k) & 1) == 1, pltpu.make_async_copy(src, dst, sems.at[j])))
    return copies


def _router_kernel(*refs, region_rows, finalize, aliased, tiles_per_step):
    ret_ref, att_ref, wo_ref, h_ref, gain_ref, wr_ref, base_ref = refs[:7]
    refs = refs[7 + (1 if aliased else 0):]
    hout_ref, xs_ref, route_ref, tile_ref, cnt_ref, sorted_ref, zero_ref, sems, zsem, prev_ref = refs
    i = pl.program_id(0)
    tm = h_ref.shape[0] // tiles_per_step
    n_sorted = sorted_ref.shape[1]

    @pl.when(i == 0)
    def _():
        cnt_ref[...] = base_ref[...]

    lane = lax.broadcasted_iota(I32, (tm, LANES), 1)
    lane_row = lane[0:1, :]
    ri = lax.broadcasted_iota(I32, (tm, tm), 0)
    ci = lax.broadcasted_iota(I32, (tm, tm), 1)
    earlier_tokens = jnp.where(ri > ci, 1.0, 0.0).astype(BF16)
    li = lax.broadcasted_iota(I32, (LANES, LANES), 0)
    lj = lax.broadcasted_iota(I32, (LANES, LANES), 1)
    earlier_experts = jnp.where(li < lj, 1.0, 0.0).astype(BF16)
    slot = lax.broadcasted_iota(I32, (n_sorted, tm), 0)
    sub = lax.broadcasted_iota(I32, (SUBLANES, LANES), 0)

    def as_row(col):
        wide = jnp.broadcast_to(col, (tm, LANES))
        return jnp.concatenate([wide[c * LANES:(c + 1) * LANES, :].T for c in range(tm // LANES)], axis=1)[0:1, :]

    recs = [dict(t=t, rows=slice(t * tm, (t + 1) * tm)) for t in range(tiles_per_step)]

    def stage(fn):
        for rec in recs:
            rec.update(fn(rec))

    def mix(r):
        h_mixed = _mixed_residual(ret_ref, att_ref, wo_ref, h_ref, r["rows"])
        hout_ref[r["rows"], :] = h_mixed
        return dict(f=_rms(h_mixed, gain_ref[...]))

    def logits(r):
        f_hi = r["f"].astype(BF16)
        f_lo = (r["f"] - f_hi.astype(F32)).astype(BF16)
        both = (jnp.dot(f_hi, wr_ref[...], preferred_element_type=F32)
                + jnp.dot(f_lo, wr_ref[...], preferred_element_type=F32))
        lg = jnp.where(lane < N_EXPERTS, both[:, :LANES] + both[:, LANES:], -jnp.inf)
        return dict(f_hi=f_hi, lg=lg)

    def first_choice(r):
        m1 = jnp.max(r["lg"], axis=-1, keepdims=True)
        return dict(m1=m1, i1=jnp.min(jnp.where(r["lg"] == m1, lane, LANES), axis=-1, keepdims=True))

    def second_choice(r):
        lg2 = jnp.where(lane == r["i1"], -jnp.inf, r["lg"])
        m2 = jnp.max(lg2, axis=-1, keepdims=True)
        return dict(m2=m2, i2=jnp.min(jnp.where(lg2 == m2, lane, LANES), axis=-1, keepdims=True))

    def rank_tokens(r):
        active = jnp.max(jnp.abs(r["f"]), axis=-1, keepdims=True) > 0.0
        sel = jnp.where(active, jnp.where(lane == r["i1"], 1.0, 0.0) + jnp.where(lane == r["i2"], 1.0, 0.0), 0.0)
        rank = jnp.dot(earlier_tokens, sel.astype(BF16), preferred_element_type=F32)
        tiles_per_run = jnp.floor((jnp.sum(sel, axis=0, keepdims=True) + (RUN_ALIGN - 1.0)) * (1.0 / RUN_ALIGN))
        start = RUN_ALIGN * jnp.dot(jnp.broadcast_to(tiles_per_run, (SUBLANES, LANES)).astype(BF16),
                                    earlier_experts, preferred_element_type=F32)[0:1, :]
        return dict(active=active, rank=rank, count=tiles_per_run * RUN_ALIGN, start=start)

    def destinations(r):
        dest = r["start"] + r["rank"]
        no_slot = -1.0
        d1 = jnp.where(r["active"], _lane_pick(dest, lane, r["i1"]), no_slot)
        d2 = jnp.where(r["active"], _lane_pick(dest, lane, r["i2"]), no_slot)
        e2 = jnp.exp(r["m2"] - r["m1"])
        g1 = 1.0 / (1.0 + e2)
        route = jnp.zeros((tm, LANES), F32)
        for j, val in ((ROUTE_G1, g1), (ROUTE_G2, e2 * g1), (ROUTE_D1, d1), (ROUTE_D2, d2)):
            route = jnp.where(lane == j, val, route)
        route_ref[r["rows"], :] = route
        return dict(d1=d1, d2=d2)

    def sort_rows(r):
        onehot_t = (jnp.where(slot == as_row(r["d1"]).astype(I32), 1.0, 0.0)
                    + jnp.where(slot == as_row(r["d2"]).astype(I32), 1.0, 0.0))
        sorted_ref[r["t"]] = jnp.dot(onehot_t.astype(BF16), r["f_hi"], preferred_element_type=F32)
        return {}

    def run_copies(records, live):
        copies = []
        for t, rec in enumerate(records):
            for e in range(N_EXPERTS):
                n_e = jnp.where(live, _lane_scalar(rec[TILE_COUNT:TILE_COUNT + 1, :], lane_row, e), 0)
                src0 = _lane_scalar(rec[TILE_START:TILE_START + 1, :], lane_row, e)
                dst0 = e * region_rows + _lane_scalar(rec[TILE_BASE:TILE_BASE + 1, :], lane_row, e)
                copies += _run_copies(n_e, src0, dst0, sorted_ref.at[t], xs_ref, sems.at[t, e])
        return copies

    @pl.when(i == 0)
    def _():
        prev_ref[...] = jnp.zeros(prev_ref.shape, F32)

    for fn in (mix, logits, first_choice, second_choice, rank_tokens, destinations):
        stage(fn)
    for cond, cp in run_copies([prev_ref[t] for t in range(tiles_per_step)], i > 0):
        pl.when(cond)(cp.wait)
    stage(sort_rows)
    base = cnt_ref[0:1, :]
    for rec in recs:
        record = jnp.where(sub == TILE_BASE, base, jnp.where(
            sub == TILE_COUNT, rec["count"], jnp.where(sub == TILE_START, rec["start"], 0.0)))
        tile_ref[rec["t"]] = record
        prev_ref[rec["t"]] = record
        rec["record"] = record
        base = base + rec["count"]
    copies = run_copies([rec["record"] for rec in recs], True)
    for k, (cond, cp) in enumerate(copies):
        pl.when(cond)(functools.partial(cp.start, priority=k % 2))
    cnt_ref[...] = jnp.broadcast_to(base, cnt_ref.shape)

    @pl.when(i == pl.num_programs(0) - 1)
    def _():
        for cond, cp in copies:
            pl.when(cond)(cp.wait)

    if finalize:
        @pl.when(i == pl.num_programs(0) - 1)
        def _():
            zero_ref[...] = jnp.zeros(zero_ref.shape, F32)
            total = cnt_ref[0:1, :]
            for e in range(N_EXPERTS):
                end = pl.multiple_of(e * region_rows + _lane_scalar(total, lane_row, e), RUN_ALIGN)
                cp = pltpu.make_async_copy(zero_ref, xs_ref.at[pl.ds(end, EXPERT_ROW_TILE)], zsem)
                cp.start()
                cp.wait()


def _router_tile(rows):
    return _row_tile(rows, (512, 640, 384, 256, 128))


def _sorted_rows(tm):
    return TOP_K * tm + pl.cdiv(N_EXPERTS * (RUN_ALIGN - 1), LANES) * LANES


def _router(mix, gain, w_router, base_counts, xs, region_rows, finalize):
    rows = mix[3].shape[0]
    tm = _router_tile(rows)
    n_tiles = rows // tm
    per_step = 2 if n_tiles % 2 == 0 else 1
    step_rows = per_step * tm
    aliased = xs is not None
    row_spec = lambda width: pl.BlockSpec((step_rows, width), lambda i: (i, 0))
    in_specs = _mix_specs(step_rows) + [_resident((1, D_MODEL)), _resident((D_MODEL, 2 * LANES)),
                                        _resident((SUBLANES, LANES))]
    args = [*mix, gain, w_router, base_counts]
    aliases = {3: 0}
    if aliased:
        aliases[len(args)] = 1
        in_specs.append(pl.BlockSpec(memory_space=pl.ANY))
        args.append(xs)
    n_pieces = _run_piece_count(tm)
    return pl.pallas_call(
        functools.partial(_router_kernel, region_rows=region_rows, finalize=finalize, aliased=aliased,
                          tiles_per_step=per_step),
        grid=(n_tiles // per_step,),
        in_specs=in_specs,
        out_specs=[row_spec(D_MODEL), pl.BlockSpec(memory_space=pl.ANY), row_spec(LANES),
                   pl.BlockSpec((per_step, SUBLANES, LANES), lambda i: (i, 0, 0)),
                   pl.BlockSpec((SUBLANES, LANES), lambda i: (0, 0))],
        out_shape=[jax.ShapeDtypeStruct((rows, D_MODEL), F32),
                   jax.ShapeDtypeStruct((N_EXPERTS * region_rows, D_MODEL), F32),
                   jax.ShapeDtypeStruct((rows, LANES), F32),
                   jax.ShapeDtypeStruct((n_tiles, SUBLANES, LANES), F32),
                   jax.ShapeDtypeStruct((SUBLANES, LANES), F32)],
        scratch_shapes=[pltpu.VMEM((per_step, _sorted_rows(tm), D_MODEL), F32),
                        pltpu.VMEM((EXPERT_ROW_TILE, D_MODEL), F32),
                        pltpu.SemaphoreType.DMA((per_step, N_EXPERTS, n_pieces)), pltpu.SemaphoreType.DMA(()),
                        pltpu.VMEM((per_step, SUBLANES, LANES), F32)],
        input_output_aliases=aliases,
        compiler_params=pltpu.CompilerParams(dimension_semantics=("arbitrary",), vmem_limit_bytes=56 * MIB),
        name="router",
    )(*args)


def _experts_kernel(blk_ref, exp_ref, nvalid_ref, x_ref, wg_ref, wu_ref, wd_ref, *y_refs):
    y_ref = y_refs[-1]
    partial_ref = y_refs[0] if len(y_refs) == 2 else None
    valid = pl.program_id(0) < nvalid_ref[0]

    @pl.when(valid)
    def _():
        x = x_ref[...].astype(BF16)
        gate = jnp.dot(x, wg_ref[...], preferred_element_type=F32)
        up = jnp.dot(x, wu_ref[...], preferred_element_type=F32)
        part = jnp.dot((_silu(gate) * up).astype(BF16), wd_ref[...], preferred_element_type=F32)
        y_ref[...] = part if partial_ref is None else partial_ref[...] + part

    @pl.when(jnp.logical_not(valid))
    def _():
        y_ref[...] = jnp.zeros(y_ref.shape, F32) if partial_ref is None else partial_ref[...]


def _experts(xs, tile_block, tile_expert, n_valid, wg, wu, wd):
    n_tiles = tile_block.shape[0]
    d_ff = wg.shape[2]
    ff_chunk = d_ff // 2
    assert ff_chunk % MXU_V7X_COLUMNS == 0
    tg = EXPERT_ROW_TILE
    y = None
    for c in range(d_ff // ff_chunk):
        in_specs = [pl.BlockSpec((tg, D_MODEL), lambda i, blk, ex, nv: (blk[i], 0)),
                    pl.BlockSpec((None, D_MODEL, ff_chunk), lambda i, blk, ex, nv, c=c: (ex[i], 0, c)),
                    pl.BlockSpec((None, D_MODEL, ff_chunk), lambda i, blk, ex, nv, c=c: (ex[i], 0, c)),
                    pl.BlockSpec((None, ff_chunk, D_MODEL), lambda i, blk, ex, nv, c=c: (ex[i], c, 0))]
        args = [tile_block, tile_expert, n_valid, xs, wg, wu, wd]
        y_spec = pl.BlockSpec((tg, D_MODEL), lambda i, blk, ex, nv: (i, 0))
        if y is not None:
            in_specs.append(y_spec)
            args.append(y)
        y = pl.pallas_call(
            _experts_kernel,
            grid_spec=pltpu.PrefetchScalarGridSpec(num_scalar_prefetch=3, grid=(n_tiles,), in_specs=in_specs,
                                                   out_specs=y_spec),
            out_shape=jax.ShapeDtypeStruct((n_tiles * tg, D_MODEL), F32),
            input_output_aliases={len(args) - 1: 0} if len(args) == 8 else {},
            compiler_params=pltpu.CompilerParams(dimension_semantics=("arbitrary",), vmem_limit_bytes=56 * MIB),
            name="experts",
        )(*args)
    return y


def _combine_kernel(tinfo_ref, yoff_ref, h_ref, route_ref, ys_ref, o_ref, sorted_ref, sems, *unpad_scratch,
                    seq_chunks):
    i = pl.program_id(0)
    n_steps = pl.num_programs(0)
    n_sorted = sorted_ref.shape[1]
    slot = i % 2

    def tile_copies(tile, buf, live):
        copies = []
        for e in range(N_EXPERTS):
            rec = (tile * 3) * N_EXPERTS + e
            base = tinfo_ref[rec + TILE_BASE * N_EXPERTS]
            n_e = jnp.where(live, tinfo_ref[rec + TILE_COUNT * N_EXPERTS], 0)
            dst0 = tinfo_ref[rec + TILE_START * N_EXPERTS]
            copies += _run_copies(n_e, yoff_ref[e] + base, dst0, ys_ref, sorted_ref.at[buf], sems.at[buf, e])
        return copies

    @pl.when(i == 0)
    def _():
        sorted_ref[...] = jnp.zeros(sorted_ref.shape, F32)
        for cond, cp in tile_copies(0, 0, True):
            pl.when(cond)(cp.start)

    for k, (cond, cp) in enumerate(tile_copies(jnp.minimum(i + 1, n_steps - 1), 1 - slot, i + 1 < n_steps)):
        pl.when(cond)(functools.partial(cp.start, priority=k % 2))
    for cond, cp in tile_copies(i, slot, True):
        pl.when(cond)(cp.wait)
    y = sorted_ref[slot].astype(BF16)
    route = route_ref[...]
    lane = lax.broadcasted_iota(I32, route.shape, 1)
    pick = lambda j: _lane_pick(route, lane, j)
    col = lax.broadcasted_iota(I32, (route.shape[0], n_sorted), 1)
    gated = (jnp.where(col == pick(ROUTE_D1).astype(I32), pick(ROUTE_G1), 0.0)
             + jnp.where(col == pick(ROUTE_D2).astype(I32), pick(ROUTE_G2), 0.0))
    result = h_ref[...] + jnp.dot(gated.astype(BF16), y, preferred_element_type=F32)
    if seq_chunks is None:
        o_ref[...] = result
        return

    res_ref, out_sems = unpad_scratch
    chunks_per_tile = h_ref.shape[0] // CHUNK

    def chunk_copies(step, buf):
        copies = []
        for j in range(chunks_per_tile):
            chunk = step * chunks_per_tile + j
            seq = chunk // seq_chunks
            in_seq = chunk - seq * seq_chunks
            dst = pl.multiple_of(seq * ((seq_chunks - 1) * CHUNK) + (in_seq - 1) * CHUNK, CHUNK)
            cp = pltpu.make_async_copy(res_ref.at[buf, pl.ds(j * CHUNK, CHUNK)], o_ref.at[pl.ds(dst, CHUNK)],
                                       out_sems.at[buf, j])
            copies.append((jnp.logical_and(step >= 0, in_seq >= 1), cp))
        return copies

    res_ref[slot] = result
    for cond, cp in chunk_copies(i, slot):
        pl.when(cond)(cp.start)
    for cond, cp in chunk_copies(i - 1, 1 - slot):
        pl.when(cond)(cp.wait)

    @pl.when(i == n_steps - 1)
    def _():
        for cond, cp in chunk_copies(i, slot):
            pl.when(cond)(cp.wait)


def _combine(h, route, tile_info, y_offsets, ys, unpad_seq_rows=None):
    rows = h.shape[0]
    tm = rows // tile_info.shape[0]
    n_pieces = _run_piece_count(tm)
    tinfo = tile_info[:, :3, :N_EXPERTS].astype(I32).reshape(-1)
    scratch = [pltpu.VMEM((2, _sorted_rows(tm), D_MODEL), F32), pltpu.SemaphoreType.DMA((2, N_EXPERTS, n_pieces))]
    if unpad_seq_rows is None:
        seq_chunks, out_rows = None, rows
        out_spec = pl.BlockSpec((tm, D_MODEL), lambda i, *_: (i, 0))
    else:
        seq_chunks = unpad_seq_rows // CHUNK
        out_rows = rows // unpad_seq_rows * (unpad_seq_rows - CHUNK)
        out_spec = pl.BlockSpec(memory_space=pl.ANY)
        scratch += [pltpu.VMEM((2, tm, D_MODEL), F32), pltpu.SemaphoreType.DMA((2, tm // CHUNK))]
    grid_spec = pltpu.PrefetchScalarGridSpec(
        num_scalar_prefetch=2,
        grid=(rows // tm,),
        in_specs=[pl.BlockSpec((tm, D_MODEL), lambda i, *_: (i, 0)),
                  pl.BlockSpec((tm, LANES), lambda i, *_: (i, 0)),
                  pl.BlockSpec(memory_space=pl.ANY)],
        out_specs=out_spec,
        scratch_shapes=scratch,
    )
    return pl.pallas_call(
        functools.partial(_combine_kernel, seq_chunks=seq_chunks),
        grid_spec=grid_spec,
        out_shape=jax.ShapeDtypeStruct((out_rows, D_MODEL), F32),
        input_output_aliases={2: 0} if unpad_seq_rows is None else {},
        compiler_params=pltpu.CompilerParams(dimension_semantics=("arbitrary",), vmem_limit_bytes=48 * MIB),
        name="combine",
    )(tinfo, y_offsets, h, route, ys)


def _routed_experts(mixes, gain, router_w, expert_w, unpad_seq_rows):
    tg = EXPERT_ROW_TILE
    group_rows = [mix[3].shape[0] for mix in mixes]
    total_rows = sum(group_rows)
    run_pad = (RUN_ALIGN - 1) * sum(rows // _router_tile(rows) for rows in group_rows)
    region_rows = (pl.cdiv(total_rows + run_pad, tg) + 1) * tg
    counts = jnp.zeros((SUBLANES, LANES), F32)
    xs, hs, routes, tiles = None, [], [], []
    for g, mix in enumerate(mixes):
        h, xs, route, tile_info, counts = _router(mix, gain, router_w, counts, xs, region_rows,
                                                  finalize=(g == len(mixes) - 1))
        hs.append(h)
        routes.append(route)
        tiles.append(tile_info)
    n_e = counts[0, :N_EXPERTS].astype(I32)
    tiles_e = (n_e + tg - 1) // tg
    first_tile = jnp.cumsum(tiles_e) - tiles_e
    n_valid = jnp.sum(tiles_e)
    n_tiles = (TOP_K * total_rows + N_EXPERTS * (run_pad + tg - 1)) // tg + 1
    t = jnp.minimum(jnp.arange(n_tiles, dtype=I32), n_valid - 1)
    tile_expert = jnp.sum((t[:, None] >= (first_tile + tiles_e)[None, :]).astype(I32), axis=1)
    tile_block = tile_expert * (region_rows // tg) + (t - first_tile[tile_expert])
    ys = _experts(xs, tile_block.astype(I32), tile_expert.astype(I32), n_valid.reshape(1).astype(I32), *expert_w)
    y_offsets = (first_tile * tg).astype(I32)
    return [_combine(h, route, tile_info, y_offsets, ys, seq_rows)
            for h, route, tile_info, seq_rows in zip(hs, routes, tiles, unpad_seq_rows)]


def _rotary_tables(batch, seq_rows):
    pos = (jnp.arange(seq_rows, dtype=I32) - PAD_FRONT).astype(F32)

    def cos_sin(n_rot, theta):
        half = n_rot // 2
        inv = theta ** (-jnp.arange(half, dtype=F32) * 2.0 / n_rot)
        ang = pos[:, None] * inv[None, :]
        return jnp.cos(ang), jnp.sin(ang)

    cos, sin = cos_sin(HEAD_DIM, RET_ROPE_THETA)
    cr = jnp.tile(cos, (1, 2 * LANES // HEAD_DIM))
    sr = jnp.tile(jnp.concatenate([-sin, sin], axis=1), (1, LANES // HEAD_DIM))
    cos, sin = cos_sin(ROPE_DIMS, ROPE_THETA)
    rest = HEAD_DIM - ROPE_DIMS
    ca = jnp.tile(jnp.concatenate([cos, cos, jnp.ones((seq_rows, rest), F32)], axis=1), (1, LANES // HEAD_DIM))
    sa = jnp.tile(jnp.concatenate([-sin, sin, jnp.zeros((seq_rows, rest), F32)], axis=1), (1, LANES // HEAD_DIM))
    return tuple(jnp.tile(t, (batch, 1)) for t in (cr, sr, ca, sa))


def _retention_params(log_gf, log_gb, gain):
    n_pairs = RET_WIDTH // LANES
    per_lane = lambda t: jnp.repeat(t.astype(F32).reshape(n_pairs, 2), HEAD_DIM, axis=1)
    per_head = lambda t, j: jnp.broadcast_to(t.astype(F32).reshape(n_pairs, 2)[:, j:j + 1], (n_pairs, LANES))
    rows = [per_lane(log_gf), per_lane(log_gb), per_head(log_gf, 0), per_head(log_gf, 1),
            per_head(log_gb, 0), per_head(log_gb, 1), gain.astype(F32).reshape(n_pairs, LANES),
            jnp.zeros((n_pairs, LANES), F32)]
    return jnp.stack(rows, axis=1)


def _token_mixer(h, lp, tabs, batch, seq_rows, embed=None):
    outs = _inproj(h, lp["norm_mix"], lp["w_in"], tabs, lp["q_gain"], lp["k_gain"], lp["ones_bd"], embed)
    if embed is not None:
        h, outs = outs[0], outs[1:]
    rq, rk, rv, rg, aq, ak, av = outs
    ret = _retention(rq, rk, rv, rg, lp["ret_dec"], batch, seq_rows)
    att = _attention(aq, ak, av, lp["sink"], lp["att_gain"], batch, seq_rows)
    return ret, att, lp["w_out"], h


def kernel(x_prompt, x_sample, meta_tokens, norm_mix, w_in, ret_log_decay_fwd, ret_log_decay_bwd, ret_out_gain,
           q_norm_gain, k_norm_gain, attn_sink, attn_out_gain, w_out, norm_ffn, ffn_w_gate, ffn_w_up, ffn_w_down,
           moe_router, moe_w_gate, moe_w_up, moe_w_down):
    depth = w_in.shape[0]
    ri = lax.broadcasted_iota(I32, (LANES, LANES), 0)
    ci = lax.broadcasted_iota(I32, (LANES, LANES), 1)
    ones_bd = jnp.where((ri < HEAD_DIM) == (ci < HEAD_DIM), 1.0 / HEAD_DIM, 0.0).astype(BF16)
    row = lambda t: t.astype(F32).reshape(1, -1)

    xs = (x_prompt, x_sample)
    shapes = [(x.shape[0], x.shape[1] + CHUNK) for x in xs]
    hs = [None] * len(xs)
    meta = meta_tokens.astype(F32)
    embeds = [(x.astype(F32).reshape(-1, D_MODEL), meta, r) for x, (_, r) in zip(xs, shapes)]
    tabs = [_rotary_tables(b, r) for b, r in shapes]

    for l in range(depth):
        lp = {
            "norm_mix": row(norm_mix[l]),
            "w_in": w_in[l].astype(BF16),
            "q_gain": jnp.tile(row(q_norm_gain[l]), (1, LANES // HEAD_DIM)),
            "k_gain": jnp.tile(row(k_norm_gain[l]), (1, LANES // HEAD_DIM)),
            "ones_bd": ones_bd,
            "ret_dec": _retention_params(ret_log_decay_fwd[l], ret_log_decay_bwd[l], ret_out_gain[l]),
            "sink": jnp.broadcast_to(attn_sink[l].astype(F32)[:, None] * LOG2_E, (N_ATT_HEADS, LANES)),
            "att_gain": row(attn_out_gain[l]),
            "w_out": w_out[l].astype(BF16),
        }
        mixes = [_token_mixer(h, lp, tab, b, r, embed if l == 0 else None)
                 for h, tab, (b, r), embed in zip(hs, tabs, shapes, embeds)]
        i = l // 2
        if l % 2 == 0:
            w = (ffn_w_gate[i].astype(BF16), ffn_w_up[i].astype(BF16), ffn_w_down[i].astype(BF16))
            hs = [_ffn(*mix, row(norm_ffn[l]), *w) for mix in mixes]
        else:
            wr = jnp.pad(moe_router[i].astype(F32), ((0, 0), (0, LANES - N_EXPERTS)))
            wr_hi = wr.astype(BF16)
            router_w = jnp.concatenate([wr_hi, (wr - wr_hi.astype(F32)).astype(BF16)], axis=1)
            expert_w = (moe_w_gate[i].astype(BF16), moe_w_up[i].astype(BF16), moe_w_down[i].astype(BF16))
            last = l == depth - 1
            hs = _routed_experts(mixes, row(norm_ffn[l]), router_w, expert_w,
                                 [r if last else None for _, r in shapes])
    if depth % 2 == 0:
        return tuple(h.reshape(x.shape).astype(x.dtype) for h, x in zip(hs, xs))
    return tuple(h.reshape(b, r, D_MODEL)[:, CHUNK:].astype(x.dtype) for h, x, (b, r) in zip(hs, xs, shapes))
```
